```python
import math
import jax, jax.numpy as jnp
from jax import lax
import numpy as np

D_MODEL = 1024
BATCH = 8
SEQ = 2048
DEPTH = 2

GRID_W = 64
CTX_LEN = 256
N_EVEN = (DEPTH + 1) // 2
N_ODD = DEPTH // 2
HEAD_DIM = 64
MIX_WIDTH = D_MODEL
HALF_MIX = MIX_WIDTH // 2
Q_BLOCK = 128
ROPE_THETA = 10000.0
EPS = 1e-6
N_MOD = 6
LRU_WIDTH = HALF_MIX
LRU_BLOCKS = LRU_WIDTH // HEAD_DIM
LRU_BLOCK_SIZE = LRU_WIDTH // LRU_BLOCKS
LRU_CONV = 4
LRU_C = 8.0
GQA_Q_HEADS = HALF_MIX // HEAD_DIM
GQA_KV_HEADS = 2
AB_IN = 2 * LRU_WIDTH + (GQA_Q_HEADS + 2 * GQA_KV_HEADS) * HEAD_DIM
HY_WIDTH = HALF_MIX
HY_ORDER = 2
HY_CONV = 3
HY_BANDS = 16
HY_POS_DIM = 1 + 2 * HY_BANDS
HY_FILTER_HIDDEN = 64
HY_TARGET = 1e-2
HY_FAST_PCT = 0.3
HY_SLOW_PCT = 1.5
HY_MIN_DECAY = math.log(HY_TARGET) / HY_SLOW_PCT
HY_MAX_DECAY = math.log(HY_TARGET) / HY_FAST_PCT
HY_SHIFT = 0.05
MLA_HEADS = HALF_MIX // HEAD_DIM
MLA_Q_RANK = D_MODEL // 4
MLA_KV_RANK = D_MODEL // 8
MLA_NOPE = HEAD_DIM
MLA_ROPE = HEAD_DIM // 2
MLA_V = HEAD_DIM
MLA_QK = MLA_NOPE + MLA_ROPE
CD_IN = (HY_ORDER + 1) * HY_WIDTH + MLA_Q_RANK + MLA_KV_RANK + MLA_ROPE
D_FF = 2816
N_EXPERTS = 8
TOP_K = 2
D_FF_EXPERT = 3584
MOE_BLOCK = 128

kernel_name = 'hybrid_lru_gqa_hyena_mla_moe'

F32 = jnp.float32


def rms_norm(x, g):
    xf = x.astype(F32)
    y = xf * lax.rsqrt(jnp.mean(xf * xf, axis=-1, keepdims=True) + EPS)
    return (y * g.astype(F32)).astype(x.dtype)


def modulate(h, shift, scale):
    return h * (1 + scale) + shift


def grid_angles(rows, rot_dim):
    n_freq = rot_dim // 4
    inv_freq = ROPE_THETA ** (-jnp.arange(n_freq, dtype=F32) / n_freq)
    t = jnp.arange(rows * GRID_W)
    row = (t // GRID_W).astype(F32)
    col = (t % GRID_W).astype(F32)
    ang = jnp.concatenate([row[:, None] * inv_freq, col[:, None] * inv_freq], axis=-1)
    return jnp.cos(ang), jnp.sin(ang)


def apply_rope(x, cos, sin):
    half = x.shape[-1] // 2
    shape = (1, cos.shape[0]) + (1,) * (x.ndim - 3) + (half,)
    cs = cos.reshape(shape).astype(x.dtype)
    sn = sin.reshape(shape).astype(x.dtype)
    x1, x2 = x[..., :half], x[..., half:]
    return jnp.concatenate([x1 * cs - x2 * sn, x1 * sn + x2 * cs], axis=-1)


def depthwise_conv(x, w, b, pad):
    y = lax.conv_general_dilated(x, w[:, None, :].astype(x.dtype), (1,), [pad],
                                 dimension_numbers=('NWC', 'WIO', 'NWC'),
                                 feature_group_count=x.shape[-1])
    return y + b.astype(x.dtype)


def attend_blocked(q, k, v):
    b_, sq, hk, g, dq = q.shape
    nb = sq // Q_BLOCK
    scale = dq ** -0.5
    qb = jnp.moveaxis(q.reshape(b_, nb, Q_BLOCK, hk, g, dq), 1, 0)

    def one_block(qi):
        s = jnp.einsum('bqhgd,bkhd->bhgqk', qi, k).astype(F32) * scale
        p = jax.nn.softmax(s, axis=-1).astype(v.dtype)
        return jnp.einsum('bhgqk,bkhe->bqhge', p, v)

    o = lax.map(one_block, qb)
    return jnp.moveaxis(o, 0, 1).reshape(b_, sq, hk, g, v.shape[-1])


def block_diag_linear(x, w, b):
    b_, l = x.shape[:2]
    xb = x.reshape(b_, l, LRU_BLOCKS, LRU_BLOCK_SIZE)
    y = jnp.einsum('blni,nij->blnj', xb, w.astype(F32)) + b.astype(F32)
    return y.reshape(b_, l, LRU_WIDTH)


def _linear_combine(left, right):
    return left[0] * right[0], right[0] * left[1] + right[1]


def rglru_scan(u, w_a, b_a, w_x, b_x, lam, h0):
    uf = u.astype(F32)
    r = jax.nn.sigmoid(block_diag_linear(uf, w_a, b_a))
    i = jax.nn.sigmoid(block_diag_linear(uf, w_x, b_x))
    log_a = -LRU_C * r * jax.nn.softplus(-lam.astype(F32))
    a = jnp.exp(log_a)
    bterm = jnp.sqrt(-jnp.expm1(2.0 * log_a)) * (i * uf)
    a_cum, b_cum = lax.associative_scan(_linear_combine, (a, bterm), axis=1)
    return a_cum * h0[:, None, :] + b_cum


def bidir_rglru(u, w_a, b_a, w_x, b_x, lam, h0_fwd, h0_bwd):
    h_f = rglru_scan(u, w_a[0], b_a[0], w_x[0], b_x[0], lam[0], h0_fwd)
    h_b = rglru_scan(u[:, ::-1], w_a[1], b_a[1], w_x[1], b_x[1], lam[1], h0_bwd)
    return h_f + h_b[:, ::-1], h_f[:, -1], h_b[:, -1]


def hyena_filters(length, w1, b1, w2, b2, w3, freq):
    t = jnp.arange(length, dtype=F32)[:, None]
    t_norm = t / max(length - 1, 1)
    bands = jnp.linspace(1e-4, HY_BANDS - 1, HY_BANDS, dtype=F32)
    ang = 2.0 * math.pi * t * bands / length
    z = jnp.concatenate([t_norm, jnp.cos(ang), -jnp.sin(ang)], axis=-1)
    fr = freq.astype(F32)
    h = jnp.sin(fr * (z @ w1.astype(F32) + b1.astype(F32)))
    h = jnp.sin(fr * (h @ w2.astype(F32) + b2.astype(F32)))
    h = h @ w3.astype(F32)
    deltas = jnp.abs(jnp.linspace(HY_MIN_DECAY, HY_MAX_DECAY, HY_WIDTH, dtype=F32))
    window = jnp.exp(-t_norm * deltas) + HY_SHIFT
    return h.reshape(length, HY_ORDER, 2, HY_WIDTH) * window[:, None, None, :]


def bidir_long_conv(u, h_fwd, h_bwd, skip):
    l, ch = h_fwd.shape
    uf = u.astype(F32)
    two_sided = jnp.concatenate([h_fwd, jnp.zeros((1, ch), F32), h_bwd[:0:-1]], axis=0)
    spec = jnp.fft.rfft(uf, n=2 * l, axis=1) * jnp.fft.rfft(two_sided, axis=0)[None]
    y = jnp.fft.irfft(spec, n=2 * l, axis=1)[:, :l]
    return (y + uf * skip.astype(F32)).astype(u.dtype)


def swiglu(h, w1, w3, w2):
    return (jax.nn.silu(h @ w1) * (h @ w3)) @ w2


def moe_swiglu(h, router, w1, w3, w2):
    b_, l, d = h.shape
    t = h.reshape(-1, d)
    n_assign = t.shape[0] * TOP_K
    logits = t.astype(F32) @ router.astype(F32)
    top_logit, top_e = lax.top_k(logits, TOP_K)
    gates = jax.nn.softmax(top_logit, axis=-1)
    flat_e = top_e.reshape(-1)
    flat_tok = jnp.arange(n_assign, dtype=jnp.int32) // TOP_K
    flat_g = gates.reshape(-1)
    order = jnp.argsort(flat_e)
    se = flat_e[order]
    counts = jnp.bincount(flat_e, length=N_EXPERTS)
    padded = (counts + MOE_BLOCK - 1) // MOE_BLOCK * MOE_BLOCK
    start = jnp.cumsum(counts) - counts
    pend = jnp.cumsum(padded)
    pstart = pend - padded
    dest = pstart[se] + jnp.arange(n_assign, dtype=jnp.int32) - start[se]
    n_blocks = -(-n_assign // MOE_BLOCK) + N_EXPERTS
    n_slots = n_blocks * MOE_BLOCK
    slot_tok = jnp.zeros((n_slots,), jnp.int32).at[dest].set(flat_tok[order])
    slot_gate = jnp.zeros((n_slots,), F32).at[dest].set(flat_g[order])
    block_expert = jnp.minimum(
        jnp.searchsorted(pend, jnp.arange(n_blocks, dtype=jnp.int32) * MOE_BLOCK, side='right'),
        N_EXPERTS - 1)
    xs = t[slot_tok].reshape(n_blocks, MOE_BLOCK, d)

    def expert_block(args):
        xb, e = args
        return swiglu(xb, w1[e], w3[e], w2[e])

    ys = lax.map(expert_block, (xs, block_expert)).reshape(n_slots, d)
    out = jnp.zeros_like(t).at[slot_tok].add(ys * slot_gate[:, None].astype(t.dtype))
    return out.reshape(b_, l, d)


def ab_mixer(hc, hl, w_in, w_out, conv_w, conv_b, w_a, b_a, w_x, b_x, lam, q_norm, k_norm,
             cos, sin, need_ctx):
    g = GQA_Q_HEADS // GQA_KV_HEADS
    cuts = [LRU_WIDTH, 2 * LRU_WIDTH, 2 * LRU_WIDTH + GQA_Q_HEADS * HEAD_DIM,
            2 * LRU_WIDTH + (GQA_Q_HEADS + GQA_KV_HEADS) * HEAD_DIM]

    def project(h):
        b_, l = h.shape[:2]
        xr, gate, q, k, v = jnp.split(h @ w_in, cuts, axis=-1)
        u = depthwise_conv(xr, conv_w, conv_b, (LRU_CONV // 2, LRU_CONV - 1 - LRU_CONV // 2))
        q = rms_norm(q.reshape(b_, l, GQA_KV_HEADS, g, HEAD_DIM), q_norm)
        k = rms_norm(k.reshape(b_, l, GQA_KV_HEADS, HEAD_DIM), k_norm)
        return u, gate, q, k, v.reshape(b_, l, GQA_KV_HEADS, HEAD_DIM)

    def merge(rec, gate, att):
        b_, l = rec.shape[:2]
        y = jnp.concatenate([rec.astype(gate.dtype) * jax.nn.gelu(gate),
                             att.reshape(b_, l, GQA_Q_HEADS * HEAD_DIM)], axis=-1)
        return y @ w_out

    uc, gc, qc, kc, vc = project(hc)
    h0 = jnp.zeros((hc.shape[0], LRU_WIDTH), F32)
    rc, s_fwd, s_bwd = bidir_rglru(uc, w_a, b_a, w_x, b_x, lam, h0, h0)
    ul, gl, ql, kl, vl = project(hl)
    rl, _, _ = bidir_rglru(ul, w_a, b_a, w_x, b_x, lam, s_fwd, s_bwd)
    ql = apply_rope(ql, cos, sin)
    kl = apply_rope(kl, cos, sin)
    al = attend_blocked(ql, jnp.concatenate([kc, kl], axis=1), jnp.concatenate([vc, vl], axis=1))
    out_l = merge(rl, gl, al)
    out_c = merge(rc, gc, attend_blocked(qc, kc, vc)) if need_ctx else None
    return out_l, out_c


def cd_mixer(hc, hl, w_in, w_out, conv_w, conv_b, fw1, fb1, fw2, fb2, fw3, freq, skip,
             q_a_norm, q_b, kv_a_norm, kv_b, q_norm, k_norm, cos, sin, need_ctx):
    hy_in = (HY_ORDER + 1) * HY_WIDTH
    col_q = hy_in
    col_kv = hy_in + MLA_Q_RANK

    def hyena(z):
        l = z.shape[1]
        z = depthwise_conv(z, conv_w, conv_b, (HY_CONV // 2, HY_CONV // 2))
        parts = jnp.split(z, HY_ORDER + 1, axis=-1)
        filt = hyena_filters(l, fw1, fb1, fw2, fb2, fw3, freq)
        y = parts[0]
        for o in range(HY_ORDER):
            y = parts[o + 1] * bidir_long_conv(y, filt[:, o, 0], filt[:, o, 1], skip[o])
        return y

    def queries(q_a):
        b_, l = q_a.shape[:2]
        q = (rms_norm(q_a, q_a_norm) @ q_b).reshape(b_, l, MLA_HEADS, 1, MLA_QK)
        return rms_norm(q, q_norm)

    def keys_values(kv_a, k_rope):
        b_, l = kv_a.shape[:2]
        kv = (rms_norm(kv_a, kv_a_norm) @ kv_b).reshape(b_, l, MLA_HEADS, MLA_NOPE + MLA_V)
        k_nope, v = jnp.split(kv, [MLA_NOPE], axis=-1)
        k_r = jnp.broadcast_to(k_rope[:, :, None, :], (b_, l, MLA_HEADS, MLA_ROPE))
        return rms_norm(jnp.concatenate([k_nope, k_r], axis=-1), k_norm), v

    def rope_tail(t):
        return jnp.concatenate([t[..., :MLA_NOPE], apply_rope(t[..., MLA_NOPE:], cos, sin)], axis=-1)

    def merge(hy, att):
        b_, l = hy.shape[:2]
        return jnp.concatenate([hy, att.reshape(b_, l, MLA_HEADS * MLA_V)], axis=-1) @ w_out

    if need_ctx:
        zc = hc @ w_in
        kc, vc = keys_values(zc[..., col_kv:col_kv + MLA_KV_RANK], zc[..., col_kv + MLA_KV_RANK:])
        out_c = merge(hyena(zc[..., :hy_in]), attend_blocked(queries(zc[..., col_q:col_kv]), kc, vc))
    else:
        zc = hc @ w_in[:, col_kv:]
        kc, vc = keys_values(zc[..., :MLA_KV_RANK], zc[..., MLA_KV_RANK:])
        out_c = None
    zl = hl @ w_in
    ql = rope_tail(queries(zl[..., col_q:col_kv]))
    kl, vl = keys_values(zl[..., col_kv:col_kv + MLA_KV_RANK], zl[..., col_kv + MLA_KV_RANK:])
    kl = rope_tail(kl)
    att = attend_blocked(ql, jnp.concatenate([kc, kl], axis=1), jnp.concatenate([vc, vl], axis=1))
    out_l = merge(hyena(zl[..., :hy_in]), att)
    return out_l, out_c


def setup_inputs(seed: int = 0) -> dict:
    key = jax.random.key(seed)
    ks = iter(jax.random.split(key, 64))

    def nrm(shape, scale):
        return scale * jax.random.normal(next(ks), shape, F32)

    def gain(shape):
        return 1.0 + nrm(shape, 0.02)

    d, ne, no = D_MODEL, N_EVEN, N_ODD
    a0 = jax.random.uniform(next(ks), (ne, 2, LRU_WIDTH), F32, 0.9, 0.999) ** (1.0 / LRU_C)
    inp = {}
    inp['x'] = nrm((BATCH, SEQ, d), 1.0)
    inp['c'] = nrm((BATCH, d), 1.0)
    inp['ctx'] = nrm((BATCH, CTX_LEN, d), 1.0)
    inp['c_ctx'] = nrm((d,), 1.0)
    inp['mod_w'] = nrm((DEPTH, d, N_MOD * d), 0.5 * d ** -0.5)
    inp['mod_b'] = nrm((DEPTH, N_MOD * d), 0.02)
    inp['norm1_g'] = gain((DEPTH, d))
    inp['norm2_g'] = gain((DEPTH, d))
    inp['ab_w_in'] = nrm((ne, d, AB_IN), d ** -0.5)
    inp['ab_w_out'] = nrm((ne, MIX_WIDTH, d), MIX_WIDTH ** -0.5)
    inp['lru_conv_w'] = nrm((ne, LRU_CONV, LRU_WIDTH), LRU_CONV ** -0.5)
    inp['lru_conv_b'] = nrm((ne, LRU_WIDTH), 0.02)
    inp['lru_w_a'] = nrm((ne, 2, LRU_BLOCKS, LRU_BLOCK_SIZE, LRU_BLOCK_SIZE), LRU_BLOCK_SIZE ** -0.5)
    inp['lru_b_a'] = nrm((ne, 2, LRU_BLOCKS, LRU_BLOCK_SIZE), 0.02)
    inp['lru_w_x'] = nrm((ne, 2, LRU_BLOCKS, LRU_BLOCK_SIZE, LRU_BLOCK_SIZE), LRU_BLOCK_SIZE ** -0.5)
    inp['lru_b_x'] = nrm((ne, 2, LRU_BLOCKS, LRU_BLOCK_SIZE), 0.02)
    inp['lru_lambda'] = jnp.log(a0) - jnp.log1p(-a0)
    inp['gqa_q_norm'] = gain((ne, HEAD_DIM))
    inp['gqa_k_norm'] = gain((ne, HEAD_DIM))
    inp['ffn_w1'] = nrm((ne, d, D_FF), d ** -0.5)
    inp['ffn_w3'] = nrm((ne, d, D_FF), d ** -0.5)
    inp['ffn_w2'] = nrm((ne, D_FF, d), D_FF ** -0.5)
    inp['cd_w_in'] = nrm((no, d, CD_IN), d ** -0.5)
    inp['cd_w_out'] = nrm((no, MIX_WIDTH, d), MIX_WIDTH ** -0.5)
    inp['hy_conv_w'] = nrm((no, HY_CONV, (HY_ORDER + 1) * HY_WIDTH), HY_CONV ** -0.5)
    inp['hy_conv_b'] = nrm((no, (HY_ORDER + 1) * HY_WIDTH), 0.02)
    inp['hy_filt_w1'] = nrm((no, HY_POS_DIM, HY_FILTER_HIDDEN), HY_POS_DIM ** -0.5)
    inp['hy_filt_b1'] = nrm((no, HY_FILTER_HIDDEN), 0.02)
    inp['hy_filt_w2'] = nrm((no, HY_FILTER_HIDDEN, HY_FILTER_HIDDEN), HY_FILTER_HIDDEN ** -0.5)
    inp['hy_filt_b2'] = nrm((no, HY_FILTER_HIDDEN), 0.02)
    inp['hy_filt_w3'] = nrm((no, HY_FILTER_HIDDEN, HY_ORDER * 2 * HY_WIDTH), 0.05 * HY_FILTER_HIDDEN ** -0.5)
    inp['hy_sin_freq'] = gain((no, HY_FILTER_HIDDEN))
    inp['hy_skip'] = nrm((no, HY_ORDER, HY_WIDTH), 0.5)
    inp['mla_q_a_norm'] = gain((no, MLA_Q_RANK))
    inp['mla_q_b'] = nrm((no, MLA_Q_RANK, MLA_HEADS * MLA_QK), MLA_Q_RANK ** -0.5)
    inp['mla_kv_a_norm'] = gain((no, MLA_KV_RANK))
    inp['mla_kv_b'] = nrm((no, MLA_KV_RANK, MLA_HEADS * (MLA_NOPE + MLA_V)), MLA_KV_RANK ** -0.5)
    inp['mla_q_norm'] = gain((no, MLA_QK))
    inp['mla_k_norm'] = gain((no, MLA_QK))
    inp['moe_router'] = nrm((no, d, N_EXPERTS), d ** -0.5)
    inp['moe_w1'] = nrm((no, N_EXPERTS, d, D_FF_EXPERT), d ** -0.5)
    inp['moe_w3'] = nrm((no, N_EXPERTS, d, D_FF_EXPERT), d ** -0.5)
    inp['moe_w2'] = nrm((no, N_EXPERTS, D_FF_EXPERT, d), D_FF_EXPERT ** -0.5)
    return inp


def reference(x, c, ctx, c_ctx, mod_w, mod_b, norm1_g, norm2_g,
              ab_w_in, ab_w_out, lru_conv_w, lru_conv_b, lru_w_a, lru_b_a, lru_w_x, lru_b_x, lru_lambda,
              gqa_q_norm, gqa_k_norm, ffn_w1, ffn_w3, ffn_w2,
              cd_w_in, cd_w_out, hy_conv_w, hy_conv_b, hy_filt_w1, hy_filt_b1, hy_filt_w2, hy_filt_b2,
              hy_filt_w3, hy_sin_freq, hy_skip, mla_q_a_norm, mla_q_b, mla_kv_a_norm, mla_kv_b,
              mla_q_norm, mla_k_norm, moe_router, moe_w1, moe_w3, moe_w2):
    rows = x.shape[1] // GRID_W
    cos_g, sin_g = grid_angles(rows, HEAD_DIM)
    cos_m, sin_m = grid_angles(rows, MLA_ROPE)
    silu_l = jax.nn.silu(c)
    silu_c = jax.nn.silu(c_ctx)[None, :]
    xl, xc = x, ctx
    for layer in range(DEPTH):
        j = layer // 2
        even = layer % 2 == 0
        need_ctx = layer < DEPTH - 1
        mod_l = [m[:, None, :] for m in jnp.split(silu_l @ mod_w[layer] + mod_b[layer], N_MOD, axis=-1)]
        mod_c = [m[:, None, :] for m in jnp.split(silu_c @ mod_w[layer] + mod_b[layer], N_MOD, axis=-1)]
        hl = modulate(rms_norm(xl, norm1_g[layer]), mod_l[0], mod_l[1])
        hc = modulate(rms_norm(xc, norm1_g[layer]), mod_c[0], mod_c[1])
        if even:
            mix_l, mix_c = ab_mixer(hc, hl, ab_w_in[j], ab_w_out[j], lru_conv_w[j], lru_conv_b[j],
                                    lru_w_a[j], lru_b_a[j], lru_w_x[j], lru_b_x[j], lru_lambda[j],
                                    gqa_q_norm[j], gqa_k_norm[j], cos_g, sin_g, need_ctx)
        else:
            mix_l, mix_c = cd_mixer(hc, hl, cd_w_in[j], cd_w_out[j], hy_conv_w[j], hy_conv_b[j],
                                    hy_filt_w1[j], hy_filt_b1[j], hy_filt_w2[j], hy_filt_b2[j],
                                    hy_filt_w3[j], hy_sin_freq[j], hy_skip[j],
                                    mla_q_a_norm[j], mla_q_b[j], mla_kv_a_norm[j], mla_kv_b[j],
                                    mla_q_norm[j], mla_k_norm[j], cos_m, sin_m, need_ctx)

        def channel_mix(h):
            if even:
                return swiglu(h, ffn_w1[j], ffn_w3[j], ffn_w2[j])
            return moe_swiglu(h, moe_router[j], moe_w1[j], moe_w3[j], moe_w2[j])

        xl = xl + mod_l[2] * mix_l
        xl = xl + mod_l[5] * channel_mix(modulate(rms_norm(xl, norm2_g[layer]), mod_l[3], mod_l[4]))
        if need_ctx:
            xc = xc + mod_c[2] * mix_c
            xc = xc + mod_c[5] * channel_mix(modulate(rms_norm(xc, norm2_g[layer]), mod_c[3], mod_c[4]))
    return xl
```

```python
import functools
import math

import jax
import jax.numpy as jnp
from jax import lax
from jax.experimental import pallas as pl
from jax.experimental.pallas import tpu as pltpu

F32 = jnp.float32
BF16 = jnp.bfloat16

D_MODEL = 1024
DEPTH = 2
GRID_W = 64
HEAD_DIM = 64
HALF_MIX = D_MODEL // 2
ROPE_THETA = 10000.0
EPS = 1e-6
N_MOD = 6
LRU_WIDTH = HALF_MIX
LRU_BLOCKS = LRU_WIDTH // HEAD_DIM
LRU_BLOCK_SIZE = LRU_WIDTH // LRU_BLOCKS
LRU_CONV = 4
LRU_C = 8.0
GQA_Q_HEADS = HALF_MIX // HEAD_DIM
GQA_KV_HEADS = 2
HY_WIDTH = HALF_MIX
HY_ORDER = 2
HY_CONV = 3
HY_BANDS = 16
HY_TARGET = 1e-2
HY_FAST_PCT = 0.3
HY_SLOW_PCT = 1.5
HY_MIN_DECAY = math.log(HY_TARGET) / HY_SLOW_PCT
HY_MAX_DECAY = math.log(HY_TARGET) / HY_FAST_PCT
HY_SHIFT = 0.05
MLA_HEADS = HALF_MIX // HEAD_DIM
MLA_Q_RANK = D_MODEL // 4
MLA_KV_RANK = D_MODEL // 8
MLA_NOPE = HEAD_DIM
MLA_ROPE = HEAD_DIM // 2
MLA_V = HEAD_DIM
MLA_QK = MLA_NOPE + MLA_ROPE
N_EXPERTS = 8
TOP_K = 2

V7X_VMEM_LIMIT_BYTES = 56 * 1024 * 1024

MOE_TILE_M = 512
MOE_TILE_F = 512
FFN_TILE_M = 1024
FFN_TILE_F = 256


def _params(sem):
    return pltpu.CompilerParams(dimension_semantics=sem, vmem_limit_bytes=V7X_VMEM_LIMIT_BYTES)


def _mm_kernel(x_ref, w_ref, o_ref):
    o_ref[...] = jnp.dot(x_ref[...].astype(BF16), w_ref[...].astype(BF16),
                         preferred_element_type=F32)


def matmul(x, w, tm=512, tn=None):
    m, k = x.shape
    n = w.shape[1]
    tm = min(tm, m)
    tn = n if tn is None else tn
    assert m % tm == 0 and n % tn == 0
    return pl.pallas_call(
        _mm_kernel,
        grid=(m // tm, n // tn),
        in_specs=[pl.BlockSpec((tm, k), lambda i, j: (i, 0)),
                  pl.BlockSpec((k, tn), lambda i, j: (0, j))],
        out_specs=pl.BlockSpec((tm, tn), lambda i, j: (i, j)),
        out_shape=jax.ShapeDtypeStruct((m, n), F32),
        compiler_params=_params(("parallel", "parallel")),
        name="matmul",
    )(x, w)


def linear(x, w, tm=512):
    lead = x.shape[:-1]
    return matmul(x.reshape(-1, x.shape[-1]), w, tm=tm).reshape(*lead, w.shape[1])


def _attn_kernel(q_ref, k_ref, v_ref, o_ref, *, scale):
    g, tq, dq = q_ref.shape
    q = (q_ref[...] * scale).astype(BF16).reshape(g * tq, dq)
    s = lax.dot_general(q, k_ref[...], (((1,), (1,)), ((), ())), preferred_element_type=F32)
    m = jnp.max(s, axis=-1, keepdims=True)
    p = jnp.exp(s - m)
    l = jnp.sum(p, axis=-1, keepdims=True)
    o = jnp.dot(p.astype(BF16), v_ref[...], preferred_element_type=F32)
    o_ref[...] = (o / l).reshape(g, tq, o_ref.shape[-1])


def attention(q, k, v, tq):
    b, sq, hk, g, dq = q.shape
    sk, dv = k.shape[1], v.shape[-1]
    tq = min(tq, sq)
    qt = jnp.transpose(q, (0, 2, 3, 1, 4))
    kt = jnp.transpose(k, (0, 2, 1, 3)).astype(BF16)
    vt = jnp.transpose(v, (0, 2, 1, 3)).astype(BF16)
    out = pl.pallas_call(
        functools.partial(_attn_kernel, scale=dq ** -0.5),
        grid=(b, hk, sq // tq),
        in_specs=[pl.BlockSpec((None, None, g, tq, dq), lambda i, h, t: (i, h, 0, t, 0)),
                  pl.BlockSpec((None, None, sk, dq), lambda i, h, t: (i, h, 0, 0)),
                  pl.BlockSpec((None, None, sk, dv), lambda i, h, t: (i, h, 0, 0))],
        out_specs=pl.BlockSpec((None, None, g, tq, dv), lambda i, h, t: (i, h, 0, t, 0)),
        out_shape=jax.ShapeDtypeStruct((b, hk, g, sq, dv), F32),
        compiler_params=_params(("parallel", "parallel", "parallel")),
        name="attention",
    )(qt, kt, vt)
    return jnp.transpose(out, (0, 3, 1, 2, 4))


def _swiglu_kernel(te_ref, nv_ref, x_ref, w1_ref, w3_ref, w2_ref, o_ref, acc_ref):
    i, j = pl.program_id(0), pl.program_id(1)
    last = pl.num_programs(1) - 1
    valid = i < nv_ref[0]

    @pl.when(valid)
    def _():
        x = x_ref[...]
        h1 = jnp.dot(x, w1_ref[...], preferred_element_type=F32)
        h3 = jnp.dot(x, w3_ref[...], preferred_element_type=F32)
        act = (h1 * jax.nn.sigmoid(h1) * h3).astype(BF16)
        part = jnp.dot(act, w2_ref[...], preferred_element_type=F32)

        @pl.when(j == 0)
        def _():
            acc_ref[...] = part

        @pl.when(j > 0)
        def _():
            acc_ref[...] += part

        @pl.when(j == last)
        def _():
            o_ref[...] = acc_ref[...]

    @pl.when(jnp.logical_and(jnp.logical_not(valid), j == last))
    def _():
        o_ref[...] = jnp.zeros_like(o_ref)


def grouped_swiglu(xs, tile_expert, n_valid, w1, w3, w2, tm, tf):
    n_rows, d = xs.shape
    f = w1.shape[-1]
    assert n_rows % tm == 0 and f % tf == 0
    n_f = f // tf

    def col(i, j, nv):
        return jnp.where(i < nv[0], j, n_f - 1)

    grid_spec = pltpu.PrefetchScalarGridSpec(
        num_scalar_prefetch=2,
        grid=(n_rows // tm, n_f),
        in_specs=[pl.BlockSpec((tm, d), lambda i, j, te, nv: (jnp.minimum(i, nv[0] - 1), 0)),
                  pl.BlockSpec((None, d, tf), lambda i, j, te, nv: (te[i], 0, col(i, j, nv))),
                  pl.BlockSpec((None, d, tf), lambda i, j, te, nv: (te[i], 0, col(i, j, nv))),
                  pl.BlockSpec((None, tf, d), lambda i, j, te, nv: (te[i], col(i, j, nv), 0))],
        out_specs=pl.BlockSpec((tm, d), lambda i, j, te, nv: (i, 0)),
        scratch_shapes=[pltpu.VMEM((tm, d), F32)],
    )
    return pl.pallas_call(
        _swiglu_kernel,
        grid_spec=grid_spec,
        out_shape=jax.ShapeDtypeStruct((n_rows, d), F32),
        compiler_params=_params(("parallel", "arbitrary")),
        name="grouped_swiglu",
    )(tile_expert, n_valid, xs, w1, w3, w2)


def dense_swiglu(h, w1, w3, w2):
    lead = h.shape[:-1]
    t = h.reshape(-1, h.shape[-1]).astype(BF16)
    n_tiles = t.shape[0] // FFN_TILE_M
    out = grouped_swiglu(t, jnp.zeros((n_tiles,), jnp.int32), jnp.full((1,), n_tiles, jnp.int32),
                         w1[None], w3[None], w2[None], FFN_TILE_M, FFN_TILE_F)
    return out.reshape(*lead, -1)


def moe_swiglu(h, router, w1, w3, w2):
    b_, l, d = h.shape
    t = h.reshape(-1, d)
    n_tok = t.shape[0]
    n_assign = n_tok * TOP_K
    tm = MOE_TILE_M
    logits = jnp.dot(t, router, precision=lax.Precision.HIGHEST)
    top_logit, top_e = lax.top_k(logits, TOP_K)
    gates = jax.nn.softmax(top_logit, axis=-1)
    flat_e = top_e.reshape(-1)
    onehot = (flat_e[:, None] == jnp.arange(N_EXPERTS, dtype=flat_e.dtype)[None, :]).astype(jnp.int32)
    csum = jnp.cumsum(onehot, axis=0)
    rank = jnp.sum(csum * onehot, axis=-1) - 1
    counts = csum[-1]
    padded = (counts + tm - 1) // tm * tm
    pend = jnp.cumsum(padded)
    pstart = pend - padded
    dest = pstart[flat_e] + rank
    n_tiles = n_assign // tm + N_EXPERTS
    n_slots = n_tiles * tm
    flat_tok = jnp.arange(n_assign, dtype=jnp.int32) // TOP_K
    slot_tok = jnp.zeros((n_slots,), jnp.int32).at[dest].set(flat_tok)
    tile_expert = jnp.minimum(
        jnp.searchsorted(pend, jnp.arange(n_tiles, dtype=jnp.int32) * tm, side='right'),
        N_EXPERTS - 1).astype(jnp.int32)
    n_valid = (pend[-1] // tm).astype(jnp.int32).reshape(1)
    xs = t.astype(BF16)[slot_tok]
    ys = grouped_swiglu(xs, tile_expert, n_valid, w1, w3, w2, tm, MOE_TILE_F)
    picked = ys[dest].reshape(n_tok, TOP_K, d)
    out = jnp.sum(picked * gates[:, :, None], axis=1)
    return out.reshape(b_, l, d)


def rms_norm(x, g):
    y = x * lax.rsqrt(jnp.mean(x * x, axis=-1, keepdims=True) + EPS)
    return y * g


def modulate(h, shift, scale):
    return h * (1 + scale) + shift


def grid_angles(rows, rot_dim):
    n_freq = rot_dim // 4
    inv_freq = ROPE_THETA ** (-jnp.arange(n_freq, dtype=F32) / n_freq)
    t = jnp.arange(rows * GRID_W)
    row = (t // GRID_W).astype(F32)
    col = (t % GRID_W).astype(F32)
    ang = jnp.concatenate([row[:, None] * inv_freq, col[:, None] * inv_freq], axis=-1)
    return jnp.cos(ang), jnp.sin(ang)


def apply_rope(x, cos, sin):
    half = x.shape[-1] // 2
    shape = (1, cos.shape[0]) + (1,) * (x.ndim - 3) + (half,)
    cs = cos.reshape(shape)
    sn = sin.reshape(shape)
    x1, x2 = x[..., :half], x[..., half:]
    return jnp.concatenate([x1 * cs - x2 * sn, x1 * sn + x2 * cs], axis=-1)


def depthwise_conv(x, w, b, pad):
    k = w.shape[0]
    l = x.shape[1]
    xp = jnp.pad(x, ((0, 0), pad, (0, 0)))
    y = sum(xp[:, j:j + l] * w[j] for j in range(k))
    return y + b


def block_diag_linear(x, w, b):
    b_, l = x.shape[:2]
    xb = x.reshape(b_, l, LRU_BLOCKS, LRU_BLOCK_SIZE)
    y = jnp.einsum('blni,nij->blnj', xb, w) + b
    return y.reshape(b_, l, LRU_WIDTH)


def _linear_combine(left, right):
    return left[0] * right[0], right[0] * left[1] + right[1]


def rglru_scan(u, w_a, b_a, w_x, b_x, lam, h0):
    r = jax.nn.sigmoid(block_diag_linear(u, w_a, b_a))
    i = jax.nn.sigmoid(block_diag_linear(u, w_x, b_x))
    log_a = -LRU_C * r * jax.nn.softplus(-lam)
    a = jnp.exp(log_a)
    bterm = jnp.sqrt(-jnp.expm1(2.0 * log_a)) * (i * u)
    a_cum, b_cum = lax.associative_scan(_linear_combine, (a, bterm), axis=1)
    return a_cum * h0[:, None, :] + b_cum


def bidir_rglru(u, w_a, b_a, w_x, b_x, lam, h0_fwd, h0_bwd):
    h_f = rglru_scan(u, w_a[0], b_a[0], w_x[0], b_x[0], lam[0], h0_fwd)
    h_b = rglru_scan(u[:, ::-1], w_a[1], b_a[1], w_x[1], b_x[1], lam[1], h0_bwd)
    return h_f + h_b[:, ::-1], h_f[:, -1], h_b[:, -1]


def hyena_filters(length, w1, b1, w2, b2, w3, freq):
    t = jnp.arange(length, dtype=F32)[:, None]
    t_norm = t / max(length - 1, 1)
    bands = jnp.linspace(1e-4, HY_BANDS - 1, HY_BANDS, dtype=F32)
    ang = 2.0 * math.pi * t * bands / length
    z = jnp.concatenate([t_norm, jnp.cos(ang), -jnp.sin(ang)], axis=-1)
    h = jnp.sin(freq * (z @ w1 + b1))
    h = jnp.sin(freq * (h @ w2 + b2))
    h = h @ w3
    deltas = jnp.abs(jnp.linspace(HY_MIN_DECAY, HY_MAX_DECAY, HY_WIDTH, dtype=F32))
    window = jnp.exp(-t_norm * deltas) + HY_SHIFT
    return h.reshape(length, HY_ORDER, 2, HY_WIDTH) * window[:, None, None, :]


def bidir_long_conv(u, h_fwd, h_bwd, skip):
    l, ch = h_fwd.shape
    two_sided = jnp.concatenate([h_fwd, jnp.zeros((1, ch), F32), h_bwd[:0:-1]], axis=0)
    spec = jnp.fft.rfft(u, n=2 * l, axis=1) * jnp.fft.rfft(two_sided, axis=0)[None]
    y = jnp.fft.irfft(spec, n=2 * l, axis=1)[:, :l]
    return y + u * skip


def ab_mixer(hc, hl, w_in, w_out, conv_w, conv_b, w_a, b_a, w_x, b_x, lam, q_norm, k_norm,
             cos, sin, need_ctx):
    g = GQA_Q_HEADS // GQA_KV_HEADS
    cuts = [LRU_WIDTH, 2 * LRU_WIDTH, 2 * LRU_WIDTH + GQA_Q_HEADS * HEAD_DIM,
            2 * LRU_WIDTH + (GQA_Q_HEADS + GQA_KV_HEADS) * HEAD_DIM]

    def project(h):
        b_, l = h.shape[:2]
        xr, gate, q, k, v = jnp.split(linear(h, w_in), cuts, axis=-1)
        u = depthwise_conv(xr, conv_w, conv_b, (LRU_CONV // 2, LRU_CONV - 1 - LRU_CONV // 2))
        q = rms_norm(q.reshape(b_, l, GQA_KV_HEADS, g, HEAD_DIM), q_norm)
        k = rms_norm(k.reshape(b_, l, GQA_KV_HEADS, HEAD_DIM), k_norm)
        return u, gate, q, k, v.reshape(b_, l, GQA_KV_HEADS, HEAD_DIM)

    def merge(rec, gate, att):
        b_, l = rec.shape[:2]
        y = jnp.concatenate([rec * jax.nn.gelu(gate),
                             att.reshape(b_, l, GQA_Q_HEADS * HEAD_DIM)], axis=-1)
        return linear(y, w_out)

    uc, gc, qc, kc, vc = project(hc)
    h0 = jnp.zeros((hc.shape[0], LRU_WIDTH), F32)
    rc, s_fwd, s_bwd = bidir_rglru(uc, w_a, b_a, w_x, b_x, lam, h0, h0)
    ul, gl, ql, kl, vl = project(hl)
    rl, _, _ = bidir_rglru(ul, w_a, b_a, w_x, b_x, lam, s_fwd, s_bwd)
    ql = apply_rope(ql, cos, sin)
    kl = apply_rope(kl, cos, sin)
    al = attention(ql, jnp.concatenate([kc, kl], axis=1), jnp.concatenate([vc, vl], axis=1), tq=128)
    out_l = merge(rl, gl, al)
    out_c = merge(rc, gc, attention(qc, kc, vc, tq=128)) if need_ctx else None
    return out_l, out_c


def cd_mixer(hc, hl, w_in, w_out, conv_w, conv_b, fw1, fb1, fw2, fb2, fw3, freq, skip,
             q_a_norm, q_b, kv_a_norm, kv_b, q_norm, k_norm, cos, sin, need_ctx):
    hy_in = (HY_ORDER + 1) * HY_WIDTH
    col_q = hy_in
    col_kv = hy_in + MLA_Q_RANK

    def hyena(z):
        l = z.shape[1]
        z = depthwise_conv(z, conv_w, conv_b, (HY_CONV // 2, HY_CONV // 2))
        parts = jnp.split(z, HY_ORDER + 1, axis=-1)
        filt = hyena_filters(l, fw1, fb1, fw2, fb2, fw3, freq)
        y = parts[0]
        for o in range(HY_ORDER):
            y = parts[o + 1] * bidir_long_conv(y, filt[:, o, 0], filt[:, o, 1], skip[o])
        return y

    def queries(q_a):
        b_, l = q_a.shape[:2]
        q = linear(rms_norm(q_a, q_a_norm), q_b).reshape(b_, l, MLA_HEADS, 1, MLA_QK)
        return rms_norm(q, q_norm)

    def keys_values(kv_a, k_rope):
        b_, l = kv_a.shape[:2]
        kv = linear(rms_norm(kv_a, kv_a_norm), kv_b).reshape(b_, l, MLA_HEADS, MLA_NOPE + MLA_V)
        k_nope, v = jnp.split(kv, [MLA_NOPE], axis=-1)
        k_r = jnp.broadcast_to(k_rope[:, :, None, :], (b_, l, MLA_HEADS, MLA_ROPE))
        return rms_norm(jnp.concatenate([k_nope, k_r], axis=-1), k_norm), v

    def rope_tail(t):
        return jnp.concatenate([t[..., :MLA_NOPE], apply_rope(t[..., MLA_NOPE:], cos, sin)], axis=-1)

    def merge(hy, att):
        b_, l = hy.shape[:2]
        return linear(jnp.concatenate([hy, att.reshape(b_, l, MLA_HEADS * MLA_V)], axis=-1), w_out)

    if need_ctx:
        zc = linear(hc, w_in)
        kc, vc = keys_values(zc[..., col_kv:col_kv + MLA_KV_RANK], zc[..., col_kv + MLA_KV_RANK:])
        out_c = merge(hyena(zc[..., :hy_in]), attention(queries(zc[..., col_q:col_kv]), kc, vc, tq=256))
    else:
        zc = linear(hc, w_in[:, col_kv:])
        kc, vc = keys_values(zc[..., :MLA_KV_RANK], zc[..., MLA_KV_RANK:])
        out_c = None
    zl = linear(hl, w_in)
    ql = rope_tail(queries(zl[..., col_q:col_kv]))
    kl, vl = keys_values(zl[..., col_kv:col_kv + MLA_KV_RANK], zl[..., col_kv + MLA_KV_RANK:])
    kl = rope_tail(kl)
    att = attention(ql, jnp.concatenate([kc, kl], axis=1), jnp.concatenate([vc, vl], axis=1), tq=512)
    out_l = merge(hyena(zl[..., :hy_in]), att)
    return out_l, out_c


def kernel(x, c, ctx, c_ctx, mod_w, mod_b, norm1_g, norm2_g, ab_w_in, ab_w_out, lru_conv_w, lru_conv_b, lru_w_a, lru_b_a, lru_w_x, lru_b_x, lru_lambda, gqa_q_norm, gqa_k_norm, ffn_w1, ffn_w3, ffn_w2, cd_w_in, cd_w_out, hy_conv_w, hy_conv_b, hy_filt_w1, hy_filt_b1, hy_filt_w2, hy_filt_b2, hy_filt_w3, hy_sin_freq, hy_skip, mla_q_a_norm, mla_q_b, mla_kv_a_norm, mla_kv_b, mla_q_norm, mla_k_norm, moe_router, moe_w1, moe_w3, moe_w2):
    rows = x.shape[1] // GRID_W
    cos_g, sin_g = grid_angles(rows, HEAD_DIM)
    cos_m, sin_m = grid_angles(rows, MLA_ROPE)
    batch = c.shape[0]
    silu_all = jnp.concatenate([jax.nn.silu(c), jax.nn.silu(c_ctx)[None, :],
                                jnp.zeros((16 - batch - 1, D_MODEL), F32)], axis=0)
    bf = lambda w: w.astype(BF16)
    xl, xc = x, ctx
    for layer in range(DEPTH):
        j = layer // 2
        even = layer % 2 == 0
        need_ctx = layer < DEPTH - 1
        mod_all = matmul(silu_all, mod_w[layer], tn=1536) + mod_b[layer]
        mod_l = [m[:, None, :] for m in jnp.split(mod_all[:batch], N_MOD, axis=-1)]
        mod_c = [m[:, None, :] for m in jnp.split(mod_all[batch:batch + 1], N_MOD, axis=-1)]
        hl = modulate(rms_norm(xl, norm1_g[layer]), mod_l[0], mod_l[1])
        hc = modulate(rms_norm(xc, norm1_g[layer]), mod_c[0], mod_c[1])
        if even:
            mix_l, mix_c = ab_mixer(hc, hl, bf(ab_w_in[j]), bf(ab_w_out[j]), lru_conv_w[j], lru_conv_b[j],
                                    lru_w_a[j], lru_b_a[j], lru_w_x[j], lru_b_x[j], lru_lambda[j],
                                    gqa_q_norm[j], gqa_k_norm[j], cos_g, sin_g, need_ctx)
        else:
            mix_l, mix_c = cd_mixer(hc, hl, bf(cd_w_in[j]), bf(cd_w_out[j]), hy_conv_w[j], hy_conv_b[j],
                                    hy_filt_w1[j], hy_filt_b1[j], hy_filt_w2[j], hy_filt_b2[j],
                                    hy_filt_w3[j], hy_sin_freq[j], hy_skip[j],
                                    mla_q_a_norm[j], bf(mla_q_b[j]), mla_kv_a_norm[j], bf(mla_kv_b[j]),
                                    mla_q_norm[j], mla_k_norm[j], cos_m, sin_m, need_ctx)

        def channel_mix(h):
            if even:
                return dense_swiglu(h, bf(ffn_w1[j]), bf(ffn_w3[j]), bf(ffn_w2[j]))
            return moe_swiglu(h, moe_router[j], bf(moe_w1[j]), bf(moe_w3[j]), bf(moe_w2[j]))

        xl = xl + mod_l[2] * mix_l
        xl = xl + mod_l[5] * channel_mix(modulate(rms_norm(xl, norm2_g[layer]), mod_l[3], mod_l[4]))
        if need_ctx:
            xc = xc + mod_c[2] * mix_c
            xc = xc + mod_c[5] * channel_mix(modulate(rms_norm(xc, norm2_g[layer]), mod_c[3], mod_c[4]))
    return xl
```

```python
import functools
import math

import jax
import jax.numpy as jnp
import numpy as np
from jax import lax
from jax.experimental import pallas as pl
from jax.experimental.pallas import tpu as pltpu

F32 = jnp.float32
BF16 = jnp.bfloat16

D_MODEL = 1024
SEQ = 2048
CTX_LEN = 256
S_ALL = SEQ + CTX_LEN
DEPTH = 2
GRID_W = 64
HEAD_DIM = 64
HALF_MIX = D_MODEL // 2
ROPE_THETA = 10000.0
EPS = 1e-6
N_MOD = 6
LRU_WIDTH = HALF_MIX
LRU_BLOCKS = LRU_WIDTH // HEAD_DIM
LRU_CONV = 4
LRU_C = 8.0
GQA_Q_HEADS = HALF_MIX // HEAD_DIM
GQA_KV_HEADS = 2
HY_WIDTH = HALF_MIX
HY_ORDER = 2
HY_CONV = 3
HY_BANDS = 16
HY_TARGET = 1e-2
HY_FAST_PCT = 0.3
HY_SLOW_PCT = 1.5
HY_MIN_DECAY = math.log(HY_TARGET) / HY_SLOW_PCT
HY_MAX_DECAY = math.log(HY_TARGET) / HY_FAST_PCT
HY_SHIFT = 0.05
HY_IN = (HY_ORDER + 1) * HY_WIDTH
MLA_HEADS = HALF_MIX // HEAD_DIM
MLA_Q_RANK = D_MODEL // 4
MLA_KV_RANK = D_MODEL // 8
MLA_NOPE = HEAD_DIM
MLA_ROPE = HEAD_DIM // 2
MLA_V = HEAD_DIM
MLA_QK = MLA_NOPE + MLA_ROPE
N_EXPERTS = 8
TOP_K = 2

LANES = 128
SUBLANES = 8
V7X_VMEM_LIMIT_BYTES = 56 * 1024 * 1024

TOKEN_TILE = 768
FFN_TILE_M = 1152
FFN_TILE_F = 256
ATTN_TILE_Q = 512
MERGE1_TILE = 512
MOE_TILE_M = 512
MOE_TILE_F = 512

LRU_GAP = SUBLANES
LRU_LAT0 = CTX_LEN + LRU_GAP
LRU_CHUNK = 256
LRU_ROWS = 2560
LRU_PAD_FRONT = SUBLANES

assert S_ALL % TOKEN_TILE == 0 and S_ALL % FFN_TILE_M == 0 and SEQ % ATTN_TILE_Q == 0
assert LRU_ROWS % LRU_CHUNK == 0 and LRU_ROWS >= LRU_LAT0 + SEQ + SUBLANES


def _params(sem):
    return pltpu.CompilerParams(dimension_semantics=sem, vmem_limit_bytes=V7X_VMEM_LIMIT_BYTES)


def _norm_mod(x, g, mod_pair, row0, shift_idx, scale_idx):
    y = x * lax.rsqrt(jnp.mean(x * x, axis=-1, keepdims=True) + EPS) * g
    shift = _mod_rows(mod_pair, shift_idx, row0, x.shape[0])
    scale = _mod_rows(mod_pair, scale_idx, row0, x.shape[0])
    return y * (1.0 + scale) + shift


def _mod_rows(mod_pair, idx, row0, rows):
    row = row0 + lax.broadcasted_iota(jnp.int32, (rows, 1), 0)
    return jnp.where(row >= SEQ, mod_pair[idx, 1], mod_pair[idx, 0])


def _mm_kernel(x_ref, w_ref, o_ref):
    o_ref[...] = jnp.dot(x_ref[...].astype(BF16), w_ref[...].astype(BF16),
                         preferred_element_type=F32)


def matmul(x, w, tm, tn):
    m, k = x.shape
    n = w.shape[1]
    assert m % tm == 0 and n % tn == 0
    return pl.pallas_call(
        _mm_kernel,
        grid=(m // tm, n // tn),
        in_specs=[pl.BlockSpec((tm, k), lambda i, j: (i, 0)),
                  pl.BlockSpec((k, tn), lambda i, j: (0, j))],
        out_specs=pl.BlockSpec((tm, tn), lambda i, j: (i, j)),
        out_shape=jax.ShapeDtypeStruct((m, n), F32),
        compiler_params=_params(("parallel", "parallel")),
        name="matmul",
    )(x, w)


def _rope_pair(a, cos, sin, first_half, shift):
    rot = jnp.where(first_half, pltpu.roll(a, LANES - shift, 1), pltpu.roll(a, shift, 1))
    return a * cos + rot * sin


def _proj0_kernel(x_ref, g_ref, mod_ref, w_ref, cos_ref, sin_ref, qg_ref, kg_ref,
                  xr_ref, gate_ref, q_ref, k_ref, v_ref):
    tm = x_ref.shape[0]
    row0 = pl.program_id(1) * tm
    h = _norm_mod(x_ref[...], g_ref[...], mod_ref, row0, 0, 1)
    z = jnp.dot(h.astype(BF16), w_ref[...], preferred_element_type=F32)
    xr_ref[...] = z[:, :LRU_WIDTH]
    gate_ref[...] = z[:, LRU_WIDTH:2 * LRU_WIDTH]
    cos, sin = cos_ref[...], sin_ref[...]
    lane = lax.broadcasted_iota(jnp.int32, (tm, LANES), 1)
    lo = lane < HEAD_DIM
    first_half = (lane & (HEAD_DIM // 2)) == 0

    def head_pair(a, gain):
        sq = a * a
        ss_lo = jnp.sum(jnp.where(lo, sq, 0.0), axis=-1, keepdims=True)
        ss_hi = jnp.sum(jnp.where(lo, 0.0, sq), axis=-1, keepdims=True)
        inv = jnp.where(lo, lax.rsqrt(ss_lo / HEAD_DIM + EPS), lax.rsqrt(ss_hi / HEAD_DIM + EPS))
        return _rope_pair(a * inv * gain, cos, sin, first_half, HEAD_DIM // 2)

    q0 = 2 * LRU_WIDTH
    for p in range(GQA_Q_HEADS // 2):
        q_ref[:, p * LANES:(p + 1) * LANES] = head_pair(
            z[:, q0 + p * LANES:q0 + (p + 1) * LANES], qg_ref[...]).astype(BF16)
    k0 = q0 + GQA_Q_HEADS * HEAD_DIM
    k_ref[...] = head_pair(z[:, k0:k0 + LANES], kg_ref[...]).astype(BF16)
    v_ref[...] = z[:, k0 + LANES:k0 + 2 * LANES].astype(BF16)


def proj0(xa, g, modp, w_in, cos, sin, q_gain, k_gain):
    b = xa.shape[0]
    tm = TOKEN_TILE
    n = w_in.shape[1]
    tok = lambda width: pl.BlockSpec((None, tm, width), lambda i, t: (i, t, 0))
    full = lambda r, c_: pl.BlockSpec((r, c_), lambda i, t: (0, 0))
    return pl.pallas_call(
        _proj0_kernel,
        grid=(b, S_ALL // tm),
        in_specs=[tok(D_MODEL), full(1, D_MODEL),
                  pl.BlockSpec((None, N_MOD, 2, 1, D_MODEL), lambda i, t: (i, 0, 0, 0, 0)),
                  full(D_MODEL, n),
                  pl.BlockSpec((tm, LANES), lambda i, t: (t, 0)),
                  pl.BlockSpec((tm, LANES), lambda i, t: (t, 0)),
                  full(1, LANES), full(1, LANES)],
        out_specs=[tok(LRU_WIDTH), tok(LRU_WIDTH), tok(GQA_Q_HEADS * HEAD_DIM), tok(LANES), tok(LANES)],
        out_shape=[jax.ShapeDtypeStruct((b, S_ALL, LRU_WIDTH), F32),
                   jax.ShapeDtypeStruct((b, S_ALL, LRU_WIDTH), F32),
                   jax.ShapeDtypeStruct((b, S_ALL, GQA_Q_HEADS * HEAD_DIM), BF16),
                   jax.ShapeDtypeStruct((b, S_ALL, LANES), BF16),
                   jax.ShapeDtypeStruct((b, S_ALL, LANES), BF16)],
        compiler_params=_params(("parallel", "parallel")),
        name="proj0",
    )(xa, g, modp, w_in, cos, sin, q_gain, k_gain)


def _lru_kernel(xr_ref, cw_ref, cb_ref, wg_ref, bg_ref, lam_ref, rec_ref,
                pad_ref, af_ref, hf_ref, ab_ref, hb_ref):
    w = LRU_WIDTH
    pad_ref[...] = jnp.zeros_like(pad_ref)
    pad_ref[LRU_PAD_FRONT:LRU_PAD_FRONT + CTX_LEN, :] = xr_ref[SEQ:S_ALL, :]
    pad_ref[LRU_PAD_FRONT + LRU_LAT0:LRU_PAD_FRONT + LRU_LAT0 + SEQ, :] = xr_ref[0:SEQ, :]
    lam = lam_ref[...]
    neg_c_softplus = -LRU_C * (jnp.maximum(-lam, 0.0) + jnp.log(1.0 + jnp.exp(-jnp.abs(lam))))
    left = LRU_CONV // 2
    for ch in range(LRU_ROWS // LRU_CHUNK):
        r0 = ch * LRU_CHUNK
        u = cb_ref[...]
        for j in range(LRU_CONV):
            start = LRU_PAD_FRONT + r0 - left + j
            u = u + cw_ref[j:j + 1, :] * pad_ref[start:start + LRU_CHUNK, :]
        gz = jnp.dot(u.astype(BF16), wg_ref[...], preferred_element_type=F32) + bg_ref[...]
        for d, (a_ref, h_ref) in enumerate(((af_ref, hf_ref), (ab_ref, hb_ref))):
            r = jax.nn.sigmoid(gz[:, 2 * d * w:(2 * d + 1) * w])
            i = jax.nn.sigmoid(gz[:, (2 * d + 1) * w:(2 * d + 2) * w])
            a = jnp.exp(neg_c_softplus[d:d + 1, :] * r)
            a_ref[r0:r0 + LRU_CHUNK, :] = a
            h_ref[r0:r0 + LRU_CHUNK, :] = jnp.sqrt(1.0 - a * a) * (i * u)

    n_ctx_groups = CTX_LEN // SUBLANES

    def group(tt, carry):
        hf, hb = carry
        is_lat = tt >= n_ctx_groups
        base_f = pl.multiple_of(tt * SUBLANES + jnp.where(is_lat, LRU_GAP, 0), SUBLANES)
        base_b = pl.multiple_of(
            jnp.where(is_lat, LRU_LAT0 + SEQ + CTX_LEN, CTX_LEN) - (tt + 1) * SUBLANES, SUBLANES)
        for i in range(SUBLANES):
            rf = pl.ds(base_f + i, 1)
            hf = af_ref[rf, :] * hf + hf_ref[rf, :]
            hf_ref[rf, :] = hf
            rb = pl.ds(base_b + (SUBLANES - 1 - i), 1)
            hb = ab_ref[rb, :] * hb + hb_ref[rb, :]
            hb_ref[rb, :] = hb
        return hf, hb

    zero = jnp.zeros((1, w), F32)
    lax.fori_loop(0, S_ALL // SUBLANES, group, (zero, zero))
    rec_ref[0:SEQ, :] = hf_ref[LRU_LAT0:LRU_LAT0 + SEQ, :] + hb_ref[LRU_LAT0:LRU_LAT0 + SEQ, :]
    rec_ref[SEQ:S_ALL, :] = hf_ref[0:CTX_LEN, :] + hb_ref[0:CTX_LEN, :]


def rglru(xr, conv_w, conv_b, w_gates, b_gates, lam):
    b = xr.shape[0]
    w = LRU_WIDTH
    full = lambda r, c_: pl.BlockSpec((r, c_), lambda i: (0, 0))
    rows = pltpu.VMEM((LRU_ROWS, w), F32)
    return pl.pallas_call(
        _lru_kernel,
        grid=(b,),
        in_specs=[pl.BlockSpec((None, S_ALL, w), lambda i: (i, 0, 0)),
                  full(LRU_CONV, w), full(1, w), full(w, 4 * w), full(1, 4 * w), full(2, w)],
        out_specs=pl.BlockSpec((None, S_ALL, w), lambda i: (i, 0, 0)),
        out_shape=jax.ShapeDtypeStruct((b, S_ALL, w), F32),
        scratch_shapes=[pltpu.VMEM((LRU_ROWS + 2 * LRU_PAD_FRONT, w), F32), rows, rows, rows, rows],
        compiler_params=_params(("parallel",)),
        name="rglru",
    )(xr, conv_w, conv_b, w_gates, b_gates, lam)


def _attn_kernel(q_ref, k_ref, v_ref, *rest, heads):
    o_ref = rest[-1]
    tq, sk = q_ref.shape[0], k_ref.shape[0]
    q_lo = lax.broadcasted_iota(jnp.int32, (tq, LANES), 1) < HEAD_DIM
    v_lo = lax.broadcasted_iota(jnp.int32, (sk, LANES), 1) < HEAD_DIM

    def keep(x, is_lo, half):
        if half is None:
            return x
        return jnp.where(is_lo if half == 0 else jnp.logical_not(is_lo), x, jnp.zeros_like(x))

    for og, members in enumerate(heads):
        acc = None
        for qg, qh, kg, vg, vh in members:
            q = keep(q_ref[:, qg * LANES:(qg + 1) * LANES], q_lo, qh)
            s = lax.dot_general(q, k_ref[:, kg * LANES:(kg + 1) * LANES],
                                (((1,), (1,)), ((), ())), preferred_element_type=F32)
            p = jnp.exp(s - jnp.max(s, axis=-1, keepdims=True))
            l = jnp.sum(p, axis=-1, keepdims=True)
            v = keep(v_ref[:, vg * LANES:(vg + 1) * LANES], v_lo, vh)
            o = jnp.dot(p.astype(BF16), v, preferred_element_type=F32) / l
            acc = o if acc is None else acc + o
        o_ref[:, og * LANES:(og + 1) * LANES] = acc.astype(o_ref.dtype)


def attention(q, k, v, heads, q_block, n_q_blocks, tq, k_rows, out_init=None):
    b, s, qw = q.shape
    kw, vw = k.shape[-1], v.shape[-1]
    ow = len(heads) * LANES
    kv_spec = lambda width: pl.BlockSpec((None, k_rows, width), lambda i, t: (i, s // k_rows - 1, 0))
    in_specs = [pl.BlockSpec((None, tq, qw), lambda i, t: (i, q_block + t, 0)), kv_spec(kw), kv_spec(vw)]
    args = [q, k, v]
    aliases = {}
    if out_init is not None:
        in_specs.append(pl.BlockSpec(memory_space=pl.ANY))
        args.append(out_init)
        aliases = {3: 0}
    return pl.pallas_call(
        functools.partial(_attn_kernel, heads=heads),
        grid=(b, n_q_blocks),
        in_specs=in_specs,
        out_specs=pl.BlockSpec((None, tq, ow), lambda i, t: (i, q_block + t, 0)),
        out_shape=jax.ShapeDtypeStruct((b, s, ow), BF16),
        input_output_aliases=aliases,
        compiler_params=_params(("parallel", "parallel")),
        name="attention",
    )(*args)


GQA_HEADS = tuple(((p, 0, 0, 0, 0), (p, 1, 0, 0, 1)) for p in range(GQA_Q_HEADS // 2))
MLA_HEAD_GROUPS = tuple(((2 * p, None, 2 * p, p, 0), (2 * p + 1, None, 2 * p + 1, p, 1))
                        for p in range(MLA_HEADS // 2))


def attention_both(q, k, v, heads, with_ctx):
    out = attention(q, k, v, heads, 0, SEQ // ATTN_TILE_Q, ATTN_TILE_Q, S_ALL)
    if with_ctx:
        out = attention(q, k, v, heads, SEQ // CTX_LEN, 1, CTX_LEN, CTX_LEN, out_init=out)
    return out


def _merge0_kernel(rec_ref, gate_ref, att_ref, x_ref, mod_ref, g2_ref,
                   w_rec_ref, w_att_ref, x1_ref, h2_ref):
    tm = x_ref.shape[0]
    row0 = pl.program_id(1) * tm
    rg = (rec_ref[...] * jax.nn.gelu(gate_ref[...])).astype(BF16)
    y = (jnp.dot(rg, w_rec_ref[...], preferred_element_type=F32)
         + jnp.dot(att_ref[...], w_att_ref[...], preferred_element_type=F32))
    x1 = x_ref[...] + _mod_rows(mod_ref, 2, row0, tm) * y
    x1_ref[...] = x1
    h2_ref[...] = _norm_mod(x1, g2_ref[...], mod_ref, row0, 3, 4).astype(BF16)


def merge0(rec, gate, att, xa, modp, g2, w_rec, w_att):
    b = xa.shape[0]
    tm = TOKEN_TILE
    tok = lambda width: pl.BlockSpec((None, tm, width), lambda i, t: (i, t, 0))
    full = lambda r, c_: pl.BlockSpec((r, c_), lambda i, t: (0, 0))
    return pl.pallas_call(
        _merge0_kernel,
        grid=(b, S_ALL // tm),
        in_specs=[tok(LRU_WIDTH), tok(LRU_WIDTH), tok(HALF_MIX), tok(D_MODEL),
                  pl.BlockSpec((None, N_MOD, 2, 1, D_MODEL), lambda i, t: (i, 0, 0, 0, 0)),
                  full(1, D_MODEL), full(LRU_WIDTH, D_MODEL), full(HALF_MIX, D_MODEL)],
        out_specs=[tok(D_MODEL), tok(D_MODEL)],
        out_shape=[jax.ShapeDtypeStruct((b, S_ALL, D_MODEL), F32),
                   jax.ShapeDtypeStruct((b, S_ALL, D_MODEL), BF16)],
        compiler_params=_params(("parallel", "parallel")),
        name="merge0",
    )(rec, gate, att, xa, modp, g2, w_rec, w_att)


def _swiglu_partial(x, w1_ref, w3_ref, w2_ref):
    h1 = jnp.dot(x, w1_ref[...], preferred_element_type=F32)
    h3 = jnp.dot(x, w3_ref[...], preferred_element_type=F32)
    act = (h1 * jax.nn.sigmoid(h1) * h3).astype(BF16)
    return jnp.dot(act, w2_ref[...], preferred_element_type=F32)


def _accumulate(acc_ref, part, j):
    @pl.when(j == 0)
    def _():
        acc_ref[...] = part

    @pl.when(j > 0)
    def _():
        acc_ref[...] += part


def _ffn_kernel(h_ref, x1_ref, mod_ref, w1_ref, w3_ref, w2_ref, o_ref, acc_ref):
    tm = h_ref.shape[0]
    j = pl.program_id(2)
    _accumulate(acc_ref, _swiglu_partial(h_ref[...], w1_ref, w3_ref, w2_ref), j)

    @pl.when(j == pl.num_programs(2) - 1)
    def _():
        o_ref[...] = x1_ref[...] + _mod_rows(mod_ref, 5, pl.program_id(1) * tm, tm) * acc_ref[...]


def ffn_residual(h2, x1, modp, w1, w3, w2):
    b = x1.shape[0]
    tm, tf = FFN_TILE_M, FFN_TILE_F
    f = w1.shape[1]
    assert f % tf == 0
    tok = lambda: pl.BlockSpec((None, tm, D_MODEL), lambda i, t, j: (i, t, 0))
    return pl.pallas_call(
        _ffn_kernel,
        grid=(b, S_ALL // tm, f // tf),
        in_specs=[tok(), tok(),
                  pl.BlockSpec((None, N_MOD, 2, 1, D_MODEL), lambda i, t, j: (i, 0, 0, 0, 0)),
                  pl.BlockSpec((D_MODEL, tf), lambda i, t, j: (0, j)),
                  pl.BlockSpec((D_MODEL, tf), lambda i, t, j: (0, j)),
                  pl.BlockSpec((tf, D_MODEL), lambda i, t, j: (j, 0))],
        out_specs=tok(),
        out_shape=jax.ShapeDtypeStruct((b, S_ALL, D_MODEL), F32),
        scratch_shapes=[pltpu.VMEM((tm, D_MODEL), F32)],
        compiler_params=_params(("parallel", "parallel", "arbitrary")),
        name="ffn",
    )(h2, x1, modp, w1, w3, w2)


def _moe_kernel(te_ref, nv_ref, x_ref, w1_ref, w3_ref, w2_ref, o_ref, acc_ref):
    i, j = pl.program_id(0), pl.program_id(1)
    last = pl.num_programs(1) - 1
    valid = i < nv_ref[0]

    @pl.when(valid)
    def _():
        _accumulate(acc_ref, _swiglu_partial(x_ref[...], w1_ref, w3_ref, w2_ref), j)

        @pl.when(j == last)
        def _():
            o_ref[...] = acc_ref[...]

    @pl.when(jnp.logical_and(jnp.logical_not(valid), j == last))
    def _():
        o_ref[...] = jnp.zeros_like(o_ref)


def grouped_swiglu(xs, tile_expert, n_valid, w1, w3, w2, tm, tf):
    n_rows, d = xs.shape
    f = w1.shape[-1]
    assert n_rows % tm == 0 and f % tf == 0
    n_f = f // tf

    def col(i, j, nv):
        return jnp.where(i < nv[0], j, n_f - 1)

    grid_spec = pltpu.PrefetchScalarGridSpec(
        num_scalar_prefetch=2,
        grid=(n_rows // tm, n_f),
        in_specs=[pl.BlockSpec((tm, d), lambda i, j, te, nv: (jnp.minimum(i, nv[0] - 1), 0)),
                  pl.BlockSpec((None, d, tf), lambda i, j, te, nv: (te[i], 0, col(i, j, nv))),
                  pl.BlockSpec((None, d, tf), lambda i, j, te, nv: (te[i], 0, col(i, j, nv))),
                  pl.BlockSpec((None, tf, d), lambda i, j, te, nv: (te[i], col(i, j, nv), 0))],
        out_specs=pl.BlockSpec((tm, d), lambda i, j, te, nv: (i, 0)),
        scratch_shapes=[pltpu.VMEM((tm, d), F32)],
    )
    return pl.pallas_call(
        _moe_kernel,
        grid_spec=grid_spec,
        out_shape=jax.ShapeDtypeStruct((n_rows, d), F32),
        compiler_params=_params(("parallel", "arbitrary")),
        name="moe",
    )(tile_expert, n_valid, xs, w1, w3, w2)


def moe_swiglu(t_bf16, logits, w1, w3, w2):
    n_tok, d = t_bf16.shape
    n_assign = n_tok * TOP_K
    tm = MOE_TILE_M
    top_logit, top_e = lax.top_k(logits, TOP_K)
    gates = jax.nn.softmax(top_logit, axis=-1)
    flat_e = top_e.reshape(-1)
    onehot = (flat_e[:, None] == jnp.arange(N_EXPERTS, dtype=flat_e.dtype)[None, :]).astype(jnp.int32)
    csum = jnp.cumsum(onehot, axis=0)
    rank = jnp.sum(csum * onehot, axis=-1) - 1
    counts = csum[-1]
    padded = (counts + tm - 1) // tm * tm
    pend = jnp.cumsum(padded)
    pstart = pend - padded
    dest = pstart[flat_e] + rank
    n_tiles = n_assign // tm + N_EXPERTS
    n_slots = n_tiles * tm
    flat_tok = jnp.arange(n_assign, dtype=jnp.int32) // TOP_K
    slot_tok = jnp.zeros((n_slots,), jnp.int32).at[dest].set(flat_tok)
    tile_expert = jnp.minimum(
        jnp.searchsorted(pend, jnp.arange(n_tiles, dtype=jnp.int32) * tm, side='right'),
        N_EXPERTS - 1).astype(jnp.int32)
    n_valid = (pend[-1] // tm).astype(jnp.int32).reshape(1)
    xs = t_bf16[slot_tok]
    ys = grouped_swiglu(xs, tile_expert, n_valid, w1, w3, w2, tm, MOE_TILE_F)
    picked = ys[dest].reshape(n_tok, TOP_K, d)
    return jnp.sum(picked * gates[:, :, None], axis=1)


def _proj1_kernel(x_ref, g_ref, mod_ref, w_ref, qan_ref, qb_ref, kvan_ref, kbk_ref, kbv_ref,
                  cos_ref, sin_ref, qg_ref, kg_ref, hy_ref, q_ref, k_ref, v_ref):
    tm = x_ref.shape[0]
    row0 = pl.program_id(1) * tm
    h = _norm_mod(x_ref[...], g_ref[...], mod_ref, row0, 0, 1)
    z = jnp.dot(h.astype(BF16), w_ref[...], preferred_element_type=F32)
    hy_ref[...] = z[:, :HY_IN]

    def rms(a, g):
        return a * lax.rsqrt(jnp.mean(a * a, axis=-1, keepdims=True) + EPS) * g

    c_q, c_kv = HY_IN, HY_IN + MLA_Q_RANK
    c_r = c_kv + MLA_KV_RANK
    qa = rms(z[:, c_q:c_kv], qan_ref[...]).astype(BF16)
    kva = rms(z[:, c_kv:c_r], kvan_ref[...]).astype(BF16)
    q = jnp.dot(qa, qb_ref[...], preferred_element_type=F32)
    kk = jnp.dot(kva, kbk_ref[...], preferred_element_type=F32)
    v_ref[...] = jnp.dot(kva, kbv_ref[...], preferred_element_type=F32).astype(BF16)
    k_rope = pltpu.roll(z[:, c_r:c_r + LANES], MLA_NOPE, 1)
    cos, sin = cos_ref[...], sin_ref[...]
    lane = lax.broadcasted_iota(jnp.int32, (tm, LANES), 1)
    first_half = lane < MLA_NOPE + MLA_ROPE // 2

    def head(a, gain):
        inv = lax.rsqrt(jnp.sum(a * a, axis=-1, keepdims=True) / MLA_QK + EPS)
        return _rope_pair(a * inv * gain, cos, sin, first_half, MLA_ROPE // 2)

    for hh in range(MLA_HEADS):
        sl = slice(hh * LANES, (hh + 1) * LANES)
        q_ref[:, sl] = head(q[:, sl], qg_ref[...]).astype(BF16)
        k_ref[:, sl] = head(kk[:, sl] + k_rope, kg_ref[...]).astype(BF16)


def proj1(xa, g, modp, w_in, qan, qb, kvan, kbk, kbv, cos, sin, q_gain, k_gain):
    b = xa.shape[0]
    tm = TOKEN_TILE
    tok = lambda width: pl.BlockSpec((None, tm, width), lambda i, t: (i, t, 0))
    full = lambda a: pl.BlockSpec(a.shape, lambda i, t: (0, 0))
    hw = MLA_HEADS * LANES
    return pl.pallas_call(
        _proj1_kernel,
        grid=(b, S_ALL // tm),
        in_specs=[tok(D_MODEL), full(g),
                  pl.BlockSpec((None, N_MOD, 2, 1, D_MODEL), lambda i, t: (i, 0, 0, 0, 0)),
                  full(w_in), full(qan), full(qb), full(kvan), full(kbk), full(kbv),
                  pl.BlockSpec((tm, LANES), lambda i, t: (t, 0)),
                  pl.BlockSpec((tm, LANES), lambda i, t: (t, 0)),
                  full(q_gain), full(k_gain)],
        out_specs=[tok(HY_IN), tok(hw), tok(hw), tok(MLA_HEADS * MLA_V)],
        out_shape=[jax.ShapeDtypeStruct((b, S_ALL, HY_IN), F32),
                   jax.ShapeDtypeStruct((b, S_ALL, hw), BF16),
                   jax.ShapeDtypeStruct((b, S_ALL, hw), BF16),
                   jax.ShapeDtypeStruct((b, S_ALL, MLA_HEADS * MLA_V), BF16)],
        compiler_params=_params(("parallel", "parallel")),
        name="proj1",
    )(xa, g, modp, w_in, qan, qb, kvan, kbk, kbv, cos, sin, q_gain, k_gain)


def _merge1_kernel(hy_ref, att_ref, x_ref, mod_ref, g2_ref, w_hy_ref, w_att_ref, r_ref,
                   x1_ref, h2_ref, lg_ref):
    y = (jnp.dot(hy_ref[...].astype(BF16), w_hy_ref[...], preferred_element_type=F32)
         + jnp.dot(att_ref[...], w_att_ref[...], preferred_element_type=F32))
    x1 = x_ref[...] + mod_ref[2, 0] * y
    x1_ref[...] = x1
    h2 = _norm_mod(x1, g2_ref[...], mod_ref, 0, 3, 4)
    h2_ref[...] = h2.astype(BF16)
    lg_ref[...] = jnp.dot(h2, r_ref[...], preferred_element_type=F32, precision=lax.Precision.HIGHEST)


def merge1(hy, att, xa, modp, g2, w_hy, w_att, router):
    b = xa.shape[0]
    tm = MERGE1_TILE
    tok = lambda width: pl.BlockSpec((None, tm, width), lambda i, t: (i, t, 0))
    full = lambda a: pl.BlockSpec(a.shape, lambda i, t: (0, 0))
    return pl.pallas_call(
        _merge1_kernel,
        grid=(b, SEQ // tm),
        in_specs=[tok(HY_WIDTH), tok(HALF_MIX), tok(D_MODEL),
                  pl.BlockSpec((None, N_MOD, 2, 1, D_MODEL), lambda i, t: (i, 0, 0, 0, 0)),
                  full(g2), full(w_hy), full(w_att), full(router)],
        out_specs=[tok(D_MODEL), tok(D_MODEL), tok(N_EXPERTS)],
        out_shape=[jax.ShapeDtypeStruct((b, SEQ, D_MODEL), F32),
                   jax.ShapeDtypeStruct((b, SEQ, D_MODEL), BF16),
                   jax.ShapeDtypeStruct((b, SEQ, N_EXPERTS), F32)],
        compiler_params=_params(("parallel", "parallel")),
        name="merge1",
    )(hy, att, xa, modp, g2, w_hy, w_att, router)


def depthwise_conv(x, w, b, pad):
    k = w.shape[0]
    l = x.shape[1]
    xp = jnp.pad(x, ((0, 0), pad, (0, 0)))
    y = sum(xp[:, j:j + l] * w[j] for j in range(k))
    return y + b


def hyena_filters(length, w1, b1, w2, b2, w3, freq):
    hp = lax.Precision.HIGHEST
    t = jnp.arange(length, dtype=F32)[:, None]
    t_norm = t / max(length - 1, 1)
    bands = jnp.linspace(1e-4, HY_BANDS - 1, HY_BANDS, dtype=F32)
    ang = 2.0 * math.pi * t * bands / length
    z = jnp.concatenate([t_norm, jnp.cos(ang), -jnp.sin(ang)], axis=-1)
    h = jnp.sin(freq * (jnp.dot(z, w1, precision=hp) + b1))
    h = jnp.sin(freq * (jnp.dot(h, w2, precision=hp) + b2))
    h = jnp.dot(h, w3, precision=hp)
    deltas = jnp.abs(jnp.linspace(HY_MIN_DECAY, HY_MAX_DECAY, HY_WIDTH, dtype=F32))
    window = jnp.exp(-t_norm * deltas) + HY_SHIFT
    return h.reshape(length, HY_ORDER, 2, HY_WIDTH) * window[:, None, None, :]


def bidir_long_conv(u, h_fwd, h_bwd, skip):
    l, ch = h_fwd.shape
    two_sided = jnp.concatenate([h_fwd, jnp.zeros((1, ch), F32), h_bwd[:0:-1]], axis=0)
    spec = jnp.fft.rfft(u, n=2 * l, axis=1) * jnp.fft.rfft(two_sided, axis=0)[None]
    y = jnp.fft.irfft(spec, n=2 * l, axis=1)[:, :l]
    return y + u * skip


def hyena(z, conv_w, conv_b, fw1, fb1, fw2, fb2, fw3, freq, skip):
    l = z.shape[1]
    z = depthwise_conv(z, conv_w, conv_b, (HY_CONV // 2, HY_CONV // 2))
    parts = jnp.split(z, HY_ORDER + 1, axis=-1)
    filt = hyena_filters(l, fw1, fb1, fw2, fb2, fw3, freq)
    y = parts[0]
    for o in range(HY_ORDER):
        y = parts[o + 1] * bidir_long_conv(y, filt[:, o, 0], filt[:, o, 1], skip[o])
    return y


def _grid_angles(rot_dim):
    n_freq = rot_dim // 4
    inv_freq = ROPE_THETA ** (-jnp.arange(n_freq, dtype=F32) / n_freq)
    t = jnp.arange(SEQ)
    r = (t // GRID_W).astype(F32)
    c_ = (t % GRID_W).astype(F32)
    return jnp.concatenate([r[:, None] * inv_freq, c_[:, None] * inv_freq], axis=-1)


def _rope_tables(rot_dim, lane_offsets):
    ang = _grid_angles(rot_dim)
    c, s = jnp.cos(ang), jnp.sin(ang)
    half = rot_dim // 2
    cos = jnp.ones((S_ALL, LANES), F32)
    sin = jnp.zeros((S_ALL, LANES), F32)
    for off in lane_offsets:
        cos = cos.at[:SEQ, off:off + half].set(c).at[:SEQ, off + half:off + rot_dim].set(c)
        sin = sin.at[:SEQ, off:off + half].set(-s).at[:SEQ, off + half:off + rot_dim].set(s)
    return cos, sin


def _block_diag(w):
    nb, bs, _ = w.shape
    eye = jnp.eye(nb, dtype=w.dtype)
    return (eye[:, None, :, None] * w[:, :, None, :]).reshape(nb * bs, nb * bs)


def _gqa_pair_order():
    g = GQA_Q_HEADS // GQA_KV_HEADS
    heads = []
    for p in range(g):
        heads += [p, g + p]
    return np.concatenate([np.arange(h * HEAD_DIM, (h + 1) * HEAD_DIM) for h in heads])


def _pad_heads(w, n_heads, width):
    k = w.shape[0]
    w = w.reshape(k, n_heads, width)
    return jnp.pad(w, ((0, 0), (0, 0), (0, LANES - width))).reshape(k, n_heads * LANES)


def kernel(x, c, ctx, c_ctx, mod_w, mod_b, norm1_g, norm2_g, ab_w_in, ab_w_out, lru_conv_w, lru_conv_b, lru_w_a, lru_b_a, lru_w_x, lru_b_x, lru_lambda, gqa_q_norm, gqa_k_norm, ffn_w1, ffn_w3, ffn_w2, cd_w_in, cd_w_out, hy_conv_w, hy_conv_b, hy_filt_w1, hy_filt_b1, hy_filt_w2, hy_filt_b2, hy_filt_w3, hy_sin_freq, hy_skip, mla_q_a_norm, mla_q_b, mla_kv_a_norm, mla_kv_b, mla_q_norm, mla_k_norm, moe_router, moe_w1, moe_w3, moe_w2):
    batch = x.shape[0]
    bf = lambda w: w.astype(BF16)
    row = lambda v: v.reshape(1, -1)

    silu_all = jnp.concatenate([jax.nn.silu(c), jax.nn.silu(c_ctx)[None, :],
                                jnp.zeros((16 - batch - 1, D_MODEL), F32)], axis=0)
    mods = []
    for layer in range(DEPTH):
        m = matmul(silu_all, mod_w[layer], 16, 1536) + mod_b[layer]
        lat = m[:batch].reshape(batch, N_MOD, 1, D_MODEL)
        cx = jnp.broadcast_to(m[batch].reshape(1, N_MOD, 1, D_MODEL), lat.shape)
        mods.append(jnp.stack([lat, cx], axis=2))

    xa = jnp.concatenate([x, ctx], axis=1)

    perm = _gqa_pair_order()
    q0 = 2 * LRU_WIDTH
    w_in0 = ab_w_in[0]
    w_in0 = jnp.concatenate([w_in0[:, :q0], w_in0[:, q0 + perm], w_in0[:, q0 + GQA_Q_HEADS * HEAD_DIM:]], axis=1)
    cos_g, sin_g = _rope_tables(HEAD_DIM, (0, HEAD_DIM))
    q_gain = row(jnp.tile(gqa_q_norm[0], 2) * HEAD_DIM ** -0.5)
    k_gain = row(jnp.tile(gqa_k_norm[0], 2))
    xr, gate, q, k, v = proj0(xa, row(norm1_g[0]), mods[0], bf(w_in0), cos_g, sin_g, q_gain, k_gain)

    w_gates = jnp.concatenate([_block_diag(lru_w_a[0, 0]), _block_diag(lru_w_x[0, 0]),
                               _block_diag(lru_w_a[0, 1]), _block_diag(lru_w_x[0, 1])], axis=1)
    b_gates = jnp.concatenate([lru_b_a[0, 0].reshape(-1), lru_b_x[0, 0].reshape(-1),
                               lru_b_a[0, 1].reshape(-1), lru_b_x[0, 1].reshape(-1)])
    rec = rglru(xr, lru_conv_w[0], row(lru_conv_b[0]), bf(w_gates), row(b_gates), lru_lambda[0])

    att = attention_both(q, k, v, GQA_HEADS, True)
    w_out0 = ab_w_out[0]
    x1, h2 = merge0(rec, gate, att, xa, mods[0], row(norm2_g[0]),
                    bf(w_out0[:LRU_WIDTH]), bf(w_out0[LRU_WIDTH:][perm]))
    xa = ffn_residual(h2, x1, mods[0], bf(ffn_w1[0]), bf(ffn_w3[0]), bf(ffn_w2[0]))

    w_in1 = jnp.pad(cd_w_in[0], ((0, 0), (0, LANES - MLA_ROPE)))
    cos_m, sin_m = _rope_tables(MLA_ROPE, (MLA_NOPE,))
    qb = _pad_heads(mla_q_b[0], MLA_HEADS, MLA_QK)
    kvb = mla_kv_b[0].reshape(MLA_KV_RANK, MLA_HEADS, MLA_NOPE + MLA_V)
    kbk = _pad_heads(kvb[:, :, :MLA_NOPE].reshape(MLA_KV_RANK, -1), MLA_HEADS, MLA_NOPE)
    kbv = kvb[:, :, MLA_NOPE:].reshape(MLA_KV_RANK, -1)
    pad_gain = lambda g_: row(jnp.pad(g_, (0, LANES - MLA_QK)))
    hyz, q, k, v = proj1(xa, row(norm1_g[1]), mods[1], bf(w_in1), row(mla_q_a_norm[0]), bf(qb),
                         row(mla_kv_a_norm[0]), bf(kbk), bf(kbv), cos_m, sin_m,
                         pad_gain(mla_q_norm[0] * MLA_QK ** -0.5), pad_gain(mla_k_norm[0]))
    att = attention_both(q, k, v, MLA_HEAD_GROUPS, False)
    hy = hyena(hyz[:, :SEQ], hy_conv_w[0], hy_conv_b[0], hy_filt_w1[0], hy_filt_b1[0], hy_filt_w2[0],
               hy_filt_b2[0], hy_filt_w3[0], hy_sin_freq[0], hy_skip[0])
    w_out1 = cd_w_out[0]
    x1, h2, logits = merge1(hy, att, xa, mods[1], row(norm2_g[1]),
                            bf(w_out1[:HY_WIDTH]), bf(w_out1[HY_WIDTH:]), moe_router[0])
    moe = moe_swiglu(h2.reshape(-1, D_MODEL), logits.reshape(-1, N_EXPERTS),
                     bf(moe_w1[0]), bf(moe_w3[0]), bf(moe_w2[0]))
    return x1 + mods[1][:, 5, 0] * moe.reshape(batch, SEQ, D_MODEL)
```

```python
import functools
import math

import jax
import jax.numpy as jnp
import numpy as np
from jax import lax
from jax.experimental import pallas as pl
from jax.experimental.pallas import tpu as pltpu

F32 = jnp.float32
BF16 = jnp.bfloat16

D_MODEL = 1024
SEQ = 2048
CTX_LEN = 256
S_ALL = SEQ + CTX_LEN
DEPTH = 2
GRID_W = 64
HEAD_DIM = 64
HALF_MIX = D_MODEL // 2
ROPE_THETA = 10000.0
EPS = 1e-6
N_MOD = 6
LRU_WIDTH = HALF_MIX
LRU_BLOCKS = LRU_WIDTH // HEAD_DIM
LRU_CONV = 4
LRU_C = 8.0
GQA_Q_HEADS = HALF_MIX // HEAD_DIM
GQA_KV_HEADS = 2
HY_WIDTH = HALF_MIX
HY_ORDER = 2
HY_CONV = 3
HY_BANDS = 16
HY_TARGET = 1e-2
HY_FAST_PCT = 0.3
HY_SLOW_PCT = 1.5
HY_MIN_DECAY = math.log(HY_TARGET) / HY_SLOW_PCT
HY_MAX_DECAY = math.log(HY_TARGET) / HY_FAST_PCT
HY_SHIFT = 0.05
HY_IN = (HY_ORDER + 1) * HY_WIDTH
MLA_HEADS = HALF_MIX // HEAD_DIM
MLA_Q_RANK = D_MODEL // 4
MLA_KV_RANK = D_MODEL // 8
MLA_NOPE = HEAD_DIM
MLA_ROPE = HEAD_DIM // 2
MLA_V = HEAD_DIM
MLA_QK = MLA_NOPE + MLA_ROPE
N_EXPERTS = 8
TOP_K = 2

LANES = 128
SUBLANES = 8
V7X_VMEM_LIMIT_BYTES = 56 * 1024 * 1024

TOKEN_TILE = 768
FFN_TILE_M = 1152
FFN_TILE_F = 256
ATTN_TILE_Q = 512
MERGE1_TILE = 512
MOE_TILE_M = 512
MOE_TILE_F = 512

LRU_GAP = SUBLANES
LRU_LAT0 = CTX_LEN + LRU_GAP
LRU_CHUNK = 256
LRU_ROWS = 2560
LRU_PAD_FRONT = SUBLANES

assert S_ALL % TOKEN_TILE == 0 and S_ALL % FFN_TILE_M == 0 and SEQ % ATTN_TILE_Q == 0
assert LRU_ROWS % LRU_CHUNK == 0 and LRU_ROWS >= LRU_LAT0 + SEQ + SUBLANES


def _params(sem):
    return pltpu.CompilerParams(dimension_semantics=sem, vmem_limit_bytes=V7X_VMEM_LIMIT_BYTES)


def _norm_mod(x, g, mod_pair, row0, shift_idx, scale_idx):
    y = x * lax.rsqrt(jnp.mean(x * x, axis=-1, keepdims=True) + EPS) * g
    shift = _mod_rows(mod_pair, shift_idx, row0, x.shape[0])
    scale = _mod_rows(mod_pair, scale_idx, row0, x.shape[0])
    return y * (1.0 + scale) + shift


def _mod_rows(mod_pair, idx, row0, rows):
    row = row0 + lax.broadcasted_iota(jnp.int32, (rows, 1), 0)
    return jnp.where(row >= SEQ, mod_pair[idx, 1], mod_pair[idx, 0])


def _mm_kernel(x_ref, w_ref, o_ref):
    o_ref[...] = jnp.dot(x_ref[...].astype(BF16), w_ref[...].astype(BF16),
                         preferred_element_type=F32)


def matmul(x, w, tm, tn):
    m, k = x.shape
    n = w.shape[1]
    assert m % tm == 0 and n % tn == 0
    return pl.pallas_call(
        _mm_kernel,
        grid=(m // tm, n // tn),
        in_specs=[pl.BlockSpec((tm, k), lambda i, j: (i, 0)),
                  pl.BlockSpec((k, tn), lambda i, j: (0, j))],
        out_specs=pl.BlockSpec((tm, tn), lambda i, j: (i, j)),
        out_shape=jax.ShapeDtypeStruct((m, n), F32),
        compiler_params=_params(("parallel", "parallel")),
        name="matmul",
    )(x, w)


def _rope_pair(a, cos, sin, first_half, shift):
    rot = jnp.where(first_half, pltpu.roll(a, LANES - shift, 1), pltpu.roll(a, shift, 1))
    return a * cos + rot * sin


def _proj0_kernel(x_ref, g_ref, mod_ref, w_ref, cos_ref, sin_ref, qg_ref, kg_ref,
                  xr_ref, gate_ref, q_ref, k_ref, v_ref):
    tm = x_ref.shape[0]
    row0 = pl.program_id(1) * tm
    h = _norm_mod(x_ref[...], g_ref[...], mod_ref, row0, 0, 1)
    z = jnp.dot(h.astype(BF16), w_ref[...], preferred_element_type=F32)
    xr_ref[...] = z[:, :LRU_WIDTH]
    gate_ref[...] = z[:, LRU_WIDTH:2 * LRU_WIDTH]
    cos, sin = cos_ref[...], sin_ref[...]
    lane = lax.broadcasted_iota(jnp.int32, (tm, LANES), 1)
    lo = lane < HEAD_DIM
    first_half = (lane & (HEAD_DIM // 2)) == 0

    def head_pair(a, gain):
        sq = a * a
        ss_lo = jnp.sum(jnp.where(lo, sq, 0.0), axis=-1, keepdims=True)
        ss_hi = jnp.sum(jnp.where(lo, 0.0, sq), axis=-1, keepdims=True)
        inv = jnp.where(lo, lax.rsqrt(ss_lo / HEAD_DIM + EPS), lax.rsqrt(ss_hi / HEAD_DIM + EPS))
        return _rope_pair(a * inv * gain, cos, sin, first_half, HEAD_DIM // 2)

    q0 = 2 * LRU_WIDTH
    for p in range(GQA_Q_HEADS // 2):
        q_ref[:, p * LANES:(p + 1) * LANES] = head_pair(
            z[:, q0 + p * LANES:q0 + (p + 1) * LANES], qg_ref[...]).astype(BF16)
    k0 = q0 + GQA_Q_HEADS * HEAD_DIM
    k_ref[...] = head_pair(z[:, k0:k0 + LANES], kg_ref[...]).astype(BF16)
    v_ref[...] = z[:, k0 + LANES:k0 + 2 * LANES].astype(BF16)


def proj0(xa, g, modp, w_in, cos, sin, q_gain, k_gain):
    b = xa.shape[0]
    tm = TOKEN_TILE
    n = w_in.shape[1]
    tok = lambda width: pl.BlockSpec((None, tm, width), lambda i, t: (i, t, 0))
    full = lambda r, c_: pl.BlockSpec((r, c_), lambda i, t: (0, 0))
    return pl.pallas_call(
        _proj0_kernel,
        grid=(b, S_ALL // tm),
        in_specs=[tok(D_MODEL), full(1, D_MODEL),
                  pl.BlockSpec((None, N_MOD, 2, 1, D_MODEL), lambda i, t: (i, 0, 0, 0, 0)),
                  full(D_MODEL, n),
                  pl.BlockSpec((tm, LANES), lambda i, t: (t, 0)),
                  pl.BlockSpec((tm, LANES), lambda i, t: (t, 0)),
                  full(1, LANES), full(1, LANES)],
        out_specs=[tok(LRU_WIDTH), tok(LRU_WIDTH), tok(GQA_Q_HEADS * HEAD_DIM), tok(LANES), tok(LANES)],
        out_shape=[jax.ShapeDtypeStruct((b, S_ALL, LRU_WIDTH), F32),
                   jax.ShapeDtypeStruct((b, S_ALL, LRU_WIDTH), F32),
                   jax.ShapeDtypeStruct((b, S_ALL, GQA_Q_HEADS * HEAD_DIM), BF16),
                   jax.ShapeDtypeStruct((b, S_ALL, LANES), BF16),
                   jax.ShapeDtypeStruct((b, S_ALL, LANES), BF16)],
        compiler_params=_params(("parallel", "parallel")),
        name="proj0",
    )(xa, g, modp, w_in, cos, sin, q_gain, k_gain)


def _lru_kernel(xr_ref, cw_ref, cb_ref, wg_ref, bg_ref, lam_ref, rec_ref,
                pad_ref, af_ref, hf_ref, ab_ref, hb_ref):
    w = LRU_WIDTH
    pad_ref[...] = jnp.zeros_like(pad_ref)
    pad_ref[LRU_PAD_FRONT:LRU_PAD_FRONT + CTX_LEN, :] = xr_ref[SEQ:S_ALL, :]
    pad_ref[LRU_PAD_FRONT + LRU_LAT0:LRU_PAD_FRONT + LRU_LAT0 + SEQ, :] = xr_ref[0:SEQ, :]
    lam = lam_ref[...]
    neg_c_softplus = -LRU_C * (jnp.maximum(-lam, 0.0) + jnp.log(1.0 + jnp.exp(-jnp.abs(lam))))
    left = LRU_CONV // 2
    for ch in range(LRU_ROWS // LRU_CHUNK):
        r0 = ch * LRU_CHUNK
        u = cb_ref[...]
        for j in range(LRU_CONV):
            start = LRU_PAD_FRONT + r0 - left + j
            u = u + cw_ref[j:j + 1, :] * pad_ref[start:start + LRU_CHUNK, :]
        gz = jnp.dot(u.astype(BF16), wg_ref[...], preferred_element_type=F32) + bg_ref[...]
        for d, (a_ref, h_ref) in enumerate(((af_ref, hf_ref), (ab_ref, hb_ref))):
            r = jax.nn.sigmoid(gz[:, 2 * d * w:(2 * d + 1) * w])
            i = jax.nn.sigmoid(gz[:, (2 * d + 1) * w:(2 * d + 2) * w])
            a = jnp.exp(neg_c_softplus[d:d + 1, :] * r)
            a_ref[r0:r0 + LRU_CHUNK, :] = a
            h_ref[r0:r0 + LRU_CHUNK, :] = jnp.sqrt(1.0 - a * a) * (i * u)

    n_ctx_groups = CTX_LEN // SUBLANES

    def group(tt, carry):
        hf, hb = carry
        is_lat = tt >= n_ctx_groups
        base_f = pl.multiple_of(tt * SUBLANES + jnp.where(is_lat, LRU_GAP, 0), SUBLANES)
        base_b = pl.multiple_of(
            jnp.where(is_lat, LRU_LAT0 + SEQ + CTX_LEN, CTX_LEN) - (tt + 1) * SUBLANES, SUBLANES)
        for i in range(SUBLANES):
            rf = pl.ds(base_f + i, 1)
            hf = af_ref[rf, :] * hf + hf_ref[rf, :]
            hf_ref[rf, :] = hf
            rb = pl.ds(base_b + (SUBLANES - 1 - i), 1)
            hb = ab_ref[rb, :] * hb + hb_ref[rb, :]
            hb_ref[rb, :] = hb
        return hf, hb

    zero = jnp.zeros((1, w), F32)
    lax.fori_loop(0, S_ALL // SUBLANES, group, (zero, zero))
    rec_ref[0:SEQ, :] = hf_ref[LRU_LAT0:LRU_LAT0 + SEQ, :] + hb_ref[LRU_LAT0:LRU_LAT0 + SEQ, :]
    rec_ref[SEQ:S_ALL, :] = hf_ref[0:CTX_LEN, :] + hb_ref[0:CTX_LEN, :]


def rglru(xr, conv_w, conv_b, w_gates, b_gates, lam):
    b = xr.shape[0]
    w = LRU_WIDTH
    full = lambda r, c_: pl.BlockSpec((r, c_), lambda i: (0, 0))
    rows = pltpu.VMEM((LRU_ROWS, w), F32)
    return pl.pallas_call(
        _lru_kernel,
        grid=(b,),
        in_specs=[pl.BlockSpec((None, S_ALL, w), lambda i: (i, 0, 0)),
                  full(LRU_CONV, w), full(1, w), full(w, 4 * w), full(1, 4 * w), full(2, w)],
        out_specs=pl.BlockSpec((None, S_ALL, w), lambda i: (i, 0, 0)),
        out_shape=jax.ShapeDtypeStruct((b, S_ALL, w), F32),
        scratch_shapes=[pltpu.VMEM((LRU_ROWS + 2 * LRU_PAD_FRONT, w), F32), rows, rows, rows, rows],
        compiler_params=_params(("parallel",)),
        name="rglru",
    )(xr, conv_w, conv_b, w_gates, b_gates, lam)


def _attn_kernel(q_ref, k_ref, v_ref, *rest, heads):
    o_ref = rest[-1]
    tq, sk = q_ref.shape[0], k_ref.shape[0]
    q_lo = lax.broadcasted_iota(jnp.int32, (tq, LANES), 1) < HEAD_DIM
    v_lo = lax.broadcasted_iota(jnp.int32, (sk, LANES), 1) < HEAD_DIM

    def keep(x, is_lo, half):
        if half is None:
            return x
        return jnp.where(is_lo if half == 0 else jnp.logical_not(is_lo), x, jnp.zeros_like(x))

    for og, members in enumerate(heads):
        acc = None
        for qg, qh, kg, vg, vh in members:
            q = keep(q_ref[:, qg * LANES:(qg + 1) * LANES], q_lo, qh)
            s = lax.dot_general(q, k_ref[:, kg * LANES:(kg + 1) * LANES],
                                (((1,), (1,)), ((), ())), preferred_element_type=F32)
            p = jnp.exp(s - jnp.max(s, axis=-1, keepdims=True))
            l = jnp.sum(p, axis=-1, keepdims=True)
            v = keep(v_ref[:, vg * LANES:(vg + 1) * LANES], v_lo, vh)
            o = jnp.dot(p.astype(BF16), v, preferred_element_type=F32) / l
            acc = o if acc is None else acc + o
        o_ref[:, og * LANES:(og + 1) * LANES] = acc.astype(o_ref.dtype)


def attention(q, k, v, heads, q_block, n_q_blocks, tq, k_rows, out_init=None):
    b, s, qw = q.shape
    kw, vw = k.shape[-1], v.shape[-1]
    ow = len(heads) * LANES
    kv_spec = lambda width: pl.BlockSpec((None, k_rows, width), lambda i, t: (i, s // k_rows - 1, 0))
    in_specs = [pl.BlockSpec((None, tq, qw), lambda i, t: (i, q_block + t, 0)), kv_spec(kw), kv_spec(vw)]
    args = [q, k, v]
    aliases = {}
    if out_init is not None:
        in_specs.append(pl.BlockSpec(memory_space=pl.ANY))
        args.append(out_init)
        aliases = {3: 0}
    return pl.pallas_call(
        functools.partial(_attn_kernel, heads=heads),
        grid=(b, n_q_blocks),
        in_specs=in_specs,
        out_specs=pl.BlockSpec((None, tq, ow), lambda i, t: (i, q_block + t, 0)),
        out_shape=jax.ShapeDtypeStruct((b, s, ow), BF16),
        input_output_aliases=aliases,
        compiler_params=_params(("parallel", "parallel")),
        name="attention",
    )(*args)


GQA_HEADS = tuple(((p, 0, 0, 0, 0), (p, 1, 0, 0, 1)) for p in range(GQA_Q_HEADS // 2))
MLA_HEAD_GROUPS = tuple(((2 * p, None, 2 * p, p, 0), (2 * p + 1, None, 2 * p + 1, p, 1))
                        for p in range(MLA_HEADS // 2))


def attention_both(q, k, v, heads, with_ctx):
    out = attention(q, k, v, heads, 0, SEQ // ATTN_TILE_Q, ATTN_TILE_Q, S_ALL)
    if with_ctx:
        out = attention(q, k, v, heads, SEQ // CTX_LEN, 1, CTX_LEN, CTX_LEN, out_init=out)
    return out


def _merge0_kernel(rec_ref, gate_ref, att_ref, x_ref, mod_ref, g2_ref,
                   w_rec_ref, w_att_ref, x1_ref, h2_ref):
    tm = x_ref.shape[0]
    row0 = pl.program_id(1) * tm
    rg = (rec_ref[...] * jax.nn.gelu(gate_ref[...])).astype(BF16)
    y = (jnp.dot(rg, w_rec_ref[...], preferred_element_type=F32)
         + jnp.dot(att_ref[...], w_att_ref[...], preferred_element_type=F32))
    x1 = x_ref[...] + _mod_rows(mod_ref, 2, row0, tm) * y
    x1_ref[...] = x1
    h2_ref[...] = _norm_mod(x1, g2_ref[...], mod_ref, row0, 3, 4).astype(BF16)


def merge0(rec, gate, att, xa, modp, g2, w_rec, w_att):
    b = xa.shape[0]
    tm = TOKEN_TILE
    tok = lambda width: pl.BlockSpec((None, tm, width), lambda i, t: (i, t, 0))
    full = lambda r, c_: pl.BlockSpec((r, c_), lambda i, t: (0, 0))
    return pl.pallas_call(
        _merge0_kernel,
        grid=(b, S_ALL // tm),
        in_specs=[tok(LRU_WIDTH), tok(LRU_WIDTH), tok(HALF_MIX), tok(D_MODEL),
                  pl.BlockSpec((None, N_MOD, 2, 1, D_MODEL), lambda i, t: (i, 0, 0, 0, 0)),
                  full(1, D_MODEL), full(LRU_WIDTH, D_MODEL), full(HALF_MIX, D_MODEL)],
        out_specs=[tok(D_MODEL), tok(D_MODEL)],
        out_shape=[jax.ShapeDtypeStruct((b, S_ALL, D_MODEL), F32),
                   jax.ShapeDtypeStruct((b, S_ALL, D_MODEL), BF16)],
        compiler_params=_params(("parallel", "parallel")),
        name="merge0",
    )(rec, gate, att, xa, modp, g2, w_rec, w_att)


def _swiglu_partial(x, w1_ref, w3_ref, w2_ref):
    h1 = jnp.dot(x, w1_ref[...], preferred_element_type=F32)
    h3 = jnp.dot(x, w3_ref[...], preferred_element_type=F32)
    act = (h1 * jax.nn.sigmoid(h1) * h3).astype(BF16)
    return jnp.dot(act, w2_ref[...], preferred_element_type=F32)


def _accumulate(acc_ref, part, j):
    @pl.when(j == 0)
    def _():
        acc_ref[...] = part

    @pl.when(j > 0)
    def _():
        acc_ref[...] += part


def _ffn_kernel(h_ref, x1_ref, mod_ref, w1_ref, w3_ref, w2_ref, o_ref, acc_ref):
    tm = h_ref.shape[0]
    j = pl.program_id(2)
    _accumulate(acc_ref, _swiglu_partial(h_ref[...], w1_ref, w3_ref, w2_ref), j)

    @pl.when(j == pl.num_programs(2) - 1)
    def _():
        o_ref[...] = x1_ref[...] + _mod_rows(mod_ref, 5, pl.program_id(1) * tm, tm) * acc_ref[...]


def ffn_residual(h2, x1, modp, w1, w3, w2):
    b = x1.shape[0]
    tm, tf = FFN_TILE_M, FFN_TILE_F
    f = w1.shape[1]
    assert f % tf == 0
    tok = lambda: pl.BlockSpec((None, tm, D_MODEL), lambda i, t, j: (i, t, 0))
    return pl.pallas_call(
        _ffn_kernel,
        grid=(b, S_ALL // tm, f // tf),
        in_specs=[tok(), tok(),
                  pl.BlockSpec((None, N_MOD, 2, 1, D_MODEL), lambda i, t, j: (i, 0, 0, 0, 0)),
                  pl.BlockSpec((D_MODEL, tf), lambda i, t, j: (0, j)),
                  pl.BlockSpec((D_MODEL, tf), lambda i, t, j: (0, j)),
                  pl.BlockSpec((tf, D_MODEL), lambda i, t, j: (j, 0))],
        out_specs=tok(),
        out_shape=jax.ShapeDtypeStruct((b, S_ALL, D_MODEL), F32),
        scratch_shapes=[pltpu.VMEM((tm, D_MODEL), F32)],
        compiler_params=_params(("parallel", "parallel", "arbitrary")),
        name="ffn",
    )(h2, x1, modp, w1, w3, w2)


def _moe_kernel(te_ref, nv_ref, x_ref, w1_ref, w3_ref, w2_ref, o_ref, acc_ref):
    i, j = pl.program_id(0), pl.program_id(1)
    last = pl.num_programs(1) - 1
    valid = i < nv_ref[0]

    @pl.when(valid)
    def _():
        _accumulate(acc_ref, _swiglu_partial(x_ref[...], w1_ref, w3_ref, w2_ref), j)

        @pl.when(j == last)
        def _():
            o_ref[...] = acc_ref[...]

    @pl.when(jnp.logical_and(jnp.logical_not(valid), j == last))
    def _():
        o_ref[...] = jnp.zeros_like(o_ref)


def grouped_swiglu(xs, tile_expert, n_valid, w1, w3, w2, tm, tf):
    n_rows, d = xs.shape
    f = w1.shape[-1]
    assert n_rows % tm == 0 and f % tf == 0
    n_f = f // tf

    def col(i, j, nv):
        return jnp.where(i < nv[0], j, n_f - 1)

    grid_spec = pltpu.PrefetchScalarGridSpec(
        num_scalar_prefetch=2,
        grid=(n_rows // tm, n_f),
        in_specs=[pl.BlockSpec((tm, d), lambda i, j, te, nv: (jnp.minimum(i, nv[0] - 1), 0)),
                  pl.BlockSpec((None, d, tf), lambda i, j, te, nv: (te[i], 0, col(i, j, nv))),
                  pl.BlockSpec((None, d, tf), lambda i, j, te, nv: (te[i], 0, col(i, j, nv))),
                  pl.BlockSpec((None, tf, d), lambda i, j, te, nv: (te[i], col(i, j, nv), 0))],
        out_specs=pl.BlockSpec((tm, d), lambda i, j, te, nv: (i, 0)),
        scratch_shapes=[pltpu.VMEM((tm, d), F32)],
    )
    return pl.pallas_call(
        _moe_kernel,
        grid_spec=grid_spec,
        out_shape=jax.ShapeDtypeStruct((n_rows, d), F32),
        compiler_params=_params(("parallel", "arbitrary")),
        name="moe",
    )(tile_expert, n_valid, xs, w1, w3, w2)


def moe_swiglu(t_bf16, logits, w1, w3, w2):
    n_tok, d = t_bf16.shape
    n_assign = n_tok * TOP_K
    tm = MOE_TILE_M
    top_logit, top_e = lax.top_k(logits, TOP_K)
    gates = jax.nn.softmax(top_logit, axis=-1)
    flat_e = top_e.reshape(-1)
    onehot = (flat_e[:, None] == jnp.arange(N_EXPERTS, dtype=flat_e.dtype)[None, :]).astype(jnp.int32)
    csum = jnp.cumsum(onehot, axis=0)
    rank = jnp.sum(csum * onehot, axis=-1) - 1
    counts = csum[-1]
    padded = (counts + tm - 1) // tm * tm
    pend = jnp.cumsum(padded)
    pstart = pend - padded
    dest = pstart[flat_e] + rank
    n_tiles = n_assign // tm + N_EXPERTS
    n_slots = n_tiles * tm
    flat_tok = jnp.arange(n_assign, dtype=jnp.int32) // TOP_K
    slot_tok = jnp.zeros((n_slots,), jnp.int32).at[dest].set(flat_tok)
    tile_expert = jnp.minimum(
        jnp.searchsorted(pend, jnp.arange(n_tiles, dtype=jnp.int32) * tm, side='right'),
        N_EXPERTS - 1).astype(jnp.int32)
    n_valid = (pend[-1] // tm).astype(jnp.int32).reshape(1)
    xs = t_bf16[slot_tok]
    ys = grouped_swiglu(xs, tile_expert, n_valid, w1, w3, w2, tm, MOE_TILE_F)
    picked = ys[dest].reshape(n_tok, TOP_K, d)
    return jnp.sum(picked * gates[:, :, None], axis=1)


def _proj1_kernel(x_ref, g_ref, mod_ref, w_ref, qan_ref, qb_ref, kvan_ref, kbk_ref, kbv_ref,
                  cos_ref, sin_ref, qg_ref, kg_ref, hy_ref, q_ref, k_ref, v_ref):
    tm = x_ref.shape[0]
    row0 = pl.program_id(1) * tm
    h = _norm_mod(x_ref[...], g_ref[...], mod_ref, row0, 0, 1)
    z = jnp.dot(h.astype(BF16), w_ref[...], preferred_element_type=F32)
    hy_ref[...] = z[:, :HY_IN]

    def rms(a, g):
        return a * lax.rsqrt(jnp.mean(a * a, axis=-1, keepdims=True) + EPS) * g

    c_q, c_kv = HY_IN, HY_IN + MLA_Q_RANK
    c_r = c_kv + MLA_KV_RANK
    qa = rms(z[:, c_q:c_kv], qan_ref[...]).astype(BF16)
    kva = rms(z[:, c_kv:c_r], kvan_ref[...]).astype(BF16)
    q = jnp.dot(qa, qb_ref[...], preferred_element_type=F32)
    kk = jnp.dot(kva, kbk_ref[...], preferred_element_type=F32)
    v_ref[...] = jnp.dot(kva, kbv_ref[...], preferred_element_type=F32).astype(BF16)
    k_rope = pltpu.roll(z[:, c_r:c_r + LANES], MLA_NOPE, 1)
    cos, sin = cos_ref[...], sin_ref[...]
    lane = lax.broadcasted_iota(jnp.int32, (tm, LANES), 1)
    first_half = lane < MLA_NOPE + MLA_ROPE // 2

    def head(a, gain):
        inv = lax.rsqrt(jnp.sum(a * a, axis=-1, keepdims=True) / MLA_QK + EPS)
        return _rope_pair(a * inv * gain, cos, sin, first_half, MLA_ROPE // 2)

    for hh in range(MLA_HEADS):
        sl = slice(hh * LANES, (hh + 1) * LANES)
        q_ref[:, sl] = head(q[:, sl], qg_ref[...]).astype(BF16)
        k_ref[:, sl] = head(kk[:, sl] + k_rope, kg_ref[...]).astype(BF16)


def proj1(xa, g, modp, w_in, qan, qb, kvan, kbk, kbv, cos, sin, q_gain, k_gain):
    b = xa.shape[0]
    tm = TOKEN_TILE
    tok = lambda width: pl.BlockSpec((None, tm, width), lambda i, t: (i, t, 0))
    full = lambda a: pl.BlockSpec(a.shape, lambda i, t: (0, 0))
    hw = MLA_HEADS * LANES
    return pl.pallas_call(
        _proj1_kernel,
        grid=(b, S_ALL // tm),
        in_specs=[tok(D_MODEL), full(g),
                  pl.BlockSpec((None, N_MOD, 2, 1, D_MODEL), lambda i, t: (i, 0, 0, 0, 0)),
                  full(w_in), full(qan), full(qb), full(kvan), full(kbk), full(kbv),
                  pl.BlockSpec((tm, LANES), lambda i, t: (t, 0)),
                  pl.BlockSpec((tm, LANES), lambda i, t: (t, 0)),
                  full(q_gain), full(k_gain)],
        out_specs=[tok(HY_IN), tok(hw), tok(hw), tok(MLA_HEADS * MLA_V)],
        out_shape=[jax.ShapeDtypeStruct((b, S_ALL, HY_IN), F32),
                   jax.ShapeDtypeStruct((b, S_ALL, hw), BF16),
                   jax.ShapeDtypeStruct((b, S_ALL, hw), BF16),
                   jax.ShapeDtypeStruct((b, S_ALL, MLA_HEADS * MLA_V), BF16)],
        compiler_params=_params(("parallel", "parallel")),
        name="proj1",
    )(xa, g, modp, w_in, qan, qb, kvan, kbk, kbv, cos, sin, q_gain, k_gain)


def _merge1_kernel(hy_ref, att_ref, x_ref, mod_ref, g2_ref, w_hy_ref, w_att_ref, r_ref,
                   x1_ref, h2_ref, lg_ref):
    y = (jnp.dot(hy_ref[...].astype(BF16), w_hy_ref[...], preferred_element_type=F32)
         + jnp.dot(att_ref[...], w_att_ref[...], preferred_element_type=F32))
    x1 = x_ref[...] + mod_ref[2, 0] * y
    x1_ref[...] = x1
    h2 = _norm_mod(x1, g2_ref[...], mod_ref, 0, 3, 4)
    h2_ref[...] = h2.astype(BF16)
    lg_ref[...] = jnp.dot(h2, r_ref[...], preferred_element_type=F32, precision=lax.Precision.HIGHEST)


def merge1(hy, att, xa, modp, g2, w_hy, w_att, router):
    b = xa.shape[0]
    tm = MERGE1_TILE
    tok = lambda width: pl.BlockSpec((None, tm, width), lambda i, t: (i, t, 0))
    full = lambda a: pl.BlockSpec(a.shape, lambda i, t: (0, 0))
    return pl.pallas_call(
        _merge1_kernel,
        grid=(b, SEQ // tm),
        in_specs=[tok(HY_WIDTH), tok(HALF_MIX), tok(D_MODEL),
                  pl.BlockSpec((None, N_MOD, 2, 1, D_MODEL), lambda i, t: (i, 0, 0, 0, 0)),
                  full(g2), full(w_hy), full(w_att), full(router)],
        out_specs=[tok(D_MODEL), tok(D_MODEL), tok(N_EXPERTS)],
        out_shape=[jax.ShapeDtypeStruct((b, SEQ, D_MODEL), F32),
                   jax.ShapeDtypeStruct((b, SEQ, D_MODEL), BF16),
                   jax.ShapeDtypeStruct((b, SEQ, N_EXPERTS), F32)],
        compiler_params=_params(("parallel", "parallel")),
        name="merge1",
    )(hy, att, xa, modp, g2, w_hy, w_att, router)


HY_N = 2 * SEQ
HY_TW_ROWS = 256
HY_SPEC_FBLK = 256
HY_CONV_FBLK = 256
HY_LANE_TILES = SEQ // LANES


def _twiddle_kernel(ca_ref, sa_ref, cb_ref, sb_ref, fre_ref, fim_ref, ic_ref, is_ref):
    rows = fre_ref.shape[0]
    r = pl.program_id(0) * rows + lax.broadcasted_iota(jnp.int32, (rows, LANES), 0)
    alt_r = (1 - 2 * (r & 1)).astype(F32)
    cb, sb = cb_ref[...], sb_ref[...]
    for a in range(HY_LANE_TILES):
        ca, sa = ca_ref[:, a:a + 1], sa_ref[:, a:a + 1]
        c = ca * cb - sa * sb
        s = sa * cb + ca * sb
        col = a * LANES + lax.broadcasted_iota(jnp.int32, (rows, LANES), 1)
        alt_c = (1 - 2 * (col & 1)).astype(F32)
        w = jnp.where(col == 0, 1.0 / HY_N, 2.0 / HY_N)
        sl = slice(a * LANES, (a + 1) * LANES)
        fre_ref[:, sl] = c.astype(BF16)
        fim_ref[:, sl] = jnp.where(r == 0, alt_c, -s).astype(BF16)
        ic_ref[:, sl] = (c * w).astype(BF16)
        is_ref[:, sl] = jnp.where(col == 0, alt_r / HY_N, -s * w).astype(BF16)


def dft_matrices():
    idx = jnp.arange(SEQ, dtype=jnp.int32)[:, None]
    step = 2.0 * math.pi / HY_N
    ph_a = ((idx * (LANES * jnp.arange(HY_LANE_TILES, dtype=jnp.int32))[None, :]) % HY_N).astype(F32) * step
    ph_b = ((idx * jnp.arange(LANES, dtype=jnp.int32)[None, :]) % HY_N).astype(F32) * step
    rows = HY_TW_ROWS
    tab = lambda width: pl.BlockSpec((rows, width), lambda i: (i, 0))
    out = jax.ShapeDtypeStruct((SEQ, SEQ), BF16)
    return pl.pallas_call(
        _twiddle_kernel,
        grid=(SEQ // rows,),
        in_specs=[tab(HY_LANE_TILES), tab(HY_LANE_TILES), tab(LANES), tab(LANES)],
        out_specs=[tab(SEQ)] * 4,
        out_shape=[out] * 4,
        compiler_params=_params(("parallel",)),
        name="twiddle",
    )(jnp.cos(ph_a), jnp.sin(ph_a), jnp.cos(ph_b), jnp.sin(ph_b))


def _spec_kernel(fre_ref, fim_ref, h_ref, k_ref):
    w = HY_WIDTH
    h = h_ref[...].astype(BF16)
    re = jnp.dot(fre_ref[...], h, preferred_element_type=F32)
    im = jnp.dot(fim_ref[...], h, preferred_element_type=F32)
    hb0 = h[0:1, w:].astype(F32)
    first = (pl.program_id(1) * re.shape[0] + lax.broadcasted_iota(jnp.int32, (re.shape[0], 1), 0)) == 0
    k_ref[0] = re[:, :w] + re[:, w:] - hb0
    k_ref[1] = jnp.where(first, im[:, :w] + im[:, w:] - hb0, im[:, :w] - im[:, w:])


def filter_spectra(filt, fre, fim):
    fb = HY_SPEC_FBLK
    return pl.pallas_call(
        _spec_kernel,
        grid=(HY_ORDER, SEQ // fb),
        in_specs=[pl.BlockSpec((fb, SEQ), lambda o, f: (f, 0)),
                  pl.BlockSpec((fb, SEQ), lambda o, f: (f, 0)),
                  pl.BlockSpec((SEQ, 2 * HY_WIDTH), lambda o, f: (0, o))],
        out_specs=pl.BlockSpec((None, 2, fb, HY_WIDTH), lambda o, f: (o, 0, f, 0)),
        out_shape=jax.ShapeDtypeStruct((HY_ORDER, 2, SEQ, HY_WIDTH), F32),
        compiler_params=_params(("parallel", "parallel")),
        name="filter_spectra",
    )(fre, fim, filt)


def _short_conv(z_ref, w_ref, b_ref, part):
    z = z_ref[...]
    l = z.shape[0]
    t = lax.broadcasted_iota(jnp.int32, (l, 1), 0)
    prev = jnp.where(t == 0, 0.0, pltpu.roll(z, 1, 0))
    nxt = jnp.where(t == l - 1, 0.0, pltpu.roll(z, l - 1, 0))
    cs = slice(part * HY_WIDTH, (part + 1) * HY_WIDTH)
    return (w_ref[0:1, cs] * prev + w_ref[1:2, cs] * z + w_ref[2:3, cs] * nxt) + b_ref[:, cs]


def _hyconv_kernel(u_ref, xg_ref, cw_ref, cb_ref, fre_ref, fim_ref, ic_ref, is_ref, k_ref, skip_ref,
                   y_ref, ub_ref, acc_ref, *, order):
    f = pl.program_id(1)

    def u_f32():
        return _short_conv(u_ref, cw_ref, cb_ref, 0) if order == 0 else u_ref[...]

    @pl.when(f == 0)
    def _():
        ub_ref[...] = u_f32().astype(BF16)

    x_re = jnp.dot(fre_ref[...], ub_ref[...], preferred_element_type=F32)
    x_im = jnp.dot(fim_ref[...], ub_ref[...], preferred_element_type=F32)
    k_re, k_im = k_ref[0], k_ref[1]
    first = (f * x_re.shape[0] + lax.broadcasted_iota(jnp.int32, (x_re.shape[0], 1), 0)) == 0
    y_re = x_re * k_re - jnp.where(first, 0.0, x_im * k_im)
    y_im = x_im * jnp.where(first, k_im, k_re) + jnp.where(first, 0.0, x_re * k_im)
    part = (jnp.dot(ic_ref[...], y_re.astype(BF16), preferred_element_type=F32)
            + jnp.dot(is_ref[...], y_im.astype(BF16), preferred_element_type=F32))
    _accumulate(acc_ref, part, f)

    @pl.when(f == pl.num_programs(1) - 1)
    def _():
        xg = _short_conv(xg_ref, cw_ref, cb_ref, order + 1)
        y_ref[...] = xg * (acc_ref[...] + skip_ref[order:order + 1, :] * u_f32())


def hyena_conv(order, u, hyz, conv_w, conv_b, fre, fim, ic, is_, spectra, skip):
    b = hyz.shape[0]
    fb = HY_CONV_FBLK
    w = HY_WIDTH
    lat = lambda part: pl.BlockSpec((None, SEQ, w), lambda i, f: (i, 0, part))
    full = lambda a: pl.BlockSpec(a.shape, lambda i, f: (0, 0))
    return pl.pallas_call(
        functools.partial(_hyconv_kernel, order=order),
        grid=(b, SEQ // fb),
        in_specs=[lat(0), lat(order + 1), full(conv_w), full(conv_b),
                  pl.BlockSpec((fb, SEQ), lambda i, f: (f, 0)),
                  pl.BlockSpec((fb, SEQ), lambda i, f: (f, 0)),
                  pl.BlockSpec((SEQ, fb), lambda i, f: (0, f)),
                  pl.BlockSpec((SEQ, fb), lambda i, f: (0, f)),
                  pl.BlockSpec((None, 2, fb, w), lambda i, f: (order, 0, f, 0)),
                  full(skip)],
        out_specs=pl.BlockSpec((None, SEQ, w), lambda i, f: (i, 0, 0)),
        out_shape=jax.ShapeDtypeStruct((b, SEQ, w), F32),
        scratch_shapes=[pltpu.VMEM((SEQ, w), BF16), pltpu.VMEM((SEQ, w), F32)],
        compiler_params=_params(("parallel", "arbitrary")),
        name="hyena_conv",
    )(u, hyz, conv_w, conv_b, fre, fim, ic, is_, spectra, skip)


def hyena_filters(length, w1, b1, w2, b2, w3, freq):
    hp = lax.Precision.HIGHEST
    t = jnp.arange(length, dtype=F32)[:, None]
    t_norm = t / max(length - 1, 1)
    bands = jnp.linspace(1e-4, HY_BANDS - 1, HY_BANDS, dtype=F32)
    ang = 2.0 * math.pi * t * bands / length
    z = jnp.concatenate([t_norm, jnp.cos(ang), -jnp.sin(ang)], axis=-1)
    h = jnp.sin(freq * (jnp.dot(z, w1, precision=hp) + b1))
    h = jnp.sin(freq * (jnp.dot(h, w2, precision=hp) + b2))
    h = jnp.dot(h, w3, precision=hp)
    deltas = jnp.abs(jnp.linspace(HY_MIN_DECAY, HY_MAX_DECAY, HY_WIDTH, dtype=F32))
    window = jnp.exp(-t_norm * deltas) + HY_SHIFT
    return h.reshape(length, HY_ORDER, 2, HY_WIDTH) * window[:, None, None, :]


def hyena(hyz, conv_w, conv_b, fw1, fb1, fw2, fb2, fw3, freq, skip):
    fre, fim, ic, is_ = dft_matrices()
    filt = hyena_filters(SEQ, fw1, fb1, fw2, fb2, fw3, freq).reshape(SEQ, HY_ORDER * 2 * HY_WIDTH)
    spectra = filter_spectra(filt, fre, fim)
    y = hyz
    for o in range(HY_ORDER):
        y = hyena_conv(o, y, hyz, conv_w, conv_b.reshape(1, -1), fre, fim, ic, is_, spectra, skip)
    return y


def _grid_angles(rot_dim):
    n_freq = rot_dim // 4
    inv_freq = ROPE_THETA ** (-jnp.arange(n_freq, dtype=F32) / n_freq)
    t = jnp.arange(SEQ)
    r = (t // GRID_W).astype(F32)
    c_ = (t % GRID_W).astype(F32)
    return jnp.concatenate([r[:, None] * inv_freq, c_[:, None] * inv_freq], axis=-1)


def _rope_tables(rot_dim, lane_offsets):
    ang = _grid_angles(rot_dim)
    c, s = jnp.cos(ang), jnp.sin(ang)
    half = rot_dim // 2
    cos = jnp.ones((S_ALL, LANES), F32)
    sin = jnp.zeros((S_ALL, LANES), F32)
    for off in lane_offsets:
        cos = cos.at[:SEQ, off:off + half].set(c).at[:SEQ, off + half:off + rot_dim].set(c)
        sin = sin.at[:SEQ, off:off + half].set(-s).at[:SEQ, off + half:off + rot_dim].set(s)
    return cos, sin


def _block_diag(w):
    nb, bs, _ = w.shape
    eye = jnp.eye(nb, dtype=w.dtype)
    return (eye[:, None, :, None] * w[:, :, None, :]).reshape(nb * bs, nb * bs)


def _gqa_pair_order():
    g = GQA_Q_HEADS // GQA_KV_HEADS
    heads = []
    for p in range(g):
        heads += [p, g + p]
    return np.concatenate([np.arange(h * HEAD_DIM, (h + 1) * HEAD_DIM) for h in heads])


def _pad_heads(w, n_heads, width):
    k = w.shape[0]
    w = w.reshape(k, n_heads, width)
    return jnp.pad(w, ((0, 0), (0, 0), (0, LANES - width))).reshape(k, n_heads * LANES)


def kernel(x, c, ctx, c_ctx, mod_w, mod_b, norm1_g, norm2_g, ab_w_in, ab_w_out, lru_conv_w, lru_conv_b, lru_w_a, lru_b_a, lru_w_x, lru_b_x, lru_lambda, gqa_q_norm, gqa_k_norm, ffn_w1, ffn_w3, ffn_w2, cd_w_in, cd_w_out, hy_conv_w, hy_conv_b, hy_filt_w1, hy_filt_b1, hy_filt_w2, hy_filt_b2, hy_filt_w3, hy_sin_freq, hy_skip, mla_q_a_norm, mla_q_b, mla_kv_a_norm, mla_kv_b, mla_q_norm, mla_k_norm, moe_router, moe_w1, moe_w3, moe_w2):
    batch = x.shape[0]
    bf = lambda w: w.astype(BF16)
    row = lambda v: v.reshape(1, -1)

    silu_all = jnp.concatenate([jax.nn.silu(c), jax.nn.silu(c_ctx)[None, :],
                                jnp.zeros((16 - batch - 1, D_MODEL), F32)], axis=0)
    mods = []
    for layer in range(DEPTH):
        m = matmul(silu_all, mod_w[layer], 16, 1536) + mod_b[layer]
        lat = m[:batch].reshape(batch, N_MOD, 1, D_MODEL)
        cx = jnp.broadcast_to(m[batch].reshape(1, N_MOD, 1, D_MODEL), lat.shape)
        mods.append(jnp.stack([lat, cx], axis=2))

    xa = jnp.concatenate([x, ctx], axis=1)

    perm = _gqa_pair_order()
    q0 = 2 * LRU_WIDTH
    w_in0 = ab_w_in[0]
    w_in0 = jnp.concatenate([w_in0[:, :q0], w_in0[:, q0 + perm], w_in0[:, q0 + GQA_Q_HEADS * HEAD_DIM:]], axis=1)
    cos_g, sin_g = _rope_tables(HEAD_DIM, (0, HEAD_DIM))
    q_gain = row(jnp.tile(gqa_q_norm[0], 2) * HEAD_DIM ** -0.5)
    k_gain = row(jnp.tile(gqa_k_norm[0], 2))
    xr, gate, q, k, v = proj0(xa, row(norm1_g[0]), mods[0], bf(w_in0), cos_g, sin_g, q_gain, k_gain)

    w_gates = jnp.concatenate([_block_diag(lru_w_a[0, 0]), _block_diag(lru_w_x[0, 0]),
                               _block_diag(lru_w_a[0, 1]), _block_diag(lru_w_x[0, 1])], axis=1)
    b_gates = jnp.concatenate([lru_b_a[0, 0].reshape(-1), lru_b_x[0, 0].reshape(-1),
                               lru_b_a[0, 1].reshape(-1), lru_b_x[0, 1].reshape(-1)])
    rec = rglru(xr, lru_conv_w[0], row(lru_conv_b[0]), bf(w_gates), row(b_gates), lru_lambda[0])

    att = attention_both(q, k, v, GQA_HEADS, True)
    w_out0 = ab_w_out[0]
    x1, h2 = merge0(rec, gate, att, xa, mods[0], row(norm2_g[0]),
                    bf(w_out0[:LRU_WIDTH]), bf(w_out0[LRU_WIDTH:][perm]))
    xa = ffn_residual(h2, x1, mods[0], bf(ffn_w1[0]), bf(ffn_w3[0]), bf(ffn_w2[0]))

    w_in1 = jnp.pad(cd_w_in[0], ((0, 0), (0, LANES - MLA_ROPE)))
    cos_m, sin_m = _rope_tables(MLA_ROPE, (MLA_NOPE,))
    qb = _pad_heads(mla_q_b[0], MLA_HEADS, MLA_QK)
    kvb = mla_kv_b[0].reshape(MLA_KV_RANK, MLA_HEADS, MLA_NOPE + MLA_V)
    kbk = _pad_heads(kvb[:, :, :MLA_NOPE].reshape(MLA_KV_RANK, -1), MLA_HEADS, MLA_NOPE)
    kbv = kvb[:, :, MLA_NOPE:].reshape(MLA_KV_RANK, -1)
    pad_gain = lambda g_: row(jnp.pad(g_, (0, LANES - MLA_QK)))
    hyz, q, k, v = proj1(xa, row(norm1_g[1]), mods[1], bf(w_in1), row(mla_q_a_norm[0]), bf(qb),
                         row(mla_kv_a_norm[0]), bf(kbk), bf(kbv), cos_m, sin_m,
                         pad_gain(mla_q_norm[0] * MLA_QK ** -0.5), pad_gain(mla_k_norm[0]))
    att = attention_both(q, k, v, MLA_HEAD_GROUPS, False)
    hy = hyena(hyz, hy_conv_w[0], hy_conv_b[0], hy_filt_w1[0], hy_filt_b1[0], hy_filt_w2[0],
               hy_filt_b2[0], hy_filt_w3[0], hy_sin_freq[0], hy_skip[0])
    w_out1 = cd_w_out[0]
    x1, h2, logits = merge1(hy, att, xa, mods[1], row(norm2_g[1]),
                            bf(w_out1[:HY_WIDTH]), bf(w_out1[HY_WIDTH:]), moe_router[0])
    moe = moe_swiglu(h2.reshape(-1, D_MODEL), logits.reshape(-1, N_EXPERTS),
                     bf(moe_w1[0]), bf(moe_w3[0]), bf(moe_w2[0]))
    return x1 + mods[1][:, 5, 0] * moe.reshape(batch, SEQ, D_MODEL)
```

```python
import functools
import math

import jax
import jax.numpy as jnp
import numpy as np
from jax import lax
from jax.experimental import pallas as pl
from jax.experimental.pallas import tpu as pltpu

F32 = jnp.float32
BF16 = jnp.bfloat16

D_MODEL = 1024
SEQ = 2048
CTX_LEN = 256
S_ALL = SEQ + CTX_LEN
DEPTH = 2
GRID_W = 64
HEAD_DIM = 64
HALF_MIX = D_MODEL // 2
ROPE_THETA = 10000.0
EPS = 1e-6
N_MOD = 6
LRU_WIDTH = HALF_MIX
LRU_BLOCKS = LRU_WIDTH // HEAD_DIM
LRU_CONV = 4
LRU_C = 8.0
GQA_Q_HEADS = HALF_MIX // HEAD_DIM
GQA_KV_HEADS = 2
HY_WIDTH = HALF_MIX
HY_ORDER = 2
HY_CONV = 3
HY_BANDS = 16
HY_TARGET = 1e-2
HY_FAST_PCT = 0.3
HY_SLOW_PCT = 1.5
HY_MIN_DECAY = math.log(HY_TARGET) / HY_SLOW_PCT
HY_MAX_DECAY = math.log(HY_TARGET) / HY_FAST_PCT
HY_SHIFT = 0.05
HY_IN = (HY_ORDER + 1) * HY_WIDTH
MLA_HEADS = HALF_MIX // HEAD_DIM
MLA_Q_RANK = D_MODEL // 4
MLA_KV_RANK = D_MODEL // 8
MLA_NOPE = HEAD_DIM
MLA_ROPE = HEAD_DIM // 2
MLA_V = HEAD_DIM
MLA_QK = MLA_NOPE + MLA_ROPE
N_EXPERTS = 8
TOP_K = 2

LANES = 128
SUBLANES = 8
V7X_VMEM_LIMIT_BYTES = 56 * 1024 * 1024

TOKEN_TILE = 768
FFN_TILE_M = 1152
FFN_TILE_F = 256
ATTN_TILE_Q = 512
MERGE1_TILE = 512
MOE_TILE_M = 512
MOE_TILE_F = 512
MOE_ROUTE_BLOCK = 256
MOE_DISPATCH_TILE = 256
MOE_COMBINE_TILE = 256

LRU_GAP = SUBLANES
LRU_LAT0 = CTX_LEN + LRU_GAP
LRU_CHUNK = 256
LRU_ROWS = 2560
LRU_PAD_FRONT = SUBLANES

assert S_ALL % TOKEN_TILE == 0 and S_ALL % FFN_TILE_M == 0 and SEQ % ATTN_TILE_Q == 0
assert LRU_ROWS % LRU_CHUNK == 0 and LRU_ROWS >= LRU_LAT0 + SEQ + SUBLANES


def _params(sem):
    return pltpu.CompilerParams(dimension_semantics=sem, vmem_limit_bytes=V7X_VMEM_LIMIT_BYTES)


def _norm_mod(x, g, mod_pair, row0, shift_idx, scale_idx):
    y = x * lax.rsqrt(jnp.mean(x * x, axis=-1, keepdims=True) + EPS) * g
    shift = _mod_rows(mod_pair, shift_idx, row0, x.shape[0])
    scale = _mod_rows(mod_pair, scale_idx, row0, x.shape[0])
    return y * (1.0 + scale) + shift


def _mod_rows(mod_pair, idx, row0, rows):
    row = row0 + lax.broadcasted_iota(jnp.int32, (rows, 1), 0)
    return jnp.where(row >= SEQ, mod_pair[idx, 1], mod_pair[idx, 0])


def _mm_kernel(x_ref, w_ref, o_ref):
    o_ref[...] = jnp.dot(x_ref[...].astype(BF16), w_ref[...].astype(BF16),
                         preferred_element_type=F32)


def matmul(x, w, tm, tn):
    m, k = x.shape
    n = w.shape[1]
    assert m % tm == 0 and n % tn == 0
    return pl.pallas_call(
        _mm_kernel,
        grid=(m // tm, n // tn),
        in_specs=[pl.BlockSpec((tm, k), lambda i, j: (i, 0)),
                  pl.BlockSpec((k, tn), lambda i, j: (0, j))],
        out_specs=pl.BlockSpec((tm, tn), lambda i, j: (i, j)),
        out_shape=jax.ShapeDtypeStruct((m, n), F32),
        compiler_params=_params(("parallel", "parallel")),
        name="matmul",
    )(x, w)


def _rope_pair(a, cos, sin, first_half, shift):
    rot = jnp.where(first_half, pltpu.roll(a, LANES - shift, 1), pltpu.roll(a, shift, 1))
    return a * cos + rot * sin


def _proj0_kernel(x_ref, g_ref, mod_ref, w_ref, cos_ref, sin_ref, qg_ref, kg_ref,
                  xr_ref, gate_ref, q_ref, k_ref, v_ref):
    tm = x_ref.shape[0]
    row0 = pl.program_id(1) * tm
    h = _norm_mod(x_ref[...], g_ref[...], mod_ref, row0, 0, 1)
    z = jnp.dot(h.astype(BF16), w_ref[...], preferred_element_type=F32)
    xr_ref[...] = z[:, :LRU_WIDTH]
    gate_ref[...] = z[:, LRU_WIDTH:2 * LRU_WIDTH]
    cos, sin = cos_ref[...], sin_ref[...]
    lane = lax.broadcasted_iota(jnp.int32, (tm, LANES), 1)
    lo = lane < HEAD_DIM
    first_half = (lane & (HEAD_DIM // 2)) == 0

    def head_pair(a, gain):
        sq = a * a
        ss_lo = jnp.sum(jnp.where(lo, sq, 0.0), axis=-1, keepdims=True)
        ss_hi = jnp.sum(jnp.where(lo, 0.0, sq), axis=-1, keepdims=True)
        inv = jnp.where(lo, lax.rsqrt(ss_lo / HEAD_DIM + EPS), lax.rsqrt(ss_hi / HEAD_DIM + EPS))
        return _rope_pair(a * inv * gain, cos, sin, first_half, HEAD_DIM // 2)

    q0 = 2 * LRU_WIDTH
    for p in range(GQA_Q_HEADS // 2):
        q_ref[:, p * LANES:(p + 1) * LANES] = head_pair(
            z[:, q0 + p * LANES:q0 + (p + 1) * LANES], qg_ref[...]).astype(BF16)
    k0 = q0 + GQA_Q_HEADS * HEAD_DIM
    k_ref[...] = head_pair(z[:, k0:k0 + LANES], kg_ref[...]).astype(BF16)
    v_ref[...] = z[:, k0 + LANES:k0 + 2 * LANES].astype(BF16)


def proj0(xa, g, modp, w_in, cos, sin, q_gain, k_gain):
    b = xa.shape[0]
    tm = TOKEN_TILE
    n = w_in.shape[1]
    tok = lambda width: pl.BlockSpec((None, tm, width), lambda i, t: (i, t, 0))
    full = lambda r, c_: pl.BlockSpec((r, c_), lambda i, t: (0, 0))
    return pl.pallas_call(
        _proj0_kernel,
        grid=(b, S_ALL // tm),
        in_specs=[tok(D_MODEL), full(1, D_MODEL),
                  pl.BlockSpec((None, N_MOD, 2, 1, D_MODEL), lambda i, t: (i, 0, 0, 0, 0)),
                  full(D_MODEL, n),
                  pl.BlockSpec((tm, LANES), lambda i, t: (t, 0)),
                  pl.BlockSpec((tm, LANES), lambda i, t: (t, 0)),
                  full(1, LANES), full(1, LANES)],
        out_specs=[tok(LRU_WIDTH), tok(LRU_WIDTH), tok(GQA_Q_HEADS * HEAD_DIM), tok(LANES), tok(LANES)],
        out_shape=[jax.ShapeDtypeStruct((b, S_ALL, LRU_WIDTH), F32),
                   jax.ShapeDtypeStruct((b, S_ALL, LRU_WIDTH), F32),
                   jax.ShapeDtypeStruct((b, S_ALL, GQA_Q_HEADS * HEAD_DIM), BF16),
                   jax.ShapeDtypeStruct((b, S_ALL, LANES), BF16),
                   jax.ShapeDtypeStruct((b, S_ALL, LANES), BF16)],
        compiler_params=_params(("parallel", "parallel")),
        name="proj0",
    )(xa, g, modp, w_in, cos, sin, q_gain, k_gain)


def _lru_kernel(xr_ref, cw_ref, cb_ref, wg_ref, bg_ref, lam_ref, rec_ref,
                pad_ref, af_ref, hf_ref, ab_ref, hb_ref):
    w = LRU_WIDTH
    pad_ref[...] = jnp.zeros_like(pad_ref)
    pad_ref[LRU_PAD_FRONT:LRU_PAD_FRONT + CTX_LEN, :] = xr_ref[SEQ:S_ALL, :]
    pad_ref[LRU_PAD_FRONT + LRU_LAT0:LRU_PAD_FRONT + LRU_LAT0 + SEQ, :] = xr_ref[0:SEQ, :]
    lam = lam_ref[...]
    neg_c_softplus = -LRU_C * (jnp.maximum(-lam, 0.0) + jnp.log(1.0 + jnp.exp(-jnp.abs(lam))))
    left = LRU_CONV // 2
    for ch in range(LRU_ROWS // LRU_CHUNK):
        r0 = ch * LRU_CHUNK
        u = cb_ref[...]
        for j in range(LRU_CONV):
            start = LRU_PAD_FRONT + r0 - left + j
            u = u + cw_ref[j:j + 1, :] * pad_ref[start:start + LRU_CHUNK, :]
        gz = jnp.dot(u.astype(BF16), wg_ref[...], preferred_element_type=F32) + bg_ref[...]
        for d, (a_ref, h_ref) in enumerate(((af_ref, hf_ref), (ab_ref, hb_ref))):
            r = jax.nn.sigmoid(gz[:, 2 * d * w:(2 * d + 1) * w])
            i = jax.nn.sigmoid(gz[:, (2 * d + 1) * w:(2 * d + 2) * w])
            a = jnp.exp(neg_c_softplus[d:d + 1, :] * r)
            a_ref[r0:r0 + LRU_CHUNK, :] = a
            h_ref[r0:r0 + LRU_CHUNK, :] = jnp.sqrt(1.0 - a * a) * (i * u)

    n_ctx_groups = CTX_LEN // SUBLANES

    def group(tt, carry):
        hf, hb = carry
        is_lat = tt >= n_ctx_groups
        base_f = pl.multiple_of(tt * SUBLANES + jnp.where(is_lat, LRU_GAP, 0), SUBLANES)
        base_b = pl.multiple_of(
            jnp.where(is_lat, LRU_LAT0 + SEQ + CTX_LEN, CTX_LEN) - (tt + 1) * SUBLANES, SUBLANES)
        for i in range(SUBLANES):
            rf = pl.ds(base_f + i, 1)
            hf = af_ref[rf, :] * hf + hf_ref[rf, :]
            hf_ref[rf, :] = hf
            rb = pl.ds(base_b + (SUBLANES - 1 - i), 1)
            hb = ab_ref[rb, :] * hb + hb_ref[rb, :]
            hb_ref[rb, :] = hb
        return hf, hb

    zero = jnp.zeros((1, w), F32)
    lax.fori_loop(0, S_ALL // SUBLANES, group, (zero, zero))
    rec_ref[0:SEQ, :] = hf_ref[LRU_LAT0:LRU_LAT0 + SEQ, :] + hb_ref[LRU_LAT0:LRU_LAT0 + SEQ, :]
    rec_ref[SEQ:S_ALL, :] = hf_ref[0:CTX_LEN, :] + hb_ref[0:CTX_LEN, :]


def rglru(xr, conv_w, conv_b, w_gates, b_gates, lam):
    b = xr.shape[0]
    w = LRU_WIDTH
    full = lambda r, c_: pl.BlockSpec((r, c_), lambda i: (0, 0))
    rows = pltpu.VMEM((LRU_ROWS, w), F32)
    return pl.pallas_call(
        _lru_kernel,
        grid=(b,),
        in_specs=[pl.BlockSpec((None, S_ALL, w), lambda i: (i, 0, 0)),
                  full(LRU_CONV, w), full(1, w), full(w, 4 * w), full(1, 4 * w), full(2, w)],
        out_specs=pl.BlockSpec((None, S_ALL, w), lambda i: (i, 0, 0)),
        out_shape=jax.ShapeDtypeStruct((b, S_ALL, w), F32),
        scratch_shapes=[pltpu.VMEM((LRU_ROWS + 2 * LRU_PAD_FRONT, w), F32), rows, rows, rows, rows],
        compiler_params=_params(("parallel",)),
        name="rglru",
    )(xr, conv_w, conv_b, w_gates, b_gates, lam)


def _attn_kernel(q_ref, k_ref, v_ref, *rest, heads):
    o_ref = rest[-1]
    tq, sk = q_ref.shape[0], k_ref.shape[0]
    q_lo = lax.broadcasted_iota(jnp.int32, (tq, LANES), 1) < HEAD_DIM
    v_lo = lax.broadcasted_iota(jnp.int32, (sk, LANES), 1) < HEAD_DIM

    def keep(x, is_lo, half):
        if half is None:
            return x
        return jnp.where(is_lo if half == 0 else jnp.logical_not(is_lo), x, jnp.zeros_like(x))

    for og, members in enumerate(heads):
        acc = None
        for qg, qh, kg, vg, vh in members:
            q = keep(q_ref[:, qg * LANES:(qg + 1) * LANES], q_lo, qh)
            s = lax.dot_general(q, k_ref[:, kg * LANES:(kg + 1) * LANES],
                                (((1,), (1,)), ((), ())), preferred_element_type=F32)
            p = jnp.exp(s - jnp.max(s, axis=-1, keepdims=True))
            l = jnp.sum(p, axis=-1, keepdims=True)
            v = keep(v_ref[:, vg * LANES:(vg + 1) * LANES], v_lo, vh)
            o = jnp.dot(p.astype(BF16), v, preferred_element_type=F32) / l
            acc = o if acc is None else acc + o
        o_ref[:, og * LANES:(og + 1) * LANES] = acc.astype(o_ref.dtype)


def attention(q, k, v, heads, q_block, n_q_blocks, tq, k_rows):
    b, s, qw = q.shape
    kw, vw = k.shape[-1], v.shape[-1]
    ow = len(heads) * LANES
    kv_spec = lambda width: pl.BlockSpec((None, k_rows, width), lambda i, t: (i, s // k_rows - 1, 0))
    return pl.pallas_call(
        functools.partial(_attn_kernel, heads=heads),
        grid=(b, n_q_blocks),
        in_specs=[pl.BlockSpec((None, tq, qw), lambda i, t: (i, q_block + t, 0)), kv_spec(kw), kv_spec(vw)],
        out_specs=pl.BlockSpec((None, tq, ow), lambda i, t: (i, t, 0)),
        out_shape=jax.ShapeDtypeStruct((b, n_q_blocks * tq, ow), BF16),
        compiler_params=_params(("parallel", "parallel")),
        name="attention",
    )(q, k, v)


GQA_HEADS = tuple(((p, 0, 0, 0, 0), (p, 1, 0, 0, 1)) for p in range(GQA_Q_HEADS // 2))
MLA_HEAD_GROUPS = tuple(((2 * p, None, 2 * p, p, 0), (2 * p + 1, None, 2 * p + 1, p, 1))
                        for p in range(MLA_HEADS // 2))


def attention_latent(q, k, v, heads):
    return attention(q, k, v, heads, 0, SEQ // ATTN_TILE_Q, ATTN_TILE_Q, S_ALL)


def attention_context(q, k, v, heads):
    return attention(q, k, v, heads, SEQ // CTX_LEN, 1, CTX_LEN, CTX_LEN)


def _merge0_kernel(rec_ref, gate_ref, att_l_ref, att_c_ref, x_ref, mod_ref, g2_ref,
                   w_rec_ref, w_att_ref, x1_ref, h2_ref):
    tm = x_ref.shape[0]
    row0 = pl.program_id(1) * tm
    rg = (rec_ref[...] * jax.nn.gelu(gate_ref[...])).astype(BF16)
    att = jnp.where(row0 >= SEQ, att_c_ref[...], att_l_ref[...])
    y = (jnp.dot(rg, w_rec_ref[...], preferred_element_type=F32)
         + jnp.dot(att, w_att_ref[...], preferred_element_type=F32))
    x1 = x_ref[...] + _mod_rows(mod_ref, 2, row0, tm) * y
    x1_ref[...] = x1
    h2_ref[...] = _norm_mod(x1, g2_ref[...], mod_ref, row0, 3, 4).astype(BF16)


def merge0(rec, gate, att_l, att_c, xa, modp, g2, w_rec, w_att):
    b = xa.shape[0]
    tm = CTX_LEN
    n_lat = SEQ // tm
    tok = lambda width: pl.BlockSpec((None, tm, width), lambda i, t: (i, t, 0))
    full = lambda r, c_: pl.BlockSpec((r, c_), lambda i, t: (0, 0))
    return pl.pallas_call(
        _merge0_kernel,
        grid=(b, S_ALL // tm),
        in_specs=[tok(LRU_WIDTH), tok(LRU_WIDTH),
                  pl.BlockSpec((None, tm, HALF_MIX), lambda i, t: (i, jnp.minimum(t, n_lat - 1), 0)),
                  pl.BlockSpec((None, tm, HALF_MIX), lambda i, t: (i, 0, 0)),
                  tok(D_MODEL),
                  pl.BlockSpec((None, N_MOD, 2, 1, D_MODEL), lambda i, t: (i, 0, 0, 0, 0)),
                  full(1, D_MODEL), full(LRU_WIDTH, D_MODEL), full(HALF_MIX, D_MODEL)],
        out_specs=[tok(D_MODEL), tok(D_MODEL)],
        out_shape=[jax.ShapeDtypeStruct((b, S_ALL, D_MODEL), F32),
                   jax.ShapeDtypeStruct((b, S_ALL, D_MODEL), BF16)],
        compiler_params=_params(("parallel", "parallel")),
        name="merge0",
    )(rec, gate, att_l, att_c, xa, modp, g2, w_rec, w_att)


def _swiglu_partial(x, w1_ref, w3_ref, w2_ref):
    h1 = jnp.dot(x, w1_ref[...], preferred_element_type=F32)
    h3 = jnp.dot(x, w3_ref[...], preferred_element_type=F32)
    act = (h1 * jax.nn.sigmoid(h1) * h3).astype(BF16)
    return jnp.dot(act, w2_ref[...], preferred_element_type=F32)


def _accumulate(acc_ref, part, j):
    @pl.when(j == 0)
    def _():
        acc_ref[...] = part

    @pl.when(j > 0)
    def _():
        acc_ref[...] += part


def _ffn_kernel(h_ref, x1_ref, mod_ref, w1_ref, w3_ref, w2_ref, o_ref, acc_ref):
    tm = h_ref.shape[0]
    j = pl.program_id(2)
    _accumulate(acc_ref, _swiglu_partial(h_ref[...], w1_ref, w3_ref, w2_ref), j)

    @pl.when(j == pl.num_programs(2) - 1)
    def _():
        o_ref[...] = x1_ref[...] + _mod_rows(mod_ref, 5, pl.program_id(1) * tm, tm) * acc_ref[...]


def ffn_residual(h2, x1, modp, w1, w3, w2):
    b = x1.shape[0]
    tm, tf = FFN_TILE_M, FFN_TILE_F
    f = w1.shape[1]
    assert f % tf == 0
    tok = lambda: pl.BlockSpec((None, tm, D_MODEL), lambda i, t, j: (i, t, 0))
    return pl.pallas_call(
        _ffn_kernel,
        grid=(b, S_ALL // tm, f // tf),
        in_specs=[tok(), tok(),
                  pl.BlockSpec((None, N_MOD, 2, 1, D_MODEL), lambda i, t, j: (i, 0, 0, 0, 0)),
                  pl.BlockSpec((D_MODEL, tf), lambda i, t, j: (0, j)),
                  pl.BlockSpec((D_MODEL, tf), lambda i, t, j: (0, j)),
                  pl.BlockSpec((tf, D_MODEL), lambda i, t, j: (j, 0))],
        out_specs=tok(),
        out_shape=jax.ShapeDtypeStruct((b, S_ALL, D_MODEL), F32),
        scratch_shapes=[pltpu.VMEM((tm, D_MODEL), F32)],
        compiler_params=_params(("parallel", "parallel", "arbitrary")),
        name="ffn",
    )(h2, x1, modp, w1, w3, w2)


def _moe_kernel(te_ref, nv_ref, x_ref, w1_ref, w3_ref, w2_ref, o_ref, xb_ref, acc_ref):
    i, j = pl.program_id(0), pl.program_id(1)
    last = pl.num_programs(1) - 1
    valid = i < nv_ref[0]

    @pl.when(valid)
    def _():
        @pl.when(j == 0)
        def _():
            xb_ref[...] = x_ref[...].astype(BF16)

        _accumulate(acc_ref, _swiglu_partial(xb_ref[...], w1_ref, w3_ref, w2_ref), j)

        @pl.when(j == last)
        def _():
            o_ref[...] = acc_ref[...]

    @pl.when(jnp.logical_and(jnp.logical_not(valid), j == last))
    def _():
        o_ref[...] = jnp.zeros_like(o_ref)


def grouped_swiglu(xs, tile_expert, n_valid, w1, w3, w2, tm, tf):
    n_rows, d = xs.shape
    f = w1.shape[-1]
    assert n_rows % tm == 0 and f % tf == 0
    n_f = f // tf

    def col(i, j, nv):
        return jnp.where(i < nv[0], j, n_f - 1)

    grid_spec = pltpu.PrefetchScalarGridSpec(
        num_scalar_prefetch=2,
        grid=(n_rows // tm, n_f),
        in_specs=[pl.BlockSpec((tm, d), lambda i, j, te, nv: (jnp.minimum(i, nv[0] - 1), 0)),
                  pl.BlockSpec((None, d, tf), lambda i, j, te, nv: (te[i], 0, col(i, j, nv))),
                  pl.BlockSpec((None, d, tf), lambda i, j, te, nv: (te[i], 0, col(i, j, nv))),
                  pl.BlockSpec((None, tf, d), lambda i, j, te, nv: (te[i], col(i, j, nv), 0))],
        out_specs=pl.BlockSpec((tm, d), lambda i, j, te, nv: (i, 0)),
        scratch_shapes=[pltpu.VMEM((tm, d), BF16), pltpu.VMEM((tm, d), F32)],
    )
    return pl.pallas_call(
        _moe_kernel,
        grid_spec=grid_spec,
        out_shape=jax.ShapeDtypeStruct((n_rows, d), F32),
        compiler_params=_params(("parallel", "arbitrary")),
        name="moe",
    )(tile_expert, n_valid, xs, w1, w3, w2)


def route(logits, tm):
    n_tok = logits.shape[0]
    n_assign = n_tok * TOP_K
    idx = jnp.arange(N_EXPERTS, dtype=jnp.int32)[None, :]
    m1 = jnp.max(logits, axis=-1, keepdims=True)
    e1 = jnp.min(jnp.where(logits == m1, idx, N_EXPERTS), axis=-1, keepdims=True)
    rest = jnp.where(idx == e1, jnp.finfo(F32).min, logits)
    m2 = jnp.max(rest, axis=-1, keepdims=True)
    e2 = jnp.min(jnp.where(rest == m2, idx, N_EXPERTS), axis=-1, keepdims=True)
    t = jnp.exp(m2 - m1)
    gates = jnp.concatenate([1.0 / (1.0 + t), t / (1.0 + t)], axis=-1)
    flat_e = jnp.concatenate([e1, e2], axis=-1).reshape(n_assign, 1)
    onehot = (flat_e == idx).astype(F32)
    blk = MOE_ROUTE_BLOCK
    nb = n_assign // blk
    oh = onehot.reshape(nb, blk, N_EXPERTS)
    lower = lambda n: (jnp.arange(n)[:, None] > jnp.arange(n)[None, :]).astype(F32)
    within = jnp.einsum('ij,bjk->bik', lower(blk), oh)
    blk_cnt = jnp.sum(oh, axis=1)
    blk_off = jnp.dot(lower(nb), blk_cnt, precision=lax.Precision.HIGHEST)
    rank = jnp.sum((within + blk_off[:, None, :]) * oh, axis=-1).reshape(n_assign)
    counts = jnp.sum(blk_cnt, axis=0).astype(jnp.int32)
    padded = (counts + tm - 1) // tm * tm
    ends = []
    for e in range(N_EXPERTS):
        ends.append(padded[e] + (ends[-1] if ends else 0))
    pend = jnp.stack(ends)
    pstart = pend - padded
    dest = (jnp.sum(onehot * pstart.astype(F32)[None, :], axis=-1) + rank).astype(jnp.int32)
    n_tiles = n_assign // tm + N_EXPERTS
    starts = jnp.arange(n_tiles, dtype=jnp.int32) * tm
    tile_expert = jnp.minimum(jnp.sum((pend[None, :] <= starts[:, None]).astype(jnp.int32), axis=-1),
                              N_EXPERTS - 1)
    n_valid = (pend[-1] // tm).reshape(1)
    return gates, dest, tile_expert, n_valid


def _row_copies(dest_ref, src, dst, sem, rows, gather):
    def copy(r, kk):
        d = dest_ref[0, TOP_K * r + kk]
        if gather:
            return pltpu.make_async_copy(src.at[pl.ds(d, 1), :], dst.at[kk, pl.ds(r, 1), :], sem)
        return pltpu.make_async_copy(src.at[pl.ds(r, 1), :], dst.at[pl.ds(d, 1), :], sem)

    def start(r, carry):
        for kk in range(TOP_K):
            copy(r, kk).start()
        return carry

    def wait(r, carry):
        for kk in range(TOP_K):
            copy(r, kk).wait()
        return carry

    lax.fori_loop(0, rows, start, 0, unroll=8)
    lax.fori_loop(0, rows, wait, 0, unroll=8)


def _dispatch_kernel(dest_ref, h_ref, init_ref, xs_ref, sem):
    del init_ref
    _row_copies(dest_ref, h_ref, xs_ref, sem, h_ref.shape[0], gather=False)


def moe_dispatch(h2, dest, n_slots):
    n_tok, d = h2.shape
    tm = MOE_DISPATCH_TILE
    n_steps = n_tok // tm
    return pl.pallas_call(
        _dispatch_kernel,
        grid=(n_steps,),
        in_specs=[pl.BlockSpec((None, 1, TOP_K * tm), lambda t: (t, 0, 0), memory_space=pltpu.SMEM),
                  pl.BlockSpec((tm, d), lambda t: (t, 0)),
                  pl.BlockSpec(memory_space=pl.ANY)],
        out_specs=pl.BlockSpec(memory_space=pl.ANY),
        out_shape=jax.ShapeDtypeStruct((n_slots, d), h2.dtype),
        scratch_shapes=[pltpu.SemaphoreType.DMA],
        input_output_aliases={2: 0},
        compiler_params=_params(("arbitrary",)),
        name="moe_dispatch",
    )(dest.reshape(n_steps, 1, TOP_K * tm), h2, jnp.zeros((n_slots, d), h2.dtype))


def _combine_kernel(dest_ref, x1_ref, g_ref, mod_ref, ys_ref, o_ref, buf_ref, sem):
    tm = x1_ref.shape[0]
    _row_copies(dest_ref, ys_ref, buf_ref, sem, tm, gather=True)
    g = g_ref[...]
    moe = g[:, 0:1] * buf_ref[0] + g[:, 1:2] * buf_ref[1]
    o_ref[...] = x1_ref[...] + mod_ref[5, 0] * moe


def moe_combine(x1, gates, dest, ys, modp):
    b, s, d = x1.shape
    tm = MOE_COMBINE_TILE
    nt = s // tm
    return pl.pallas_call(
        _combine_kernel,
        grid=(b, nt),
        in_specs=[pl.BlockSpec((None, 1, TOP_K * tm), lambda i, t: (i * nt + t, 0, 0), memory_space=pltpu.SMEM),
                  pl.BlockSpec((None, tm, d), lambda i, t: (i, t, 0)),
                  pl.BlockSpec((None, tm, TOP_K), lambda i, t: (i, t, 0)),
                  pl.BlockSpec((None, N_MOD, 2, 1, d), lambda i, t: (i, 0, 0, 0, 0)),
                  pl.BlockSpec(memory_space=pl.ANY)],
        out_specs=pl.BlockSpec((None, tm, d), lambda i, t: (i, t, 0)),
        out_shape=jax.ShapeDtypeStruct((b, s, d), F32),
        scratch_shapes=[pltpu.VMEM((TOP_K, tm, d), F32), pltpu.SemaphoreType.DMA],
        compiler_params=_params(("arbitrary", "arbitrary")),
        name="moe_combine",
    )(dest.reshape(b * nt, 1, TOP_K * tm), x1, gates.reshape(b, s, TOP_K), modp, ys)


def moe_residual(x1, h2, logits, modp, w1, w3, w2):
    b, s, d = x1.shape
    n_tok = b * s
    tm = MOE_TILE_M
    gates, dest, tile_expert, n_valid = route(logits.reshape(n_tok, N_EXPERTS), tm)
    n_slots = (n_tok * TOP_K // tm + N_EXPERTS) * tm
    xs = moe_dispatch(h2.reshape(n_tok, d), dest, n_slots)
    ys = grouped_swiglu(xs, tile_expert, n_valid, w1, w3, w2, tm, MOE_TILE_F)
    return moe_combine(x1, gates, dest, ys, modp)


def _proj1_kernel(x_ref, g_ref, mod_ref, w_ref, qan_ref, qb_ref, kvan_ref, kbk_ref, kbv_ref,
                  cos_ref, sin_ref, qg_ref, kg_ref, hy_ref, q_ref, k_ref, v_ref):
    tm = x_ref.shape[0]
    row0 = pl.program_id(1) * tm
    h = _norm_mod(x_ref[...], g_ref[...], mod_ref, row0, 0, 1)
    z = jnp.dot(h.astype(BF16), w_ref[...], preferred_element_type=F32)
    hy_ref[...] = z[:, :HY_IN]

    def rms(a, g):
        return a * lax.rsqrt(jnp.mean(a * a, axis=-1, keepdims=True) + EPS) * g

    c_q, c_kv = HY_IN, HY_IN + MLA_Q_RANK
    c_r = c_kv + MLA_KV_RANK
    qa = rms(z[:, c_q:c_kv], qan_ref[...]).astype(BF16)
    kva = rms(z[:, c_kv:c_r], kvan_ref[...]).astype(BF16)
    q = jnp.dot(qa, qb_ref[...], preferred_element_type=F32)
    kk = jnp.dot(kva, kbk_ref[...], preferred_element_type=F32)
    v_ref[...] = jnp.dot(kva, kbv_ref[...], preferred_element_type=F32).astype(BF16)
    k_rope = pltpu.roll(z[:, c_r:c_r + LANES], MLA_NOPE, 1)
    cos, sin = cos_ref[...], sin_ref[...]
    lane = lax.broadcasted_iota(jnp.int32, (tm, LANES), 1)
    first_half = lane < MLA_NOPE + MLA_ROPE // 2

    def head(a, gain):
        inv = lax.rsqrt(jnp.sum(a * a, axis=-1, keepdims=True) / MLA_QK + EPS)
        return _rope_pair(a * inv * gain, cos, sin, first_half, MLA_ROPE // 2)

    for hh in range(MLA_HEADS):
        sl = slice(hh * LANES, (hh + 1) * LANES)
        q_ref[:, sl] = head(q[:, sl], qg_ref[...]).astype(BF16)
        k_ref[:, sl] = head(kk[:, sl] + k_rope, kg_ref[...]).astype(BF16)


def proj1(xa, g, modp, w_in, qan, qb, kvan, kbk, kbv, cos, sin, q_gain, k_gain):
    b = xa.shape[0]
    tm = TOKEN_TILE
    tok = lambda width: pl.BlockSpec((None, tm, width), lambda i, t: (i, t, 0))
    full = lambda a: pl.BlockSpec(a.shape, lambda i, t: (0, 0))
    hw = MLA_HEADS * LANES
    return pl.pallas_call(
        _proj1_kernel,
        grid=(b, S_ALL // tm),
        in_specs=[tok(D_MODEL), full(g),
                  pl.BlockSpec((None, N_MOD, 2, 1, D_MODEL), lambda i, t: (i, 0, 0, 0, 0)),
                  full(w_in), full(qan), full(qb), full(kvan), full(kbk), full(kbv),
                  pl.BlockSpec((tm, LANES), lambda i, t: (t, 0)),
                  pl.BlockSpec((tm, LANES), lambda i, t: (t, 0)),
                  full(q_gain), full(k_gain)],
        out_specs=[tok(HY_IN), tok(hw), tok(hw), tok(MLA_HEADS * MLA_V)],
        out_shape=[jax.ShapeDtypeStruct((b, S_ALL, HY_IN), F32),
                   jax.ShapeDtypeStruct((b, S_ALL, hw), BF16),
                   jax.ShapeDtypeStruct((b, S_ALL, hw), BF16),
                   jax.ShapeDtypeStruct((b, S_ALL, MLA_HEADS * MLA_V), BF16)],
        compiler_params=_params(("parallel", "parallel")),
        name="proj1",
    )(xa, g, modp, w_in, qan, qb, kvan, kbk, kbv, cos, sin, q_gain, k_gain)


def _merge1_kernel(hy_ref, att_ref, x_ref, mod_ref, g2_ref, w_hy_ref, w_att_ref, r_ref,
                   x1_ref, h2_ref, lg_ref):
    y = (jnp.dot(hy_ref[...].astype(BF16), w_hy_ref[...], preferred_element_type=F32)
         + jnp.dot(att_ref[...], w_att_ref[...], preferred_element_type=F32))
    x1 = x_ref[...] + mod_ref[2, 0] * y
    x1_ref[...] = x1
    h2 = _norm_mod(x1, g2_ref[...], mod_ref, 0, 3, 4)
    h2_ref[...] = h2
    lg_ref[...] = jnp.dot(h2, r_ref[...], preferred_element_type=F32, precision=lax.Precision.HIGHEST)


def merge1(hy, att, xa, modp, g2, w_hy, w_att, router):
    b = xa.shape[0]
    tm = MERGE1_TILE
    tok = lambda width: pl.BlockSpec((None, tm, width), lambda i, t: (i, t, 0))
    full = lambda a: pl.BlockSpec(a.shape, lambda i, t: (0, 0))
    return pl.pallas_call(
        _merge1_kernel,
        grid=(b, SEQ // tm),
        in_specs=[tok(HY_WIDTH), tok(HALF_MIX), tok(D_MODEL),
                  pl.BlockSpec((None, N_MOD, 2, 1, D_MODEL), lambda i, t: (i, 0, 0, 0, 0)),
                  full(g2), full(w_hy), full(w_att), full(router)],
        out_specs=[tok(D_MODEL), tok(D_MODEL), tok(N_EXPERTS)],
        out_shape=[jax.ShapeDtypeStruct((b, SEQ, D_MODEL), F32),
                   jax.ShapeDtypeStruct((b, SEQ, D_MODEL), F32),
                   jax.ShapeDtypeStruct((b, SEQ, N_EXPERTS), F32)],
        compiler_params=_params(("parallel", "parallel")),
        name="merge1",
    )(hy, att, xa, modp, g2, w_hy, w_att, router)


HY_N = 2 * SEQ
HY_TW_ROWS = 256
HY_SPEC_FBLK = 256
HY_CONV_FBLK = 256
HY_LANE_TILES = SEQ // LANES


def _twiddle_kernel(ca_ref, sa_ref, cb_ref, sb_ref, fre_ref, fim_ref, ic_ref, is_ref):
    rows = fre_ref.shape[0]
    r = pl.program_id(0) * rows + lax.broadcasted_iota(jnp.int32, (rows, LANES), 0)
    alt_r = (1 - 2 * (r & 1)).astype(F32)
    cb, sb = cb_ref[...], sb_ref[...]
    for a in range(HY_LANE_TILES):
        ca, sa = ca_ref[:, a:a + 1], sa_ref[:, a:a + 1]
        c = ca * cb - sa * sb
        s = sa * cb + ca * sb
        col = a * LANES + lax.broadcasted_iota(jnp.int32, (rows, LANES), 1)
        alt_c = (1 - 2 * (col & 1)).astype(F32)
        w = jnp.where(col == 0, 1.0 / HY_N, 2.0 / HY_N)
        sl = slice(a * LANES, (a + 1) * LANES)
        fre_ref[:, sl] = c.astype(BF16)
        fim_ref[:, sl] = jnp.where(r == 0, alt_c, -s).astype(BF16)
        ic_ref[:, sl] = (c * w).astype(BF16)
        is_ref[:, sl] = jnp.where(col == 0, alt_r / HY_N, -s * w).astype(BF16)


def dft_matrices():
    idx = jnp.arange(SEQ, dtype=jnp.int32)[:, None]
    step = 2.0 * math.pi / HY_N
    ph_a = ((idx * (LANES * jnp.arange(HY_LANE_TILES, dtype=jnp.int32))[None, :]) % HY_N).astype(F32) * step
    ph_b = ((idx * jnp.arange(LANES, dtype=jnp.int32)[None, :]) % HY_N).astype(F32) * step
    rows = HY_TW_ROWS
    tab = lambda width: pl.BlockSpec((rows, width), lambda i: (i, 0))
    out = jax.ShapeDtypeStruct((SEQ, SEQ), BF16)
    return pl.pallas_call(
        _twiddle_kernel,
        grid=(SEQ // rows,),
        in_specs=[tab(HY_LANE_TILES), tab(HY_LANE_TILES), tab(LANES), tab(LANES)],
        out_specs=[tab(SEQ)] * 4,
        out_shape=[out] * 4,
        compiler_params=_params(("parallel",)),
        name="twiddle",
    )(jnp.cos(ph_a), jnp.sin(ph_a), jnp.cos(ph_b), jnp.sin(ph_b))


def _spec_kernel(fre_ref, fim_ref, h_ref, k_ref):
    w = HY_WIDTH
    h = h_ref[...].astype(BF16)
    re = jnp.dot(fre_ref[...], h, preferred_element_type=F32)
    im = jnp.dot(fim_ref[...], h, preferred_element_type=F32)
    hb0 = h[0:1, w:].astype(F32)
    first = (pl.program_id(1) * re.shape[0] + lax.broadcasted_iota(jnp.int32, (re.shape[0], 1), 0)) == 0
    k_ref[0] = re[:, :w] + re[:, w:] - hb0
    k_ref[1] = jnp.where(first, im[:, :w] + im[:, w:] - hb0, im[:, :w] - im[:, w:])


def filter_spectra(filt, fre, fim):
    fb = HY_SPEC_FBLK
    return pl.pallas_call(
        _spec_kernel,
        grid=(HY_ORDER, SEQ // fb),
        in_specs=[pl.BlockSpec((fb, SEQ), lambda o, f: (f, 0)),
                  pl.BlockSpec((fb, SEQ), lambda o, f: (f, 0)),
                  pl.BlockSpec((SEQ, 2 * HY_WIDTH), lambda o, f: (0, o))],
        out_specs=pl.BlockSpec((None, 2, fb, HY_WIDTH), lambda o, f: (o, 0, f, 0)),
        out_shape=jax.ShapeDtypeStruct((HY_ORDER, 2, SEQ, HY_WIDTH), F32),
        compiler_params=_params(("parallel", "parallel")),
        name="filter_spectra",
    )(fre, fim, filt)


def _short_conv(z_ref, w_ref, b_ref, part):
    z = z_ref[...]
    l = z.shape[0]
    t = lax.broadcasted_iota(jnp.int32, (l, 1), 0)
    prev = jnp.where(t == 0, 0.0, pltpu.roll(z, 1, 0))
    nxt = jnp.where(t == l - 1, 0.0, pltpu.roll(z, l - 1, 0))
    cs = slice(part * HY_WIDTH, (part + 1) * HY_WIDTH)
    return (w_ref[0:1, cs] * prev + w_ref[1:2, cs] * z + w_ref[2:3, cs] * nxt) + b_ref[:, cs]


def _hyconv_kernel(u_ref, xg_ref, cw_ref, cb_ref, fre_ref, fim_ref, ic_ref, is_ref, k_ref, skip_ref,
                   y_ref, ub_ref, acc_ref, *, order):
    f = pl.program_id(1)

    def u_f32():
        return _short_conv(u_ref, cw_ref, cb_ref, 0) if order == 0 else u_ref[...]

    @pl.when(f == 0)
    def _():
        ub_ref[...] = u_f32().astype(BF16)

    x_re = jnp.dot(fre_ref[...], ub_ref[...], preferred_element_type=F32)
    x_im = jnp.dot(fim_ref[...], ub_ref[...], preferred_element_type=F32)
    k_re, k_im = k_ref[0], k_ref[1]
    first = (f * x_re.shape[0] + lax.broadcasted_iota(jnp.int32, (x_re.shape[0], 1), 0)) == 0
    y_re = x_re * k_re - jnp.where(first, 0.0, x_im * k_im)
    y_im = x_im * jnp.where(first, k_im, k_re) + jnp.where(first, 0.0, x_re * k_im)
    part = (jnp.dot(ic_ref[...], y_re.astype(BF16), preferred_element_type=F32)
            + jnp.dot(is_ref[...], y_im.astype(BF16), preferred_element_type=F32))
    _accumulate(acc_ref, part, f)

    @pl.when(f == pl.num_programs(1) - 1)
    def _():
        xg = _short_conv(xg_ref, cw_ref, cb_ref, order + 1)
        y_ref[...] = xg * (acc_ref[...] + skip_ref[order:order + 1, :] * u_f32())


def hyena_conv(order, u, hyz, conv_w, conv_b, fre, fim, ic, is_, spectra, skip):
    b = hyz.shape[0]
    fb = HY_CONV_FBLK
    w = HY_WIDTH
    lat = lambda part: pl.BlockSpec((None, SEQ, w), lambda i, f: (i, 0, part))
    full = lambda a: pl.BlockSpec(a.shape, lambda i, f: (0, 0))
    return pl.pallas_call(
        functools.partial(_hyconv_kernel, order=order),
        grid=(b, SEQ // fb),
        in_specs=[lat(0), lat(order + 1), full(conv_w), full(conv_b),
                  pl.BlockSpec((fb, SEQ), lambda i, f: (f, 0)),
                  pl.BlockSpec((fb, SEQ), lambda i, f: (f, 0)),
                  pl.BlockSpec((SEQ, fb), lambda i, f: (0, f)),
                  pl.BlockSpec((SEQ, fb), lambda i, f: (0, f)),
                  pl.BlockSpec((None, 2, fb, w), lambda i, f: (order, 0, f, 0)),
                  full(skip)],
        out_specs=pl.BlockSpec((None, SEQ, w), lambda i, f: (i, 0, 0)),
        out_shape=jax.ShapeDtypeStruct((b, SEQ, w), F32),
        scratch_shapes=[pltpu.VMEM((SEQ, w), BF16), pltpu.VMEM((SEQ, w), F32)],
        compiler_params=_params(("parallel", "arbitrary")),
        name="hyena_conv",
    )(u, hyz, conv_w, conv_b, fre, fim, ic, is_, spectra, skip)


def hyena_filters(length, w1, b1, w2, b2, w3, freq):
    hp = lax.Precision.HIGHEST
    t = jnp.arange(length, dtype=F32)[:, None]
    t_norm = t / max(length - 1, 1)
    bands = jnp.linspace(1e-4, HY_BANDS - 1, HY_BANDS, dtype=F32)
    ang = 2.0 * math.pi * t * bands / length
    z = jnp.concatenate([t_norm, jnp.cos(ang), -jnp.sin(ang)], axis=-1)
    h = jnp.sin(freq * (jnp.dot(z, w1, precision=hp) + b1))
    h = jnp.sin(freq * (jnp.dot(h, w2, precision=hp) + b2))
    h = jnp.dot(h, w3, precision=hp)
    deltas = jnp.abs(jnp.linspace(HY_MIN_DECAY, HY_MAX_DECAY, HY_WIDTH, dtype=F32))
    window = jnp.exp(-t_norm * deltas) + HY_SHIFT
    return h.reshape(length, HY_ORDER, 2, HY_WIDTH) * window[:, None, None, :]


def hyena(hyz, conv_w, conv_b, fw1, fb1, fw2, fb2, fw3, freq, skip):
    fre, fim, ic, is_ = dft_matrices()
    filt = hyena_filters(SEQ, fw1, fb1, fw2, fb2, fw3, freq).reshape(SEQ, HY_ORDER * 2 * HY_WIDTH)
    spectra = filter_spectra(filt, fre, fim)
    y = hyz
    for o in range(HY_ORDER):
        y = hyena_conv(o, y, hyz, conv_w, conv_b.reshape(1, -1), fre, fim, ic, is_, spectra, skip)
    return y


def _grid_angles(rot_dim):
    n_freq = rot_dim // 4
    inv_freq = ROPE_THETA ** (-jnp.arange(n_freq, dtype=F32) / n_freq)
    t = jnp.arange(SEQ)
    r = (t // GRID_W).astype(F32)
    c_ = (t % GRID_W).astype(F32)
    return jnp.concatenate([r[:, None] * inv_freq, c_[:, None] * inv_freq], axis=-1)


def _rope_tables(rot_dim, lane_offsets):
    ang = _grid_angles(rot_dim)
    c, s = jnp.cos(ang), jnp.sin(ang)
    half = rot_dim // 2
    cos = jnp.ones((S_ALL, LANES), F32)
    sin = jnp.zeros((S_ALL, LANES), F32)
    for off in lane_offsets:
        cos = cos.at[:SEQ, off:off + half].set(c).at[:SEQ, off + half:off + rot_dim].set(c)
        sin = sin.at[:SEQ, off:off + half].set(-s).at[:SEQ, off + half:off + rot_dim].set(s)
    return cos, sin


def _block_diag(w):
    nb, bs, _ = w.shape
    eye = jnp.eye(nb, dtype=w.dtype)
    return (eye[:, None, :, None] * w[:, :, None, :]).reshape(nb * bs, nb * bs)


def _gqa_pair_order():
    g = GQA_Q_HEADS // GQA_KV_HEADS
    heads = []
    for p in range(g):
        heads += [p, g + p]
    return np.concatenate([np.arange(h * HEAD_DIM, (h + 1) * HEAD_DIM) for h in heads])


def _pad_heads(w, n_heads, width):
    k = w.shape[0]
    w = w.reshape(k, n_heads, width)
    return jnp.pad(w, ((0, 0), (0, 0), (0, LANES - width))).reshape(k, n_heads * LANES)


def kernel(x, c, ctx, c_ctx, mod_w, mod_b, norm1_g, norm2_g, ab_w_in, ab_w_out, lru_conv_w, lru_conv_b, lru_w_a, lru_b_a, lru_w_x, lru_b_x, lru_lambda, gqa_q_norm, gqa_k_norm, ffn_w1, ffn_w3, ffn_w2, cd_w_in, cd_w_out, hy_conv_w, hy_conv_b, hy_filt_w1, hy_filt_b1, hy_filt_w2, hy_filt_b2, hy_filt_w3, hy_sin_freq, hy_skip, mla_q_a_norm, mla_q_b, mla_kv_a_norm, mla_kv_b, mla_q_norm, mla_k_norm, moe_router, moe_w1, moe_w3, moe_w2):
    batch = x.shape[0]
    bf = lambda w: w.astype(BF16)
    row = lambda v: v.reshape(1, -1)

    silu_all = jnp.concatenate([jax.nn.silu(c), jax.nn.silu(c_ctx)[None, :],
                                jnp.zeros((16 - batch - 1, D_MODEL), F32)], axis=0)
    mods = []
    for layer in range(DEPTH):
        m = matmul(silu_all, mod_w[layer], 16, 1536) + mod_b[layer]
        lat = m[:batch].reshape(batch, N_MOD, 1, D_MODEL)
        cx = jnp.broadcast_to(m[batch].reshape(1, N_MOD, 1, D_MODEL), lat.shape)
        mods.append(jnp.stack([lat, cx], axis=2))

    xa = jnp.concatenate([x, ctx], axis=1)

    perm = _gqa_pair_order()
    q0 = 2 * LRU_WIDTH
    w_in0 = ab_w_in[0]
    w_in0 = jnp.concatenate([w_in0[:, :q0], w_in0[:, q0 + perm], w_in0[:, q0 + GQA_Q_HEADS * HEAD_DIM:]], axis=1)
    cos_g, sin_g = _rope_tables(HEAD_DIM, (0, HEAD_DIM))
    q_gain = row(jnp.tile(gqa_q_norm[0], 2) * HEAD_DIM ** -0.5)
    k_gain = row(jnp.tile(gqa_k_norm[0], 2))
    xr, gate, q, k, v = proj0(xa, row(norm1_g[0]), mods[0], bf(w_in0), cos_g, sin_g, q_gain, k_gain)

    w_gates = jnp.concatenate([_block_diag(lru_w_a[0, 0]), _block_diag(lru_w_x[0, 0]),
                               _block_diag(lru_w_a[0, 1]), _block_diag(lru_w_x[0, 1])], axis=1)
    b_gates = jnp.concatenate([lru_b_a[0, 0].reshape(-1), lru_b_x[0, 0].reshape(-1),
                               lru_b_a[0, 1].reshape(-1), lru_b_x[0, 1].reshape(-1)])
    rec = rglru(xr, lru_conv_w[0], row(lru_conv_b[0]), bf(w_gates), row(b_gates), lru_lambda[0])

    att_l = attention_latent(q, k, v, GQA_HEADS)
    att_c = attention_context(q, k, v, GQA_HEADS)
    w_out0 = ab_w_out[0]
    x1, h2 = merge0(rec, gate, att_l, att_c, xa, mods[0], row(norm2_g[0]),
                    bf(w_out0[:LRU_WIDTH]), bf(w_out0[LRU_WIDTH:][perm]))
    xa = ffn_residual(h2, x1, mods[0], bf(ffn_w1[0]), bf(ffn_w3[0]), bf(ffn_w2[0]))

    w_in1 = jnp.pad(cd_w_in[0], ((0, 0), (0, LANES - MLA_ROPE)))
    cos_m, sin_m = _rope_tables(MLA_ROPE, (MLA_NOPE,))
    qb = _pad_heads(mla_q_b[0], MLA_HEADS, MLA_QK)
    kvb = mla_kv_b[0].reshape(MLA_KV_RANK, MLA_HEADS, MLA_NOPE + MLA_V)
    kbk = _pad_heads(kvb[:, :, :MLA_NOPE].reshape(MLA_KV_RANK, -1), MLA_HEADS, MLA_NOPE)
    kbv = kvb[:, :, MLA_NOPE:].reshape(MLA_KV_RANK, -1)
    pad_gain = lambda g_: row(jnp.pad(g_, (0, LANES - MLA_QK)))
    hyz, q, k, v = proj1(xa, row(norm1_g[1]), mods[1], bf(w_in1), row(mla_q_a_norm[0]), bf(qb),
                         row(mla_kv_a_norm[0]), bf(kbk), bf(kbv), cos_m, sin_m,
                         pad_gain(mla_q_norm[0] * MLA_QK ** -0.5), pad_gain(mla_k_norm[0]))
    att = attention_latent(q, k, v, MLA_HEAD_GROUPS)
    hy = hyena(hyz, hy_conv_w[0], hy_conv_b[0], hy_filt_w1[0], hy_filt_b1[0], hy_filt_w2[0],
               hy_filt_b2[0], hy_filt_w3[0], hy_sin_freq[0], hy_skip[0])
    w_out1 = cd_w_out[0]
    x1, h2, logits = merge1(hy, att, xa, mods[1], row(norm2_g[1]),
                            bf(w_out1[:HY_WIDTH]), bf(w_out1[HY_WIDTH:]), moe_router[0])
    return moe_residual(x1, h2, logits, mods[1], bf(moe_w1[0]), bf(moe_w3[0]), bf(moe_w2[0]))
```

```python
import functools
import math

import jax
import jax.numpy as jnp
import numpy as np
from jax import lax
from jax.experimental import pallas as pl
from jax.experimental.pallas import tpu as pltpu

F32 = jnp.float32
BF16 = jnp.bfloat16

D_MODEL = 1024
SEQ = 2048
CTX_LEN = 256
S_ALL = SEQ + CTX_LEN
DEPTH = 2
GRID_W = 64
HEAD_DIM = 64
HALF_MIX = D_MODEL // 2
ROPE_THETA = 10000.0
EPS = 1e-6
N_MOD = 6
LRU_WIDTH = HALF_MIX
LRU_BLOCKS = LRU_WIDTH // HEAD_DIM
LRU_CONV = 4
LRU_C = 8.0
GQA_Q_HEADS = HALF_MIX // HEAD_DIM
GQA_KV_HEADS = 2
HY_WIDTH = HALF_MIX
HY_ORDER = 2
HY_CONV = 3
HY_BANDS = 16
HY_TARGET = 1e-2
HY_FAST_PCT = 0.3
HY_SLOW_PCT = 1.5
HY_MIN_DECAY = math.log(HY_TARGET) / HY_SLOW_PCT
HY_MAX_DECAY = math.log(HY_TARGET) / HY_FAST_PCT
HY_SHIFT = 0.05
HY_IN = (HY_ORDER + 1) * HY_WIDTH
MLA_HEADS = HALF_MIX // HEAD_DIM
MLA_Q_RANK = D_MODEL // 4
MLA_KV_RANK = D_MODEL // 8
MLA_NOPE = HEAD_DIM
MLA_ROPE = HEAD_DIM // 2
MLA_V = HEAD_DIM
MLA_QK = MLA_NOPE + MLA_ROPE
N_EXPERTS = 8
TOP_K = 2

LANES = 128
SUBLANES = 8
V7X_VMEM_LIMIT_BYTES = 56 * 1024 * 1024

TOKEN_TILE = 768
EPILOGUE_ROWS = 64
FFN_TILE_M = 768
FFN_TILE_F = 256
ATTN_TILE_Q = 512
MERGE1_TILE = 512
MOE_TILE_M = 512
MOE_TILE_F = 1792
MOE_ROUTE_BLOCK = 256
MOE_DISPATCH_TILE = 256
MOE_COMBINE_TILE = 256

LRU_GAP = SUBLANES
LRU_LAT0 = CTX_LEN + LRU_GAP
LRU_CHUNK = 256
LRU_ROWS = 2560
LRU_PAD_FRONT = SUBLANES

assert S_ALL % TOKEN_TILE == 0 and S_ALL % FFN_TILE_M == 0 and SEQ % ATTN_TILE_Q == 0
assert LRU_ROWS % LRU_CHUNK == 0 and LRU_ROWS >= LRU_LAT0 + SEQ + SUBLANES


def _params(sem):
    return pltpu.CompilerParams(dimension_semantics=sem, vmem_limit_bytes=V7X_VMEM_LIMIT_BYTES)


def _norm_mod(x, g, mod_pair, row0, shift_idx, scale_idx):
    y = x * lax.rsqrt(jnp.mean(x * x, axis=-1, keepdims=True) + EPS) * g
    shift = _mod_rows(mod_pair, shift_idx, row0, x.shape[0])
    scale = _mod_rows(mod_pair, scale_idx, row0, x.shape[0])
    return y * (1.0 + scale) + shift


def _mod_rows(mod_pair, idx, row0, rows):
    row = row0 + lax.broadcasted_iota(jnp.int32, (rows, 1), 0)
    return jnp.where(row >= SEQ, mod_pair[idx, 1], mod_pair[idx, 0])


def _mm_kernel(x_ref, w_ref, o_ref):
    o_ref[...] = jnp.dot(x_ref[...].astype(BF16), w_ref[...].astype(BF16),
                         preferred_element_type=F32)


def matmul(x, w, tm, tn):
    m, k = x.shape
    n = w.shape[1]
    assert m % tm == 0 and n % tn == 0
    return pl.pallas_call(
        _mm_kernel,
        grid=(m // tm, n // tn),
        in_specs=[pl.BlockSpec((tm, k), lambda i, j: (i, 0)),
                  pl.BlockSpec((k, tn), lambda i, j: (0, j))],
        out_specs=pl.BlockSpec((tm, tn), lambda i, j: (i, j)),
        out_shape=jax.ShapeDtypeStruct((m, n), F32),
        compiler_params=_params(("parallel", "parallel")),
        name="matmul",
    )(x, w)


def _rope_pair(a, cos, sin, first_half, shift):
    rot = jnp.where(first_half, pltpu.roll(a, LANES - shift, 1), pltpu.roll(a, shift, 1))
    return a * cos + rot * sin


def _proj0_kernel(x_ref, g_ref, mod_ref, w_ref, cos_ref, sin_ref, qg_ref, kg_ref,
                  xr_ref, gate_ref, q_ref, k_ref, v_ref, qk_ref):
    tm = x_ref.shape[0]
    row0 = pl.program_id(1) * tm
    h = _norm_mod(x_ref[...], g_ref[...], mod_ref, row0, 0, 1)
    z = jnp.dot(h.astype(BF16), w_ref[...], preferred_element_type=F32)
    xr_ref[...] = z[:, :LRU_WIDTH]
    gate_ref[...] = z[:, LRU_WIDTH:2 * LRU_WIDTH]
    q0 = 2 * LRU_WIDTH
    n_q = GQA_Q_HEADS // 2
    qk_ref[...] = z[:, q0:q0 + (n_q + 1) * LANES]
    v_ref[...] = z[:, q0 + (n_q + 1) * LANES:q0 + (n_q + 2) * LANES].astype(BF16)

    rc = EPILOGUE_ROWS
    lane = lax.broadcasted_iota(jnp.int32, (rc, LANES), 1)
    lo = lane < HEAD_DIM
    first_half = (lane & (HEAD_DIM // 2)) == 0

    def chunk(c, carry):
        rows = pl.ds(pl.multiple_of(c * rc, rc), rc)
        cos, sin = cos_ref[rows, :], sin_ref[rows, :]

        def head_pair(a, gain):
            sq = a * a
            ss_lo = jnp.sum(jnp.where(lo, sq, 0.0), axis=-1, keepdims=True)
            ss_hi = jnp.sum(jnp.where(lo, 0.0, sq), axis=-1, keepdims=True)
            inv = jnp.where(lo, lax.rsqrt(ss_lo / HEAD_DIM + EPS), lax.rsqrt(ss_hi / HEAD_DIM + EPS))
            return _rope_pair(a * inv * gain, cos, sin, first_half, HEAD_DIM // 2).astype(BF16)

        for p in range(n_q):
            sl = slice(p * LANES, (p + 1) * LANES)
            q_ref[rows, sl] = head_pair(qk_ref[rows, sl], qg_ref[...])
        k_ref[rows, :] = head_pair(qk_ref[rows, n_q * LANES:(n_q + 1) * LANES], kg_ref[...])
        return carry

    lax.fori_loop(0, tm // rc, chunk, 0)


def proj0(xa, g, modp, w_in, cos, sin, q_gain, k_gain):
    b = xa.shape[0]
    tm = TOKEN_TILE
    n = w_in.shape[1]
    tok = lambda width: pl.BlockSpec((None, tm, width), lambda i, t: (i, t, 0))
    full = lambda r, c_: pl.BlockSpec((r, c_), lambda i, t: (0, 0))
    return pl.pallas_call(
        _proj0_kernel,
        grid=(b, S_ALL // tm),
        in_specs=[tok(D_MODEL), full(1, D_MODEL),
                  pl.BlockSpec((None, N_MOD, 2, 1, D_MODEL), lambda i, t: (i, 0, 0, 0, 0)),
                  full(D_MODEL, n),
                  pl.BlockSpec((tm, LANES), lambda i, t: (t, 0)),
                  pl.BlockSpec((tm, LANES), lambda i, t: (t, 0)),
                  full(1, LANES), full(1, LANES)],
        out_specs=[tok(LRU_WIDTH), tok(LRU_WIDTH), tok(GQA_Q_HEADS * HEAD_DIM), tok(LANES), tok(LANES)],
        out_shape=[jax.ShapeDtypeStruct((b, S_ALL, LRU_WIDTH), F32),
                   jax.ShapeDtypeStruct((b, S_ALL, LRU_WIDTH), F32),
                   jax.ShapeDtypeStruct((b, S_ALL, GQA_Q_HEADS * HEAD_DIM), BF16),
                   jax.ShapeDtypeStruct((b, S_ALL, LANES), BF16),
                   jax.ShapeDtypeStruct((b, S_ALL, LANES), BF16)],
        scratch_shapes=[pltpu.VMEM((tm, (GQA_Q_HEADS // 2 + 1) * LANES), F32)],
        compiler_params=_params(("parallel", "parallel")),
        name="proj0",
    )(xa, g, modp, w_in, cos, sin, q_gain, k_gain)


def _lru_kernel(xr_ref, cw_ref, cb_ref, wg_ref, bg_ref, lam_ref, rec_ref,
                pad_ref, af_ref, hf_ref, ab_ref, hb_ref):
    w = LRU_WIDTH
    pad_ref[...] = jnp.zeros_like(pad_ref)
    pad_ref[LRU_PAD_FRONT:LRU_PAD_FRONT + CTX_LEN, :] = xr_ref[SEQ:S_ALL, :]
    pad_ref[LRU_PAD_FRONT + LRU_LAT0:LRU_PAD_FRONT + LRU_LAT0 + SEQ, :] = xr_ref[0:SEQ, :]
    lam = lam_ref[...]
    neg_c_softplus = -LRU_C * (jnp.maximum(-lam, 0.0) + jnp.log(1.0 + jnp.exp(-jnp.abs(lam))))
    left = LRU_CONV // 2
    for ch in range(LRU_ROWS // LRU_CHUNK):
        r0 = ch * LRU_CHUNK
        u = cb_ref[...]
        for j in range(LRU_CONV):
            start = LRU_PAD_FRONT + r0 - left + j
            u = u + cw_ref[j:j + 1, :] * pad_ref[start:start + LRU_CHUNK, :]
        gz = jnp.dot(u.astype(BF16), wg_ref[...], preferred_element_type=F32) + bg_ref[...]
        for d, (a_ref, h_ref) in enumerate(((af_ref, hf_ref), (ab_ref, hb_ref))):
            r = jax.nn.sigmoid(gz[:, 2 * d * w:(2 * d + 1) * w])
            i = jax.nn.sigmoid(gz[:, (2 * d + 1) * w:(2 * d + 2) * w])
            a = jnp.exp(neg_c_softplus[d:d + 1, :] * r)
            a_ref[r0:r0 + LRU_CHUNK, :] = a
            h_ref[r0:r0 + LRU_CHUNK, :] = jnp.sqrt(1.0 - a * a) * (i * u)

    n_ctx_groups = CTX_LEN // SUBLANES

    def group(tt, carry):
        hf, hb = carry
        is_lat = tt >= n_ctx_groups
        base_f = pl.multiple_of(tt * SUBLANES + jnp.where(is_lat, LRU_GAP, 0), SUBLANES)
        base_b = pl.multiple_of(
            jnp.where(is_lat, LRU_LAT0 + SEQ + CTX_LEN, CTX_LEN) - (tt + 1) * SUBLANES, SUBLANES)
        for i in range(SUBLANES):
            rf = pl.ds(base_f + i, 1)
            hf = af_ref[rf, :] * hf + hf_ref[rf, :]
            hf_ref[rf, :] = hf
            rb = pl.ds(base_b + (SUBLANES - 1 - i), 1)
            hb = ab_ref[rb, :] * hb + hb_ref[rb, :]
            hb_ref[rb, :] = hb
        return hf, hb

    zero = jnp.zeros((1, w), F32)
    lax.fori_loop(0, S_ALL // SUBLANES, group, (zero, zero))
    rec_ref[0:SEQ, :] = hf_ref[LRU_LAT0:LRU_LAT0 + SEQ, :] + hb_ref[LRU_LAT0:LRU_LAT0 + SEQ, :]
    rec_ref[SEQ:S_ALL, :] = hf_ref[0:CTX_LEN, :] + hb_ref[0:CTX_LEN, :]


def rglru(xr, conv_w, conv_b, w_gates, b_gates, lam):
    b = xr.shape[0]
    w = LRU_WIDTH
    full = lambda r, c_: pl.BlockSpec((r, c_), lambda i: (0, 0))
    rows = pltpu.VMEM((LRU_ROWS, w), F32)
    return pl.pallas_call(
        _lru_kernel,
        grid=(b,),
        in_specs=[pl.BlockSpec((None, S_ALL, w), lambda i: (i, 0, 0)),
                  full(LRU_CONV, w), full(1, w), full(w, 4 * w), full(1, 4 * w), full(2, w)],
        out_specs=pl.BlockSpec((None, S_ALL, w), lambda i: (i, 0, 0)),
        out_shape=jax.ShapeDtypeStruct((b, S_ALL, w), F32),
        scratch_shapes=[pltpu.VMEM((LRU_ROWS + 2 * LRU_PAD_FRONT, w), F32), rows, rows, rows, rows],
        compiler_params=_params(("parallel",)),
        name="rglru",
    )(xr, conv_w, conv_b, w_gates, b_gates, lam)


def _attn_kernel(q_ref, k_ref, v_ref, *rest, heads):
    o_ref = rest[-1]
    tq, sk = q_ref.shape[0], k_ref.shape[0]
    q_lo = lax.broadcasted_iota(jnp.int32, (tq, LANES), 1) < HEAD_DIM
    v_lo = lax.broadcasted_iota(jnp.int32, (sk, LANES), 1) < HEAD_DIM

    def keep(x, is_lo, half):
        if half is None:
            return x
        return jnp.where(is_lo if half == 0 else jnp.logical_not(is_lo), x, jnp.zeros_like(x))

    for og, members in enumerate(heads):
        acc = None
        for qg, qh, kg, vg, vh in members:
            q = keep(q_ref[:, qg * LANES:(qg + 1) * LANES], q_lo, qh)
            s = lax.dot_general(q, k_ref[:, kg * LANES:(kg + 1) * LANES],
                                (((1,), (1,)), ((), ())), preferred_element_type=F32)
            p = jnp.exp(s - jnp.max(s, axis=-1, keepdims=True))
            l = jnp.sum(p, axis=-1, keepdims=True)
            v = keep(v_ref[:, vg * LANES:(vg + 1) * LANES], v_lo, vh)
            o = jnp.dot(p.astype(BF16), v, preferred_element_type=F32) / l
            acc = o if acc is None else acc + o
        o_ref[:, og * LANES:(og + 1) * LANES] = acc.astype(o_ref.dtype)


def attention(q, k, v, heads, q_block, n_q_blocks, tq, k_rows):
    b, s, qw = q.shape
    kw, vw = k.shape[-1], v.shape[-1]
    ow = len(heads) * LANES
    kv_spec = lambda width: pl.BlockSpec((None, k_rows, width), lambda i, t: (i, s // k_rows - 1, 0))
    return pl.pallas_call(
        functools.partial(_attn_kernel, heads=heads),
        grid=(b, n_q_blocks),
        in_specs=[pl.BlockSpec((None, tq, qw), lambda i, t: (i, q_block + t, 0)), kv_spec(kw), kv_spec(vw)],
        out_specs=pl.BlockSpec((None, tq, ow), lambda i, t: (i, t, 0)),
        out_shape=jax.ShapeDtypeStruct((b, n_q_blocks * tq, ow), BF16),
        compiler_params=_params(("parallel", "parallel")),
        name="attention",
    )(q, k, v)


GQA_HEADS = tuple(((p, 0, 0, 0, 0), (p, 1, 0, 0, 1)) for p in range(GQA_Q_HEADS // 2))
MLA_HEAD_GROUPS = tuple(((2 * p, None, 2 * p, p, 0), (2 * p + 1, None, 2 * p + 1, p, 1))
                        for p in range(MLA_HEADS // 2))


def attention_latent(q, k, v, heads):
    return attention(q, k, v, heads, 0, SEQ // ATTN_TILE_Q, ATTN_TILE_Q, S_ALL)


def attention_context(q, k, v, heads):
    return attention(q, k, v, heads, SEQ // CTX_LEN, 1, CTX_LEN, CTX_LEN)


def _merge0_kernel(rec_ref, gate_ref, att_l_ref, att_c_ref, x_ref, mod_ref, g2_ref,
                   w_rec_ref, w_att_ref, x1_ref, h2_ref):
    tm = x_ref.shape[0]
    row0 = pl.program_id(1) * tm
    rg = (rec_ref[...] * jax.nn.gelu(gate_ref[...])).astype(BF16)
    att = jnp.where(row0 >= SEQ, att_c_ref[...], att_l_ref[...])
    y = (jnp.dot(rg, w_rec_ref[...], preferred_element_type=F32)
         + jnp.dot(att, w_att_ref[...], preferred_element_type=F32))
    x1 = x_ref[...] + _mod_rows(mod_ref, 2, row0, tm) * y
    x1_ref[...] = x1
    h2_ref[...] = _norm_mod(x1, g2_ref[...], mod_ref, row0, 3, 4).astype(BF16)


def merge0(rec, gate, att_l, att_c, xa, modp, g2, w_rec, w_att):
    b = xa.shape[0]
    tm = CTX_LEN
    n_lat = SEQ // tm
    tok = lambda width: pl.BlockSpec((None, tm, width), lambda i, t: (i, t, 0))
    full = lambda r, c_: pl.BlockSpec((r, c_), lambda i, t: (0, 0))
    return pl.pallas_call(
        _merge0_kernel,
        grid=(b, S_ALL // tm),
        in_specs=[tok(LRU_WIDTH), tok(LRU_WIDTH),
                  pl.BlockSpec((None, tm, HALF_MIX), lambda i, t: (i, jnp.minimum(t, n_lat - 1), 0)),
                  pl.BlockSpec((None, tm, HALF_MIX), lambda i, t: (i, 0, 0)),
                  tok(D_MODEL),
                  pl.BlockSpec((None, N_MOD, 2, 1, D_MODEL), lambda i, t: (i, 0, 0, 0, 0)),
                  full(1, D_MODEL), full(LRU_WIDTH, D_MODEL), full(HALF_MIX, D_MODEL)],
        out_specs=[tok(D_MODEL), tok(D_MODEL)],
        out_shape=[jax.ShapeDtypeStruct((b, S_ALL, D_MODEL), F32),
                   jax.ShapeDtypeStruct((b, S_ALL, D_MODEL), BF16)],
        compiler_params=_params(("parallel", "parallel")),
        name="merge0",
    )(rec, gate, att_l, att_c, xa, modp, g2, w_rec, w_att)


def _swiglu_partial(x, w1_ref, w3_ref, w2_ref):
    h1 = jnp.dot(x, w1_ref[...], preferred_element_type=F32)
    h3 = jnp.dot(x, w3_ref[...], preferred_element_type=F32)
    act = (h1 * jax.nn.sigmoid(h1) * h3).astype(BF16)
    return jnp.dot(act, w2_ref[...], preferred_element_type=F32)


def _accumulate(acc_ref, part, j):
    @pl.when(j == 0)
    def _():
        acc_ref[...] = part

    @pl.when(j > 0)
    def _():
        acc_ref[...] += part


def _ffn_kernel(h_ref, x1_ref, mod_ref, w1_ref, w3_ref, w2_ref, o_ref, act_ref):
    tm = h_ref.shape[0]
    x = h_ref[...]
    for c in range(w1_ref.shape[1] // FFN_TILE_F):
        sl = slice(c * FFN_TILE_F, (c + 1) * FFN_TILE_F)
        h1 = jnp.dot(x, w1_ref[:, sl], preferred_element_type=F32)
        h3 = jnp.dot(x, w3_ref[:, sl], preferred_element_type=F32)
        act_ref[:, sl] = (h1 * jax.nn.sigmoid(h1) * h3).astype(BF16)
    y = jnp.dot(act_ref[...], w2_ref[...], preferred_element_type=F32)
    o_ref[...] = x1_ref[...] + _mod_rows(mod_ref, 5, pl.program_id(1) * tm, tm) * y


def ffn_residual(h2, x1, modp, w1, w3, w2):
    b = x1.shape[0]
    tm = FFN_TILE_M
    f = w1.shape[1]
    assert f % FFN_TILE_F == 0
    tok = lambda: pl.BlockSpec((None, tm, D_MODEL), lambda i, t: (i, t, 0))
    resident = lambda a: pl.BlockSpec(a.shape, lambda i, t: (0, 0), pipeline_mode=pl.Buffered(1))
    return pl.pallas_call(
        _ffn_kernel,
        grid=(b, S_ALL // tm),
        in_specs=[tok(), tok(),
                  pl.BlockSpec((None, N_MOD, 2, 1, D_MODEL), lambda i, t: (i, 0, 0, 0, 0)),
                  resident(w1), resident(w3), resident(w2)],
        out_specs=tok(),
        out_shape=jax.ShapeDtypeStruct((b, S_ALL, D_MODEL), F32),
        scratch_shapes=[pltpu.VMEM((tm, f), BF16)],
        compiler_params=_params(("parallel", "parallel")),
        name="ffn",
    )(h2, x1, modp, w1, w3, w2)


def _moe_kernel(te_ref, nv_ref, x_ref, w1_ref, w3_ref, w2_ref, o_ref, xb_ref, acc_ref):
    i, j = pl.program_id(0), pl.program_id(1)
    last = pl.num_programs(1) - 1
    valid = i < nv_ref[0]

    @pl.when(valid)
    def _():
        @pl.when(j == 0)
        def _():
            xb_ref[...] = x_ref[...].astype(BF16)

        _accumulate(acc_ref, _swiglu_partial(xb_ref[...], w1_ref, w3_ref, w2_ref), j)

        @pl.when(j == last)
        def _():
            o_ref[...] = acc_ref[...]

    @pl.when(jnp.logical_and(jnp.logical_not(valid), j == last))
    def _():
        o_ref[...] = jnp.zeros_like(o_ref)


def grouped_swiglu(xs, tile_expert, n_valid, w1, w3, w2, tm, tf):
    n_rows, d = xs.shape
    f = w1.shape[-1]
    assert n_rows % tm == 0 and f % tf == 0
    n_f = f // tf

    def col(i, j, nv):
        return jnp.where(i < nv[0], j, n_f - 1)

    grid_spec = pltpu.PrefetchScalarGridSpec(
        num_scalar_prefetch=2,
        grid=(n_rows // tm, n_f),
        in_specs=[pl.BlockSpec((tm, d), lambda i, j, te, nv: (jnp.minimum(i, nv[0] - 1), 0)),
                  pl.BlockSpec((None, d, tf), lambda i, j, te, nv: (te[i], 0, col(i, j, nv))),
                  pl.BlockSpec((None, d, tf), lambda i, j, te, nv: (te[i], 0, col(i, j, nv))),
                  pl.BlockSpec((None, tf, d), lambda i, j, te, nv: (te[i], col(i, j, nv), 0))],
        out_specs=pl.BlockSpec((tm, d), lambda i, j, te, nv: (i, 0)),
        scratch_shapes=[pltpu.VMEM((tm, d), BF16), pltpu.VMEM((tm, d), F32)],
    )
    return pl.pallas_call(
        _moe_kernel,
        grid_spec=grid_spec,
        out_shape=jax.ShapeDtypeStruct((n_rows, d), F32),
        compiler_params=_params(("parallel", "arbitrary")),
        name="moe",
    )(tile_expert, n_valid, xs, w1, w3, w2)


def route(logits, tm):
    n_tok = logits.shape[0]
    n_assign = n_tok * TOP_K
    idx = jnp.arange(N_EXPERTS, dtype=jnp.int32)[None, :]
    m1 = jnp.max(logits, axis=-1, keepdims=True)
    e1 = jnp.min(jnp.where(logits == m1, idx, N_EXPERTS), axis=-1, keepdims=True)
    rest = jnp.where(idx == e1, jnp.finfo(F32).min, logits)
    m2 = jnp.max(rest, axis=-1, keepdims=True)
    e2 = jnp.min(jnp.where(rest == m2, idx, N_EXPERTS), axis=-1, keepdims=True)
    t = jnp.exp(m2 - m1)
    gates = jnp.concatenate([1.0 / (1.0 + t), t / (1.0 + t)], axis=-1)
    flat_e = jnp.concatenate([e1, e2], axis=-1).reshape(n_assign, 1)
    onehot = (flat_e == idx).astype(F32)
    blk = MOE_ROUTE_BLOCK
    nb = n_assign // blk
    oh = onehot.reshape(nb, blk, N_EXPERTS)
    lower = lambda n: (jnp.arange(n)[:, None] > jnp.arange(n)[None, :]).astype(F32)
    within = jnp.einsum('ij,bjk->bik', lower(blk), oh)
    blk_cnt = jnp.sum(oh, axis=1)
    blk_off = jnp.dot(lower(nb), blk_cnt, precision=lax.Precision.HIGHEST)
    rank = jnp.sum((within + blk_off[:, None, :]) * oh, axis=-1).reshape(n_assign)
    counts = jnp.sum(blk_cnt, axis=0).astype(jnp.int32)
    padded = (counts + tm - 1) // tm * tm
    ends = []
    for e in range(N_EXPERTS):
        ends.append(padded[e] + (ends[-1] if ends else 0))
    pend = jnp.stack(ends)
    pstart = pend - padded
    dest = (jnp.sum(onehot * pstart.astype(F32)[None, :], axis=-1) + rank).astype(jnp.int32)
    n_tiles = n_assign // tm + N_EXPERTS
    starts = jnp.arange(n_tiles, dtype=jnp.int32) * tm
    tile_expert = jnp.minimum(jnp.sum((pend[None, :] <= starts[:, None]).astype(jnp.int32), axis=-1),
                              N_EXPERTS - 1)
    n_valid = (pend[-1] // tm).reshape(1)
    meta = jnp.concatenate([pend, padded, n_valid])
    return gates, dest, tile_expert, n_valid, meta


def _row_copies(dest_ref, src, dst, sem, rows, gather):
    def start(r, carry):
        for kk in range(TOP_K):
            d = dest_ref[0, TOP_K * r + kk]
            if gather:
                pltpu.make_async_copy(src.at[pl.ds(d, 1), :], dst.at[kk, pl.ds(r, 1), :], sem).start()
            else:
                pltpu.make_async_copy(src.at[pl.ds(r, 1), :], dst.at[pl.ds(d, 1), :], sem).start()
        return carry

    lax.fori_loop(0, rows, start, 0, unroll=8)
    if gather:
        pltpu.make_async_copy(dst, dst, sem).wait()
    else:
        for _ in range(TOP_K):
            pltpu.make_async_copy(src, src, sem).wait()


def _dispatch_kernel(meta_ref, dest_ref, h_ref, xs_ref, zero_ref, sem, zsem):
    tm_moe = zero_ref.shape[0]

    @pl.when(pl.program_id(0) == 0)
    def _():
        zero_ref[...] = jnp.zeros_like(zero_ref)
        n_tiles = xs_ref.shape[0] // tm_moe
        fills = []
        for e in range(N_EXPERTS):
            start = pl.multiple_of(jnp.maximum(meta_ref[e] - tm_moe, 0), tm_moe)
            fills.append((meta_ref[N_EXPERTS + e] > 0, start))
        for i in range(n_tiles - N_EXPERTS, n_tiles):
            fills.append((i >= meta_ref[2 * N_EXPERTS], i * tm_moe))
        for cond, start in fills:
            @pl.when(cond)
            def _():
                pltpu.make_async_copy(zero_ref, xs_ref.at[pl.ds(start, tm_moe), :], zsem).start()
        for cond, start in fills:
            @pl.when(cond)
            def _():
                pltpu.make_async_copy(zero_ref, xs_ref.at[pl.ds(start, tm_moe), :], zsem).wait()

    _row_copies(dest_ref, h_ref, xs_ref, sem, h_ref.shape[0], gather=False)


def moe_dispatch(h2, dest, meta, n_slots):
    n_tok, d = h2.shape
    tm = MOE_DISPATCH_TILE
    n_steps = n_tok // tm
    grid_spec = pltpu.PrefetchScalarGridSpec(
        num_scalar_prefetch=1,
        grid=(n_steps,),
        in_specs=[pl.BlockSpec((None, 1, TOP_K * tm), lambda t, m: (t, 0, 0), memory_space=pltpu.SMEM),
                  pl.BlockSpec((tm, d), lambda t, m: (t, 0))],
        out_specs=pl.BlockSpec(memory_space=pl.ANY),
        scratch_shapes=[pltpu.VMEM((MOE_TILE_M, d), h2.dtype), pltpu.SemaphoreType.DMA,
                        pltpu.SemaphoreType.DMA],
    )
    return pl.pallas_call(
        _dispatch_kernel,
        grid_spec=grid_spec,
        out_shape=jax.ShapeDtypeStruct((n_slots, d), h2.dtype),
        compiler_params=_params(("arbitrary",)),
        name="moe_dispatch",
    )(meta, dest.reshape(n_steps, 1, TOP_K * tm), h2)


def _combine_kernel(dest_ref, x1_ref, g_ref, mod_ref, ys_ref, o_ref, buf_ref, sem):
    tm = x1_ref.shape[0]
    _row_copies(dest_ref, ys_ref, buf_ref, sem, tm, gather=True)
    g = g_ref[...]
    moe = g[:, 0:1] * buf_ref[0] + g[:, 1:2] * buf_ref[1]
    o_ref[...] = x1_ref[...] + mod_ref[5, 0] * moe


def moe_combine(x1, gates, dest, ys, modp):
    b, s, d = x1.shape
    tm = MOE_COMBINE_TILE
    nt = s // tm
    return pl.pallas_call(
        _combine_kernel,
        grid=(b, nt),
        in_specs=[pl.BlockSpec((None, 1, TOP_K * tm), lambda i, t: (i * nt + t, 0, 0), memory_space=pltpu.SMEM),
                  pl.BlockSpec((None, tm, d), lambda i, t: (i, t, 0)),
                  pl.BlockSpec((None, tm, TOP_K), lambda i, t: (i, t, 0)),
                  pl.BlockSpec((None, N_MOD, 2, 1, d), lambda i, t: (i, 0, 0, 0, 0)),
                  pl.BlockSpec(memory_space=pl.ANY)],
        out_specs=pl.BlockSpec((None, tm, d), lambda i, t: (i, t, 0)),
        out_shape=jax.ShapeDtypeStruct((b, s, d), F32),
        scratch_shapes=[pltpu.VMEM((TOP_K, tm, d), F32), pltpu.SemaphoreType.DMA],
        compiler_params=_params(("arbitrary", "arbitrary")),
        name="moe_combine",
    )(dest.reshape(b * nt, 1, TOP_K * tm), x1, gates.reshape(b, s, TOP_K), modp, ys)


def moe_residual(x1, h2, logits, modp, w1, w3, w2):
    b, s, d = x1.shape
    n_tok = b * s
    tm = MOE_TILE_M
    gates, dest, tile_expert, n_valid, meta = route(logits.reshape(n_tok, N_EXPERTS), tm)
    n_slots = (n_tok * TOP_K // tm + N_EXPERTS) * tm
    xs = moe_dispatch(h2.reshape(n_tok, d), dest, meta, n_slots)
    ys = grouped_swiglu(xs, tile_expert, n_valid, w1, w3, w2, tm, MOE_TILE_F)
    return moe_combine(x1, gates, dest, ys, modp)


def _proj1_kernel(x_ref, g_ref, mod_ref, w_ref, qan_ref, qb_ref, kvan_ref, kbk_ref, kbv_ref,
                  cos_ref, sin_ref, qg_ref, kg_ref, hy_ref, q_ref, k_ref, v_ref,
                  qs_ref, ks_ref, kr_ref):
    tm = x_ref.shape[0]
    row0 = pl.program_id(1) * tm
    h = _norm_mod(x_ref[...], g_ref[...], mod_ref, row0, 0, 1)
    z = jnp.dot(h.astype(BF16), w_ref[...], preferred_element_type=F32)
    hy_ref[...] = z[:, :HY_IN]

    def rms(a, g):
        return a * lax.rsqrt(jnp.mean(a * a, axis=-1, keepdims=True) + EPS) * g

    c_q, c_kv = HY_IN, HY_IN + MLA_Q_RANK
    c_r = c_kv + MLA_KV_RANK
    qa = rms(z[:, c_q:c_kv], qan_ref[...]).astype(BF16)
    kva = rms(z[:, c_kv:c_r], kvan_ref[...]).astype(BF16)
    qs_ref[...] = jnp.dot(qa, qb_ref[...], preferred_element_type=F32)
    ks_ref[...] = jnp.dot(kva, kbk_ref[...], preferred_element_type=F32)
    v_ref[...] = jnp.dot(kva, kbv_ref[...], preferred_element_type=F32).astype(BF16)
    kr_ref[...] = pltpu.roll(z[:, c_r:c_r + LANES], MLA_NOPE, 1)

    rc = EPILOGUE_ROWS
    first_half = lax.broadcasted_iota(jnp.int32, (rc, LANES), 1) < MLA_NOPE + MLA_ROPE // 2

    def chunk(c, carry):
        rows = pl.ds(pl.multiple_of(c * rc, rc), rc)
        cos, sin = cos_ref[rows, :], sin_ref[rows, :]
        k_rope = kr_ref[rows, :]

        def head(a, gain):
            inv = lax.rsqrt(jnp.sum(a * a, axis=-1, keepdims=True) / MLA_QK + EPS)
            return _rope_pair(a * inv * gain, cos, sin, first_half, MLA_ROPE // 2).astype(BF16)

        for hh in range(MLA_HEADS):
            sl = slice(hh * LANES, (hh + 1) * LANES)
            q_ref[rows, sl] = head(qs_ref[rows, sl], qg_ref[...])
            k_ref[rows, sl] = head(ks_ref[rows, sl] + k_rope, kg_ref[...])
        return carry

    lax.fori_loop(0, tm // rc, chunk, 0)


def proj1(xa, g, modp, w_in, qan, qb, kvan, kbk, kbv, cos, sin, q_gain, k_gain):
    b = xa.shape[0]
    tm = TOKEN_TILE
    tok = lambda width: pl.BlockSpec((None, tm, width), lambda i, t: (i, t, 0))
    full = lambda a: pl.BlockSpec(a.shape, lambda i, t: (0, 0))
    hw = MLA_HEADS * LANES
    return pl.pallas_call(
        _proj1_kernel,
        grid=(b, S_ALL // tm),
        in_specs=[tok(D_MODEL), full(g),
                  pl.BlockSpec((None, N_MOD, 2, 1, D_MODEL), lambda i, t: (i, 0, 0, 0, 0)),
                  full(w_in), full(qan), full(qb), full(kvan), full(kbk), full(kbv),
                  pl.BlockSpec((tm, LANES), lambda i, t: (t, 0)),
                  pl.BlockSpec((tm, LANES), lambda i, t: (t, 0)),
                  full(q_gain), full(k_gain)],
        out_specs=[tok(HY_IN), tok(hw), tok(hw), tok(MLA_HEADS * MLA_V)],
        out_shape=[jax.ShapeDtypeStruct((b, S_ALL, HY_IN), F32),
                   jax.ShapeDtypeStruct((b, S_ALL, hw), BF16),
                   jax.ShapeDtypeStruct((b, S_ALL, hw), BF16),
                   jax.ShapeDtypeStruct((b, S_ALL, MLA_HEADS * MLA_V), BF16)],
        scratch_shapes=[pltpu.VMEM((tm, hw), F32), pltpu.VMEM((tm, hw), F32), pltpu.VMEM((tm, LANES), F32)],
        compiler_params=_params(("parallel", "parallel")),
        name="proj1",
    )(xa, g, modp, w_in, qan, qb, kvan, kbk, kbv, cos, sin, q_gain, k_gain)


def _merge1_kernel(hy_ref, att_ref, x_ref, mod_ref, g2_ref, w_hy_ref, w_att_ref, r_ref,
                   x1_ref, h2_ref, lg_ref):
    y = (jnp.dot(hy_ref[...].astype(BF16), w_hy_ref[...], preferred_element_type=F32)
         + jnp.dot(att_ref[...], w_att_ref[...], preferred_element_type=F32))
    x1 = x_ref[...] + mod_ref[2, 0] * y
    x1_ref[...] = x1
    h2 = _norm_mod(x1, g2_ref[...], mod_ref, 0, 3, 4)
    h2_ref[...] = h2
    lane = lax.broadcasted_iota(jnp.int32, lg_ref.shape, 1)
    logits = jnp.zeros(lg_ref.shape, F32)
    for e in range(N_EXPERTS):
        col = jnp.sum(h2 * r_ref[e:e + 1, :], axis=-1, keepdims=True)
        logits = jnp.where(lane == e, col, logits)
    lg_ref[...] = logits


def merge1(hy, att, xa, modp, g2, w_hy, w_att, router):
    b = xa.shape[0]
    tm = MERGE1_TILE
    tok = lambda width: pl.BlockSpec((None, tm, width), lambda i, t: (i, t, 0))
    full = lambda a: pl.BlockSpec(a.shape, lambda i, t: (0, 0))
    return pl.pallas_call(
        _merge1_kernel,
        grid=(b, SEQ // tm),
        in_specs=[tok(HY_WIDTH), tok(HALF_MIX), tok(D_MODEL),
                  pl.BlockSpec((None, N_MOD, 2, 1, D_MODEL), lambda i, t: (i, 0, 0, 0, 0)),
                  full(g2), full(w_hy), full(w_att), full(router)],
        out_specs=[tok(D_MODEL), tok(D_MODEL), tok(N_EXPERTS)],
        out_shape=[jax.ShapeDtypeStruct((b, SEQ, D_MODEL), F32),
                   jax.ShapeDtypeStruct((b, SEQ, D_MODEL), F32),
                   jax.ShapeDtypeStruct((b, SEQ, N_EXPERTS), F32)],
        compiler_params=_params(("parallel", "parallel")),
        name="merge1",
    )(hy, att, xa, modp, g2, w_hy, w_att, router)


HY_N = 2 * SEQ
HY_TW_ROWS = 256
HY_SPEC_FBLK = 256
HY_CONV_FBLK = 256
HY_LANE_TILES = SEQ // LANES


def _twiddle_kernel(ca_ref, sa_ref, cb_ref, sb_ref, fre_ref, fim_ref, ic_ref, is_ref):
    rows = fre_ref.shape[0]
    r = pl.program_id(0) * rows + lax.broadcasted_iota(jnp.int32, (rows, LANES), 0)
    alt_r = (1 - 2 * (r & 1)).astype(F32)
    cb, sb = cb_ref[...], sb_ref[...]
    for a in range(HY_LANE_TILES):
        ca, sa = ca_ref[:, a:a + 1], sa_ref[:, a:a + 1]
        c = ca * cb - sa * sb
        s = sa * cb + ca * sb
        col = a * LANES + lax.broadcasted_iota(jnp.int32, (rows, LANES), 1)
        alt_c = (1 - 2 * (col & 1)).astype(F32)
        w = jnp.where(col == 0, 1.0 / HY_N, 2.0 / HY_N)
        sl = slice(a * LANES, (a + 1) * LANES)
        fre_ref[:, sl] = c.astype(BF16)
        fim_ref[:, sl] = jnp.where(r == 0, alt_c, -s).astype(BF16)
        ic_ref[:, sl] = (c * w).astype(BF16)
        is_ref[:, sl] = jnp.where(col == 0, alt_r / HY_N, -s * w).astype(BF16)


def dft_matrices():
    idx = jnp.arange(SEQ, dtype=jnp.int32)[:, None]
    step = 2.0 * math.pi / HY_N
    ph_a = ((idx * (LANES * jnp.arange(HY_LANE_TILES, dtype=jnp.int32))[None, :]) % HY_N).astype(F32) * step
    ph_b = ((idx * jnp.arange(LANES, dtype=jnp.int32)[None, :]) % HY_N).astype(F32) * step
    rows = HY_TW_ROWS
    tab = lambda width: pl.BlockSpec((rows, width), lambda i: (i, 0))
    out = jax.ShapeDtypeStruct((SEQ, SEQ), BF16)
    return pl.pallas_call(
        _twiddle_kernel,
        grid=(SEQ // rows,),
        in_specs=[tab(HY_LANE_TILES), tab(HY_LANE_TILES), tab(LANES), tab(LANES)],
        out_specs=[tab(SEQ)] * 4,
        out_shape=[out] * 4,
        compiler_params=_params(("parallel",)),
        name="twiddle",
    )(jnp.cos(ph_a), jnp.sin(ph_a), jnp.cos(ph_b), jnp.sin(ph_b))


def _spec_kernel(fre_ref, fim_ref, h_ref, k_ref):
    w = HY_WIDTH
    h = h_ref[...].astype(BF16)
    re = jnp.dot(fre_ref[...], h, preferred_element_type=F32)
    im = jnp.dot(fim_ref[...], h, preferred_element_type=F32)
    hb0 = h[0:1, w:].astype(F32)
    first = (pl.program_id(1) * re.shape[0] + lax.broadcasted_iota(jnp.int32, (re.shape[0], 1), 0)) == 0
    k_ref[0] = re[:, :w] + re[:, w:] - hb0
    k_ref[1] = jnp.where(first, im[:, :w] + im[:, w:] - hb0, im[:, :w] - im[:, w:])


def filter_spectra(filt, fre, fim):
    fb = HY_SPEC_FBLK
    return pl.pallas_call(
        _spec_kernel,
        grid=(HY_ORDER, SEQ // fb),
        in_specs=[pl.BlockSpec((fb, SEQ), lambda o, f: (f, 0)),
                  pl.BlockSpec((fb, SEQ), lambda o, f: (f, 0)),
                  pl.BlockSpec((SEQ, 2 * HY_WIDTH), lambda o, f: (0, o))],
        out_specs=pl.BlockSpec((None, 2, fb, HY_WIDTH), lambda o, f: (o, 0, f, 0)),
        out_shape=jax.ShapeDtypeStruct((HY_ORDER, 2, SEQ, HY_WIDTH), F32),
        compiler_params=_params(("parallel", "parallel")),
        name="filter_spectra",
    )(fre, fim, filt)


def _short_conv(z_ref, w_ref, b_ref, part):
    z = z_ref[...]
    l = z.shape[0]
    t = lax.broadcasted_iota(jnp.int32, (l, 1), 0)
    prev = jnp.where(t == 0, 0.0, pltpu.roll(z, 1, 0))
    nxt = jnp.where(t == l - 1, 0.0, pltpu.roll(z, l - 1, 0))
    cs = slice(part * HY_WIDTH, (part + 1) * HY_WIDTH)
    return (w_ref[0:1, cs] * prev + w_ref[1:2, cs] * z + w_ref[2:3, cs] * nxt) + b_ref[:, cs]


def _hyconv_kernel(u_ref, xg_ref, cw_ref, cb_ref, fre_ref, fim_ref, ic_ref, is_ref, k_ref, skip_ref,
                   y_ref, ub_ref, acc_ref, *, order):
    f = pl.program_id(1)

    def u_f32():
        return _short_conv(u_ref, cw_ref, cb_ref, 0) if order == 0 else u_ref[...]

    @pl.when(f == 0)
    def _():
        ub_ref[...] = u_f32().astype(BF16)

    x_re = jnp.dot(fre_ref[...], ub_ref[...], preferred_element_type=F32)
    x_im = jnp.dot(fim_ref[...], ub_ref[...], preferred_element_type=F32)
    k_re, k_im = k_ref[0], k_ref[1]
    first = (f * x_re.shape[0] + lax.broadcasted_iota(jnp.int32, (x_re.shape[0], 1), 0)) == 0
    y_re = x_re * k_re - jnp.where(first, 0.0, x_im * k_im)
    y_im = x_im * jnp.where(first, k_im, k_re) + jnp.where(first, 0.0, x_re * k_im)
    part = (jnp.dot(ic_ref[...], y_re.astype(BF16), preferred_element_type=F32)
            + jnp.dot(is_ref[...], y_im.astype(BF16), preferred_element_type=F32))
    _accumulate(acc_ref, part, f)

    @pl.when(f == pl.num_programs(1) - 1)
    def _():
        xg = _short_conv(xg_ref, cw_ref, cb_ref, order + 1)
        y_ref[...] = xg * (acc_ref[...] + skip_ref[order:order + 1, :] * u_f32())


def hyena_conv(order, u, hyz, conv_w, conv_b, fre, fim, ic, is_, spectra, skip):
    b = hyz.shape[0]
    fb = HY_CONV_FBLK
    w = HY_WIDTH
    lat = lambda part: pl.BlockSpec((None, SEQ, w), lambda i, f: (i, 0, part))
    full = lambda a: pl.BlockSpec(a.shape, lambda i, f: (0, 0))
    return pl.pallas_call(
        functools.partial(_hyconv_kernel, order=order),
        grid=(b, SEQ // fb),
        in_specs=[lat(0), lat(order + 1), full(conv_w), full(conv_b),
                  pl.BlockSpec((fb, SEQ), lambda i, f: (f, 0)),
                  pl.BlockSpec((fb, SEQ), lambda i, f: (f, 0)),
                  pl.BlockSpec((SEQ, fb), lambda i, f: (0, f)),
                  pl.BlockSpec((SEQ, fb), lambda i, f: (0, f)),
                  pl.BlockSpec((None, 2, fb, w), lambda i, f: (order, 0, f, 0)),
                  full(skip)],
        out_specs=pl.BlockSpec((None, SEQ, w), lambda i, f: (i, 0, 0)),
        out_shape=jax.ShapeDtypeStruct((b, SEQ, w), F32),
        scratch_shapes=[pltpu.VMEM((SEQ, w), BF16), pltpu.VMEM((SEQ, w), F32)],
        compiler_params=_params(("parallel", "arbitrary")),
        name="hyena_conv",
    )(u, hyz, conv_w, conv_b, fre, fim, ic, is_, spectra, skip)


def hyena_filters(length, w1, b1, w2, b2, w3, freq):
    hp = lax.Precision.HIGHEST
    t = jnp.arange(length, dtype=F32)[:, None]
    t_norm = t / max(length - 1, 1)
    bands = jnp.linspace(1e-4, HY_BANDS - 1, HY_BANDS, dtype=F32)
    ang = 2.0 * math.pi * t * bands / length
    z = jnp.concatenate([t_norm, jnp.cos(ang), -jnp.sin(ang)], axis=-1)
    h = jnp.sin(freq * (jnp.dot(z, w1, precision=hp) + b1))
    h = jnp.sin(freq * (jnp.dot(h, w2, precision=hp) + b2))
    h = jnp.dot(h, w3, precision=hp)
    deltas = jnp.abs(jnp.linspace(HY_MIN_DECAY, HY_MAX_DECAY, HY_WIDTH, dtype=F32))
    window = jnp.exp(-t_norm * deltas) + HY_SHIFT
    return h.reshape(length, HY_ORDER, 2, HY_WIDTH) * window[:, None, None, :]


def hyena(hyz, conv_w, conv_b, fw1, fb1, fw2, fb2, fw3, freq, skip):
    fre, fim, ic, is_ = dft_matrices()
    filt = hyena_filters(SEQ, fw1, fb1, fw2, fb2, fw3, freq).reshape(SEQ, HY_ORDER * 2 * HY_WIDTH)
    spectra = filter_spectra(filt, fre, fim)
    y = hyz
    for o in range(HY_ORDER):
        y = hyena_conv(o, y, hyz, conv_w, conv_b.reshape(1, -1), fre, fim, ic, is_, spectra, skip)
    return y


def _grid_angles(rot_dim):
    n_freq = rot_dim // 4
    inv_freq = ROPE_THETA ** (-jnp.arange(n_freq, dtype=F32) / n_freq)
    t = jnp.arange(SEQ)
    r = (t // GRID_W).astype(F32)
    c_ = (t % GRID_W).astype(F32)
    return jnp.concatenate([r[:, None] * inv_freq, c_[:, None] * inv_freq], axis=-1)


def _rope_tables(rot_dim, lane_offsets):
    ang = _grid_angles(rot_dim)
    c, s = jnp.cos(ang), jnp.sin(ang)
    cos_parts, sin_parts, lane = [], [], 0
    for off in lane_offsets:
        cos_parts += [jnp.ones((SEQ, off - lane), F32), c, c]
        sin_parts += [jnp.zeros((SEQ, off - lane), F32), -s, s]
        lane = off + rot_dim
    cos_parts.append(jnp.ones((SEQ, LANES - lane), F32))
    sin_parts.append(jnp.zeros((SEQ, LANES - lane), F32))
    cos = jnp.concatenate([jnp.concatenate(cos_parts, axis=1), jnp.ones((CTX_LEN, LANES), F32)], axis=0)
    sin = jnp.concatenate([jnp.concatenate(sin_parts, axis=1), jnp.zeros((CTX_LEN, LANES), F32)], axis=0)
    return cos, sin


def _block_diag(w):
    nb, bs, _ = w.shape
    eye = jnp.eye(nb, dtype=w.dtype)
    return (eye[:, None, :, None] * w[:, :, None, :]).reshape(nb * bs, nb * bs)


def _gqa_pair_order():
    g = GQA_Q_HEADS // GQA_KV_HEADS
    heads = []
    for p in range(g):
        heads += [p, g + p]
    return np.concatenate([np.arange(h * HEAD_DIM, (h + 1) * HEAD_DIM) for h in heads])


def _pad_heads(w, n_heads, width):
    k = w.shape[0]
    w = w.reshape(k, n_heads, width)
    return jnp.pad(w, ((0, 0), (0, 0), (0, LANES - width))).reshape(k, n_heads * LANES)


def kernel(x, c, ctx, c_ctx, mod_w, mod_b, norm1_g, norm2_g, ab_w_in, ab_w_out, lru_conv_w, lru_conv_b, lru_w_a, lru_b_a, lru_w_x, lru_b_x, lru_lambda, gqa_q_norm, gqa_k_norm, ffn_w1, ffn_w3, ffn_w2, cd_w_in, cd_w_out, hy_conv_w, hy_conv_b, hy_filt_w1, hy_filt_b1, hy_filt_w2, hy_filt_b2, hy_filt_w3, hy_sin_freq, hy_skip, mla_q_a_norm, mla_q_b, mla_kv_a_norm, mla_kv_b, mla_q_norm, mla_k_norm, moe_router, moe_w1, moe_w3, moe_w2):
    batch = x.shape[0]
    bf = lambda w: w.astype(BF16)
    row = lambda v: v.reshape(1, -1)

    silu_all = jnp.concatenate([jax.nn.silu(c), jax.nn.silu(c_ctx)[None, :],
                                jnp.zeros((16 - batch - 1, D_MODEL), F32)], axis=0)
    mods = []
    for layer in range(DEPTH):
        m = matmul(silu_all, mod_w[layer], 16, 1536) + mod_b[layer]
        lat = m[:batch].reshape(batch, N_MOD, 1, D_MODEL)
        cx = jnp.broadcast_to(m[batch].reshape(1, N_MOD, 1, D_MODEL), lat.shape)
        mods.append(jnp.stack([lat, cx], axis=2))

    xa = jnp.concatenate([x, ctx], axis=1)

    perm = _gqa_pair_order()
    q0 = 2 * LRU_WIDTH
    w_in0 = ab_w_in[0]
    w_in0 = jnp.concatenate([w_in0[:, :q0], w_in0[:, q0 + perm], w_in0[:, q0 + GQA_Q_HEADS * HEAD_DIM:]], axis=1)
    cos_g, sin_g = _rope_tables(HEAD_DIM, (0, HEAD_DIM))
    q_gain = row(jnp.tile(gqa_q_norm[0], 2) * HEAD_DIM ** -0.5)
    k_gain = row(jnp.tile(gqa_k_norm[0], 2))
    xr, gate, q, k, v = proj0(xa, row(norm1_g[0]), mods[0], bf(w_in0), cos_g, sin_g, q_gain, k_gain)

    w_gates = jnp.concatenate([_block_diag(lru_w_a[0, 0]), _block_diag(lru_w_x[0, 0]),
                               _block_diag(lru_w_a[0, 1]), _block_diag(lru_w_x[0, 1])], axis=1)
    b_gates = jnp.concatenate([lru_b_a[0, 0].reshape(-1), lru_b_x[0, 0].reshape(-1),
                               lru_b_a[0, 1].reshape(-1), lru_b_x[0, 1].reshape(-1)])
    rec = rglru(xr, lru_conv_w[0], row(lru_conv_b[0]), bf(w_gates), row(b_gates), lru_lambda[0])

    att_l = attention_latent(q, k, v, GQA_HEADS)
    att_c = attention_context(q, k, v, GQA_HEADS)
    w_out0 = ab_w_out[0]
    x1, h2 = merge0(rec, gate, att_l, att_c, xa, mods[0], row(norm2_g[0]),
                    bf(w_out0[:LRU_WIDTH]), bf(w_out0[LRU_WIDTH:][perm]))
    xa = ffn_residual(h2, x1, mods[0], bf(ffn_w1[0]), bf(ffn_w3[0]), bf(ffn_w2[0]))

    w_in1 = jnp.pad(cd_w_in[0], ((0, 0), (0, LANES - MLA_ROPE)))
    cos_m, sin_m = _rope_tables(MLA_ROPE, (MLA_NOPE,))
    qb = _pad_heads(mla_q_b[0], MLA_HEADS, MLA_QK)
    kvb = mla_kv_b[0].reshape(MLA_KV_RANK, MLA_HEADS, MLA_NOPE + MLA_V)
    kbk = _pad_heads(kvb[:, :, :MLA_NOPE].reshape(MLA_KV_RANK, -1), MLA_HEADS, MLA_NOPE)
    kbv = kvb[:, :, MLA_NOPE:].reshape(MLA_KV_RANK, -1)
    pad_gain = lambda g_: row(jnp.pad(g_, (0, LANES - MLA_QK)))
    hyz, q, k, v = proj1(xa, row(norm1_g[1]), mods[1], bf(w_in1), row(mla_q_a_norm[0]), bf(qb),
                         row(mla_kv_a_norm[0]), bf(kbk), bf(kbv), cos_m, sin_m,
                         pad_gain(mla_q_norm[0] * MLA_QK ** -0.5), pad_gain(mla_k_norm[0]))
    att = attention_latent(q, k, v, MLA_HEAD_GROUPS)
    hy = hyena(hyz, hy_conv_w[0], hy_conv_b[0], hy_filt_w1[0], hy_filt_b1[0], hy_filt_w2[0],
               hy_filt_b2[0], hy_filt_w3[0], hy_sin_freq[0], hy_skip[0])
    w_out1 = cd_w_out[0]
    x1, h2, logits = merge1(hy, att, xa, mods[1], row(norm2_g[1]),
                            bf(w_out1[:HY_WIDTH]), bf(w_out1[HY_WIDTH:]), moe_router[0].T)
    return moe_residual(x1, h2, logits, mods[1], bf(moe_w1[0]), bf(moe_w3[0]), bf(moe_w2[0]))
```

```python
import functools
import math

import jax
import jax.numpy as jnp
import numpy as np
from jax import lax
from jax.experimental import pallas as pl
from jax.experimental.pallas import tpu as pltpu

F32 = jnp.float32
BF16 = jnp.bfloat16

D_MODEL = 1024
SEQ = 2048
CTX_LEN = 256
S_ALL = SEQ + CTX_LEN
DEPTH = 2
GRID_W = 64
HEAD_DIM = 64
HALF_MIX = D_MODEL // 2
ROPE_THETA = 10000.0
EPS = 1e-6
N_MOD = 6
LRU_WIDTH = HALF_MIX
LRU_BLOCKS = LRU_WIDTH // HEAD_DIM
LRU_CONV = 4
LRU_C = 8.0
GQA_Q_HEADS = HALF_MIX // HEAD_DIM
GQA_KV_HEADS = 2
HY_WIDTH = HALF_MIX
HY_ORDER = 2
HY_CONV = 3
HY_BANDS = 16
HY_TARGET = 1e-2
HY_FAST_PCT = 0.3
HY_SLOW_PCT = 1.5
HY_MIN_DECAY = math.log(HY_TARGET) / HY_SLOW_PCT
HY_MAX_DECAY = math.log(HY_TARGET) / HY_FAST_PCT
HY_SHIFT = 0.05
HY_IN = (HY_ORDER + 1) * HY_WIDTH
MLA_HEADS = HALF_MIX // HEAD_DIM
MLA_Q_RANK = D_MODEL // 4
MLA_KV_RANK = D_MODEL // 8
MLA_NOPE = HEAD_DIM
MLA_ROPE = HEAD_DIM // 2
MLA_V = HEAD_DIM
MLA_QK = MLA_NOPE + MLA_ROPE
N_EXPERTS = 8
TOP_K = 2

LANES = 128
SUBLANES = 8
V7X_VMEM_LIMIT_BYTES = 56 * 1024 * 1024

TOKEN_TILE = 768
FFN_TILE_M = 768
FFN_TILE_F = 256
ATTN_TILE_Q = 512
MERGE1_TILE = 512
MOE_TILE_M = 512
MOE_TILE_F = 1792
MOE_ROUTE_BLOCK = 256
MOE_DISPATCH_TILE = 256
MOE_COMBINE_TILE = 256

LRU_GAP = SUBLANES
LRU_LAT0 = CTX_LEN + LRU_GAP
LRU_CHUNK = 256
LRU_ROWS = 2560
LRU_PAD_FRONT = SUBLANES

assert S_ALL % TOKEN_TILE == 0 and S_ALL % FFN_TILE_M == 0 and SEQ % ATTN_TILE_Q == 0
assert LRU_ROWS % LRU_CHUNK == 0 and LRU_ROWS >= LRU_LAT0 + SEQ + SUBLANES


def _params(sem):
    return pltpu.CompilerParams(dimension_semantics=sem, vmem_limit_bytes=V7X_VMEM_LIMIT_BYTES)


def _norm_mod(x, g, mod_pair, row0, shift_idx, scale_idx):
    y = x * lax.rsqrt(jnp.mean(x * x, axis=-1, keepdims=True) + EPS) * g
    shift = _mod_rows(mod_pair, shift_idx, row0, x.shape[0])
    scale = _mod_rows(mod_pair, scale_idx, row0, x.shape[0])
    return y * (1.0 + scale) + shift


def _mod_rows(mod_pair, idx, row0, rows):
    row = row0 + lax.broadcasted_iota(jnp.int32, (rows, 1), 0)
    return jnp.where(row >= SEQ, mod_pair[idx, 1], mod_pair[idx, 0])


def _mm_kernel(x_ref, w_ref, o_ref):
    o_ref[...] = jnp.dot(x_ref[...].astype(BF16), w_ref[...].astype(BF16),
                         preferred_element_type=F32)


def matmul(x, w, tm, tn):
    m, k = x.shape
    n = w.shape[1]
    assert m % tm == 0 and n % tn == 0
    return pl.pallas_call(
        _mm_kernel,
        grid=(m // tm, n // tn),
        in_specs=[pl.BlockSpec((tm, k), lambda i, j: (i, 0)),
                  pl.BlockSpec((k, tn), lambda i, j: (0, j))],
        out_specs=pl.BlockSpec((tm, tn), lambda i, j: (i, j)),
        out_shape=jax.ShapeDtypeStruct((m, n), F32),
        compiler_params=_params(("parallel", "parallel")),
        name="matmul",
    )(x, w)


def _rope_pair(a, cos, sin, first_half, shift):
    rot = jnp.where(first_half, pltpu.roll(a, LANES - shift, 1), pltpu.roll(a, shift, 1))
    return a * cos + rot * sin


def _group_sums(a, ones_ref):
    return jnp.dot((a * a).astype(BF16), ones_ref[...], preferred_element_type=F32)


def _proj0_kernel(x_ref, g_ref, mod_ref, w_ref, cos_ref, sin_ref, qg_ref, kg_ref, ones_ref,
                  xr_ref, gate_ref, q_ref, k_ref, v_ref):
    tm = x_ref.shape[0]
    row0 = pl.program_id(1) * tm
    h = _norm_mod(x_ref[...], g_ref[...], mod_ref, row0, 0, 1)
    z = jnp.dot(h.astype(BF16), w_ref[...], preferred_element_type=F32)
    xr_ref[...] = z[:, :LRU_WIDTH]
    gate_ref[...] = z[:, LRU_WIDTH:2 * LRU_WIDTH]
    cos, sin = cos_ref[...], sin_ref[...]
    first_half = (lax.broadcasted_iota(jnp.int32, (tm, LANES), 1) & (HEAD_DIM // 2)) == 0

    def head_pair(a, gain):
        inv = lax.rsqrt(_group_sums(a, ones_ref) / HEAD_DIM + EPS)
        return _rope_pair(a * inv * gain, cos, sin, first_half, HEAD_DIM // 2).astype(BF16)

    q0 = 2 * LRU_WIDTH
    for p in range(GQA_Q_HEADS // 2):
        q_ref[:, p * LANES:(p + 1) * LANES] = head_pair(z[:, q0 + p * LANES:q0 + (p + 1) * LANES], qg_ref[...])
    k0 = q0 + GQA_Q_HEADS * HEAD_DIM
    k_ref[...] = head_pair(z[:, k0:k0 + LANES], kg_ref[...])
    v_ref[...] = z[:, k0 + LANES:k0 + 2 * LANES].astype(BF16)


def proj0(xa, g, modp, w_in, cos, sin, q_gain, k_gain):
    b = xa.shape[0]
    tm = TOKEN_TILE
    n = w_in.shape[1]
    tok = lambda width: pl.BlockSpec((None, tm, width), lambda i, t: (i, t, 0))
    full = lambda r, c_: pl.BlockSpec((r, c_), lambda i, t: (0, 0))
    half = jnp.arange(LANES) // HEAD_DIM
    ones_pair = (half[:, None] == half[None, :]).astype(BF16)
    return pl.pallas_call(
        _proj0_kernel,
        grid=(b, S_ALL // tm),
        in_specs=[tok(D_MODEL), full(1, D_MODEL),
                  pl.BlockSpec((None, N_MOD, 2, 1, D_MODEL), lambda i, t: (i, 0, 0, 0, 0)),
                  full(D_MODEL, n),
                  pl.BlockSpec((tm, LANES), lambda i, t: (t, 0)),
                  pl.BlockSpec((tm, LANES), lambda i, t: (t, 0)),
                  full(1, LANES), full(1, LANES), full(LANES, LANES)],
        out_specs=[tok(LRU_WIDTH), tok(LRU_WIDTH), tok(GQA_Q_HEADS * HEAD_DIM), tok(LANES), tok(LANES)],
        out_shape=[jax.ShapeDtypeStruct((b, S_ALL, LRU_WIDTH), F32),
                   jax.ShapeDtypeStruct((b, S_ALL, LRU_WIDTH), F32),
                   jax.ShapeDtypeStruct((b, S_ALL, GQA_Q_HEADS * HEAD_DIM), BF16),
                   jax.ShapeDtypeStruct((b, S_ALL, LANES), BF16),
                   jax.ShapeDtypeStruct((b, S_ALL, LANES), BF16)],
        compiler_params=_params(("parallel", "parallel")),
        name="proj0",
    )(xa, g, modp, w_in, cos, sin, q_gain, k_gain, ones_pair)


def _lru_kernel(xr_ref, cw_ref, cb_ref, wg_ref, bg_ref, lam_ref, rec_ref,
                pad_ref, af_ref, hf_ref, ab_ref, hb_ref):
    w = LRU_WIDTH
    pad_ref[...] = jnp.zeros_like(pad_ref)
    pad_ref[LRU_PAD_FRONT:LRU_PAD_FRONT + CTX_LEN, :] = xr_ref[SEQ:S_ALL, :]
    pad_ref[LRU_PAD_FRONT + LRU_LAT0:LRU_PAD_FRONT + LRU_LAT0 + SEQ, :] = xr_ref[0:SEQ, :]
    lam = lam_ref[...]
    neg_c_softplus = -LRU_C * (jnp.maximum(-lam, 0.0) + jnp.log(1.0 + jnp.exp(-jnp.abs(lam))))
    left = LRU_CONV // 2
    for ch in range(LRU_ROWS // LRU_CHUNK):
        r0 = ch * LRU_CHUNK
        u = cb_ref[...]
        for j in range(LRU_CONV):
            start = LRU_PAD_FRONT + r0 - left + j
            u = u + cw_ref[j:j + 1, :] * pad_ref[start:start + LRU_CHUNK, :]
        gz = jnp.dot(u.astype(BF16), wg_ref[...], preferred_element_type=F32) + bg_ref[...]
        for d, (a_ref, h_ref) in enumerate(((af_ref, hf_ref), (ab_ref, hb_ref))):
            r = jax.nn.sigmoid(gz[:, 2 * d * w:(2 * d + 1) * w])
            i = jax.nn.sigmoid(gz[:, (2 * d + 1) * w:(2 * d + 2) * w])
            a = jnp.exp(neg_c_softplus[d:d + 1, :] * r)
            a_ref[r0:r0 + LRU_CHUNK, :] = a
            h_ref[r0:r0 + LRU_CHUNK, :] = jnp.sqrt(1.0 - a * a) * (i * u)

    n_ctx_groups = CTX_LEN // SUBLANES

    def group(tt, carry):
        hf, hb = carry
        is_lat = tt >= n_ctx_groups
        base_f = pl.multiple_of(tt * SUBLANES + jnp.where(is_lat, LRU_GAP, 0), SUBLANES)
        base_b = pl.multiple_of(
            jnp.where(is_lat, LRU_LAT0 + SEQ + CTX_LEN, CTX_LEN) - (tt + 1) * SUBLANES, SUBLANES)
        for i in range(SUBLANES):
            rf = pl.ds(base_f + i, 1)
            hf = af_ref[rf, :] * hf + hf_ref[rf, :]
            hf_ref[rf, :] = hf
            rb = pl.ds(base_b + (SUBLANES - 1 - i), 1)
            hb = ab_ref[rb, :] * hb + hb_ref[rb, :]
            hb_ref[rb, :] = hb
        return hf, hb

    zero = jnp.zeros((1, w), F32)
    lax.fori_loop(0, S_ALL // SUBLANES, group, (zero, zero))
    rec_ref[0:SEQ, :] = hf_ref[LRU_LAT0:LRU_LAT0 + SEQ, :] + hb_ref[LRU_LAT0:LRU_LAT0 + SEQ, :]
    rec_ref[SEQ:S_ALL, :] = hf_ref[0:CTX_LEN, :] + hb_ref[0:CTX_LEN, :]


def rglru(xr, conv_w, conv_b, w_gates, b_gates, lam):
    b = xr.shape[0]
    w = LRU_WIDTH
    full = lambda r, c_: pl.BlockSpec((r, c_), lambda i: (0, 0))
    rows = pltpu.VMEM((LRU_ROWS, w), F32)
    return pl.pallas_call(
        _lru_kernel,
        grid=(b,),
        in_specs=[pl.BlockSpec((None, S_ALL, w), lambda i: (i, 0, 0)),
                  full(LRU_CONV, w), full(1, w), full(w, 4 * w), full(1, 4 * w), full(2, w)],
        out_specs=pl.BlockSpec((None, S_ALL, w), lambda i: (i, 0, 0)),
        out_shape=jax.ShapeDtypeStruct((b, S_ALL, w), F32),
        scratch_shapes=[pltpu.VMEM((LRU_ROWS + 2 * LRU_PAD_FRONT, w), F32), rows, rows, rows, rows],
        compiler_params=_params(("parallel",)),
        name="rglru",
    )(xr, conv_w, conv_b, w_gates, b_gates, lam)


def _attn_kernel(q_ref, k_ref, v_ref, *rest, heads):
    o_ref = rest[-1]
    tq, sk = q_ref.shape[0], k_ref.shape[0]
    q_lo = lax.broadcasted_iota(jnp.int32, (tq, LANES), 1) < HEAD_DIM
    v_lo = lax.broadcasted_iota(jnp.int32, (sk, LANES), 1) < HEAD_DIM

    def keep(x, is_lo, half):
        if half is None:
            return x
        return jnp.where(is_lo if half == 0 else jnp.logical_not(is_lo), x, jnp.zeros_like(x))

    for og, members in enumerate(heads):
        acc = None
        for qg, qh, kg, vg, vh in members:
            q = keep(q_ref[:, qg * LANES:(qg + 1) * LANES], q_lo, qh)
            s = lax.dot_general(q, k_ref[:, kg * LANES:(kg + 1) * LANES],
                                (((1,), (1,)), ((), ())), preferred_element_type=F32)
            p = jnp.exp(s - jnp.max(s, axis=-1, keepdims=True))
            l = jnp.sum(p, axis=-1, keepdims=True)
            v = keep(v_ref[:, vg * LANES:(vg + 1) * LANES], v_lo, vh)
            o = jnp.dot(p.astype(BF16), v, preferred_element_type=F32) / l
            acc = o if acc is None else acc + o
        o_ref[:, og * LANES:(og + 1) * LANES] = acc.astype(o_ref.dtype)


def attention(q, k, v, heads, q_block, n_q_blocks, tq, k_rows):
    b, s, qw = q.shape
    kw, vw = k.shape[-1], v.shape[-1]
    ow = len(heads) * LANES
    kv_spec = lambda width: pl.BlockSpec((None, k_rows, width), lambda i, t: (i, s // k_rows - 1, 0))
    return pl.pallas_call(
        functools.partial(_attn_kernel, heads=heads),
        grid=(b, n_q_blocks),
        in_specs=[pl.BlockSpec((None, tq, qw), lambda i, t: (i, q_block + t, 0)), kv_spec(kw), kv_spec(vw)],
        out_specs=pl.BlockSpec((None, tq, ow), lambda i, t: (i, t, 0)),
        out_shape=jax.ShapeDtypeStruct((b, n_q_blocks * tq, ow), BF16),
        compiler_params=_params(("parallel", "parallel")),
        name="attention",
    )(q, k, v)


GQA_HEADS = tuple(((p, 0, 0, 0, 0), (p, 1, 0, 0, 1)) for p in range(GQA_Q_HEADS // 2))
MLA_HEAD_GROUPS = tuple(((2 * p, None, 2 * p, p, 0), (2 * p + 1, None, 2 * p + 1, p, 1))
                        for p in range(MLA_HEADS // 2))


def attention_latent(q, k, v, heads):
    return attention(q, k, v, heads, 0, SEQ // ATTN_TILE_Q, ATTN_TILE_Q, S_ALL)


def attention_context(q, k, v, heads):
    return attention(q, k, v, heads, SEQ // CTX_LEN, 1, CTX_LEN, CTX_LEN)


def _merge0_kernel(rec_ref, gate_ref, att_l_ref, att_c_ref, x_ref, mod_ref, g2_ref,
                   w_rec_ref, w_att_ref, x1_ref, h2_ref):
    tm = x_ref.shape[0]
    row0 = pl.program_id(1) * tm
    rg = (rec_ref[...] * jax.nn.gelu(gate_ref[...])).astype(BF16)
    att = jnp.where(row0 >= SEQ, att_c_ref[...], att_l_ref[...])
    y = (jnp.dot(rg, w_rec_ref[...], preferred_element_type=F32)
         + jnp.dot(att, w_att_ref[...], preferred_element_type=F32))
    x1 = x_ref[...] + _mod_rows(mod_ref, 2, row0, tm) * y
    x1_ref[...] = x1
    h2_ref[...] = _norm_mod(x1, g2_ref[...], mod_ref, row0, 3, 4).astype(BF16)


def merge0(rec, gate, att_l, att_c, xa, modp, g2, w_rec, w_att):
    b = xa.shape[0]
    tm = CTX_LEN
    n_lat = SEQ // tm
    tok = lambda width: pl.BlockSpec((None, tm, width), lambda i, t: (i, t, 0))
    full = lambda r, c_: pl.BlockSpec((r, c_), lambda i, t: (0, 0))
    return pl.pallas_call(
        _merge0_kernel,
        grid=(b, S_ALL // tm),
        in_specs=[tok(LRU_WIDTH), tok(LRU_WIDTH),
                  pl.BlockSpec((None, tm, HALF_MIX), lambda i, t: (i, jnp.minimum(t, n_lat - 1), 0)),
                  pl.BlockSpec((None, tm, HALF_MIX), lambda i, t: (i, 0, 0)),
                  tok(D_MODEL),
                  pl.BlockSpec((None, N_MOD, 2, 1, D_MODEL), lambda i, t: (i, 0, 0, 0, 0)),
                  full(1, D_MODEL), full(LRU_WIDTH, D_MODEL), full(HALF_MIX, D_MODEL)],
        out_specs=[tok(D_MODEL), tok(D_MODEL)],
        out_shape=[jax.ShapeDtypeStruct((b, S_ALL, D_MODEL), F32),
                   jax.ShapeDtypeStruct((b, S_ALL, D_MODEL), BF16)],
        compiler_params=_params(("parallel", "parallel")),
        name="merge0",
    )(rec, gate, att_l, att_c, xa, modp, g2, w_rec, w_att)


def _swiglu_partial(x, w1_ref, w3_ref, w2_ref):
    h1 = jnp.dot(x, w1_ref[...], preferred_element_type=F32)
    h3 = jnp.dot(x, w3_ref[...], preferred_element_type=F32)
    act = (h1 * jax.nn.sigmoid(h1) * h3).astype(BF16)
    return jnp.dot(act, w2_ref[...], preferred_element_type=F32)


def _accumulate(acc_ref, part, j):
    @pl.when(j == 0)
    def _():
        acc_ref[...] = part

    @pl.when(j > 0)
    def _():
        acc_ref[...] += part


def _ffn_kernel(h_ref, x1_ref, mod_ref, w1_ref, w3_ref, w2_ref, o_ref, act_ref):
    tm = h_ref.shape[0]
    x = h_ref[...]
    for c in range(w1_ref.shape[1] // FFN_TILE_F):
        sl = slice(c * FFN_TILE_F, (c + 1) * FFN_TILE_F)
        h1 = jnp.dot(x, w1_ref[:, sl], preferred_element_type=F32)
        h3 = jnp.dot(x, w3_ref[:, sl], preferred_element_type=F32)
        act_ref[:, sl] = (h1 * jax.nn.sigmoid(h1) * h3).astype(BF16)
    y = jnp.dot(act_ref[...], w2_ref[...], preferred_element_type=F32)
    o_ref[...] = x1_ref[...] + _mod_rows(mod_ref, 5, pl.program_id(1) * tm, tm) * y


def ffn_residual(h2, x1, modp, w1, w3, w2):
    b = x1.shape[0]
    tm = FFN_TILE_M
    f = w1.shape[1]
    assert f % FFN_TILE_F == 0
    tok = lambda: pl.BlockSpec((None, tm, D_MODEL), lambda i, t: (i, t, 0))
    resident = lambda a: pl.BlockSpec(a.shape, lambda i, t: (0, 0), pipeline_mode=pl.Buffered(1))
    return pl.pallas_call(
        _ffn_kernel,
        grid=(b, S_ALL // tm),
        in_specs=[tok(), tok(),
                  pl.BlockSpec((None, N_MOD, 2, 1, D_MODEL), lambda i, t: (i, 0, 0, 0, 0)),
                  resident(w1), resident(w3), resident(w2)],
        out_specs=tok(),
        out_shape=jax.ShapeDtypeStruct((b, S_ALL, D_MODEL), F32),
        scratch_shapes=[pltpu.VMEM((tm, f), BF16)],
        compiler_params=_params(("parallel", "parallel")),
        name="ffn",
    )(h2, x1, modp, w1, w3, w2)


def _moe_kernel(te_ref, nv_ref, x_ref, w1_ref, w3_ref, w2_ref, o_ref, xb_ref, acc_ref):
    i, j = pl.program_id(0), pl.program_id(1)
    last = pl.num_programs(1) - 1
    valid = i < nv_ref[0]

    @pl.when(valid)
    def _():
        @pl.when(j == 0)
        def _():
            xb_ref[...] = x_ref[...].astype(BF16)

        _accumulate(acc_ref, _swiglu_partial(xb_ref[...], w1_ref, w3_ref, w2_ref), j)

        @pl.when(j == last)
        def _():
            o_ref[...] = acc_ref[...]

    @pl.when(jnp.logical_and(jnp.logical_not(valid), j == last))
    def _():
        o_ref[...] = jnp.zeros_like(o_ref)


def grouped_swiglu(xs, tile_expert, n_valid, w1, w3, w2, tm, tf):
    n_rows, d = xs.shape
    f = w1.shape[-1]
    assert n_rows % tm == 0 and f % tf == 0
    n_f = f // tf

    def col(i, j, nv):
        return jnp.where(i < nv[0], j, n_f - 1)

    grid_spec = pltpu.PrefetchScalarGridSpec(
        num_scalar_prefetch=2,
        grid=(n_rows // tm, n_f),
        in_specs=[pl.BlockSpec((tm, d), lambda i, j, te, nv: (jnp.minimum(i, nv[0] - 1), 0)),
                  pl.BlockSpec((None, d, tf), lambda i, j, te, nv: (te[i], 0, col(i, j, nv))),
                  pl.BlockSpec((None, d, tf), lambda i, j, te, nv: (te[i], 0, col(i, j, nv))),
                  pl.BlockSpec((None, tf, d), lambda i, j, te, nv: (te[i], col(i, j, nv), 0))],
        out_specs=pl.BlockSpec((tm, d), lambda i, j, te, nv: (i, 0)),
        scratch_shapes=[pltpu.VMEM((tm, d), BF16), pltpu.VMEM((tm, d), F32)],
    )
    return pl.pallas_call(
        _moe_kernel,
        grid_spec=grid_spec,
        out_shape=jax.ShapeDtypeStruct((n_rows, d), F32),
        compiler_params=_params(("parallel", "arbitrary")),
        name="moe",
    )(tile_expert, n_valid, xs, w1, w3, w2)


def route(logits, tm):
    n_tok = logits.shape[0]
    n_assign = n_tok * TOP_K
    idx = jnp.arange(N_EXPERTS, dtype=jnp.int32)[None, :]
    m1 = jnp.max(logits, axis=-1, keepdims=True)
    e1 = jnp.min(jnp.where(logits == m1, idx, N_EXPERTS), axis=-1, keepdims=True)
    rest = jnp.where(idx == e1, jnp.finfo(F32).min, logits)
    m2 = jnp.max(rest, axis=-1, keepdims=True)
    e2 = jnp.min(jnp.where(rest == m2, idx, N_EXPERTS), axis=-1, keepdims=True)
    t = jnp.exp(m2 - m1)
    gates = jnp.concatenate([1.0 / (1.0 + t), t / (1.0 + t)], axis=-1)
    flat_e = jnp.concatenate([e1, e2], axis=-1).reshape(n_assign, 1)
    onehot = (flat_e == idx).astype(F32)
    blk = MOE_ROUTE_BLOCK
    nb = n_assign // blk
    oh = onehot.reshape(nb, blk, N_EXPERTS)
    lower = lambda n: (jnp.arange(n)[:, None] > jnp.arange(n)[None, :]).astype(F32)
    within = jnp.einsum('ij,bjk->bik', lower(blk), oh)
    blk_cnt = jnp.sum(oh, axis=1)
    blk_off = jnp.dot(lower(nb), blk_cnt, precision=lax.Precision.HIGHEST)
    rank = jnp.sum((within + blk_off[:, None, :]) * oh, axis=-1).reshape(n_assign)
    counts = jnp.sum(blk_cnt, axis=0).astype(jnp.int32)
    padded = (counts + tm - 1) // tm * tm
    ends = []
    for e in range(N_EXPERTS):
        ends.append(padded[e] + (ends[-1] if ends else 0))
    pend = jnp.stack(ends)
    pstart = pend - padded
    dest = (jnp.sum(onehot * pstart.astype(F32)[None, :], axis=-1) + rank).astype(jnp.int32)
    n_tiles = n_assign // tm + N_EXPERTS
    starts = jnp.arange(n_tiles, dtype=jnp.int32) * tm
    tile_expert = jnp.minimum(jnp.sum((pend[None, :] <= starts[:, None]).astype(jnp.int32), axis=-1),
                              N_EXPERTS - 1)
    n_valid = (pend[-1] // tm).reshape(1)
    meta = jnp.concatenate([pend, padded, n_valid])
    return gates, dest, tile_expert, n_valid, meta


def _row_copies(dest_ref, src, dst, sem, rows, gather):
    def start(r, carry):
        for kk in range(TOP_K):
            d = dest_ref[0, TOP_K * r + kk]
            if gather:
                pltpu.make_async_copy(src.at[pl.ds(d, 1), :], dst.at[kk, pl.ds(r, 1), :], sem).start()
            else:
                pltpu.make_async_copy(src.at[pl.ds(r, 1), :], dst.at[pl.ds(d, 1), :], sem).start()
        return carry

    lax.fori_loop(0, rows, start, 0, unroll=8)
    if gather:
        pltpu.make_async_copy(dst, dst, sem).wait()
    else:
        for _ in range(TOP_K):
            pltpu.make_async_copy(src, src, sem).wait()


def _dispatch_kernel(meta_ref, dest_ref, h_ref, xs_ref, zero_ref, sem, zsem):
    tm_moe = zero_ref.shape[0]

    @pl.when(pl.program_id(0) == 0)
    def _():
        zero_ref[...] = jnp.zeros_like(zero_ref)
        n_tiles = xs_ref.shape[0] // tm_moe
        fills = []
        for e in range(N_EXPERTS):
            start = pl.multiple_of(jnp.maximum(meta_ref[e] - tm_moe, 0), tm_moe)
            fills.append((meta_ref[N_EXPERTS + e] > 0, start))
        for i in range(n_tiles - N_EXPERTS, n_tiles):
            fills.append((i >= meta_ref[2 * N_EXPERTS], i * tm_moe))
        for cond, start in fills:
            @pl.when(cond)
            def _():
                pltpu.make_async_copy(zero_ref, xs_ref.at[pl.ds(start, tm_moe), :], zsem).start()
        for cond, start in fills:
            @pl.when(cond)
            def _():
                pltpu.make_async_copy(zero_ref, xs_ref.at[pl.ds(start, tm_moe), :], zsem).wait()

    _row_copies(dest_ref, h_ref, xs_ref, sem, h_ref.shape[0], gather=False)


def moe_dispatch(h2, dest, meta, n_slots):
    n_tok, d = h2.shape
    tm = MOE_DISPATCH_TILE
    n_steps = n_tok // tm
    grid_spec = pltpu.PrefetchScalarGridSpec(
        num_scalar_prefetch=1,
        grid=(n_steps,),
        in_specs=[pl.BlockSpec((None, 1, TOP_K * tm), lambda t, m: (t, 0, 0), memory_space=pltpu.SMEM),
                  pl.BlockSpec((tm, d), lambda t, m: (t, 0))],
        out_specs=pl.BlockSpec(memory_space=pl.ANY),
        scratch_shapes=[pltpu.VMEM((MOE_TILE_M, d), h2.dtype), pltpu.SemaphoreType.DMA,
                        pltpu.SemaphoreType.DMA],
    )
    return pl.pallas_call(
        _dispatch_kernel,
        grid_spec=grid_spec,
        out_shape=jax.ShapeDtypeStruct((n_slots, d), h2.dtype),
        compiler_params=_params(("arbitrary",)),
        name="moe_dispatch",
    )(meta, dest.reshape(n_steps, 1, TOP_K * tm), h2)


def _combine_kernel(dest_ref, x1_ref, g_ref, mod_ref, ys_ref, o_ref, buf_ref, sem):
    tm = x1_ref.shape[0]
    _row_copies(dest_ref, ys_ref, buf_ref, sem, tm, gather=True)
    g = g_ref[...]
    moe = g[:, 0:1] * buf_ref[0] + g[:, 1:2] * buf_ref[1]
    o_ref[...] = x1_ref[...] + mod_ref[5, 0] * moe


def moe_combine(x1, gates, dest, ys, modp):
    b, s, d = x1.shape
    tm = MOE_COMBINE_TILE
    nt = s // tm
    return pl.pallas_call(
        _combine_kernel,
        grid=(b, nt),
        in_specs=[pl.BlockSpec((None, 1, TOP_K * tm), lambda i, t: (i * nt + t, 0, 0), memory_space=pltpu.SMEM),
                  pl.BlockSpec((None, tm, d), lambda i, t: (i, t, 0)),
                  pl.BlockSpec((None, tm, TOP_K), lambda i, t: (i, t, 0)),
                  pl.BlockSpec((None, N_MOD, 2, 1, d), lambda i, t: (i, 0, 0, 0, 0)),
                  pl.BlockSpec(memory_space=pl.ANY)],
        out_specs=pl.BlockSpec((None, tm, d), lambda i, t: (i, t, 0)),
        out_shape=jax.ShapeDtypeStruct((b, s, d), F32),
        scratch_shapes=[pltpu.VMEM((TOP_K, tm, d), F32), pltpu.SemaphoreType.DMA],
        compiler_params=_params(("arbitrary", "arbitrary")),
        name="moe_combine",
    )(dest.reshape(b * nt, 1, TOP_K * tm), x1, gates.reshape(b, s, TOP_K), modp, ys)


def moe_residual(x1, h2, logits, modp, w1, w3, w2):
    b, s, d = x1.shape
    n_tok = b * s
    tm = MOE_TILE_M
    gates, dest, tile_expert, n_valid, meta = route(logits.reshape(n_tok, N_EXPERTS), tm)
    n_slots = (n_tok * TOP_K // tm + N_EXPERTS) * tm
    xs = moe_dispatch(h2.reshape(n_tok, d), dest, meta, n_slots)
    ys = grouped_swiglu(xs, tile_expert, n_valid, w1, w3, w2, tm, MOE_TILE_F)
    return moe_combine(x1, gates, dest, ys, modp)


def _proj1_kernel(x_ref, g_ref, mod_ref, w_ref, qan_ref, qb_ref, qbs_ref, kvan_ref, kbk_ref, kbv_ref,
                  cos_ref, sin_ref, gains_ref, ones_ref, hy_ref, q_ref, k_ref, v_ref):
    tm = x_ref.shape[0]
    row0 = pl.program_id(1) * tm
    h = _norm_mod(x_ref[...], g_ref[...], mod_ref, row0, 0, 1)
    z = jnp.dot(h.astype(BF16), w_ref[...], preferred_element_type=F32)
    hy_ref[...] = z[:, :HY_IN]

    def rms(a, g):
        return a * lax.rsqrt(jnp.mean(a * a, axis=-1, keepdims=True) + EPS) * g

    c_q, c_kv = HY_IN, HY_IN + MLA_Q_RANK
    c_r = c_kv + MLA_KV_RANK
    qa = rms(z[:, c_q:c_kv], qan_ref[...]).astype(BF16)
    kva = rms(z[:, c_kv:c_r], kvan_ref[...]).astype(BF16)
    q = jnp.dot(qa, qb_ref[...], preferred_element_type=F32)
    q_sw = jnp.dot(qa, qbs_ref[...], preferred_element_type=F32)
    kk = jnp.dot(kva, kbk_ref[...], preferred_element_type=F32)
    v_ref[...] = jnp.dot(kva, kbv_ref[...], preferred_element_type=F32).astype(BF16)
    k_rope = pltpu.roll(z[:, c_r:c_r + LANES], MLA_NOPE, 1)
    first_half = lax.broadcasted_iota(jnp.int32, (tm, LANES), 1) < MLA_NOPE + MLA_ROPE // 2
    sh = MLA_ROPE // 2
    k_rope_sw = jnp.where(first_half, pltpu.roll(k_rope, LANES - sh, 1), pltpu.roll(k_rope, sh, 1))
    cos, sin = cos_ref[...], sin_ref[...]
    qg, qg_sw, kg, kg_sw = (gains_ref[i:i + 1, :] for i in range(4))

    def head(a, a_sw, gain, gain_sw):
        inv = lax.rsqrt(_group_sums(a, ones_ref) / MLA_QK + EPS)
        return (a * (inv * gain) * cos + a_sw * (inv * gain_sw) * sin).astype(BF16)

    for hh in range(MLA_HEADS):
        sl = slice(hh * LANES, (hh + 1) * LANES)
        q_ref[:, sl] = head(q[:, sl], q_sw[:, sl], qg, qg_sw)
        k_ref[:, sl] = head(kk[:, sl] + k_rope, k_rope_sw, kg, kg_sw)


def _swap_rope_halves(a):
    lead = a.shape[:-1]
    g = a.reshape(*lead, -1, LANES)
    lo, mid = MLA_NOPE, MLA_NOPE + MLA_ROPE // 2
    g = jnp.concatenate([g[..., :lo], g[..., mid:MLA_QK], g[..., lo:mid], g[..., MLA_QK:]], axis=-1)
    return g.reshape(a.shape)


def proj1(xa, g, modp, w_in, qan, qb, kvan, kbk, kbv, cos, sin, q_gain, k_gain):
    b = xa.shape[0]
    tm = TOKEN_TILE
    tok = lambda width: pl.BlockSpec((None, tm, width), lambda i, t: (i, t, 0))
    full = lambda a: pl.BlockSpec(a.shape, lambda i, t: (0, 0))
    hw = MLA_HEADS * LANES
    qbs = _swap_rope_halves(qb)
    gains = jnp.concatenate([q_gain, _swap_rope_halves(q_gain), k_gain, _swap_rope_halves(k_gain)], axis=0)
    ones = jnp.ones((LANES, LANES), BF16)
    return pl.pallas_call(
        _proj1_kernel,
        grid=(b, S_ALL // tm),
        in_specs=[tok(D_MODEL), full(g),
                  pl.BlockSpec((None, N_MOD, 2, 1, D_MODEL), lambda i, t: (i, 0, 0, 0, 0)),
                  full(w_in), full(qan), full(qb), full(qbs), full(kvan), full(kbk), full(kbv),
                  pl.BlockSpec((tm, LANES), lambda i, t: (t, 0)),
                  pl.BlockSpec((tm, LANES), lambda i, t: (t, 0)),
                  full(gains), full(ones)],
        out_specs=[tok(HY_IN), tok(hw), tok(hw), tok(MLA_HEADS * MLA_V)],
        out_shape=[jax.ShapeDtypeStruct((b, S_ALL, HY_IN), F32),
                   jax.ShapeDtypeStruct((b, S_ALL, hw), BF16),
                   jax.ShapeDtypeStruct((b, S_ALL, hw), BF16),
                   jax.ShapeDtypeStruct((b, S_ALL, MLA_HEADS * MLA_V), BF16)],
        compiler_params=_params(("parallel", "parallel")),
        name="proj1",
    )(xa, g, modp, w_in, qan, qb, qbs, kvan, kbk, kbv, cos, sin, gains, ones)


def _merge1_kernel(hy_ref, att_ref, x_ref, mod_ref, g2_ref, w_hy_ref, w_att_ref, r_ref,
                   x1_ref, h2_ref, lg_ref):
    y = (jnp.dot(hy_ref[...].astype(BF16), w_hy_ref[...], preferred_element_type=F32)
         + jnp.dot(att_ref[...], w_att_ref[...], preferred_element_type=F32))
    x1 = x_ref[...] + mod_ref[2, 0] * y
    x1_ref[...] = x1
    h2 = _norm_mod(x1, g2_ref[...], mod_ref, 0, 3, 4)
    h2_ref[...] = h2
    lane = lax.broadcasted_iota(jnp.int32, lg_ref.shape, 1)
    logits = jnp.zeros(lg_ref.shape, F32)
    for e in range(N_EXPERTS):
        col = jnp.sum(h2 * r_ref[e:e + 1, :], axis=-1, keepdims=True)
        logits = jnp.where(lane == e, col, logits)
    lg_ref[...] = logits


def merge1(hy, att, xa, modp, g2, w_hy, w_att, router):
    b = xa.shape[0]
    tm = MERGE1_TILE
    tok = lambda width: pl.BlockSpec((None, tm, width), lambda i, t: (i, t, 0))
    full = lambda a: pl.BlockSpec(a.shape, lambda i, t: (0, 0))
    return pl.pallas_call(
        _merge1_kernel,
        grid=(b, SEQ // tm),
        in_specs=[tok(HY_WIDTH), tok(HALF_MIX), tok(D_MODEL),
                  pl.BlockSpec((None, N_MOD, 2, 1, D_MODEL), lambda i, t: (i, 0, 0, 0, 0)),
                  full(g2), full(w_hy), full(w_att), full(router)],
        out_specs=[tok(D_MODEL), tok(D_MODEL), tok(N_EXPERTS)],
        out_shape=[jax.ShapeDtypeStruct((b, SEQ, D_MODEL), F32),
                   jax.ShapeDtypeStruct((b, SEQ, D_MODEL), F32),
                   jax.ShapeDtypeStruct((b, SEQ, N_EXPERTS), F32)],
        compiler_params=_params(("parallel", "parallel")),
        name="merge1",
    )(hy, att, xa, modp, g2, w_hy, w_att, router)


HY_N = 2 * SEQ
HY_TW_ROWS = 256
HY_SPEC_FBLK = 256
HY_CONV_FBLK = 256
HY_LANE_TILES = SEQ // LANES


def _twiddle_kernel(ca_ref, sa_ref, cb_ref, sb_ref, fre_ref, fim_ref, ic_ref, is_ref):
    rows = fre_ref.shape[0]
    r = pl.program_id(0) * rows + lax.broadcasted_iota(jnp.int32, (rows, LANES), 0)
    alt_r = (1 - 2 * (r & 1)).astype(F32)
    cb, sb = cb_ref[...], sb_ref[...]
    for a in range(HY_LANE_TILES):
        ca, sa = ca_ref[:, a:a + 1], sa_ref[:, a:a + 1]
        c = ca * cb - sa * sb
        s = sa * cb + ca * sb
        col = a * LANES + lax.broadcasted_iota(jnp.int32, (rows, LANES), 1)
        alt_c = (1 - 2 * (col & 1)).astype(F32)
        w = jnp.where(col == 0, 1.0 / HY_N, 2.0 / HY_N)
        sl = slice(a * LANES, (a + 1) * LANES)
        fre_ref[:, sl] = c.astype(BF16)
        fim_ref[:, sl] = jnp.where(r == 0, alt_c, -s).astype(BF16)
        ic_ref[:, sl] = (c * w).astype(BF16)
        is_ref[:, sl] = jnp.where(col == 0, alt_r / HY_N, -s * w).astype(BF16)


def dft_matrices():
    idx = jnp.arange(SEQ, dtype=jnp.int32)[:, None]
    step = 2.0 * math.pi / HY_N
    ph_a = ((idx * (LANES * jnp.arange(HY_LANE_TILES, dtype=jnp.int32))[None, :]) % HY_N).astype(F32) * step
    ph_b = ((idx * jnp.arange(LANES, dtype=jnp.int32)[None, :]) % HY_N).astype(F32) * step
    rows = HY_TW_ROWS
    tab = lambda width: pl.BlockSpec((rows, width), lambda i: (i, 0))
    out = jax.ShapeDtypeStruct((SEQ, SEQ), BF16)
    return pl.pallas_call(
        _twiddle_kernel,
        grid=(SEQ // rows,),
        in_specs=[tab(HY_LANE_TILES), tab(HY_LANE_TILES), tab(LANES), tab(LANES)],
        out_specs=[tab(SEQ)] * 4,
        out_shape=[out] * 4,
        compiler_params=_params(("parallel",)),
        name="twiddle",
    )(jnp.cos(ph_a), jnp.sin(ph_a), jnp.cos(ph_b), jnp.sin(ph_b))


def _spec_kernel(fre_ref, fim_ref, h_ref, k_ref):
    w = HY_WIDTH
    h = h_ref[...].astype(BF16)
    re = jnp.dot(fre_ref[...], h, preferred_element_type=F32)
    im = jnp.dot(fim_ref[...], h, preferred_element_type=F32)
    hb0 = h[0:1, w:].astype(F32)
    first = (pl.program_id(1) * re.shape[0] + lax.broadcasted_iota(jnp.int32, (re.shape[0], 1), 0)) == 0
    k_ref[0] = re[:, :w] + re[:, w:] - hb0
    k_ref[1] = jnp.where(first, im[:, :w] + im[:, w:] - hb0, im[:, :w] - im[:, w:])


def filter_spectra(filt, fre, fim):
    fb = HY_SPEC_FBLK
    return pl.pallas_call(
        _spec_kernel,
        grid=(HY_ORDER, SEQ // fb),
        in_specs=[pl.BlockSpec((fb, SEQ), lambda o, f: (f, 0)),
                  pl.BlockSpec((fb, SEQ), lambda o, f: (f, 0)),
                  pl.BlockSpec((SEQ, 2 * HY_WIDTH), lambda o, f: (0, o))],
        out_specs=pl.BlockSpec((None, 2, fb, HY_WIDTH), lambda o, f: (o, 0, f, 0)),
        out_shape=jax.ShapeDtypeStruct((HY_ORDER, 2, SEQ, HY_WIDTH), F32),
        compiler_params=_params(("parallel", "parallel")),
        name="filter_spectra",
    )(fre, fim, filt)


def _short_conv(z_ref, w_ref, b_ref, part):
    z = z_ref[...]
    l = z.shape[0]
    t = lax.broadcasted_iota(jnp.int32, (l, 1), 0)
    prev = jnp.where(t == 0, 0.0, pltpu.roll(z, 1, 0))
    nxt = jnp.where(t == l - 1, 0.0, pltpu.roll(z, l - 1, 0))
    cs = slice(part * HY_WIDTH, (part + 1) * HY_WIDTH)
    return (w_ref[0:1, cs] * prev + w_ref[1:2, cs] * z + w_ref[2:3, cs] * nxt) + b_ref[:, cs]


def _hyconv_kernel(u_ref, xg_ref, cw_ref, cb_ref, fre_ref, fim_ref, ic_ref, is_ref, k_ref, skip_ref,
                   y_ref, ub_ref, acc_ref, *, order):
    f = pl.program_id(1)

    def u_f32():
        return _short_conv(u_ref, cw_ref, cb_ref, 0) if order == 0 else u_ref[...]

    @pl.when(f == 0)
    def _():
        ub_ref[...] = u_f32().astype(BF16)

    x_re = jnp.dot(fre_ref[...], ub_ref[...], preferred_element_type=F32)
    x_im = jnp.dot(fim_ref[...], ub_ref[...], preferred_element_type=F32)
    k_re, k_im = k_ref[0], k_ref[1]
    first = (f * x_re.shape[0] + lax.broadcasted_iota(jnp.int32, (x_re.shape[0], 1), 0)) == 0
    y_re = x_re * k_re - jnp.where(first, 0.0, x_im * k_im)
    y_im = x_im * jnp.where(first, k_im, k_re) + jnp.where(first, 0.0, x_re * k_im)
    part = (jnp.dot(ic_ref[...], y_re.astype(BF16), preferred_element_type=F32)
            + jnp.dot(is_ref[...], y_im.astype(BF16), preferred_element_type=F32))
    _accumulate(acc_ref, part, f)

    @pl.when(f == pl.num_programs(1) - 1)
    def _():
        xg = _short_conv(xg_ref, cw_ref, cb_ref, order + 1)
        y_ref[...] = xg * (acc_ref[...] + skip_ref[order:order + 1, :] * u_f32())


def hyena_conv(order, u, hyz, conv_w, conv_b, fre, fim, ic, is_, spectra, skip):
    b = hyz.shape[0]
    fb = HY_CONV_FBLK
    w = HY_WIDTH
    lat = lambda part: pl.BlockSpec((None, SEQ, w), lambda i, f: (i, 0, part))
    full = lambda a: pl.BlockSpec(a.shape, lambda i, f: (0, 0))
    return pl.pallas_call(
        functools.partial(_hyconv_kernel, order=order),
        grid=(b, SEQ // fb),
        in_specs=[lat(0), lat(order + 1), full(conv_w), full(conv_b),
                  pl.BlockSpec((fb, SEQ), lambda i, f: (f, 0)),
                  pl.BlockSpec((fb, SEQ), lambda i, f: (f, 0)),
                  pl.BlockSpec((SEQ, fb), lambda i, f: (0, f)),
                  pl.BlockSpec((SEQ, fb), lambda i, f: (0, f)),
                  pl.BlockSpec((None, 2, fb, w), lambda i, f: (order, 0, f, 0)),
                  full(skip)],
        out_specs=pl.BlockSpec((None, SEQ, w), lambda i, f: (i, 0, 0)),
        out_shape=jax.ShapeDtypeStruct((b, SEQ, w), F32),
        scratch_shapes=[pltpu.VMEM((SEQ, w), BF16), pltpu.VMEM((SEQ, w), F32)],
        compiler_params=_params(("parallel", "arbitrary")),
        name="hyena_conv",
    )(u, hyz, conv_w, conv_b, fre, fim, ic, is_, spectra, skip)


def hyena_filters(length, w1, b1, w2, b2, w3, freq):
    hp = lax.Precision.HIGHEST
    t = jnp.arange(length, dtype=F32)[:, None]
    t_norm = t / max(length - 1, 1)
    bands = jnp.linspace(1e-4, HY_BANDS - 1, HY_BANDS, dtype=F32)
    ang = 2.0 * math.pi * t * bands / length
    z = jnp.concatenate([t_norm, jnp.cos(ang), -jnp.sin(ang)], axis=-1)
    h = jnp.sin(freq * (jnp.dot(z, w1, precision=hp) + b1))
    h = jnp.sin(freq * (jnp.dot(h, w2, precision=hp) + b2))
    h = jnp.dot(h, w3, precision=hp)
    deltas = jnp.abs(jnp.linspace(HY_MIN_DECAY, HY_MAX_DECAY, HY_WIDTH, dtype=F32))
    window = jnp.exp(-t_norm * deltas) + HY_SHIFT
    return h.reshape(length, HY_ORDER, 2, HY_WIDTH) * window[:, None, None, :]


def hyena(hyz, conv_w, conv_b, fw1, fb1, fw2, fb2, fw3, freq, skip):
    fre, fim, ic, is_ = dft_matrices()
    filt = hyena_filters(SEQ, fw1, fb1, fw2, fb2, fw3, freq).reshape(SEQ, HY_ORDER * 2 * HY_WIDTH)
    spectra = filter_spectra(filt, fre, fim)
    y = hyz
    for o in range(HY_ORDER):
        y = hyena_conv(o, y, hyz, conv_w, conv_b.reshape(1, -1), fre, fim, ic, is_, spectra, skip)
    return y


def _grid_angles(rot_dim):
    n_freq = rot_dim // 4
    inv_freq = ROPE_THETA ** (-jnp.arange(n_freq, dtype=F32) / n_freq)
    t = jnp.arange(SEQ)
    r = (t // GRID_W).astype(F32)
    c_ = (t % GRID_W).astype(F32)
    return jnp.concatenate([r[:, None] * inv_freq, c_[:, None] * inv_freq], axis=-1)


def _rope_tables(rot_dim, lane_offsets):
    ang = _grid_angles(rot_dim)
    c, s = jnp.cos(ang), jnp.sin(ang)
    cos_parts, sin_parts, lane = [], [], 0
    for off in lane_offsets:
        cos_parts += [jnp.ones((SEQ, off - lane), F32), c, c]
        sin_parts += [jnp.zeros((SEQ, off - lane), F32), -s, s]
        lane = off + rot_dim
    cos_parts.append(jnp.ones((SEQ, LANES - lane), F32))
    sin_parts.append(jnp.zeros((SEQ, LANES - lane), F32))
    cos = jnp.concatenate([jnp.concatenate(cos_parts, axis=1), jnp.ones((CTX_LEN, LANES), F32)], axis=0)
    sin = jnp.concatenate([jnp.concatenate(sin_parts, axis=1), jnp.zeros((CTX_LEN, LANES), F32)], axis=0)
    return cos, sin


def _block_diag(w):
    nb, bs, _ = w.shape
    eye = jnp.eye(nb, dtype=w.dtype)
    return (eye[:, None, :, None] * w[:, :, None, :]).reshape(nb * bs, nb * bs)


def _gqa_pair_order():
    g = GQA_Q_HEADS // GQA_KV_HEADS
    heads = []
    for p in range(g):
        heads += [p, g + p]
    return np.concatenate([np.arange(h * HEAD_DIM, (h + 1) * HEAD_DIM) for h in heads])


def _pad_heads(w, n_heads, width):
    k = w.shape[0]
    w = w.reshape(k, n_heads, width)
    return jnp.pad(w, ((0, 0), (0, 0), (0, LANES - width))).reshape(k, n_heads * LANES)


def kernel(x, c, ctx, c_ctx, mod_w, mod_b, norm1_g, norm2_g, ab_w_in, ab_w_out, lru_conv_w, lru_conv_b, lru_w_a, lru_b_a, lru_w_x, lru_b_x, lru_lambda, gqa_q_norm, gqa_k_norm, ffn_w1, ffn_w3, ffn_w2, cd_w_in, cd_w_out, hy_conv_w, hy_conv_b, hy_filt_w1, hy_filt_b1, hy_filt_w2, hy_filt_b2, hy_filt_w3, hy_sin_freq, hy_skip, mla_q_a_norm, mla_q_b, mla_kv_a_norm, mla_kv_b, mla_q_norm, mla_k_norm, moe_router, moe_w1, moe_w3, moe_w2):
    batch = x.shape[0]
    bf = lambda w: w.astype(BF16)
    row = lambda v: v.reshape(1, -1)

    silu_all = jnp.concatenate([jax.nn.silu(c), jax.nn.silu(c_ctx)[None, :],
                                jnp.zeros((16 - batch - 1, D_MODEL), F32)], axis=0)
    mods = []
    for layer in range(DEPTH):
        m = matmul(silu_all, mod_w[layer], 16, 1536) + mod_b[layer]
        lat = m[:batch].reshape(batch, N_MOD, 1, D_MODEL)
        cx = jnp.broadcast_to(m[batch].reshape(1, N_MOD, 1, D_MODEL), lat.shape)
        mods.append(jnp.stack([lat, cx], axis=2))

    xa = jnp.concatenate([x, ctx], axis=1)

    perm = _gqa_pair_order()
    q0 = 2 * LRU_WIDTH
    w_in0 = ab_w_in[0]
    w_in0 = jnp.concatenate([w_in0[:, :q0], w_in0[:, q0 + perm], w_in0[:, q0 + GQA_Q_HEADS * HEAD_DIM:]], axis=1)
    cos_g, sin_g = _rope_tables(HEAD_DIM, (0, HEAD_DIM))
    q_gain = row(jnp.tile(gqa_q_norm[0], 2) * HEAD_DIM ** -0.5)
    k_gain = row(jnp.tile(gqa_k_norm[0], 2))
    xr, gate, q, k, v = proj0(xa, row(norm1_g[0]), mods[0], bf(w_in0), cos_g, sin_g, q_gain, k_gain)

    w_gates = jnp.concatenate([_block_diag(lru_w_a[0, 0]), _block_diag(lru_w_x[0, 0]),
                               _block_diag(lru_w_a[0, 1]), _block_diag(lru_w_x[0, 1])], axis=1)
    b_gates = jnp.concatenate([lru_b_a[0, 0].reshape(-1), lru_b_x[0, 0].reshape(-1),
                               lru_b_a[0, 1].reshape(-1), lru_b_x[0, 1].reshape(-1)])
    rec = rglru(xr, lru_conv_w[0], row(lru_conv_b[0]), bf(w_gates), row(b_gates), lru_lambda[0])

    att_l = attention_latent(q, k, v, GQA_HEADS)
    att_c = attention_context(q, k, v, GQA_HEADS)
    w_out0 = ab_w_out[0]
    x1, h2 = merge0(rec, gate, att_l, att_c, xa, mods[0], row(norm2_g[0]),
                    bf(w_out0[:LRU_WIDTH]), bf(w_out0[LRU_WIDTH:][perm]))
    xa = ffn_residual(h2, x1, mods[0], bf(ffn_w1[0]), bf(ffn_w3[0]), bf(ffn_w2[0]))

    w_in1 = jnp.pad(cd_w_in[0], ((0, 0), (0, LANES - MLA_ROPE)))
    cos_m, sin_m = _rope_tables(MLA_ROPE, (MLA_NOPE,))
    qb = _pad_heads(mla_q_b[0], MLA_HEADS, MLA_QK)
    kvb = mla_kv_b[0].reshape(MLA_KV_RANK, MLA_HEADS, MLA_NOPE + MLA_V)
    kbk = _pad_heads(kvb[:, :, :MLA_NOPE].reshape(MLA_KV_RANK, -1), MLA_HEADS, MLA_NOPE)
    kbv = kvb[:, :, MLA_NOPE:].reshape(MLA_KV_RANK, -1)
    pad_gain = lambda g_: row(jnp.pad(g_, (0, LANES - MLA_QK)))
    hyz, q, k, v = proj1(xa, row(norm1_g[1]), mods[1], bf(w_in1), row(mla_q_a_norm[0]), bf(qb),
                         row(mla_kv_a_norm[0]), bf(kbk), bf(kbv), cos_m, sin_m,
                         pad_gain(mla_q_norm[0] * MLA_QK ** -0.5), pad_gain(mla_k_norm[0]))
    att = attention_latent(q, k, v, MLA_HEAD_GROUPS)
    hy = hyena(hyz, hy_conv_w[0], hy_conv_b[0], hy_filt_w1[0], hy_filt_b1[0], hy_filt_w2[0],
               hy_filt_b2[0], hy_filt_w3[0], hy_sin_freq[0], hy_skip[0])
    w_out1 = cd_w_out[0]
    x1, h2, logits = merge1(hy, att, xa, mods[1], row(norm2_g[1]),
                            bf(w_out1[:HY_WIDTH]), bf(w_out1[HY_WIDTH:]), moe_router[0].T)
    return moe_residual(x1, h2, logits, mods[1], bf(moe_w1[0]), bf(moe_w3[0]), bf(moe_w2[0]))
```

```python
import functools
import math

import jax
import jax.numpy as jnp
import numpy as np
from jax import lax
from jax.experimental import pallas as pl
from jax.experimental.pallas import tpu as pltpu

F32 = jnp.float32
BF16 = jnp.bfloat16

D_MODEL = 1024
SEQ = 2048
CTX_LEN = 256
S_ALL = SEQ + CTX_LEN
DEPTH = 2
GRID_W = 64
HEAD_DIM = 64
HALF_MIX = D_MODEL // 2
ROPE_THETA = 10000.0
EPS = 1e-6
N_MOD = 6
LRU_WIDTH = HALF_MIX
LRU_BLOCKS = LRU_WIDTH // HEAD_DIM
LRU_CONV = 4
LRU_C = 8.0
GQA_Q_HEADS = HALF_MIX // HEAD_DIM
GQA_KV_HEADS = 2
HY_WIDTH = HALF_MIX
HY_ORDER = 2
HY_CONV = 3
HY_BANDS = 16
HY_TARGET = 1e-2
HY_FAST_PCT = 0.3
HY_SLOW_PCT = 1.5
HY_MIN_DECAY = math.log(HY_TARGET) / HY_SLOW_PCT
HY_MAX_DECAY = math.log(HY_TARGET) / HY_FAST_PCT
HY_SHIFT = 0.05
HY_IN = (HY_ORDER + 1) * HY_WIDTH
MLA_HEADS = HALF_MIX // HEAD_DIM
MLA_Q_RANK = D_MODEL // 4
MLA_KV_RANK = D_MODEL // 8
MLA_NOPE = HEAD_DIM
MLA_ROPE = HEAD_DIM // 2
MLA_V = HEAD_DIM
MLA_QK = MLA_NOPE + MLA_ROPE
N_EXPERTS = 8
TOP_K = 2

LANES = 128
SUBLANES = 8
V7X_VMEM_LIMIT_BYTES = 56 * 1024 * 1024

TOKEN_TILE = 768
FFN_TILE_M = 768
FFN_TILE_F = 256
GQA_TILE_Q = 1024
MLA_TILE_Q = 512
MERGE1_TILE = 512
MOE_TILE_M = 512
MOE_TILE_F = 1792
MOE_ROUTE_BLOCK = 256
MOE_DISPATCH_TILE = 256
MOE_COMBINE_TILE = 256

LRU_GAP = SUBLANES
LRU_LAT0 = CTX_LEN + LRU_GAP
LRU_CHUNK = 256
LRU_ROWS = 2560
LRU_PAD_FRONT = SUBLANES

assert S_ALL % TOKEN_TILE == 0 and S_ALL % FFN_TILE_M == 0
assert SEQ % GQA_TILE_Q == 0 and SEQ % MLA_TILE_Q == 0
assert LRU_ROWS % LRU_CHUNK == 0 and LRU_ROWS >= LRU_LAT0 + SEQ + SUBLANES


def _params(sem):
    return pltpu.CompilerParams(dimension_semantics=sem, vmem_limit_bytes=V7X_VMEM_LIMIT_BYTES)


def _norm_mod(x, g, mod_pair, row0, shift_idx, scale_idx):
    y = x * lax.rsqrt(jnp.mean(x * x, axis=-1, keepdims=True) + EPS) * g
    shift = _mod_rows(mod_pair, shift_idx, row0, x.shape[0])
    scale = _mod_rows(mod_pair, scale_idx, row0, x.shape[0])
    return y * (1.0 + scale) + shift


def _mod_rows(mod_pair, idx, row0, rows):
    row = row0 + lax.broadcasted_iota(jnp.int32, (rows, 1), 0)
    return jnp.where(row >= SEQ, mod_pair[idx, 1], mod_pair[idx, 0])


def _mm_kernel(x_ref, w_ref, o_ref):
    o_ref[...] = jnp.dot(x_ref[...].astype(BF16), w_ref[...].astype(BF16),
                         preferred_element_type=F32)


def matmul(x, w, tm, tn):
    m, k = x.shape
    n = w.shape[1]
    assert m % tm == 0 and n % tn == 0
    return pl.pallas_call(
        _mm_kernel,
        grid=(m // tm, n // tn),
        in_specs=[pl.BlockSpec((tm, k), lambda i, j: (i, 0)),
                  pl.BlockSpec((k, tn), lambda i, j: (0, j))],
        out_specs=pl.BlockSpec((tm, tn), lambda i, j: (i, j)),
        out_shape=jax.ShapeDtypeStruct((m, n), F32),
        compiler_params=_params(("parallel", "parallel")),
        name="matmul",
    )(x, w)


def _rope_pair(a, cos, sin, first_half, shift):
    rot = jnp.where(first_half, pltpu.roll(a, LANES - shift, 1), pltpu.roll(a, shift, 1))
    return a * cos + rot * sin


def _group_sums(a, ones_ref):
    return jnp.dot((a * a).astype(BF16), ones_ref[...], preferred_element_type=F32)


def _proj0_kernel(x_ref, g_ref, mod_ref, w_ref, cos_ref, sin_ref, qg_ref, kg_ref, ones_ref,
                  xr_ref, gate_ref, q_ref, k_ref, v_ref):
    tm = x_ref.shape[0]
    row0 = pl.program_id(1) * tm
    h = _norm_mod(x_ref[...], g_ref[...], mod_ref, row0, 0, 1)
    z = jnp.dot(h.astype(BF16), w_ref[...], preferred_element_type=F32)
    xr_ref[...] = z[:, :LRU_WIDTH]
    gate_ref[...] = z[:, LRU_WIDTH:2 * LRU_WIDTH]
    cos, sin = cos_ref[...], sin_ref[...]
    first_half = (lax.broadcasted_iota(jnp.int32, (tm, LANES), 1) & (HEAD_DIM // 2)) == 0

    def head_pair(a, gain):
        inv = lax.rsqrt(_group_sums(a, ones_ref) / HEAD_DIM + EPS)
        return _rope_pair(a * inv * gain, cos, sin, first_half, HEAD_DIM // 2).astype(BF16)

    q0 = 2 * LRU_WIDTH
    for p in range(GQA_Q_HEADS // 2):
        q_ref[:, p * LANES:(p + 1) * LANES] = head_pair(z[:, q0 + p * LANES:q0 + (p + 1) * LANES], qg_ref[...])
    k0 = q0 + GQA_Q_HEADS * HEAD_DIM
    k_ref[...] = head_pair(z[:, k0:k0 + LANES], kg_ref[...])
    v_ref[...] = z[:, k0 + LANES:k0 + 2 * LANES].astype(BF16)


def proj0(xa, g, modp, w_in, cos, sin, q_gain, k_gain):
    b = xa.shape[0]
    tm = TOKEN_TILE
    n = w_in.shape[1]
    tok = lambda width: pl.BlockSpec((None, tm, width), lambda i, t: (i, t, 0))
    full = lambda r, c_: pl.BlockSpec((r, c_), lambda i, t: (0, 0))
    half = jnp.arange(LANES) // HEAD_DIM
    ones_pair = (half[:, None] == half[None, :]).astype(BF16)
    return pl.pallas_call(
        _proj0_kernel,
        grid=(b, S_ALL // tm),
        in_specs=[tok(D_MODEL), full(1, D_MODEL),
                  pl.BlockSpec((None, N_MOD, 2, 1, D_MODEL), lambda i, t: (i, 0, 0, 0, 0)),
                  full(D_MODEL, n),
                  pl.BlockSpec((tm, LANES), lambda i, t: (t, 0)),
                  pl.BlockSpec((tm, LANES), lambda i, t: (t, 0)),
                  full(1, LANES), full(1, LANES), full(LANES, LANES)],
        out_specs=[tok(LRU_WIDTH), tok(LRU_WIDTH), tok(GQA_Q_HEADS * HEAD_DIM), tok(LANES), tok(LANES)],
        out_shape=[jax.ShapeDtypeStruct((b, S_ALL, LRU_WIDTH), F32),
                   jax.ShapeDtypeStruct((b, S_ALL, LRU_WIDTH), F32),
                   jax.ShapeDtypeStruct((b, S_ALL, GQA_Q_HEADS * HEAD_DIM), BF16),
                   jax.ShapeDtypeStruct((b, S_ALL, LANES), BF16),
                   jax.ShapeDtypeStruct((b, S_ALL, LANES), BF16)],
        compiler_params=_params(("parallel", "parallel")),
        name="proj0",
    )(xa, g, modp, w_in, cos, sin, q_gain, k_gain, ones_pair)


def _lru_kernel(xr_ref, cw_ref, cb_ref, wg_ref, bg_ref, lam_ref, rec_ref,
                pad_ref, af_ref, hf_ref, ab_ref, hb_ref):
    w = LRU_WIDTH
    pad_ref[...] = jnp.zeros_like(pad_ref)
    pad_ref[LRU_PAD_FRONT:LRU_PAD_FRONT + CTX_LEN, :] = xr_ref[SEQ:S_ALL, :]
    pad_ref[LRU_PAD_FRONT + LRU_LAT0:LRU_PAD_FRONT + LRU_LAT0 + SEQ, :] = xr_ref[0:SEQ, :]
    lam = lam_ref[...]
    neg_c_softplus = -LRU_C * (jnp.maximum(-lam, 0.0) + jnp.log(1.0 + jnp.exp(-jnp.abs(lam))))
    left = LRU_CONV // 2
    for ch in range(LRU_ROWS // LRU_CHUNK):
        r0 = ch * LRU_CHUNK
        u = cb_ref[...]
        for j in range(LRU_CONV):
            start = LRU_PAD_FRONT + r0 - left + j
            u = u + cw_ref[j:j + 1, :] * pad_ref[start:start + LRU_CHUNK, :]
        gz = jnp.dot(u.astype(BF16), wg_ref[...], preferred_element_type=F32) + bg_ref[...]
        for d, (a_ref, h_ref) in enumerate(((af_ref, hf_ref), (ab_ref, hb_ref))):
            r = jax.nn.sigmoid(gz[:, 2 * d * w:(2 * d + 1) * w])
            i = jax.nn.sigmoid(gz[:, (2 * d + 1) * w:(2 * d + 2) * w])
            a = jnp.exp(neg_c_softplus[d:d + 1, :] * r)
            a_ref[r0:r0 + LRU_CHUNK, :] = a
            h_ref[r0:r0 + LRU_CHUNK, :] = jnp.sqrt(1.0 - a * a) * (i * u)

    n_ctx_groups = CTX_LEN // SUBLANES
    sub = lax.broadcasted_iota(jnp.int32, (SUBLANES, w), 0)

    def tile_scan(a, b, descending):
        for s in (1, 2, 4):
            if descending:
                ok, shift = sub < SUBLANES - s, SUBLANES - s
            else:
                ok, shift = sub >= s, s
            a_s = jnp.where(ok, pltpu.roll(a, shift, 0), 1.0)
            b_s = jnp.where(ok, pltpu.roll(b, shift, 0), 0.0)
            a, b = a * a_s, a * b_s + b
        return a, b

    def group(tt, carry):
        hf, hb = carry
        is_lat = tt >= n_ctx_groups
        rf = pl.ds(pl.multiple_of(tt * SUBLANES + jnp.where(is_lat, LRU_GAP, 0), SUBLANES), SUBLANES)
        rb = pl.ds(pl.multiple_of(
            jnp.where(is_lat, LRU_LAT0 + SEQ + CTX_LEN, CTX_LEN) - (tt + 1) * SUBLANES, SUBLANES), SUBLANES)
        a, b = tile_scan(af_ref[rf, :], hf_ref[rf, :], False)
        h = a * hf + b
        hf_ref[rf, :] = h
        hf = jnp.broadcast_to(h[SUBLANES - 1:SUBLANES, :], (SUBLANES, w))
        a, b = tile_scan(ab_ref[rb, :], hb_ref[rb, :], True)
        h = a * hb + b
        hb_ref[rb, :] = h
        hb = jnp.broadcast_to(h[0:1, :], (SUBLANES, w))
        return hf, hb

    zero = jnp.zeros((SUBLANES, w), F32)
    lax.fori_loop(0, S_ALL // SUBLANES, group, (zero, zero))
    rec_ref[0:SEQ, :] = hf_ref[LRU_LAT0:LRU_LAT0 + SEQ, :] + hb_ref[LRU_LAT0:LRU_LAT0 + SEQ, :]
    rec_ref[SEQ:S_ALL, :] = hf_ref[0:CTX_LEN, :] + hb_ref[0:CTX_LEN, :]


def rglru(xr, conv_w, conv_b, w_gates, b_gates, lam):
    b = xr.shape[0]
    w = LRU_WIDTH
    full = lambda r, c_: pl.BlockSpec((r, c_), lambda i: (0, 0))
    rows = pltpu.VMEM((LRU_ROWS, w), F32)
    return pl.pallas_call(
        _lru_kernel,
        grid=(b,),
        in_specs=[pl.BlockSpec((None, S_ALL, w), lambda i: (i, 0, 0)),
                  full(LRU_CONV, w), full(1, w), full(w, 4 * w), full(1, 4 * w), full(2, w)],
        out_specs=pl.BlockSpec((None, S_ALL, w), lambda i: (i, 0, 0)),
        out_shape=jax.ShapeDtypeStruct((b, S_ALL, w), F32),
        scratch_shapes=[pltpu.VMEM((LRU_ROWS + 2 * LRU_PAD_FRONT, w), F32), rows, rows, rows, rows],
        compiler_params=_params(("parallel",)),
        name="rglru",
    )(xr, conv_w, conv_b, w_gates, b_gates, lam)


def _attn_kernel(q_ref, k_ref, v_ref, *rest, heads):
    o_ref = rest[-1]
    tq, sk = q_ref.shape[0], k_ref.shape[0]
    q_lo = lax.broadcasted_iota(jnp.int32, (tq, LANES), 1) < HEAD_DIM
    v_lo = lax.broadcasted_iota(jnp.int32, (sk, LANES), 1) < HEAD_DIM

    def keep(x, is_lo, half):
        if half is None:
            return x
        return jnp.where(is_lo if half == 0 else jnp.logical_not(is_lo), x, jnp.zeros_like(x))

    for og, members in enumerate(heads):
        acc = None
        for qg, qh, kg, vg, vh in members:
            q = keep(q_ref[:, qg * LANES:(qg + 1) * LANES], q_lo, qh)
            s = lax.dot_general(q, k_ref[:, kg * LANES:(kg + 1) * LANES],
                                (((1,), (1,)), ((), ())), preferred_element_type=F32)
            p = jnp.exp(s - jnp.max(s, axis=-1, keepdims=True))
            l = jnp.sum(p, axis=-1, keepdims=True)
            v = keep(v_ref[:, vg * LANES:(vg + 1) * LANES], v_lo, vh)
            o = jnp.dot(p.astype(BF16), v, preferred_element_type=F32) / l
            acc = o if acc is None else acc + o
        o_ref[:, og * LANES:(og + 1) * LANES] = acc.astype(o_ref.dtype)


def attention(q, k, v, heads, q_block, n_q_blocks, tq, k_rows):
    b, s, qw = q.shape
    kw, vw = k.shape[-1], v.shape[-1]
    ow = len(heads) * LANES
    kv_spec = lambda width: pl.BlockSpec((None, k_rows, width), lambda i, t: (i, s // k_rows - 1, 0))
    return pl.pallas_call(
        functools.partial(_attn_kernel, heads=heads),
        grid=(b, n_q_blocks),
        in_specs=[pl.BlockSpec((None, tq, qw), lambda i, t: (i, q_block + t, 0)), kv_spec(kw), kv_spec(vw)],
        out_specs=pl.BlockSpec((None, tq, ow), lambda i, t: (i, t, 0)),
        out_shape=jax.ShapeDtypeStruct((b, n_q_blocks * tq, ow), BF16),
        compiler_params=_params(("parallel", "parallel")),
        name="attention",
    )(q, k, v)


GQA_HEADS = tuple(((p, 0, 0, 0, 0), (p, 1, 0, 0, 1)) for p in range(GQA_Q_HEADS // 2))
MLA_HEAD_GROUPS = tuple(((2 * p, None, 2 * p, p, 0), (2 * p + 1, None, 2 * p + 1, p, 1))
                        for p in range(MLA_HEADS // 2))


def attention_latent(q, k, v, heads, tq):
    return attention(q, k, v, heads, 0, SEQ // tq, tq, S_ALL)


def attention_context(q, k, v, heads):
    return attention(q, k, v, heads, SEQ // CTX_LEN, 1, CTX_LEN, CTX_LEN)


def _merge0_kernel(rec_ref, gate_ref, att_l_ref, att_c_ref, x_ref, mod_ref,
                   w_rec_ref, w_att_ref, x1_ref):
    tm = x_ref.shape[0]
    row0 = pl.program_id(1) * tm
    rg = (rec_ref[...] * jax.nn.gelu(gate_ref[...])).astype(BF16)
    att = jnp.where(row0 >= SEQ, att_c_ref[...], att_l_ref[...])
    y = (jnp.dot(rg, w_rec_ref[...], preferred_element_type=F32)
         + jnp.dot(att, w_att_ref[...], preferred_element_type=F32))
    x1_ref[...] = x_ref[...] + _mod_rows(mod_ref, 2, row0, tm) * y


def merge0(rec, gate, att_l, att_c, xa, modp, w_rec, w_att):
    b = xa.shape[0]
    tm = CTX_LEN
    n_lat = SEQ // tm
    tok = lambda width: pl.BlockSpec((None, tm, width), lambda i, t: (i, t, 0))
    full = lambda r, c_: pl.BlockSpec((r, c_), lambda i, t: (0, 0))
    return pl.pallas_call(
        _merge0_kernel,
        grid=(b, S_ALL // tm),
        in_specs=[tok(LRU_WIDTH), tok(LRU_WIDTH),
                  pl.BlockSpec((None, tm, HALF_MIX), lambda i, t: (i, jnp.minimum(t, n_lat - 1), 0)),
                  pl.BlockSpec((None, tm, HALF_MIX), lambda i, t: (i, 0, 0)),
                  tok(D_MODEL),
                  pl.BlockSpec((None, N_MOD, 2, 1, D_MODEL), lambda i, t: (i, 0, 0, 0, 0)),
                  full(LRU_WIDTH, D_MODEL), full(HALF_MIX, D_MODEL)],
        out_specs=tok(D_MODEL),
        out_shape=jax.ShapeDtypeStruct((b, S_ALL, D_MODEL), F32),
        compiler_params=_params(("parallel", "parallel")),
        name="merge0",
    )(rec, gate, att_l, att_c, xa, modp, w_rec, w_att)


def _swiglu_partial(x, w1_ref, w3_ref, w2_ref):
    h1 = jnp.dot(x, w1_ref[...], preferred_element_type=F32)
    h3 = jnp.dot(x, w3_ref[...], preferred_element_type=F32)
    act = (h1 * jax.nn.sigmoid(h1) * h3).astype(BF16)
    return jnp.dot(act, w2_ref[...], preferred_element_type=F32)


def _accumulate(acc_ref, part, j):
    @pl.when(j == 0)
    def _():
        acc_ref[...] = part

    @pl.when(j > 0)
    def _():
        acc_ref[...] += part


def _ffn_kernel(x1_ref, mod_ref, g2_ref, w1_ref, w3_ref, w2_ref, o_ref, act_ref):
    tm = x1_ref.shape[0]
    row0 = pl.program_id(1) * tm
    x = _norm_mod(x1_ref[...], g2_ref[...], mod_ref, row0, 3, 4).astype(BF16)
    for c in range(w1_ref.shape[1] // FFN_TILE_F):
        sl = slice(c * FFN_TILE_F, (c + 1) * FFN_TILE_F)
        h1 = jnp.dot(x, w1_ref[:, sl], preferred_element_type=F32)
        h3 = jnp.dot(x, w3_ref[:, sl], preferred_element_type=F32)
        act_ref[:, sl] = (h1 * jax.nn.sigmoid(h1) * h3).astype(BF16)
    y = jnp.dot(act_ref[...], w2_ref[...], preferred_element_type=F32)
    o_ref[...] = x1_ref[...] + _mod_rows(mod_ref, 5, row0, tm) * y


def ffn_residual(x1, modp, g2, w1, w3, w2):
    b = x1.shape[0]
    tm = FFN_TILE_M
    f = w1.shape[1]
    assert f % FFN_TILE_F == 0
    tok = lambda: pl.BlockSpec((None, tm, D_MODEL), lambda i, t: (i, t, 0))
    resident = lambda a: pl.BlockSpec(a.shape, lambda i, t: (0, 0), pipeline_mode=pl.Buffered(1))
    return pl.pallas_call(
        _ffn_kernel,
        grid=(b, S_ALL // tm),
        in_specs=[tok(),
                  pl.BlockSpec((None, N_MOD, 2, 1, D_MODEL), lambda i, t: (i, 0, 0, 0, 0)),
                  pl.BlockSpec((1, D_MODEL), lambda i, t: (0, 0)),
                  resident(w1), resident(w3), resident(w2)],
        out_specs=tok(),
        out_shape=jax.ShapeDtypeStruct((b, S_ALL, D_MODEL), F32),
        scratch_shapes=[pltpu.VMEM((tm, f), BF16)],
        compiler_params=_params(("parallel", "parallel")),
        name="ffn",
    )(x1, modp, g2, w1, w3, w2)


def _moe_kernel(te_ref, nv_ref, x_ref, w1_ref, w3_ref, w2_ref, o_ref, xb_ref, acc_ref):
    i, j = pl.program_id(0), pl.program_id(1)
    last = pl.num_programs(1) - 1
    valid = i < nv_ref[0]

    @pl.when(valid)
    def _():
        @pl.when(j == 0)
        def _():
            xb_ref[...] = x_ref[...].astype(BF16)

        _accumulate(acc_ref, _swiglu_partial(xb_ref[...], w1_ref, w3_ref, w2_ref), j)

        @pl.when(j == last)
        def _():
            o_ref[...] = acc_ref[...]

    @pl.when(jnp.logical_and(jnp.logical_not(valid), j == last))
    def _():
        o_ref[...] = jnp.zeros_like(o_ref)


def grouped_swiglu(xs, tile_expert, n_valid, w1, w3, w2, tm, tf):
    n_rows, d = xs.shape
    f = w1.shape[-1]
    assert n_rows % tm == 0 and f % tf == 0
    n_f = f // tf

    def col(i, j, nv):
        return jnp.where(i < nv[0], j, n_f - 1)

    grid_spec = pltpu.PrefetchScalarGridSpec(
        num_scalar_prefetch=2,
        grid=(n_rows // tm, n_f),
        in_specs=[pl.BlockSpec((tm, d), lambda i, j, te, nv: (jnp.minimum(i, nv[0] - 1), 0)),
                  pl.BlockSpec((None, d, tf), lambda i, j, te, nv: (te[i], 0, col(i, j, nv))),
                  pl.BlockSpec((None, d, tf), lambda i, j, te, nv: (te[i], 0, col(i, j, nv))),
                  pl.BlockSpec((None, tf, d), lambda i, j, te, nv: (te[i], col(i, j, nv), 0))],
        out_specs=pl.BlockSpec((tm, d), lambda i, j, te, nv: (i, 0)),
        scratch_shapes=[pltpu.VMEM((tm, d), BF16), pltpu.VMEM((tm, d), F32)],
    )
    return pl.pallas_call(
        _moe_kernel,
        grid_spec=grid_spec,
        out_shape=jax.ShapeDtypeStruct((n_rows, d), F32),
        compiler_params=_params(("parallel", "arbitrary")),
        name="moe",
    )(tile_expert, n_valid, xs, w1, w3, w2)


def route(logits, tm):
    n_tok = logits.shape[0]
    n_assign = n_tok * TOP_K
    idx = jnp.arange(N_EXPERTS, dtype=jnp.int32)[None, :]
    m1 = jnp.max(logits, axis=-1, keepdims=True)
    e1 = jnp.min(jnp.where(logits == m1, idx, N_EXPERTS), axis=-1, keepdims=True)
    rest = jnp.where(idx == e1, jnp.finfo(F32).min, logits)
    m2 = jnp.max(rest, axis=-1, keepdims=True)
    e2 = jnp.min(jnp.where(rest == m2, idx, N_EXPERTS), axis=-1, keepdims=True)
    t = jnp.exp(m2 - m1)
    gates = jnp.concatenate([1.0 / (1.0 + t), t / (1.0 + t)], axis=-1)
    flat_e = jnp.concatenate([e1, e2], axis=-1).reshape(n_assign, 1)
    onehot = (flat_e == idx).astype(F32)
    blk = MOE_ROUTE_BLOCK
    nb = n_assign // blk
    oh = onehot.reshape(nb, blk, N_EXPERTS)
    lower = lambda n: (jnp.arange(n)[:, None] > jnp.arange(n)[None, :]).astype(F32)
    within = jnp.einsum('ij,bjk->bik', lower(blk), oh)
    blk_cnt = jnp.sum(oh, axis=1)
    blk_off = jnp.dot(lower(nb), blk_cnt, precision=lax.Precision.HIGHEST)
    rank = jnp.sum((within + blk_off[:, None, :]) * oh, axis=-1).reshape(n_assign)
    counts = jnp.sum(blk_cnt, axis=0).astype(jnp.int32)
    padded = (counts + tm - 1) // tm * tm
    ends = []
    for e in range(N_EXPERTS):
        ends.append(padded[e] + (ends[-1] if ends else 0))
    pend = jnp.stack(ends)
    pstart = pend - padded
    dest = (jnp.sum(onehot * pstart.astype(F32)[None, :], axis=-1) + rank).astype(jnp.int32)
    n_tiles = n_assign // tm + N_EXPERTS
    starts = jnp.arange(n_tiles, dtype=jnp.int32) * tm
    tile_expert = jnp.minimum(jnp.sum((pend[None, :] <= starts[:, None]).astype(jnp.int32), axis=-1),
                              N_EXPERTS - 1)
    n_valid = (pend[-1] // tm).reshape(1)
    meta = jnp.concatenate([pend, padded, n_valid])
    return gates, dest, tile_expert, n_valid, meta


def _row_copies(dest_ref, src, dst, sem, rows, gather, side_work=()):
    def start(r, carry):
        for kk in range(TOP_K):
            d = dest_ref[0, TOP_K * r + kk]
            if gather:
                pltpu.make_async_copy(src.at[pl.ds(d, 1), :], dst.at[kk, pl.ds(r, 1), :], sem).start()
            else:
                pltpu.make_async_copy(src.at[pl.ds(r, 1), :], dst.at[pl.ds(d, 1), :], sem).start()
        return carry

    if side_work:
        n_groups = rows // SUBLANES
        for g in range(n_groups):
            for r in range(g * SUBLANES, (g + 1) * SUBLANES):
                start(r, 0)
            for thunk in side_work[g * len(side_work) // n_groups:(g + 1) * len(side_work) // n_groups]:
                thunk()
    else:
        lax.fori_loop(0, rows, start, 0, unroll=8)
    if gather:
        pltpu.make_async_copy(dst, dst, sem).wait()
    else:
        for _ in range(TOP_K):
            pltpu.make_async_copy(src, src, sem).wait()


def _dispatch_kernel(meta_ref, dest_ref, h_ref, w1_ref, w3_ref, w2_ref,
                     xs_ref, w1b_ref, w3b_ref, w2b_ref, zero_ref, sem, zsem):
    tm_moe = zero_ref.shape[0]

    @pl.when(pl.program_id(0) == 0)
    def _():
        zero_ref[...] = jnp.zeros_like(zero_ref)
        n_tiles = xs_ref.shape[0] // tm_moe
        fills = []
        for e in range(N_EXPERTS):
            start = pl.multiple_of(jnp.maximum(meta_ref[e] - tm_moe, 0), tm_moe)
            fills.append((meta_ref[N_EXPERTS + e] > 0, start))
        for i in range(n_tiles - N_EXPERTS, n_tiles):
            fills.append((i >= meta_ref[2 * N_EXPERTS], i * tm_moe))
        for cond, start in fills:
            @pl.when(cond)
            def _():
                pltpu.make_async_copy(zero_ref, xs_ref.at[pl.ds(start, tm_moe), :], zsem).start()
        for cond, start in fills:
            @pl.when(cond)
            def _():
                pltpu.make_async_copy(zero_ref, xs_ref.at[pl.ds(start, tm_moe), :], zsem).wait()

    def cast_lanes(src, dst, c):
        def thunk():
            dst[:, c * LANES:(c + 1) * LANES] = src[:, c * LANES:(c + 1) * LANES].astype(BF16)
        return thunk

    def cast_rows(src, dst, c):
        def thunk():
            dst[c * 16:(c + 1) * 16, :] = src[c * 16:(c + 1) * 16, :].astype(BF16)
        return thunk

    casts = ([cast_lanes(w1_ref, w1b_ref, c) for c in range(w1_ref.shape[1] // LANES)]
             + [cast_lanes(w3_ref, w3b_ref, c) for c in range(w3_ref.shape[1] // LANES)]
             + [cast_rows(w2_ref, w2b_ref, c) for c in range(w2_ref.shape[0] // 16)])
    _row_copies(dest_ref, h_ref, xs_ref, sem, h_ref.shape[0], gather=False, side_work=casts)


def moe_dispatch(h2, dest, meta, n_slots, w1, w3, w2):
    n_tok, d = h2.shape
    tm = MOE_DISPATCH_TILE
    n_steps = n_tok // tm
    w1f, w3f, w2f = (w.reshape(-1, w.shape[-1]) for w in (w1, w3, w2))
    slab = lambda w: pl.BlockSpec((w.shape[0] // n_steps, w.shape[1]), lambda t, m: (t, 0))
    assert all(w.shape[0] % (16 * n_steps) == 0 for w in (w1f, w3f, w2f))
    grid_spec = pltpu.PrefetchScalarGridSpec(
        num_scalar_prefetch=1,
        grid=(n_steps,),
        in_specs=[pl.BlockSpec((None, 1, TOP_K * tm), lambda t, m: (t, 0, 0), memory_space=pltpu.SMEM),
                  pl.BlockSpec((tm, d), lambda t, m: (t, 0)),
                  slab(w1f), slab(w3f), slab(w2f)],
        out_specs=[pl.BlockSpec(memory_space=pl.ANY), slab(w1f), slab(w3f), slab(w2f)],
        scratch_shapes=[pltpu.VMEM((MOE_TILE_M, d), h2.dtype), pltpu.SemaphoreType.DMA,
                        pltpu.SemaphoreType.DMA],
    )
    xs, w1b, w3b, w2b = pl.pallas_call(
        _dispatch_kernel,
        grid_spec=grid_spec,
        out_shape=[jax.ShapeDtypeStruct((n_slots, d), h2.dtype)]
        + [jax.ShapeDtypeStruct(w.shape, BF16) for w in (w1f, w3f, w2f)],
        compiler_params=_params(("arbitrary",)),
        name="moe_dispatch",
    )(meta, dest.reshape(n_steps, 1, TOP_K * tm), h2, w1f, w3f, w2f)
    return xs, w1b.reshape(w1.shape), w3b.reshape(w3.shape), w2b.reshape(w2.shape)


def _combine_kernel(dest_ref, x1_ref, g_ref, mod_ref, ys_ref, o_ref, buf_ref, sem):
    tm = x1_ref.shape[0]
    _row_copies(dest_ref, ys_ref, buf_ref, sem, tm, gather=True)
    g = g_ref[...]
    moe = g[:, 0:1] * buf_ref[0] + g[:, 1:2] * buf_ref[1]
    o_ref[...] = x1_ref[...] + mod_ref[5, 0] * moe


def moe_combine(x1, gates, dest, ys, modp):
    b, s, d = x1.shape
    tm = MOE_COMBINE_TILE
    nt = s // tm
    return pl.pallas_call(
        _combine_kernel,
        grid=(b, nt),
        in_specs=[pl.BlockSpec((None, 1, TOP_K * tm), lambda i, t: (i * nt + t, 0, 0), memory_space=pltpu.SMEM),
                  pl.BlockSpec((None, tm, d), lambda i, t: (i, t, 0)),
                  pl.BlockSpec((None, tm, TOP_K), lambda i, t: (i, t, 0)),
                  pl.BlockSpec((None, N_MOD, 2, 1, d), lambda i, t: (i, 0, 0, 0, 0)),
                  pl.BlockSpec(memory_space=pl.ANY)],
        out_specs=pl.BlockSpec((None, tm, d), lambda i, t: (i, t, 0)),
        out_shape=jax.ShapeDtypeStruct((b, s, d), F32),
        scratch_shapes=[pltpu.VMEM((TOP_K, tm, d), F32), pltpu.SemaphoreType.DMA],
        compiler_params=_params(("arbitrary", "arbitrary")),
        name="moe_combine",
    )(dest.reshape(b * nt, 1, TOP_K * tm), x1, gates.reshape(b, s, TOP_K), modp, ys)


def moe_residual(x1, h2, logits, modp, w1, w3, w2):
    b, s, d = x1.shape
    n_tok = b * s
    tm = MOE_TILE_M
    gates, dest, tile_expert, n_valid, meta = route(logits.reshape(n_tok, N_EXPERTS), tm)
    n_slots = (n_tok * TOP_K // tm + N_EXPERTS) * tm
    xs, w1, w3, w2 = moe_dispatch(h2.reshape(n_tok, d), dest, meta, n_slots, w1, w3, w2)
    ys = grouped_swiglu(xs, tile_expert, n_valid, w1, w3, w2, tm, MOE_TILE_F)
    return moe_combine(x1, gates, dest, ys, modp)


def _proj1_kernel(x_ref, g_ref, mod_ref, w_ref, qan_ref, qb_ref, qbs_ref, kvan_ref, kbk_ref, kbv_ref,
                  cos_ref, sin_ref, gains_ref, ones_ref, hy_ref, q_ref, k_ref, v_ref):
    tm = x_ref.shape[0]
    row0 = pl.program_id(1) * tm
    h = _norm_mod(x_ref[...], g_ref[...], mod_ref, row0, 0, 1)
    z = jnp.dot(h.astype(BF16), w_ref[...], preferred_element_type=F32)
    hy_ref[...] = z[:, :HY_IN]

    def rms(a, g):
        return a * lax.rsqrt(jnp.mean(a * a, axis=-1, keepdims=True) + EPS) * g

    c_q, c_kv = HY_IN, HY_IN + MLA_Q_RANK
    c_r = c_kv + MLA_KV_RANK
    qa = rms(z[:, c_q:c_kv], qan_ref[...]).astype(BF16)
    kva = rms(z[:, c_kv:c_r], kvan_ref[...]).astype(BF16)
    q = jnp.dot(qa, qb_ref[...], preferred_element_type=F32)
    q_sw = jnp.dot(qa, qbs_ref[...], preferred_element_type=F32)
    kk = jnp.dot(kva, kbk_ref[...], preferred_element_type=F32)
    v_ref[...] = jnp.dot(kva, kbv_ref[...], preferred_element_type=F32).astype(BF16)
    k_rope = pltpu.roll(z[:, c_r:c_r + LANES], MLA_NOPE, 1)
    first_half = lax.broadcasted_iota(jnp.int32, (tm, LANES), 1) < MLA_NOPE + MLA_ROPE // 2
    sh = MLA_ROPE // 2
    k_rope_sw = jnp.where(first_half, pltpu.roll(k_rope, LANES - sh, 1), pltpu.roll(k_rope, sh, 1))
    cos, sin = cos_ref[...], sin_ref[...]
    qg, qg_sw, kg, kg_sw = (gains_ref[i:i + 1, :] for i in range(4))

    def head(a, a_sw, gain, gain_sw):
        inv = lax.rsqrt(_group_sums(a, ones_ref) / MLA_QK + EPS)
        return (a * (inv * gain) * cos + a_sw * (inv * gain_sw) * sin).astype(BF16)

    for hh in range(MLA_HEADS):
        sl = slice(hh * LANES, (hh + 1) * LANES)
        q_ref[:, sl] = head(q[:, sl], q_sw[:, sl], qg, qg_sw)
        k_ref[:, sl] = head(kk[:, sl] + k_rope, k_rope_sw, kg, kg_sw)


def _swap_rope_halves(a):
    lead = a.shape[:-1]
    g = a.reshape(*lead, -1, LANES)
    lo, mid = MLA_NOPE, MLA_NOPE + MLA_ROPE // 2
    g = jnp.concatenate([g[..., :lo], g[..., mid:MLA_QK], g[..., lo:mid], g[..., MLA_QK:]], axis=-1)
    return g.reshape(a.shape)


def proj1(xa, g, modp, w_in, qan, qb, kvan, kbk, kbv, cos, sin, q_gain, k_gain):
    b = xa.shape[0]
    tm = TOKEN_TILE
    tok = lambda width: pl.BlockSpec((None, tm, width), lambda i, t: (i, t, 0))
    full = lambda a: pl.BlockSpec(a.shape, lambda i, t: (0, 0))
    hw = MLA_HEADS * LANES
    qbs = _swap_rope_halves(qb)
    gains = jnp.concatenate([q_gain, _swap_rope_halves(q_gain), k_gain, _swap_rope_halves(k_gain)], axis=0)
    ones = jnp.ones((LANES, LANES), BF16)
    return pl.pallas_call(
        _proj1_kernel,
        grid=(b, S_ALL // tm),
        in_specs=[tok(D_MODEL), full(g),
                  pl.BlockSpec((None, N_MOD, 2, 1, D_MODEL), lambda i, t: (i, 0, 0, 0, 0)),
                  full(w_in), full(qan), full(qb), full(qbs), full(kvan), full(kbk), full(kbv),
                  pl.BlockSpec((tm, LANES), lambda i, t: (t, 0)),
                  pl.BlockSpec((tm, LANES), lambda i, t: (t, 0)),
                  full(gains), full(ones)],
        out_specs=[tok(HY_IN), tok(hw), tok(hw), tok(MLA_HEADS * MLA_V)],
        out_shape=[jax.ShapeDtypeStruct((b, S_ALL, HY_IN), F32),
                   jax.ShapeDtypeStruct((b, S_ALL, hw), BF16),
                   jax.ShapeDtypeStruct((b, S_ALL, hw), BF16),
                   jax.ShapeDtypeStruct((b, S_ALL, MLA_HEADS * MLA_V), BF16)],
        compiler_params=_params(("parallel", "parallel")),
        name="proj1",
    )(xa, g, modp, w_in, qan, qb, qbs, kvan, kbk, kbv, cos, sin, gains, ones)


def _merge1_kernel(hy_ref, att_ref, x_ref, mod_ref, g2_ref, w_hy_ref, w_att_ref, r_ref,
                   x1_ref, h2_ref, lg_ref):
    y = (jnp.dot(hy_ref[...].astype(BF16), w_hy_ref[...], preferred_element_type=F32)
         + jnp.dot(att_ref[...], w_att_ref[...], preferred_element_type=F32))
    x1 = x_ref[...] + mod_ref[2, 0] * y
    x1_ref[...] = x1
    h2 = _norm_mod(x1, g2_ref[...], mod_ref, 0, 3, 4)
    h2_ref[...] = h2
    lane = lax.broadcasted_iota(jnp.int32, lg_ref.shape, 1)
    logits = jnp.zeros(lg_ref.shape, F32)
    for e in range(N_EXPERTS):
        col = jnp.sum(h2 * r_ref[e:e + 1, :], axis=-1, keepdims=True)
        logits = jnp.where(lane == e, col, logits)
    lg_ref[...] = logits


def merge1(hy, att, xa, modp, g2, w_hy, w_att, router):
    b = xa.shape[0]
    tm = MERGE1_TILE
    tok = lambda width: pl.BlockSpec((None, tm, width), lambda i, t: (i, t, 0))
    full = lambda a: pl.BlockSpec(a.shape, lambda i, t: (0, 0))
    return pl.pallas_call(
        _merge1_kernel,
        grid=(b, SEQ // tm),
        in_specs=[tok(HY_WIDTH), tok(HALF_MIX), tok(D_MODEL),
                  pl.BlockSpec((None, N_MOD, 2, 1, D_MODEL), lambda i, t: (i, 0, 0, 0, 0)),
                  full(g2), full(w_hy), full(w_att), full(router)],
        out_specs=[tok(D_MODEL), tok(D_MODEL), tok(N_EXPERTS)],
        out_shape=[jax.ShapeDtypeStruct((b, SEQ, D_MODEL), F32),
                   jax.ShapeDtypeStruct((b, SEQ, D_MODEL), F32),
                   jax.ShapeDtypeStruct((b, SEQ, N_EXPERTS), F32)],
        compiler_params=_params(("parallel", "parallel")),
        name="merge1",
    )(hy, att, xa, modp, g2, w_hy, w_att, router)


HY_N = 2 * SEQ
HY_TW_ROWS = 256
HY_SPEC_FBLK = 256
HY_CONV_FBLK = 256
HY_LANE_TILES = SEQ // LANES


def _twiddle_kernel(ca_ref, sa_ref, cb_ref, sb_ref, fre_ref, fim_ref, ic_ref, is_ref):
    rows = fre_ref.shape[0]
    r = pl.program_id(0) * rows + lax.broadcasted_iota(jnp.int32, (rows, LANES), 0)
    alt_r = (1 - 2 * (r & 1)).astype(F32)
    cb, sb = cb_ref[...], sb_ref[...]
    for a in range(HY_LANE_TILES):
        ca, sa = ca_ref[:, a:a + 1], sa_ref[:, a:a + 1]
        c = ca * cb - sa * sb
        s = sa * cb + ca * sb
        col = a * LANES + lax.broadcasted_iota(jnp.int32, (rows, LANES), 1)
        alt_c = (1 - 2 * (col & 1)).astype(F32)
        w = jnp.where(col == 0, 1.0 / HY_N, 2.0 / HY_N)
        sl = slice(a * LANES, (a + 1) * LANES)
        fre_ref[:, sl] = c.astype(BF16)
        fim_ref[:, sl] = jnp.where(r == 0, alt_c, -s).astype(BF16)
        ic_ref[:, sl] = (c * w).astype(BF16)
        is_ref[:, sl] = jnp.where(col == 0, alt_r / HY_N, -s * w).astype(BF16)


def dft_matrices():
    idx = jnp.arange(SEQ, dtype=jnp.int32)[:, None]
    step = 2.0 * math.pi / HY_N
    ph_a = ((idx * (LANES * jnp.arange(HY_LANE_TILES, dtype=jnp.int32))[None, :]) % HY_N).astype(F32) * step
    ph_b = ((idx * jnp.arange(LANES, dtype=jnp.int32)[None, :]) % HY_N).astype(F32) * step
    rows = HY_TW_ROWS
    tab = lambda width: pl.BlockSpec((rows, width), lambda i: (i, 0))
    out = jax.ShapeDtypeStruct((SEQ, SEQ), BF16)
    return pl.pallas_call(
        _twiddle_kernel,
        grid=(SEQ // rows,),
        in_specs=[tab(HY_LANE_TILES), tab(HY_LANE_TILES), tab(LANES), tab(LANES)],
        out_specs=[tab(SEQ)] * 4,
        out_shape=[out] * 4,
        compiler_params=_params(("parallel",)),
        name="twiddle",
    )(jnp.cos(ph_a), jnp.sin(ph_a), jnp.cos(ph_b), jnp.sin(ph_b))


def _spec_kernel(fre_ref, fim_ref, h_ref, k_ref):
    w = HY_WIDTH
    h = h_ref[...].astype(BF16)
    re = jnp.dot(fre_ref[...], h, preferred_element_type=F32)
    im = jnp.dot(fim_ref[...], h, preferred_element_type=F32)
    hb0 = h[0:1, w:].astype(F32)
    first = (pl.program_id(1) * re.shape[0] + lax.broadcasted_iota(jnp.int32, (re.shape[0], 1), 0)) == 0
    k_ref[0] = re[:, :w] + re[:, w:] - hb0
    k_ref[1] = jnp.where(first, im[:, :w] + im[:, w:] - hb0, im[:, :w] - im[:, w:])


def filter_spectra(filt, fre, fim):
    fb = HY_SPEC_FBLK
    return pl.pallas_call(
        _spec_kernel,
        grid=(HY_ORDER, SEQ // fb),
        in_specs=[pl.BlockSpec((fb, SEQ), lambda o, f: (f, 0)),
                  pl.BlockSpec((fb, SEQ), lambda o, f: (f, 0)),
                  pl.BlockSpec((SEQ, 2 * HY_WIDTH), lambda o, f: (0, o))],
        out_specs=pl.BlockSpec((None, 2, fb, HY_WIDTH), lambda o, f: (o, 0, f, 0)),
        out_shape=jax.ShapeDtypeStruct((HY_ORDER, 2, SEQ, HY_WIDTH), F32),
        compiler_params=_params(("parallel", "parallel")),
        name="filter_spectra",
    )(fre, fim, filt)


def _short_conv(z_ref, w_ref, b_ref, part):
    z = z_ref[...]
    l = z.shape[0]
    t = lax.broadcasted_iota(jnp.int32, (l, 1), 0)
    prev = jnp.where(t == 0, 0.0, pltpu.roll(z, 1, 0))
    nxt = jnp.where(t == l - 1, 0.0, pltpu.roll(z, l - 1, 0))
    cs = slice(part * HY_WIDTH, (part + 1) * HY_WIDTH)
    return (w_ref[0:1, cs] * prev + w_ref[1:2, cs] * z + w_ref[2:3, cs] * nxt) + b_ref[:, cs]


def _hyconv_kernel(u_ref, xg_ref, cw_ref, cb_ref, fre_ref, fim_ref, ic_ref, is_ref, k_ref, skip_ref,
                   y_ref, ub_ref, acc_ref, *, order):
    f = pl.program_id(1)

    def u_f32():
        return _short_conv(u_ref, cw_ref, cb_ref, 0) if order == 0 else u_ref[...]

    @pl.when(f == 0)
    def _():
        ub_ref[...] = u_f32().astype(BF16)

    x_re = jnp.dot(fre_ref[...], ub_ref[...], preferred_element_type=F32)
    x_im = jnp.dot(fim_ref[...], ub_ref[...], preferred_element_type=F32)
    k_re, k_im = k_ref[0], k_ref[1]
    first = (f * x_re.shape[0] + lax.broadcasted_iota(jnp.int32, (x_re.shape[0], 1), 0)) == 0
    y_re = x_re * k_re - jnp.where(first, 0.0, x_im * k_im)
    y_im = x_im * jnp.where(first, k_im, k_re) + jnp.where(first, 0.0, x_re * k_im)
    part = (jnp.dot(ic_ref[...], y_re.astype(BF16), preferred_element_type=F32)
            + jnp.dot(is_ref[...], y_im.astype(BF16), preferred_element_type=F32))
    _accumulate(acc_ref, part, f)

    @pl.when(f == pl.num_programs(1) - 1)
    def _():
        xg = _short_conv(xg_ref, cw_ref, cb_ref, order + 1)
        y_ref[...] = xg * (acc_ref[...] + skip_ref[order:order + 1, :] * u_f32())


def hyena_conv(order, u, hyz, conv_w, conv_b, fre, fim, ic, is_, spectra, skip):
    b = hyz.shape[0]
    fb = HY_CONV_FBLK
    w = HY_WIDTH
    lat = lambda part: pl.BlockSpec((None, SEQ, w), lambda i, f: (i, 0, part))
    full = lambda a: pl.BlockSpec(a.shape, lambda i, f: (0, 0))
    return pl.pallas_call(
        functools.partial(_hyconv_kernel, order=order),
        grid=(b, SEQ // fb),
        in_specs=[lat(0), lat(order + 1), full(conv_w), full(conv_b),
                  pl.BlockSpec((fb, SEQ), lambda i, f: (f, 0)),
                  pl.BlockSpec((fb, SEQ), lambda i, f: (f, 0)),
                  pl.BlockSpec((SEQ, fb), lambda i, f: (0, f)),
                  pl.BlockSpec((SEQ, fb), lambda i, f: (0, f)),
                  pl.BlockSpec((None, 2, fb, w), lambda i, f: (order, 0, f, 0)),
                  full(skip)],
        out_specs=pl.BlockSpec((None, SEQ, w), lambda i, f: (i, 0, 0)),
        out_shape=jax.ShapeDtypeStruct((b, SEQ, w), F32),
        scratch_shapes=[pltpu.VMEM((SEQ, w), BF16), pltpu.VMEM((SEQ, w), F32)],
        compiler_params=_params(("parallel", "arbitrary")),
        name="hyena_conv",
    )(u, hyz, conv_w, conv_b, fre, fim, ic, is_, spectra, skip)


def hyena_filters(length, w1, b1, w2, b2, w3, freq):
    hp = lax.Precision.HIGHEST
    t = jnp.arange(length, dtype=F32)[:, None]
    t_norm = t / max(length - 1, 1)
    bands = jnp.linspace(1e-4, HY_BANDS - 1, HY_BANDS, dtype=F32)
    ang = 2.0 * math.pi * t * bands / length
    z = jnp.concatenate([t_norm, jnp.cos(ang), -jnp.sin(ang)], axis=-1)
    h = jnp.sin(freq * (jnp.dot(z, w1, precision=hp) + b1))
    h = jnp.sin(freq * (jnp.dot(h, w2, precision=hp) + b2))
    h = jnp.dot(h, w3, precision=hp)
    deltas = jnp.abs(jnp.linspace(HY_MIN_DECAY, HY_MAX_DECAY, HY_WIDTH, dtype=F32))
    window = jnp.exp(-t_norm * deltas) + HY_SHIFT
    return h.reshape(length, HY_ORDER, 2, HY_WIDTH) * window[:, None, None, :]


def hyena(hyz, conv_w, conv_b, fw1, fb1, fw2, fb2, fw3, freq, skip):
    fre, fim, ic, is_ = dft_matrices()
    filt = hyena_filters(SEQ, fw1, fb1, fw2, fb2, fw3, freq).reshape(SEQ, HY_ORDER * 2 * HY_WIDTH)
    spectra = filter_spectra(filt, fre, fim)
    y = hyz
    for o in range(HY_ORDER):
        y = hyena_conv(o, y, hyz, conv_w, conv_b.reshape(1, -1), fre, fim, ic, is_, spectra, skip)
    return y


def _grid_angles(rot_dim):
    n_freq = rot_dim // 4
    inv_freq = ROPE_THETA ** (-jnp.arange(n_freq, dtype=F32) / n_freq)
    t = jnp.arange(SEQ)
    r = (t // GRID_W).astype(F32)
    c_ = (t % GRID_W).astype(F32)
    return jnp.concatenate([r[:, None] * inv_freq, c_[:, None] * inv_freq], axis=-1)


def _rope_tables(rot_dim, lane_offsets):
    ang = _grid_angles(rot_dim)
    c, s = jnp.cos(ang), jnp.sin(ang)
    cos_parts, sin_parts, lane = [], [], 0
    for off in lane_offsets:
        cos_parts += [jnp.ones((SEQ, off - lane), F32), c, c]
        sin_parts += [jnp.zeros((SEQ, off - lane), F32), -s, s]
        lane = off + rot_dim
    cos_parts.append(jnp.ones((SEQ, LANES - lane), F32))
    sin_parts.append(jnp.zeros((SEQ, LANES - lane), F32))
    cos = jnp.concatenate([jnp.concatenate(cos_parts, axis=1), jnp.ones((CTX_LEN, LANES), F32)], axis=0)
    sin = jnp.concatenate([jnp.concatenate(sin_parts, axis=1), jnp.zeros((CTX_LEN, LANES), F32)], axis=0)
    return cos, sin


def _block_diag(w):
    nb, bs, _ = w.shape
    eye = jnp.eye(nb, dtype=w.dtype)
    return (eye[:, None, :, None] * w[:, :, None, :]).reshape(nb * bs, nb * bs)


def _gqa_pair_order():
    g = GQA_Q_HEADS // GQA_KV_HEADS
    heads = []
    for p in range(g):
        heads += [p, g + p]
    return np.concatenate([np.arange(h * HEAD_DIM, (h + 1) * HEAD_DIM) for h in heads])


def _pad_heads(w, n_heads, width):
    k = w.shape[0]
    w = w.reshape(k, n_heads, width)
    return jnp.pad(w, ((0, 0), (0, 0), (0, LANES - width))).reshape(k, n_heads * LANES)


def kernel(x, c, ctx, c_ctx, mod_w, mod_b, norm1_g, norm2_g, ab_w_in, ab_w_out, lru_conv_w, lru_conv_b, lru_w_a, lru_b_a, lru_w_x, lru_b_x, lru_lambda, gqa_q_norm, gqa_k_norm, ffn_w1, ffn_w3, ffn_w2, cd_w_in, cd_w_out, hy_conv_w, hy_conv_b, hy_filt_w1, hy_filt_b1, hy_filt_w2, hy_filt_b2, hy_filt_w3, hy_sin_freq, hy_skip, mla_q_a_norm, mla_q_b, mla_kv_a_norm, mla_kv_b, mla_q_norm, mla_k_norm, moe_router, moe_w1, moe_w3, moe_w2):
    batch = x.shape[0]
    bf = lambda w: w.astype(BF16)
    row = lambda v: v.reshape(1, -1)

    silu_all = jnp.concatenate([jax.nn.silu(c), jax.nn.silu(c_ctx)[None, :],
                                jnp.zeros((16 - batch - 1, D_MODEL), F32)], axis=0)
    mods = []
    for layer in range(DEPTH):
        m = matmul(silu_all, mod_w[layer], 16, 1536) + mod_b[layer]
        lat = m[:batch].reshape(batch, N_MOD, 1, D_MODEL)
        cx = jnp.broadcast_to(m[batch].reshape(1, N_MOD, 1, D_MODEL), lat.shape)
        mods.append(jnp.stack([lat, cx], axis=2))

    xa = jnp.concatenate([x, ctx], axis=1)

    perm = _gqa_pair_order()
    q0 = 2 * LRU_WIDTH
    w_in0 = ab_w_in[0]
    w_in0 = jnp.concatenate([w_in0[:, :q0], w_in0[:, q0 + perm], w_in0[:, q0 + GQA_Q_HEADS * HEAD_DIM:]], axis=1)
    cos_g, sin_g = _rope_tables(HEAD_DIM, (0, HEAD_DIM))
    q_gain = row(jnp.tile(gqa_q_norm[0], 2) * HEAD_DIM ** -0.5)
    k_gain = row(jnp.tile(gqa_k_norm[0], 2))
    xr, gate, q, k, v = proj0(xa, row(norm1_g[0]), mods[0], bf(w_in0), cos_g, sin_g, q_gain, k_gain)

    w_gates = jnp.concatenate([_block_diag(lru_w_a[0, 0]), _block_diag(lru_w_x[0, 0]),
                               _block_diag(lru_w_a[0, 1]), _block_diag(lru_w_x[0, 1])], axis=1)
    b_gates = jnp.concatenate([lru_b_a[0, 0].reshape(-1), lru_b_x[0, 0].reshape(-1),
                               lru_b_a[0, 1].reshape(-1), lru_b_x[0, 1].reshape(-1)])
    rec = rglru(xr, lru_conv_w[0], row(lru_conv_b[0]), bf(w_gates), row(b_gates), lru_lambda[0])

    att_l = attention_latent(q, k, v, GQA_HEADS, GQA_TILE_Q)
    att_c = attention_context(q, k, v, GQA_HEADS)
    w_out0 = ab_w_out[0]
    x1 = merge0(rec, gate, att_l, att_c, xa, mods[0], bf(w_out0[:LRU_WIDTH]), bf(w_out0[LRU_WIDTH:][perm]))
    xa = ffn_residual(x1, mods[0], row(norm2_g[0]), bf(ffn_w1[0]), bf(ffn_w3[0]), bf(ffn_w2[0]))

    w_in1 = jnp.pad(cd_w_in[0], ((0, 0), (0, LANES - MLA_ROPE)))
    cos_m, sin_m = _rope_tables(MLA_ROPE, (MLA_NOPE,))
    qb = _pad_heads(mla_q_b[0], MLA_HEADS, MLA_QK)
    kvb = mla_kv_b[0].reshape(MLA_KV_RANK, MLA_HEADS, MLA_NOPE + MLA_V)
    kbk = _pad_heads(kvb[:, :, :MLA_NOPE].reshape(MLA_KV_RANK, -1), MLA_HEADS, MLA_NOPE)
    kbv = kvb[:, :, MLA_NOPE:].reshape(MLA_KV_RANK, -1)
    pad_gain = lambda g_: row(jnp.pad(g_, (0, LANES - MLA_QK)))
    hyz, q, k, v = proj1(xa, row(norm1_g[1]), mods[1], bf(w_in1), row(mla_q_a_norm[0]), bf(qb),
                         row(mla_kv_a_norm[0]), bf(kbk), bf(kbv), cos_m, sin_m,
                         pad_gain(mla_q_norm[0] * MLA_QK ** -0.5), pad_gain(mla_k_norm[0]))
    att = attention_latent(q, k, v, MLA_HEAD_GROUPS, MLA_TILE_Q)
    hy = hyena(hyz, hy_conv_w[0], hy_conv_b[0], hy_filt_w1[0], hy_filt_b1[0], hy_filt_w2[0],
               hy_filt_b2[0], hy_filt_w3[0], hy_sin_freq[0], hy_skip[0])
    w_out1 = cd_w_out[0]
    x1, h2, logits = merge1(hy, att, xa, mods[1], row(norm2_g[1]),
                            bf(w_out1[:HY_WIDTH]), bf(w_out1[HY_WIDTH:]), moe_router[0].T)
    return moe_residual(x1, h2, logits, mods[1], moe_w1[0], moe_w3[0], moe_w2[0])
```

```python
import functools
import math

import jax
import jax.numpy as jnp
import numpy as np
from jax import lax
from jax.experimental import pallas as pl
from jax.experimental.pallas import tpu as pltpu

F32 = jnp.float32
BF16 = jnp.bfloat16

D_MODEL = 1024
SEQ = 2048
CTX_LEN = 256
S_ALL = SEQ + CTX_LEN
DEPTH = 2
GRID_W = 64
HEAD_DIM = 64
HALF_MIX = D_MODEL // 2
ROPE_THETA = 10000.0
EPS = 1e-6
N_MOD = 6
LRU_WIDTH = HALF_MIX
LRU_BLOCKS = LRU_WIDTH // HEAD_DIM
LRU_CONV = 4
LRU_C = 8.0
GQA_Q_HEADS = HALF_MIX // HEAD_DIM
GQA_KV_HEADS = 2
HY_WIDTH = HALF_MIX
HY_ORDER = 2
HY_CONV = 3
HY_BANDS = 16
HY_TARGET = 1e-2
HY_FAST_PCT = 0.3
HY_SLOW_PCT = 1.5
HY_MIN_DECAY = math.log(HY_TARGET) / HY_SLOW_PCT
HY_MAX_DECAY = math.log(HY_TARGET) / HY_FAST_PCT
HY_SHIFT = 0.05
HY_IN = (HY_ORDER + 1) * HY_WIDTH
MLA_HEADS = HALF_MIX // HEAD_DIM
MLA_Q_RANK = D_MODEL // 4
MLA_KV_RANK = D_MODEL // 8
MLA_NOPE = HEAD_DIM
MLA_ROPE = HEAD_DIM // 2
MLA_V = HEAD_DIM
MLA_QK = MLA_NOPE + MLA_ROPE
N_EXPERTS = 8
TOP_K = 2

LANES = 128
SUBLANES = 8
V7X_VMEM_LIMIT_BYTES = 56 * 1024 * 1024

TOKEN_TILE = 768
FFN_TILE_M = 768
FFN_TILE_F = 256
GQA_TILE_Q = 512
MLA_TILE_Q = 512
MERGE1_TILE = 512
MOE_TILE_M = 512
MOE_TILE_F = 1792
MOE_ROUTE_BLOCK = 256
MOE_DISPATCH_TILE = 256
MOE_COMBINE_TILE = 256

LRU_GAP = SUBLANES
LRU_LAT0 = CTX_LEN + LRU_GAP
LRU_CHUNK = 256
LRU_ROWS = 2560
LRU_PAD_FRONT = SUBLANES

assert S_ALL % TOKEN_TILE == 0 and S_ALL % FFN_TILE_M == 0
assert SEQ % GQA_TILE_Q == 0 and SEQ % MLA_TILE_Q == 0
assert LRU_ROWS % LRU_CHUNK == 0 and LRU_ROWS >= LRU_LAT0 + SEQ + SUBLANES


def _params(sem):
    return pltpu.CompilerParams(dimension_semantics=sem, vmem_limit_bytes=V7X_VMEM_LIMIT_BYTES)


def _norm_mod(x, g, mod_pair, row0, shift_idx, scale_idx):
    y = x * lax.rsqrt(jnp.mean(x * x, axis=-1, keepdims=True) + EPS) * g
    shift = _mod_rows(mod_pair, shift_idx, row0, x.shape[0])
    scale = _mod_rows(mod_pair, scale_idx, row0, x.shape[0])
    return y * (1.0 + scale) + shift


def _mod_rows(mod_pair, idx, row0, rows):
    row = row0 + lax.broadcasted_iota(jnp.int32, (rows, 1), 0)
    return jnp.where(row >= SEQ, mod_pair[idx, 1], mod_pair[idx, 0])


def _mm_kernel(x_ref, w_ref, o_ref):
    o_ref[...] = jnp.dot(x_ref[...].astype(BF16), w_ref[...].astype(BF16),
                         preferred_element_type=F32)


def stacked_matmul(x, w, tn):
    m, k = x.shape
    n_l, _, n = w.shape
    assert n % tn == 0
    return pl.pallas_call(
        _mm_kernel,
        grid=(n_l, n // tn),
        in_specs=[pl.BlockSpec((m, k), lambda l, j: (0, 0)),
                  pl.BlockSpec((None, k, tn), lambda l, j: (l, 0, j))],
        out_specs=pl.BlockSpec((None, m, tn), lambda l, j: (l, 0, j)),
        out_shape=jax.ShapeDtypeStruct((n_l, m, n), F32),
        compiler_params=_params(("parallel", "parallel")),
        name="matmul",
    )(x, w)


def _rope_pair(a, cos, sin, first_half, shift):
    rot = jnp.where(first_half, pltpu.roll(a, LANES - shift, 1), pltpu.roll(a, shift, 1))
    return a * cos + rot * sin


def _group_sums(a, ones_ref):
    return jnp.dot((a * a).astype(BF16), ones_ref[...], preferred_element_type=F32)


def _proj0_kernel(x_ref, g_ref, mod_ref, w_ref, cos_ref, sin_ref, qg_ref, kg_ref, ones_ref,
                  xr_ref, gate_ref, q_ref, k_ref, v_ref):
    tm = x_ref.shape[0]
    row0 = pl.program_id(1) * tm
    h = _norm_mod(x_ref[...], g_ref[...], mod_ref, row0, 0, 1)
    z = jnp.dot(h.astype(BF16), w_ref[...], preferred_element_type=F32)
    xr_ref[...] = z[:, :LRU_WIDTH]
    gate_ref[...] = z[:, LRU_WIDTH:2 * LRU_WIDTH]
    cos, sin = cos_ref[...], sin_ref[...]
    first_half = (lax.broadcasted_iota(jnp.int32, (tm, LANES), 1) & (HEAD_DIM // 2)) == 0

    def head_pair(a, gain):
        inv = lax.rsqrt(_group_sums(a, ones_ref) / HEAD_DIM + EPS)
        return _rope_pair(a * inv * gain, cos, sin, first_half, HEAD_DIM // 2).astype(BF16)

    q0 = 2 * LRU_WIDTH
    for p in range(GQA_Q_HEADS // 2):
        q_ref[:, p * LANES:(p + 1) * LANES] = head_pair(z[:, q0 + p * LANES:q0 + (p + 1) * LANES], qg_ref[...])
    k0 = q0 + GQA_Q_HEADS * HEAD_DIM
    k_ref[...] = head_pair(z[:, k0:k0 + LANES], kg_ref[...])
    v_ref[...] = z[:, k0 + LANES:k0 + 2 * LANES].astype(BF16)


def proj0(xa, g, modp, w_in, cos, sin, q_gain, k_gain):
    b = xa.shape[0]
    tm = TOKEN_TILE
    n = w_in.shape[1]
    tok = lambda width: pl.BlockSpec((None, tm, width), lambda i, t: (i, t, 0))
    full = lambda r, c_: pl.BlockSpec((r, c_), lambda i, t: (0, 0))
    half = jnp.arange(LANES) // HEAD_DIM
    ones_pair = (half[:, None] == half[None, :]).astype(BF16)
    return pl.pallas_call(
        _proj0_kernel,
        grid=(b, S_ALL // tm),
        in_specs=[tok(D_MODEL), full(1, D_MODEL),
                  pl.BlockSpec((None, N_MOD, 2, 1, D_MODEL), lambda i, t: (i, 0, 0, 0, 0)),
                  full(D_MODEL, n),
                  pl.BlockSpec((tm, LANES), lambda i, t: (t, 0)),
                  pl.BlockSpec((tm, LANES), lambda i, t: (t, 0)),
                  full(1, LANES), full(1, LANES), full(LANES, LANES)],
        out_specs=[tok(LRU_WIDTH), tok(LRU_WIDTH), tok(GQA_Q_HEADS * HEAD_DIM), tok(LANES), tok(LANES)],
        out_shape=[jax.ShapeDtypeStruct((b, S_ALL, LRU_WIDTH), F32),
                   jax.ShapeDtypeStruct((b, S_ALL, LRU_WIDTH), F32),
                   jax.ShapeDtypeStruct((b, S_ALL, GQA_Q_HEADS * HEAD_DIM), BF16),
                   jax.ShapeDtypeStruct((b, S_ALL, LANES), BF16),
                   jax.ShapeDtypeStruct((b, S_ALL, LANES), BF16)],
        compiler_params=_params(("parallel", "parallel")),
        name="proj0",
    )(xa, g, modp, w_in, cos, sin, q_gain, k_gain, ones_pair)


def _lru_kernel(xr_ref, cw_ref, cb_ref, wg_ref, bg_ref, lam_ref, rec_ref,
                pad_ref, af_ref, hf_ref, ab_ref, hb_ref):
    w = LRU_WIDTH
    pad_ref[...] = jnp.zeros_like(pad_ref)
    pad_ref[LRU_PAD_FRONT:LRU_PAD_FRONT + CTX_LEN, :] = xr_ref[SEQ:S_ALL, :]
    pad_ref[LRU_PAD_FRONT + LRU_LAT0:LRU_PAD_FRONT + LRU_LAT0 + SEQ, :] = xr_ref[0:SEQ, :]
    lam = lam_ref[...]
    neg_c_softplus = -LRU_C * (jnp.maximum(-lam, 0.0) + jnp.log(1.0 + jnp.exp(-jnp.abs(lam))))
    left = LRU_CONV // 2
    for ch in range(LRU_ROWS // LRU_CHUNK):
        r0 = ch * LRU_CHUNK
        u = cb_ref[...]
        for j in range(LRU_CONV):
            start = LRU_PAD_FRONT + r0 - left + j
            u = u + cw_ref[j:j + 1, :] * pad_ref[start:start + LRU_CHUNK, :]
        gz = jnp.dot(u.astype(BF16), wg_ref[...], preferred_element_type=F32) + bg_ref[...]
        for d, (a_ref, h_ref) in enumerate(((af_ref, hf_ref), (ab_ref, hb_ref))):
            r = jax.nn.sigmoid(gz[:, 2 * d * w:(2 * d + 1) * w])
            i = jax.nn.sigmoid(gz[:, (2 * d + 1) * w:(2 * d + 2) * w])
            a = jnp.exp(neg_c_softplus[d:d + 1, :] * r)
            a_ref[r0:r0 + LRU_CHUNK, :] = a
            h_ref[r0:r0 + LRU_CHUNK, :] = jnp.sqrt(1.0 - a * a) * (i * u)

    n_ctx_groups = CTX_LEN // SUBLANES
    sub = lax.broadcasted_iota(jnp.int32, (SUBLANES, w), 0)

    def tile_scan(a, b, descending):
        for s in (1, 2, 4):
            if descending:
                ok, shift = sub < SUBLANES - s, SUBLANES - s
            else:
                ok, shift = sub >= s, s
            a_s = jnp.where(ok, pltpu.roll(a, shift, 0), 1.0)
            b_s = jnp.where(ok, pltpu.roll(b, shift, 0), 0.0)
            a, b = a * a_s, a * b_s + b
        return a, b

    def group(tt, carry):
        hf, hb = carry
        is_lat = tt >= n_ctx_groups
        rf = pl.ds(pl.multiple_of(tt * SUBLANES + jnp.where(is_lat, LRU_GAP, 0), SUBLANES), SUBLANES)
        rb = pl.ds(pl.multiple_of(
            jnp.where(is_lat, LRU_LAT0 + SEQ + CTX_LEN, CTX_LEN) - (tt + 1) * SUBLANES, SUBLANES), SUBLANES)
        a, b = tile_scan(af_ref[rf, :], hf_ref[rf, :], False)
        h = a * hf + b
        hf_ref[rf, :] = h
        hf = jnp.broadcast_to(h[SUBLANES - 1:SUBLANES, :], (SUBLANES, w))
        a, b = tile_scan(ab_ref[rb, :], hb_ref[rb, :], True)
        h = a * hb + b
        hb_ref[rb, :] = h
        hb = jnp.broadcast_to(h[0:1, :], (SUBLANES, w))
        return hf, hb

    zero = jnp.zeros((SUBLANES, w), F32)
    lax.fori_loop(0, S_ALL // SUBLANES, group, (zero, zero))
    rec_ref[0:SEQ, :] = hf_ref[LRU_LAT0:LRU_LAT0 + SEQ, :] + hb_ref[LRU_LAT0:LRU_LAT0 + SEQ, :]
    rec_ref[SEQ:S_ALL, :] = hf_ref[0:CTX_LEN, :] + hb_ref[0:CTX_LEN, :]


def rglru(xr, conv_w, conv_b, w_gates, b_gates, lam):
    b = xr.shape[0]
    w = LRU_WIDTH
    full = lambda r, c_: pl.BlockSpec((r, c_), lambda i: (0, 0))
    rows = pltpu.VMEM((LRU_ROWS, w), F32)
    return pl.pallas_call(
        _lru_kernel,
        grid=(b,),
        in_specs=[pl.BlockSpec((None, S_ALL, w), lambda i: (i, 0, 0)),
                  full(LRU_CONV, w), full(1, w), full(w, 4 * w), full(1, 4 * w), full(2, w)],
        out_specs=pl.BlockSpec((None, S_ALL, w), lambda i: (i, 0, 0)),
        out_shape=jax.ShapeDtypeStruct((b, S_ALL, w), F32),
        scratch_shapes=[pltpu.VMEM((LRU_ROWS + 2 * LRU_PAD_FRONT, w), F32), rows, rows, rows, rows],
        compiler_params=_params(("parallel",)),
        name="rglru",
    )(xr, conv_w, conv_b, w_gates, b_gates, lam)


def _attn_kernel(q_ref, k_ref, v_ref, *rest, heads):
    o_ref = rest[-1]
    tq, sk = q_ref.shape[0], k_ref.shape[0]
    q_lo = lax.broadcasted_iota(jnp.int32, (tq, LANES), 1) < HEAD_DIM
    v_lo = lax.broadcasted_iota(jnp.int32, (sk, LANES), 1) < HEAD_DIM

    def keep(x, is_lo, half):
        if half is None:
            return x
        return jnp.where(is_lo if half == 0 else jnp.logical_not(is_lo), x, jnp.zeros_like(x))

    for og, members in enumerate(heads):
        acc = None
        for qg, qh, kg, vg, vh in members:
            q = keep(q_ref[:, qg * LANES:(qg + 1) * LANES], q_lo, qh)
            s = lax.dot_general(q, k_ref[:, kg * LANES:(kg + 1) * LANES],
                                (((1,), (1,)), ((), ())), preferred_element_type=F32)
            p = jnp.exp(s - jnp.max(s, axis=-1, keepdims=True))
            l = jnp.sum(p, axis=-1, keepdims=True)
            v = keep(v_ref[:, vg * LANES:(vg + 1) * LANES], v_lo, vh)
            o = jnp.dot(p.astype(BF16), v, preferred_element_type=F32) / l
            acc = o if acc is None else acc + o
        o_ref[:, og * LANES:(og + 1) * LANES] = acc.astype(o_ref.dtype)


def attention(q, k, v, heads, q_block, n_q_blocks, tq, k_rows):
    b, s, qw = q.shape
    kw, vw = k.shape[-1], v.shape[-1]
    ow = len(heads) * LANES
    kv_spec = lambda width: pl.BlockSpec((None, k_rows, width), lambda i, t: (i, s // k_rows - 1, 0))
    return pl.pallas_call(
        functools.partial(_attn_kernel, heads=heads),
        grid=(b, n_q_blocks),
        in_specs=[pl.BlockSpec((None, tq, qw), lambda i, t: (i, q_block + t, 0)), kv_spec(kw), kv_spec(vw)],
        out_specs=pl.BlockSpec((None, tq, ow), lambda i, t: (i, t, 0)),
        out_shape=jax.ShapeDtypeStruct((b, n_q_blocks * tq, ow), BF16),
        compiler_params=_params(("parallel", "parallel")),
        name="attention",
    )(q, k, v)


GQA_HEADS = tuple(((p, 0, 0, 0, 0), (p, 1, 0, 0, 1)) for p in range(GQA_Q_HEADS // 2))
MLA_HEAD_GROUPS = tuple(((2 * p, None, 2 * p, p, 0), (2 * p + 1, None, 2 * p + 1, p, 1))
                        for p in range(MLA_HEADS // 2))


def attention_latent(q, k, v, heads, tq):
    return attention(q, k, v, heads, 0, SEQ // tq, tq, S_ALL)


def attention_context(q, k, v, heads):
    return attention(q, k, v, heads, SEQ // CTX_LEN, 1, CTX_LEN, CTX_LEN)


def _merge0_kernel(rec_ref, gate_ref, att_l_ref, att_c_ref, x_ref, mod_ref,
                   w_rec_ref, w_att_ref, x1_ref):
    tm = x_ref.shape[0]
    row0 = pl.program_id(1) * tm
    rg = (rec_ref[...] * jax.nn.gelu(gate_ref[...])).astype(BF16)
    att = jnp.where(row0 >= SEQ, att_c_ref[...], att_l_ref[...])
    y = (jnp.dot(rg, w_rec_ref[...], preferred_element_type=F32)
         + jnp.dot(att, w_att_ref[...], preferred_element_type=F32))
    x1_ref[...] = x_ref[...] + _mod_rows(mod_ref, 2, row0, tm) * y


def merge0(rec, gate, att_l, att_c, xa, modp, w_rec, w_att):
    b = xa.shape[0]
    tm = CTX_LEN
    n_lat = SEQ // tm
    tok = lambda width: pl.BlockSpec((None, tm, width), lambda i, t: (i, t, 0))
    full = lambda r, c_: pl.BlockSpec((r, c_), lambda i, t: (0, 0))
    return pl.pallas_call(
        _merge0_kernel,
        grid=(b, S_ALL // tm),
        in_specs=[tok(LRU_WIDTH), tok(LRU_WIDTH),
                  pl.BlockSpec((None, tm, HALF_MIX), lambda i, t: (i, jnp.minimum(t, n_lat - 1), 0)),
                  pl.BlockSpec((None, tm, HALF_MIX), lambda i, t: (i, 0, 0)),
                  tok(D_MODEL),
                  pl.BlockSpec((None, N_MOD, 2, 1, D_MODEL), lambda i, t: (i, 0, 0, 0, 0)),
                  full(LRU_WIDTH, D_MODEL), full(HALF_MIX, D_MODEL)],
        out_specs=tok(D_MODEL),
        out_shape=jax.ShapeDtypeStruct((b, S_ALL, D_MODEL), F32),
        compiler_params=_params(("parallel", "parallel")),
        name="merge0",
    )(rec, gate, att_l, att_c, xa, modp, w_rec, w_att)


def _swiglu_partial(x, w1_ref, w3_ref, w2_ref):
    h1 = jnp.dot(x, w1_ref[...], preferred_element_type=F32)
    h3 = jnp.dot(x, w3_ref[...], preferred_element_type=F32)
    act = (h1 * jax.nn.sigmoid(h1) * h3).astype(BF16)
    return jnp.dot(act, w2_ref[...], preferred_element_type=F32)


def _accumulate(acc_ref, part, j):
    @pl.when(j == 0)
    def _():
        acc_ref[...] = part

    @pl.when(j > 0)
    def _():
        acc_ref[...] += part


def _ffn_kernel(x1_ref, mod_ref, g2_ref, w1_ref, w3_ref, w2_ref, o_ref, act_ref):
    tm = x1_ref.shape[0]
    row0 = pl.program_id(1) * tm
    x = _norm_mod(x1_ref[...], g2_ref[...], mod_ref, row0, 3, 4).astype(BF16)
    for c in range(w1_ref.shape[1] // FFN_TILE_F):
        sl = slice(c * FFN_TILE_F, (c + 1) * FFN_TILE_F)
        h1 = jnp.dot(x, w1_ref[:, sl], preferred_element_type=F32)
        h3 = jnp.dot(x, w3_ref[:, sl], preferred_element_type=F32)
        act_ref[:, sl] = (h1 * jax.nn.sigmoid(h1) * h3).astype(BF16)
    y = jnp.dot(act_ref[...], w2_ref[...], preferred_element_type=F32)
    o_ref[...] = x1_ref[...] + _mod_rows(mod_ref, 5, row0, tm) * y


def ffn_residual(x1, modp, g2, w1, w3, w2):
    b = x1.shape[0]
    tm = FFN_TILE_M
    f = w1.shape[1]
    assert f % FFN_TILE_F == 0
    tok = lambda: pl.BlockSpec((None, tm, D_MODEL), lambda i, t: (i, t, 0))
    resident = lambda a: pl.BlockSpec(a.shape, lambda i, t: (0, 0), pipeline_mode=pl.Buffered(1))
    return pl.pallas_call(
        _ffn_kernel,
        grid=(b, S_ALL // tm),
        in_specs=[tok(),
                  pl.BlockSpec((None, N_MOD, 2, 1, D_MODEL), lambda i, t: (i, 0, 0, 0, 0)),
                  pl.BlockSpec((1, D_MODEL), lambda i, t: (0, 0)),
                  resident(w1), resident(w3), resident(w2)],
        out_specs=tok(),
        out_shape=jax.ShapeDtypeStruct((b, S_ALL, D_MODEL), F32),
        scratch_shapes=[pltpu.VMEM((tm, f), BF16)],
        compiler_params=_params(("parallel", "parallel")),
        name="ffn",
    )(x1, modp, g2, w1, w3, w2)


def _moe_kernel(te_ref, nv_ref, x_ref, w1_ref, w3_ref, w2_ref, o_ref, xb_ref, acc_ref):
    i, j = pl.program_id(0), pl.program_id(1)
    last = pl.num_programs(1) - 1
    valid = i < nv_ref[0]

    @pl.when(valid)
    def _():
        @pl.when(j == 0)
        def _():
            xb_ref[...] = x_ref[...].astype(BF16)

        _accumulate(acc_ref, _swiglu_partial(xb_ref[...], w1_ref, w3_ref, w2_ref), j)

        @pl.when(j == last)
        def _():
            o_ref[...] = acc_ref[...]

    @pl.when(jnp.logical_and(jnp.logical_not(valid), j == last))
    def _():
        o_ref[...] = jnp.zeros_like(o_ref)


def grouped_swiglu(xs, tile_expert, n_valid, w1, w3, w2, tm, tf):
    n_rows, d = xs.shape
    f = w1.shape[-1]
    assert n_rows % tm == 0 and f % tf == 0
    n_f = f // tf

    def col(i, j, nv):
        return jnp.where(i < nv[0], j, n_f - 1)

    grid_spec = pltpu.PrefetchScalarGridSpec(
        num_scalar_prefetch=2,
        grid=(n_rows // tm, n_f),
        in_specs=[pl.BlockSpec((tm, d), lambda i, j, te, nv: (jnp.minimum(i, nv[0] - 1), 0)),
                  pl.BlockSpec((None, d, tf), lambda i, j, te, nv: (te[i], 0, col(i, j, nv))),
                  pl.BlockSpec((None, d, tf), lambda i, j, te, nv: (te[i], 0, col(i, j, nv))),
                  pl.BlockSpec((None, tf, d), lambda i, j, te, nv: (te[i], col(i, j, nv), 0))],
        out_specs=pl.BlockSpec((tm, d), lambda i, j, te, nv: (i, 0)),
        scratch_shapes=[pltpu.VMEM((tm, d), BF16), pltpu.VMEM((tm, d), F32)],
    )
    return pl.pallas_call(
        _moe_kernel,
        grid_spec=grid_spec,
        out_shape=jax.ShapeDtypeStruct((n_rows, d), F32),
        compiler_params=_params(("parallel", "arbitrary")),
        name="moe",
    )(tile_expert, n_valid, xs, w1, w3, w2)


def route(logits, tm):
    n_tok = logits.shape[0]
    n_assign = n_tok * TOP_K
    idx = jnp.arange(N_EXPERTS, dtype=jnp.int32)[None, :]
    m1 = jnp.max(logits, axis=-1, keepdims=True)
    e1 = jnp.min(jnp.where(logits == m1, idx, N_EXPERTS), axis=-1, keepdims=True)
    rest = jnp.where(idx == e1, jnp.finfo(F32).min, logits)
    m2 = jnp.max(rest, axis=-1, keepdims=True)
    e2 = jnp.min(jnp.where(rest == m2, idx, N_EXPERTS), axis=-1, keepdims=True)
    t = jnp.exp(m2 - m1)
    gates = jnp.concatenate([1.0 / (1.0 + t), t / (1.0 + t)], axis=-1)
    flat_e = jnp.concatenate([e1, e2], axis=-1).reshape(n_assign, 1)
    onehot = (flat_e == idx).astype(F32)
    blk = MOE_ROUTE_BLOCK
    nb = n_assign // blk
    oh = onehot.reshape(nb, blk, N_EXPERTS)
    lower = lambda n: (jnp.arange(n)[:, None] > jnp.arange(n)[None, :]).astype(F32)
    within = jnp.einsum('ij,bjk->bik', lower(blk), oh)
    blk_cnt = jnp.sum(oh, axis=1)
    blk_off = jnp.dot(lower(nb), blk_cnt, precision=lax.Precision.HIGHEST)
    rank = jnp.sum((within + blk_off[:, None, :]) * oh, axis=-1).reshape(n_assign)
    counts = jnp.sum(blk_cnt, axis=0).astype(jnp.int32)
    padded = (counts + tm - 1) // tm * tm
    ends = []
    for e in range(N_EXPERTS):
        ends.append(padded[e] + (ends[-1] if ends else 0))
    pend = jnp.stack(ends)
    pstart = pend - padded
    dest = (jnp.sum(onehot * pstart.astype(F32)[None, :], axis=-1) + rank).astype(jnp.int32)
    n_tiles = n_assign // tm + N_EXPERTS
    starts = jnp.arange(n_tiles, dtype=jnp.int32) * tm
    tile_expert = jnp.minimum(jnp.sum((pend[None, :] <= starts[:, None]).astype(jnp.int32), axis=-1),
                              N_EXPERTS - 1)
    n_valid = (pend[-1] // tm).reshape(1)
    meta = jnp.concatenate([pend, padded, n_valid])
    return gates, dest, tile_expert, n_valid, meta


def _row_copies(dest_ref, src, dst, sem, rows, gather, side_work=()):
    n_groups = rows // SUBLANES
    for g in range(n_groups):
        for r in range(g * SUBLANES, (g + 1) * SUBLANES):
            for kk in range(TOP_K):
                d = dest_ref[0, TOP_K * r + kk]
                if gather:
                    pltpu.make_async_copy(src.at[pl.ds(d, 1), :], dst.at[kk, pl.ds(r, 1), :], sem).start()
                else:
                    pltpu.make_async_copy(src.at[pl.ds(r, 1), :], dst.at[pl.ds(d, 1), :], sem).start()
        for thunk in side_work[g * len(side_work) // n_groups:(g + 1) * len(side_work) // n_groups]:
            thunk()
    if gather:
        pltpu.make_async_copy(dst, dst, sem).wait()
    else:
        for _ in range(TOP_K):
            pltpu.make_async_copy(src, src, sem).wait()


def _dispatch_kernel(meta_ref, dest_ref, h_ref, w1_ref, w3_ref, w2_ref,
                     xs_ref, w1b_ref, w3b_ref, w2b_ref, zero_ref, sem, zsem):
    tm_moe = zero_ref.shape[0]

    @pl.when(pl.program_id(0) == 0)
    def _():
        zero_ref[...] = jnp.zeros_like(zero_ref)
        n_tiles = xs_ref.shape[0] // tm_moe
        fills = []
        for e in range(N_EXPERTS):
            start = pl.multiple_of(jnp.maximum(meta_ref[e] - tm_moe, 0), tm_moe)
            fills.append((meta_ref[N_EXPERTS + e] > 0, start))
        for i in range(n_tiles - N_EXPERTS, n_tiles):
            fills.append((i >= meta_ref[2 * N_EXPERTS], i * tm_moe))
        for cond, start in fills:
            @pl.when(cond)
            def _():
                pltpu.make_async_copy(zero_ref, xs_ref.at[pl.ds(start, tm_moe), :], zsem).start()
        for cond, start in fills:
            @pl.when(cond)
            def _():
                pltpu.make_async_copy(zero_ref, xs_ref.at[pl.ds(start, tm_moe), :], zsem).wait()

    def cast_lanes(src, dst, c):
        def thunk():
            dst[:, c * LANES:(c + 1) * LANES] = src[:, c * LANES:(c + 1) * LANES].astype(BF16)
        return thunk

    def cast_rows(src, dst, c):
        def thunk():
            dst[c * 16:(c + 1) * 16, :] = src[c * 16:(c + 1) * 16, :].astype(BF16)
        return thunk

    casts = ([cast_lanes(w1_ref, w1b_ref, c) for c in range(w1_ref.shape[1] // LANES)]
             + [cast_lanes(w3_ref, w3b_ref, c) for c in range(w3_ref.shape[1] // LANES)]
             + [cast_rows(w2_ref, w2b_ref, c) for c in range(w2_ref.shape[0] // 16)])
    _row_copies(dest_ref, h_ref, xs_ref, sem, h_ref.shape[0], gather=False, side_work=casts)


def moe_dispatch(h2, dest, meta, n_slots, w1, w3, w2):
    n_tok, d = h2.shape
    tm = MOE_DISPATCH_TILE
    n_steps = n_tok // tm
    w1f, w3f, w2f = (w.reshape(-1, w.shape[-1]) for w in (w1, w3, w2))
    slab = lambda w: pl.BlockSpec((w.shape[0] // n_steps, w.shape[1]), lambda t, m: (t, 0))
    assert all(w.shape[0] % (16 * n_steps) == 0 for w in (w1f, w3f, w2f))
    grid_spec = pltpu.PrefetchScalarGridSpec(
        num_scalar_prefetch=1,
        grid=(n_steps,),
        in_specs=[pl.BlockSpec((None, 1, TOP_K * tm), lambda t, m: (t, 0, 0), memory_space=pltpu.SMEM),
                  pl.BlockSpec((tm, d), lambda t, m: (t, 0)),
                  slab(w1f), slab(w3f), slab(w2f)],
        out_specs=[pl.BlockSpec(memory_space=pl.ANY), slab(w1f), slab(w3f), slab(w2f)],
        scratch_shapes=[pltpu.VMEM((MOE_TILE_M, d), h2.dtype), pltpu.SemaphoreType.DMA,
                        pltpu.SemaphoreType.DMA],
    )
    xs, w1b, w3b, w2b = pl.pallas_call(
        _dispatch_kernel,
        grid_spec=grid_spec,
        out_shape=[jax.ShapeDtypeStruct((n_slots, d), h2.dtype)]
        + [jax.ShapeDtypeStruct(w.shape, BF16) for w in (w1f, w3f, w2f)],
        compiler_params=_params(("arbitrary",)),
        name="moe_dispatch",
    )(meta, dest.reshape(n_steps, 1, TOP_K * tm), h2, w1f, w3f, w2f)
    return xs, w1b.reshape(w1.shape), w3b.reshape(w3.shape), w2b.reshape(w2.shape)


def _combine_kernel(dest_ref, x1_ref, g_ref, mod_ref, ys_ref, o_ref, buf_ref, sem):
    tm = x1_ref.shape[0]
    _row_copies(dest_ref, ys_ref, buf_ref, sem, tm, gather=True)
    g = g_ref[...]
    moe = g[:, 0:1] * buf_ref[0] + g[:, 1:2] * buf_ref[1]
    o_ref[...] = x1_ref[...] + mod_ref[5, 0] * moe


def moe_combine(x1, gates, dest, ys, modp):
    b, s, d = x1.shape
    tm = MOE_COMBINE_TILE
    nt = s // tm
    return pl.pallas_call(
        _combine_kernel,
        grid=(b, nt),
        in_specs=[pl.BlockSpec((None, 1, TOP_K * tm), lambda i, t: (i * nt + t, 0, 0), memory_space=pltpu.SMEM),
                  pl.BlockSpec((None, tm, d), lambda i, t: (i, t, 0)),
                  pl.BlockSpec((None, tm, TOP_K), lambda i, t: (i, t, 0)),
                  pl.BlockSpec((None, N_MOD, 2, 1, d), lambda i, t: (i, 0, 0, 0, 0)),
                  pl.BlockSpec(memory_space=pl.ANY)],
        out_specs=pl.BlockSpec((None, tm, d), lambda i, t: (i, t, 0)),
        out_shape=jax.ShapeDtypeStruct((b, s, d), F32),
        scratch_shapes=[pltpu.VMEM((TOP_K, tm, d), F32), pltpu.SemaphoreType.DMA],
        compiler_params=_params(("arbitrary", "arbitrary")),
        name="moe_combine",
    )(dest.reshape(b * nt, 1, TOP_K * tm), x1, gates.reshape(b, s, TOP_K), modp, ys)


def moe_residual(x1, h2, logits, modp, w1, w3, w2):
    b, s, d = x1.shape
    n_tok = b * s
    tm = MOE_TILE_M
    gates, dest, tile_expert, n_valid, meta = route(logits.reshape(n_tok, N_EXPERTS), tm)
    n_slots = (n_tok * TOP_K // tm + N_EXPERTS) * tm
    xs, w1, w3, w2 = moe_dispatch(h2.reshape(n_tok, d), dest, meta, n_slots, w1, w3, w2)
    ys = grouped_swiglu(xs, tile_expert, n_valid, w1, w3, w2, tm, MOE_TILE_F)
    return moe_combine(x1, gates, dest, ys, modp)


def _proj1_kernel(x_ref, g_ref, mod_ref, w_ref, qan_ref, qb_ref, qbs_ref, kvan_ref, kbk_ref, kbv_ref,
                  cos_ref, sin_ref, gains_ref, ones_ref, hy_ref, q_ref, k_ref, v_ref):
    tm = x_ref.shape[0]
    row0 = pl.program_id(1) * tm
    h = _norm_mod(x_ref[...], g_ref[...], mod_ref, row0, 0, 1)
    z = jnp.dot(h.astype(BF16), w_ref[...], preferred_element_type=F32)
    hy_ref[...] = z[:, :HY_IN]

    def rms(a, g):
        return a * lax.rsqrt(jnp.mean(a * a, axis=-1, keepdims=True) + EPS) * g

    c_q, c_kv = HY_IN, HY_IN + MLA_Q_RANK
    c_r = c_kv + MLA_KV_RANK
    qa = rms(z[:, c_q:c_kv], qan_ref[...]).astype(BF16)
    kva = rms(z[:, c_kv:c_r], kvan_ref[...]).astype(BF16)
    q = jnp.dot(qa, qb_ref[...], preferred_element_type=F32)
    q_sw = jnp.dot(qa, qbs_ref[...], preferred_element_type=F32)
    kk = jnp.dot(kva, kbk_ref[...], preferred_element_type=F32)
    v_ref[...] = jnp.dot(kva, kbv_ref[...], preferred_element_type=F32).astype(BF16)
    k_rope = pltpu.roll(z[:, c_r:c_r + LANES], MLA_NOPE, 1)
    first_half = lax.broadcasted_iota(jnp.int32, (tm, LANES), 1) < MLA_NOPE + MLA_ROPE // 2
    sh = MLA_ROPE // 2
    k_rope_sw = jnp.where(first_half, pltpu.roll(k_rope, LANES - sh, 1), pltpu.roll(k_rope, sh, 1))
    cos, sin = cos_ref[...], sin_ref[...]
    qg, qg_sw, kg, kg_sw = (gains_ref[i:i + 1, :] for i in range(4))

    def head(a, a_sw, gain, gain_sw):
        inv = lax.rsqrt(_group_sums(a, ones_ref) / MLA_QK + EPS)
        return (a * (inv * gain) * cos + a_sw * (inv * gain_sw) * sin).astype(BF16)

    for hh in range(MLA_HEADS):
        sl = slice(hh * LANES, (hh + 1) * LANES)
        q_ref[:, sl] = head(q[:, sl], q_sw[:, sl], qg, qg_sw)
        k_ref[:, sl] = head(kk[:, sl] + k_rope, k_rope_sw, kg, kg_sw)


def _swap_rope_halves(a):
    lead = a.shape[:-1]
    g = a.reshape(*lead, -1, LANES)
    lo, mid = MLA_NOPE, MLA_NOPE + MLA_ROPE // 2
    g = jnp.concatenate([g[..., :lo], g[..., mid:MLA_QK], g[..., lo:mid], g[..., MLA_QK:]], axis=-1)
    return g.reshape(a.shape)


def proj1(xa, g, modp, w_in, qan, qb, kvan, kbk, kbv, cos, sin, q_gain, k_gain):
    b = xa.shape[0]
    tm = TOKEN_TILE
    tok = lambda width: pl.BlockSpec((None, tm, width), lambda i, t: (i, t, 0))
    full = lambda a: pl.BlockSpec(a.shape, lambda i, t: (0, 0))
    hw = MLA_HEADS * LANES
    qbs = _swap_rope_halves(qb)
    gains = jnp.concatenate([q_gain, _swap_rope_halves(q_gain), k_gain, _swap_rope_halves(k_gain)], axis=0)
    ones = jnp.ones((LANES, LANES), BF16)
    return pl.pallas_call(
        _proj1_kernel,
        grid=(b, S_ALL // tm),
        in_specs=[tok(D_MODEL), full(g),
                  pl.BlockSpec((None, N_MOD, 2, 1, D_MODEL), lambda i, t: (i, 0, 0, 0, 0)),
                  full(w_in), full(qan), full(qb), full(qbs), full(kvan), full(kbk), full(kbv),
                  pl.BlockSpec((tm, LANES), lambda i, t: (t, 0)),
                  pl.BlockSpec((tm, LANES), lambda i, t: (t, 0)),
                  full(gains), full(ones)],
        out_specs=[tok(HY_IN), tok(hw), tok(hw), tok(MLA_HEADS * MLA_V)],
        out_shape=[jax.ShapeDtypeStruct((b, S_ALL, HY_IN), F32),
                   jax.ShapeDtypeStruct((b, S_ALL, hw), BF16),
                   jax.ShapeDtypeStruct((b, S_ALL, hw), BF16),
                   jax.ShapeDtypeStruct((b, S_ALL, MLA_HEADS * MLA_V), BF16)],
        compiler_params=_params(("parallel", "parallel")),
        name="proj1",
    )(xa, g, modp, w_in, qan, qb, qbs, kvan, kbk, kbv, cos, sin, gains, ones)


def _merge1_kernel(hy_ref, att_ref, x_ref, mod_ref, g2_ref, w_hy_ref, w_att_ref, r_ref,
                   x1_ref, h2_ref, lg_ref):
    y = (jnp.dot(hy_ref[...].astype(BF16), w_hy_ref[...], preferred_element_type=F32)
         + jnp.dot(att_ref[...], w_att_ref[...], preferred_element_type=F32))
    x1 = x_ref[...] + mod_ref[2, 0] * y
    x1_ref[...] = x1
    h2 = _norm_mod(x1, g2_ref[...], mod_ref, 0, 3, 4)
    h2_ref[...] = h2
    lane = lax.broadcasted_iota(jnp.int32, lg_ref.shape, 1)
    logits = jnp.zeros(lg_ref.shape, F32)
    for e in range(N_EXPERTS):
        col = jnp.sum(h2 * r_ref[e:e + 1, :], axis=-1, keepdims=True)
        logits = jnp.where(lane == e, col, logits)
    lg_ref[...] = logits


def merge1(hy, att, xa, modp, g2, w_hy, w_att, router):
    b = xa.shape[0]
    tm = MERGE1_TILE
    tok = lambda width: pl.BlockSpec((None, tm, width), lambda i, t: (i, t, 0))
    full = lambda a: pl.BlockSpec(a.shape, lambda i, t: (0, 0))
    return pl.pallas_call(
        _merge1_kernel,
        grid=(b, SEQ // tm),
        in_specs=[tok(HY_WIDTH), tok(HALF_MIX), tok(D_MODEL),
                  pl.BlockSpec((None, N_MOD, 2, 1, D_MODEL), lambda i, t: (i, 0, 0, 0, 0)),
                  full(g2), full(w_hy), full(w_att), full(router)],
        out_specs=[tok(D_MODEL), tok(D_MODEL), tok(N_EXPERTS)],
        out_shape=[jax.ShapeDtypeStruct((b, SEQ, D_MODEL), F32),
                   jax.ShapeDtypeStruct((b, SEQ, D_MODEL), F32),
                   jax.ShapeDtypeStruct((b, SEQ, N_EXPERTS), F32)],
        compiler_params=_params(("parallel", "parallel")),
        name="merge1",
    )(hy, att, xa, modp, g2, w_hy, w_att, router)


HY_N = 2 * SEQ
HY_TW_ROWS = 256
HY_SPEC_FBLK = 256
HY_CONV_FBLK = 512
HY_LANE_TILES = SEQ // LANES


def _twiddle_kernel(ca_ref, sa_ref, cb_ref, sb_ref, fre_ref, fim_ref, ic_ref, is_ref):
    rows = fre_ref.shape[0]
    r = pl.program_id(0) * rows + lax.broadcasted_iota(jnp.int32, (rows, LANES), 0)
    alt_r = (1 - 2 * (r & 1)).astype(F32)
    cb, sb = cb_ref[...], sb_ref[...]
    for a in range(HY_LANE_TILES):
        ca, sa = ca_ref[:, a:a + 1], sa_ref[:, a:a + 1]
        c = ca * cb - sa * sb
        s = sa * cb + ca * sb
        col = a * LANES + lax.broadcasted_iota(jnp.int32, (rows, LANES), 1)
        alt_c = (1 - 2 * (col & 1)).astype(F32)
        w = jnp.where(col == 0, 1.0 / HY_N, 2.0 / HY_N)
        sl = slice(a * LANES, (a + 1) * LANES)
        fre_ref[:, sl] = c.astype(BF16)
        fim_ref[:, sl] = jnp.where(r == 0, alt_c, -s).astype(BF16)
        ic_ref[:, sl] = (c * w).astype(BF16)
        is_ref[:, sl] = jnp.where(col == 0, alt_r / HY_N, -s * w).astype(BF16)


def dft_matrices():
    idx = jnp.arange(SEQ, dtype=jnp.int32)[:, None]
    step = 2.0 * math.pi / HY_N
    ph_a = ((idx * (LANES * jnp.arange(HY_LANE_TILES, dtype=jnp.int32))[None, :]) % HY_N).astype(F32) * step
    ph_b = ((idx * jnp.arange(LANES, dtype=jnp.int32)[None, :]) % HY_N).astype(F32) * step
    rows = HY_TW_ROWS
    tab = lambda width: pl.BlockSpec((rows, width), lambda i: (i, 0))
    out = jax.ShapeDtypeStruct((SEQ, SEQ), BF16)
    return pl.pallas_call(
        _twiddle_kernel,
        grid=(SEQ // rows,),
        in_specs=[tab(HY_LANE_TILES), tab(HY_LANE_TILES), tab(LANES), tab(LANES)],
        out_specs=[tab(SEQ)] * 4,
        out_shape=[out] * 4,
        compiler_params=_params(("parallel",)),
        name="twiddle",
    )(jnp.cos(ph_a), jnp.sin(ph_a), jnp.cos(ph_b), jnp.sin(ph_b))


def _spec_kernel(fre_ref, fim_ref, h_ref, k_ref):
    w = HY_WIDTH
    h = h_ref[...].astype(BF16)
    re = jnp.dot(fre_ref[...], h, preferred_element_type=F32)
    im = jnp.dot(fim_ref[...], h, preferred_element_type=F32)
    hb0 = h[0:1, w:].astype(F32)
    first = (pl.program_id(1) * re.shape[0] + lax.broadcasted_iota(jnp.int32, (re.shape[0], 1), 0)) == 0
    k_ref[0] = re[:, :w] + re[:, w:] - hb0
    k_ref[1] = jnp.where(first, im[:, :w] + im[:, w:] - hb0, im[:, :w] - im[:, w:])


def filter_spectra(filt, fre, fim):
    fb = HY_SPEC_FBLK
    return pl.pallas_call(
        _spec_kernel,
        grid=(HY_ORDER, SEQ // fb),
        in_specs=[pl.BlockSpec((fb, SEQ), lambda o, f: (f, 0)),
                  pl.BlockSpec((fb, SEQ), lambda o, f: (f, 0)),
                  pl.BlockSpec((SEQ, 2 * HY_WIDTH), lambda o, f: (0, o))],
        out_specs=pl.BlockSpec((None, 2, fb, HY_WIDTH), lambda o, f: (o, 0, f, 0)),
        out_shape=jax.ShapeDtypeStruct((HY_ORDER, 2, SEQ, HY_WIDTH), F32),
        compiler_params=_params(("parallel", "parallel")),
        name="filter_spectra",
    )(fre, fim, filt)


def _short_conv(z_ref, w_ref, b_ref, part):
    z = z_ref[...]
    l = z.shape[0]
    t = lax.broadcasted_iota(jnp.int32, (l, 1), 0)
    prev = jnp.where(t == 0, 0.0, pltpu.roll(z, 1, 0))
    nxt = jnp.where(t == l - 1, 0.0, pltpu.roll(z, l - 1, 0))
    cs = slice(part * HY_WIDTH, (part + 1) * HY_WIDTH)
    return (w_ref[0:1, cs] * prev + w_ref[1:2, cs] * z + w_ref[2:3, cs] * nxt) + b_ref[:, cs]


def _hyconv_kernel(u_ref, xg_ref, cw_ref, cb_ref, fre_ref, fim_ref, ic_ref, is_ref, k_ref, skip_ref,
                   y_ref, ub_ref, acc_ref, *, order):
    f = pl.program_id(1)

    def u_f32():
        return _short_conv(u_ref, cw_ref, cb_ref, 0) if order == 0 else u_ref[...]

    @pl.when(f == 0)
    def _():
        ub_ref[...] = u_f32().astype(BF16)

    x_re = jnp.dot(fre_ref[...], ub_ref[...], preferred_element_type=F32)
    x_im = jnp.dot(fim_ref[...], ub_ref[...], preferred_element_type=F32)
    k_re, k_im = k_ref[0], k_ref[1]
    first = (f * x_re.shape[0] + lax.broadcasted_iota(jnp.int32, (x_re.shape[0], 1), 0)) == 0
    y_re = x_re * k_re - jnp.where(first, 0.0, x_im * k_im)
    y_im = x_im * jnp.where(first, k_im, k_re) + jnp.where(first, 0.0, x_re * k_im)
    part = (jnp.dot(ic_ref[...], y_re.astype(BF16), preferred_element_type=F32)
            + jnp.dot(is_ref[...], y_im.astype(BF16), preferred_element_type=F32))
    _accumulate(acc_ref, part, f)

    @pl.when(f == pl.num_programs(1) - 1)
    def _():
        xg = _short_conv(xg_ref, cw_ref, cb_ref, order + 1)
        y_ref[...] = xg * (acc_ref[...] + skip_ref[order:order + 1, :] * u_f32())


def hyena_conv(order, u, hyz, conv_w, conv_b, fre, fim, ic, is_, spectra, skip):
    b = hyz.shape[0]
    fb = HY_CONV_FBLK
    w = HY_WIDTH
    lat = lambda part: pl.BlockSpec((None, SEQ, w), lambda i, f: (i, 0, part))
    full = lambda a: pl.BlockSpec(a.shape, lambda i, f: (0, 0))
    return pl.pallas_call(
        functools.partial(_hyconv_kernel, order=order),
        grid=(b, SEQ // fb),
        in_specs=[lat(0), lat(order + 1), full(conv_w), full(conv_b),
                  pl.BlockSpec((fb, SEQ), lambda i, f: (f, 0)),
                  pl.BlockSpec((fb, SEQ), lambda i, f: (f, 0)),
                  pl.BlockSpec((SEQ, fb), lambda i, f: (0, f)),
                  pl.BlockSpec((SEQ, fb), lambda i, f: (0, f)),
                  pl.BlockSpec((None, 2, fb, w), lambda i, f: (order, 0, f, 0)),
                  full(skip)],
        out_specs=pl.BlockSpec((None, SEQ, w), lambda i, f: (i, 0, 0)),
        out_shape=jax.ShapeDtypeStruct((b, SEQ, w), F32),
        scratch_shapes=[pltpu.VMEM((SEQ, w), BF16), pltpu.VMEM((SEQ, w), F32)],
        compiler_params=_params(("parallel", "arbitrary")),
        name="hyena_conv",
    )(u, hyz, conv_w, conv_b, fre, fim, ic, is_, spectra, skip)


def hyena_filters(length, w1, b1, w2, b2, w3, freq):
    hp = lax.Precision.HIGHEST
    t = jnp.arange(length, dtype=F32)[:, None]
    t_norm = t / max(length - 1, 1)
    bands = jnp.linspace(1e-4, HY_BANDS - 1, HY_BANDS, dtype=F32)
    ang = 2.0 * math.pi * t * bands / length
    z = jnp.concatenate([t_norm, jnp.cos(ang), -jnp.sin(ang)], axis=-1)
    h = jnp.sin(freq * (jnp.dot(z, w1, precision=hp) + b1))
    h = jnp.sin(freq * (jnp.dot(h, w2, precision=hp) + b2))
    h = jnp.dot(h, w3, precision=hp)
    deltas = jnp.abs(jnp.linspace(HY_MIN_DECAY, HY_MAX_DECAY, HY_WIDTH, dtype=F32))
    window = jnp.exp(-t_norm * deltas) + HY_SHIFT
    return h * jnp.tile(window, (1, HY_ORDER * 2))


def hyena(hyz, conv_w, conv_b, fw1, fb1, fw2, fb2, fw3, freq, skip):
    fre, fim, ic, is_ = dft_matrices()
    filt = hyena_filters(SEQ, fw1, fb1, fw2, fb2, fw3, freq)
    spectra = filter_spectra(filt, fre, fim)
    y = hyz
    for o in range(HY_ORDER):
        y = hyena_conv(o, y, hyz, conv_w, conv_b.reshape(1, -1), fre, fim, ic, is_, spectra, skip)
    return y


def _grid_angles(rot_dim):
    n_freq = rot_dim // 4
    inv_freq = ROPE_THETA ** (-jnp.arange(n_freq, dtype=F32) / n_freq)
    t = jnp.arange(SEQ)
    r = (t // GRID_W).astype(F32)
    c_ = (t % GRID_W).astype(F32)
    return jnp.concatenate([r[:, None] * inv_freq, c_[:, None] * inv_freq], axis=-1)


def _rope_tables(rot_dim, lane_offsets):
    ang = _grid_angles(rot_dim)
    c, s = jnp.cos(ang), jnp.sin(ang)
    cos_parts, sin_parts, lane = [], [], 0
    for off in lane_offsets:
        cos_parts += [jnp.ones((SEQ, off - lane), F32), c, c]
        sin_parts += [jnp.zeros((SEQ, off - lane), F32), -s, s]
        lane = off + rot_dim
    cos_parts.append(jnp.ones((SEQ, LANES - lane), F32))
    sin_parts.append(jnp.zeros((SEQ, LANES - lane), F32))
    cos = jnp.concatenate([jnp.concatenate(cos_parts, axis=1), jnp.ones((CTX_LEN, LANES), F32)], axis=0)
    sin = jnp.concatenate([jnp.concatenate(sin_parts, axis=1), jnp.zeros((CTX_LEN, LANES), F32)], axis=0)
    return cos, sin


def _block_diag(w):
    nb, bs, _ = w.shape
    eye = jnp.eye(nb, dtype=w.dtype)
    return (eye[:, None, :, None] * w[:, :, None, :]).reshape(nb * bs, nb * bs)


def _gqa_pair_order():
    g = GQA_Q_HEADS // GQA_KV_HEADS
    heads = []
    for p in range(g):
        heads += [p, g + p]
    return np.concatenate([np.arange(h * HEAD_DIM, (h + 1) * HEAD_DIM) for h in heads])


def _pad_heads(w, n_heads, width):
    k = w.shape[0]
    w = w.reshape(k, n_heads, width)
    return jnp.pad(w, ((0, 0), (0, 0), (0, LANES - width))).reshape(k, n_heads * LANES)


def kernel(x, c, ctx, c_ctx, mod_w, mod_b, norm1_g, norm2_g, ab_w_in, ab_w_out, lru_conv_w, lru_conv_b, lru_w_a, lru_b_a, lru_w_x, lru_b_x, lru_lambda, gqa_q_norm, gqa_k_norm, ffn_w1, ffn_w3, ffn_w2, cd_w_in, cd_w_out, hy_conv_w, hy_conv_b, hy_filt_w1, hy_filt_b1, hy_filt_w2, hy_filt_b2, hy_filt_w3, hy_sin_freq, hy_skip, mla_q_a_norm, mla_q_b, mla_kv_a_norm, mla_kv_b, mla_q_norm, mla_k_norm, moe_router, moe_w1, moe_w3, moe_w2):
    batch = x.shape[0]
    bf = lambda w: w.astype(BF16)
    row = lambda v: v.reshape(1, -1)

    silu_all = jnp.concatenate([jax.nn.silu(c), jax.nn.silu(c_ctx)[None, :],
                                jnp.zeros((16 - batch - 1, D_MODEL), F32)], axis=0)
    mods = []
    mod_all = stacked_matmul(silu_all, mod_w, 1536) + mod_b[:, None, :]
    for layer in range(DEPTH):
        m = mod_all[layer]
        lat = m[:batch].reshape(batch, N_MOD, 1, D_MODEL)
        cx = jnp.broadcast_to(m[batch].reshape(1, N_MOD, 1, D_MODEL), lat.shape)
        mods.append(jnp.stack([lat, cx], axis=2))

    xa = jnp.concatenate([x, ctx], axis=1)

    perm = _gqa_pair_order()
    q0 = 2 * LRU_WIDTH
    w_in0 = ab_w_in[0]
    w_in0 = jnp.concatenate([w_in0[:, :q0], w_in0[:, q0 + perm], w_in0[:, q0 + GQA_Q_HEADS * HEAD_DIM:]], axis=1)
    cos_g, sin_g = _rope_tables(HEAD_DIM, (0, HEAD_DIM))
    q_gain = row(jnp.tile(gqa_q_norm[0], 2) * HEAD_DIM ** -0.5)
    k_gain = row(jnp.tile(gqa_k_norm[0], 2))
    xr, gate, q, k, v = proj0(xa, row(norm1_g[0]), mods[0], bf(w_in0), cos_g, sin_g, q_gain, k_gain)

    w_gates = jnp.concatenate([_block_diag(lru_w_a[0, 0]), _block_diag(lru_w_x[0, 0]),
                               _block_diag(lru_w_a[0, 1]), _block_diag(lru_w_x[0, 1])], axis=1)
    b_gates = jnp.concatenate([lru_b_a[0, 0].reshape(-1), lru_b_x[0, 0].reshape(-1),
                               lru_b_a[0, 1].reshape(-1), lru_b_x[0, 1].reshape(-1)])
    rec = rglru(xr, lru_conv_w[0], row(lru_conv_b[0]), bf(w_gates), row(b_gates), lru_lambda[0])

    att_l = attention_latent(q, k, v, GQA_HEADS, GQA_TILE_Q)
    att_c = attention_context(q, k, v, GQA_HEADS)
    w_out0 = ab_w_out[0]
    x1 = merge0(rec, gate, att_l, att_c, xa, mods[0], bf(w_out0[:LRU_WIDTH]), bf(w_out0[LRU_WIDTH:][perm]))
    xa = ffn_residual(x1, mods[0], row(norm2_g[0]), bf(ffn_w1[0]), bf(ffn_w3[0]), bf(ffn_w2[0]))

    w_in1 = jnp.pad(cd_w_in[0], ((0, 0), (0, LANES - MLA_ROPE)))
    cos_m, sin_m = _rope_tables(MLA_ROPE, (MLA_NOPE,))
    qb = _pad_heads(mla_q_b[0], MLA_HEADS, MLA_QK)
    kvb = mla_kv_b[0].reshape(MLA_KV_RANK, MLA_HEADS, MLA_NOPE + MLA_V)
    kbk = _pad_heads(kvb[:, :, :MLA_NOPE].reshape(MLA_KV_RANK, -1), MLA_HEADS, MLA_NOPE)
    kbv = kvb[:, :, MLA_NOPE:].reshape(MLA_KV_RANK, -1)
    pad_gain = lambda g_: row(jnp.pad(g_, (0, LANES - MLA_QK)))
    hyz, q, k, v = proj1(xa, row(norm1_g[1]), mods[1], bf(w_in1), row(mla_q_a_norm[0]), bf(qb),
                         row(mla_kv_a_norm[0]), bf(kbk), bf(kbv), cos_m, sin_m,
                         pad_gain(mla_q_norm[0] * MLA_QK ** -0.5), pad_gain(mla_k_norm[0]))
    att = attention_latent(q, k, v, MLA_HEAD_GROUPS, MLA_TILE_Q)
    hy = hyena(hyz, hy_conv_w[0], hy_conv_b[0], hy_filt_w1[0], hy_filt_b1[0], hy_filt_w2[0],
               hy_filt_b2[0], hy_filt_w3[0], hy_sin_freq[0], hy_skip[0])
    w_out1 = cd_w_out[0]
    x1, h2, logits = merge1(hy, att, xa, mods[1], row(norm2_g[1]),
                            bf(w_out1[:HY_WIDTH]), bf(w_out1[HY_WIDTH:]), moe_router[0].T)
    return moe_residual(x1, h2, logits, mods[1], moe_w1[0], moe_w3[0], moe_w2[0])
```

```python
import functools
import math

import jax
import jax.numpy as jnp
import numpy as np
from jax import lax
from jax.experimental import pallas as pl
from jax.experimental.pallas import tpu as pltpu

F32 = jnp.float32
BF16 = jnp.bfloat16

D_MODEL = 1024
SEQ = 2048
CTX_LEN = 256
S_ALL = SEQ + CTX_LEN
DEPTH = 2
GRID_W = 64
HEAD_DIM = 64
HALF_MIX = D_MODEL // 2
ROPE_THETA = 10000.0
EPS = 1e-6
N_MOD = 6
LRU_WIDTH = HALF_MIX
LRU_BLOCKS = LRU_WIDTH // HEAD_DIM
LRU_CONV = 4
LRU_C = 8.0
GQA_Q_HEADS = HALF_MIX // HEAD_DIM
GQA_KV_HEADS = 2
HY_WIDTH = HALF_MIX
HY_ORDER = 2
HY_CONV = 3
HY_BANDS = 16
HY_TARGET = 1e-2
HY_FAST_PCT = 0.3
HY_SLOW_PCT = 1.5
HY_MIN_DECAY = math.log(HY_TARGET) / HY_SLOW_PCT
HY_MAX_DECAY = math.log(HY_TARGET) / HY_FAST_PCT
HY_SHIFT = 0.05
HY_IN = (HY_ORDER + 1) * HY_WIDTH
MLA_HEADS = HALF_MIX // HEAD_DIM
MLA_Q_RANK = D_MODEL // 4
MLA_KV_RANK = D_MODEL // 8
MLA_NOPE = HEAD_DIM
MLA_ROPE = HEAD_DIM // 2
MLA_V = HEAD_DIM
MLA_QK = MLA_NOPE + MLA_ROPE
N_EXPERTS = 8
TOP_K = 2

LANES = 128
SUBLANES = 8
V7X_VMEM_LIMIT_BYTES = 56 * 1024 * 1024

TOKEN_TILE = 768
FFN_TILE_M = 768
FFN_TILE_F = 256
GQA_TILE_Q = 512
MLA_TILE_Q = 512
MERGE1_TILE = 512
MOE_TILE_M = 512
MOE_TILE_F = 1792
MOE_ROUTE_BLOCK = 256
MOE_DISPATCH_TILE = 256
MOE_COMBINE_TILE = 256

LRU_GAP = SUBLANES
LRU_LAT0 = CTX_LEN + LRU_GAP
LRU_CHUNK = 464
LRU_ROWS = 2320
LRU_PAD_FRONT = SUBLANES

assert S_ALL % TOKEN_TILE == 0 and S_ALL % FFN_TILE_M == 0
assert SEQ % GQA_TILE_Q == 0 and SEQ % MLA_TILE_Q == 0
assert LRU_ROWS % LRU_CHUNK == 0 and LRU_ROWS >= LRU_LAT0 + SEQ + SUBLANES


def _params(sem):
    return pltpu.CompilerParams(dimension_semantics=sem, vmem_limit_bytes=V7X_VMEM_LIMIT_BYTES)


def _norm_mod(x, g, mod_pair, row0, shift_idx, scale_idx):
    y = x * lax.rsqrt(jnp.mean(x * x, axis=-1, keepdims=True) + EPS) * g
    shift = _mod_rows(mod_pair, shift_idx, row0, x.shape[0])
    scale = _mod_rows(mod_pair, scale_idx, row0, x.shape[0])
    return y * (1.0 + scale) + shift


def _mod_rows(mod_pair, idx, row0, rows):
    row = row0 + lax.broadcasted_iota(jnp.int32, (rows, 1), 0)
    return jnp.where(row >= SEQ, mod_pair[idx, 1], mod_pair[idx, 0])


def _mm_kernel(x_ref, w_ref, o_ref):
    o_ref[...] = jnp.dot(x_ref[...].astype(BF16), w_ref[...].astype(BF16),
                         preferred_element_type=F32)


def stacked_matmul(x, w, tn):
    m, k = x.shape
    n_l, _, n = w.shape
    assert n % tn == 0
    return pl.pallas_call(
        _mm_kernel,
        grid=(n_l, n // tn),
        in_specs=[pl.BlockSpec((m, k), lambda l, j: (0, 0)),
                  pl.BlockSpec((None, k, tn), lambda l, j: (l, 0, j))],
        out_specs=pl.BlockSpec((None, m, tn), lambda l, j: (l, 0, j)),
        out_shape=jax.ShapeDtypeStruct((n_l, m, n), F32),
        compiler_params=_params(("parallel", "parallel")),
        name="matmul",
    )(x, w)


def _rope_pair(a, cos, sin, first_half, shift):
    rot = jnp.where(first_half, pltpu.roll(a, LANES - shift, 1), pltpu.roll(a, shift, 1))
    return a * cos + rot * sin


def _group_sums(a, ones_ref):
    return jnp.dot((a * a).astype(BF16), ones_ref[...], preferred_element_type=F32)


def _proj0_kernel(x_ref, g_ref, mod_ref, w_ref, cos_ref, sin_ref, qg_ref, kg_ref, ones_ref,
                  xr_ref, gate_ref, q_ref, k_ref, v_ref):
    tm = x_ref.shape[0]
    row0 = pl.program_id(1) * tm
    h = _norm_mod(x_ref[...], g_ref[...], mod_ref, row0, 0, 1)
    z = jnp.dot(h.astype(BF16), w_ref[...], preferred_element_type=F32)
    xr_ref[...] = z[:, :LRU_WIDTH]
    gate_ref[...] = z[:, LRU_WIDTH:2 * LRU_WIDTH]
    cos, sin = cos_ref[...], sin_ref[...]
    first_half = (lax.broadcasted_iota(jnp.int32, (tm, LANES), 1) & (HEAD_DIM // 2)) == 0

    def head_pair(a, gain):
        inv = lax.rsqrt(_group_sums(a, ones_ref) / HEAD_DIM + EPS)
        return _rope_pair(a * inv * gain, cos, sin, first_half, HEAD_DIM // 2).astype(BF16)

    q0 = 2 * LRU_WIDTH
    for p in range(GQA_Q_HEADS // 2):
        q_ref[:, p * LANES:(p + 1) * LANES] = head_pair(z[:, q0 + p * LANES:q0 + (p + 1) * LANES], qg_ref[...])
    k0 = q0 + GQA_Q_HEADS * HEAD_DIM
    k_ref[...] = head_pair(z[:, k0:k0 + LANES], kg_ref[...])
    v_ref[...] = z[:, k0 + LANES:k0 + 2 * LANES].astype(BF16)


def proj0(xa, g, modp, w_in, cos, sin, q_gain, k_gain):
    b = xa.shape[0]
    tm = TOKEN_TILE
    n = w_in.shape[1]
    tok = lambda width: pl.BlockSpec((None, tm, width), lambda i, t: (i, t, 0))
    full = lambda r, c_: pl.BlockSpec((r, c_), lambda i, t: (0, 0))
    half = jnp.arange(LANES) // HEAD_DIM
    ones_pair = (half[:, None] == half[None, :]).astype(BF16)
    return pl.pallas_call(
        _proj0_kernel,
        grid=(b, S_ALL // tm),
        in_specs=[tok(D_MODEL), full(1, D_MODEL),
                  pl.BlockSpec((None, N_MOD, 2, 1, D_MODEL), lambda i, t: (i, 0, 0, 0, 0)),
                  full(D_MODEL, n),
                  pl.BlockSpec((tm, LANES), lambda i, t: (t, 0)),
                  pl.BlockSpec((tm, LANES), lambda i, t: (t, 0)),
                  full(1, LANES), full(1, LANES), full(LANES, LANES)],
        out_specs=[tok(LRU_WIDTH), tok(LRU_WIDTH), tok(GQA_Q_HEADS * HEAD_DIM), tok(LANES), tok(LANES)],
        out_shape=[jax.ShapeDtypeStruct((b, S_ALL, LRU_WIDTH), F32),
                   jax.ShapeDtypeStruct((b, S_ALL, LRU_WIDTH), F32),
                   jax.ShapeDtypeStruct((b, S_ALL, GQA_Q_HEADS * HEAD_DIM), BF16),
                   jax.ShapeDtypeStruct((b, S_ALL, LANES), BF16),
                   jax.ShapeDtypeStruct((b, S_ALL, LANES), BF16)],
        compiler_params=_params(("parallel", "parallel")),
        name="proj0",
    )(xa, g, modp, w_in, cos, sin, q_gain, k_gain, ones_pair)


def _lru_kernel(xr_ref, cw_ref, cb_ref, wg_ref, bg_ref, lam_ref, rec_ref,
                pad_ref, af_ref, hf_ref, ab_ref, hb_ref):
    w = LRU_WIDTH
    pad_ref[...] = jnp.zeros_like(pad_ref)
    pad_ref[LRU_PAD_FRONT:LRU_PAD_FRONT + CTX_LEN, :] = xr_ref[SEQ:S_ALL, :]
    pad_ref[LRU_PAD_FRONT + LRU_LAT0:LRU_PAD_FRONT + LRU_LAT0 + SEQ, :] = xr_ref[0:SEQ, :]
    lam = lam_ref[...]
    neg_c_softplus = -LRU_C * (jnp.maximum(-lam, 0.0) + jnp.log(1.0 + jnp.exp(-jnp.abs(lam))))
    left = LRU_CONV // 2
    for ch in range(LRU_ROWS // LRU_CHUNK):
        r0 = ch * LRU_CHUNK
        u = cb_ref[...]
        for j in range(LRU_CONV):
            start = LRU_PAD_FRONT + r0 - left + j
            u = u + cw_ref[j:j + 1, :] * pad_ref[start:start + LRU_CHUNK, :]
        gz = jnp.dot(u.astype(BF16), wg_ref[...], preferred_element_type=F32) + bg_ref[...]
        for d, (a_ref, h_ref) in enumerate(((af_ref, hf_ref), (ab_ref, hb_ref))):
            r = jax.nn.sigmoid(gz[:, 2 * d * w:(2 * d + 1) * w])
            i = jax.nn.sigmoid(gz[:, (2 * d + 1) * w:(2 * d + 2) * w])
            a = jnp.exp(neg_c_softplus[d:d + 1, :] * r)
            a_ref[r0:r0 + LRU_CHUNK, :] = a
            h_ref[r0:r0 + LRU_CHUNK, :] = jnp.sqrt(1.0 - a * a) * (i * u)

    n_ctx_groups = CTX_LEN // SUBLANES
    sub = lax.broadcasted_iota(jnp.int32, (SUBLANES, w), 0)

    def tile_scan(a, b, descending):
        for s in (1, 2, 4):
            if descending:
                ok, shift = sub < SUBLANES - s, SUBLANES - s
            else:
                ok, shift = sub >= s, s
            a_s = jnp.where(ok, pltpu.roll(a, shift, 0), 1.0)
            b_s = jnp.where(ok, pltpu.roll(b, shift, 0), 0.0)
            a, b = a * a_s, a * b_s + b
        return a, b

    def group(tt, carry):
        hf, hb = carry
        is_lat = tt >= n_ctx_groups
        rf = pl.ds(pl.multiple_of(tt * SUBLANES + jnp.where(is_lat, LRU_GAP, 0), SUBLANES), SUBLANES)
        rb = pl.ds(pl.multiple_of(
            jnp.where(is_lat, LRU_LAT0 + SEQ + CTX_LEN, CTX_LEN) - (tt + 1) * SUBLANES, SUBLANES), SUBLANES)
        a, b = tile_scan(af_ref[rf, :], hf_ref[rf, :], False)
        h = a * hf + b
        hf_ref[rf, :] = h
        hf = jnp.broadcast_to(h[SUBLANES - 1:SUBLANES, :], (SUBLANES, w))
        a, b = tile_scan(ab_ref[rb, :], hb_ref[rb, :], True)
        h = a * hb + b
        hb_ref[rb, :] = h
        hb = jnp.broadcast_to(h[0:1, :], (SUBLANES, w))
        return hf, hb

    zero = jnp.zeros((SUBLANES, w), F32)
    lax.fori_loop(0, S_ALL // SUBLANES, group, (zero, zero))
    rec_ref[0:SEQ, :] = hf_ref[LRU_LAT0:LRU_LAT0 + SEQ, :] + hb_ref[LRU_LAT0:LRU_LAT0 + SEQ, :]
    rec_ref[SEQ:S_ALL, :] = hf_ref[0:CTX_LEN, :] + hb_ref[0:CTX_LEN, :]


def rglru(xr, conv_w, conv_b, w_gates, b_gates, lam):
    b = xr.shape[0]
    w = LRU_WIDTH
    full = lambda r, c_: pl.BlockSpec((r, c_), lambda i: (0, 0))
    rows = pltpu.VMEM((LRU_ROWS, w), F32)
    return pl.pallas_call(
        _lru_kernel,
        grid=(b,),
        in_specs=[pl.BlockSpec((None, S_ALL, w), lambda i: (i, 0, 0)),
                  full(LRU_CONV, w), full(1, w), full(w, 4 * w), full(1, 4 * w), full(2, w)],
        out_specs=pl.BlockSpec((None, S_ALL, w), lambda i: (i, 0, 0)),
        out_shape=jax.ShapeDtypeStruct((b, S_ALL, w), F32),
        scratch_shapes=[pltpu.VMEM((LRU_ROWS + 2 * LRU_PAD_FRONT, w), F32), rows, rows, rows, rows],
        compiler_params=_params(("parallel",)),
        name="rglru",
    )(xr, conv_w, conv_b, w_gates, b_gates, lam)


def _attn_kernel(q_ref, k_ref, v_ref, *rest, heads):
    n_cast = (len(rest) - 1) // 2
    o_ref = rest[n_cast]
    cast_pairs = list(zip(rest[:n_cast], rest[n_cast + 1:]))
    tq, sk = q_ref.shape[0], k_ref.shape[0]
    q_lo = lax.broadcasted_iota(jnp.int32, (tq, LANES), 1) < HEAD_DIM
    v_lo = lax.broadcasted_iota(jnp.int32, (sk, LANES), 1) < HEAD_DIM

    def keep(x, is_lo, half):
        if half is None:
            return x
        return jnp.where(is_lo if half == 0 else jnp.logical_not(is_lo), x, jnp.zeros_like(x))

    for og, members in enumerate(heads):
        acc = None
        for qg, qh, kg, vg, vh in members:
            q = keep(q_ref[:, qg * LANES:(qg + 1) * LANES], q_lo, qh)
            s = lax.dot_general(q, k_ref[:, kg * LANES:(kg + 1) * LANES],
                                (((1,), (1,)), ((), ())), preferred_element_type=F32)
            p = jnp.exp(s - jnp.max(s, axis=-1, keepdims=True))
            l = jnp.sum(p, axis=-1, keepdims=True)
            v = keep(v_ref[:, vg * LANES:(vg + 1) * LANES], v_lo, vh)
            o = jnp.dot(p.astype(BF16), v, preferred_element_type=F32) / l
            acc = o if acc is None else acc + o
        o_ref[:, og * LANES:(og + 1) * LANES] = acc.astype(o_ref.dtype)
        for src, dst in cast_pairs:
            r0, r1 = (og * src.shape[0] // len(heads), (og + 1) * src.shape[0] // len(heads))
            dst[r0:r1, :] = src[r0:r1, :].astype(BF16)


def attention(q, k, v, heads, q_block, n_q_blocks, tq, k_rows, cast=()):
    b, s, qw = q.shape
    kw, vw = k.shape[-1], v.shape[-1]
    ow = len(heads) * LANES
    n_steps = b * n_q_blocks
    kv_spec = lambda width: pl.BlockSpec((None, k_rows, width), lambda i, t: (i, s // k_rows - 1, 0))
    slab = lambda w: pl.BlockSpec((w.shape[0] // n_steps, w.shape[1]), lambda i, t: (i * n_q_blocks + t, 0))
    assert all(w.shape[0] % (16 * len(heads) * n_steps) == 0 for w in cast)
    out = pl.pallas_call(
        functools.partial(_attn_kernel, heads=heads),
        grid=(b, n_q_blocks),
        in_specs=[pl.BlockSpec((None, tq, qw), lambda i, t: (i, q_block + t, 0)), kv_spec(kw), kv_spec(vw)]
        + [slab(w) for w in cast],
        out_specs=[pl.BlockSpec((None, tq, ow), lambda i, t: (i, t, 0))] + [slab(w) for w in cast],
        out_shape=[jax.ShapeDtypeStruct((b, n_q_blocks * tq, ow), BF16)]
        + [jax.ShapeDtypeStruct(w.shape, BF16) for w in cast],
        compiler_params=_params(("parallel", "parallel")),
        name="attention",
    )(q, k, v, *cast)
    return out if cast else out[0]


GQA_HEADS = tuple(((p, 0, 0, 0, 0), (p, 1, 0, 0, 1)) for p in range(GQA_Q_HEADS // 2))
MLA_HEAD_GROUPS = tuple(((2 * p, None, 2 * p, p, 0), (2 * p + 1, None, 2 * p + 1, p, 1))
                        for p in range(MLA_HEADS // 2))


def attention_latent(q, k, v, heads, tq, cast=()):
    return attention(q, k, v, heads, 0, SEQ // tq, tq, S_ALL, cast)


def attention_context(q, k, v, heads):
    return attention(q, k, v, heads, SEQ // CTX_LEN, 1, CTX_LEN, CTX_LEN)


def _merge0_kernel(rec_ref, gate_ref, att_l_ref, att_c_ref, x_ref, mod_ref,
                   w_rec_ref, w_att_ref, x1_ref):
    tm = x_ref.shape[0]
    row0 = pl.program_id(1) * tm
    rg = (rec_ref[...] * jax.nn.gelu(gate_ref[...])).astype(BF16)
    att = jnp.where(row0 >= SEQ, att_c_ref[...], att_l_ref[...])
    y = (jnp.dot(rg, w_rec_ref[...], preferred_element_type=F32)
         + jnp.dot(att, w_att_ref[...], preferred_element_type=F32))
    x1_ref[...] = x_ref[...] + _mod_rows(mod_ref, 2, row0, tm) * y


def merge0(rec, gate, att_l, att_c, xa, modp, w_rec, w_att):
    b = xa.shape[0]
    tm = CTX_LEN
    n_lat = SEQ // tm
    tok = lambda width: pl.BlockSpec((None, tm, width), lambda i, t: (i, t, 0))
    full = lambda r, c_: pl.BlockSpec((r, c_), lambda i, t: (0, 0))
    return pl.pallas_call(
        _merge0_kernel,
        grid=(b, S_ALL // tm),
        in_specs=[tok(LRU_WIDTH), tok(LRU_WIDTH),
                  pl.BlockSpec((None, tm, HALF_MIX), lambda i, t: (i, jnp.minimum(t, n_lat - 1), 0)),
                  pl.BlockSpec((None, tm, HALF_MIX), lambda i, t: (i, 0, 0)),
                  tok(D_MODEL),
                  pl.BlockSpec((None, N_MOD, 2, 1, D_MODEL), lambda i, t: (i, 0, 0, 0, 0)),
                  full(LRU_WIDTH, D_MODEL), full(HALF_MIX, D_MODEL)],
        out_specs=tok(D_MODEL),
        out_shape=jax.ShapeDtypeStruct((b, S_ALL, D_MODEL), F32),
        compiler_params=_params(("parallel", "parallel")),
        name="merge0",
    )(rec, gate, att_l, att_c, xa, modp, w_rec, w_att)


def _swiglu_partial(x, w1_ref, w3_ref, w2_ref):
    h1 = jnp.dot(x, w1_ref[...], preferred_element_type=F32)
    h3 = jnp.dot(x, w3_ref[...], preferred_element_type=F32)
    act = (h1 * jax.nn.sigmoid(h1) * h3).astype(BF16)
    return jnp.dot(act, w2_ref[...], preferred_element_type=F32)


def _accumulate(acc_ref, part, j):
    @pl.when(j == 0)
    def _():
        acc_ref[...] = part

    @pl.when(j > 0)
    def _():
        acc_ref[...] += part


def _ffn_kernel(x1_ref, mod_ref, g2_ref, w1_ref, w3_ref, w2_ref, o_ref, act_ref):
    tm = x1_ref.shape[0]
    row0 = pl.program_id(1) * tm
    x = _norm_mod(x1_ref[...], g2_ref[...], mod_ref, row0, 3, 4).astype(BF16)
    for c in range(w1_ref.shape[1] // FFN_TILE_F):
        sl = slice(c * FFN_TILE_F, (c + 1) * FFN_TILE_F)
        h1 = jnp.dot(x, w1_ref[:, sl], preferred_element_type=F32)
        h3 = jnp.dot(x, w3_ref[:, sl], preferred_element_type=F32)
        act_ref[:, sl] = (h1 * jax.nn.sigmoid(h1) * h3).astype(BF16)
    y = jnp.dot(act_ref[...], w2_ref[...], preferred_element_type=F32)
    o_ref[...] = x1_ref[...] + _mod_rows(mod_ref, 5, row0, tm) * y


def ffn_residual(x1, modp, g2, w1, w3, w2):
    b = x1.shape[0]
    tm = FFN_TILE_M
    f = w1.shape[1]
    assert f % FFN_TILE_F == 0
    tok = lambda: pl.BlockSpec((None, tm, D_MODEL), lambda i, t: (i, t, 0))
    resident = lambda a: pl.BlockSpec(a.shape, lambda i, t: (0, 0), pipeline_mode=pl.Buffered(1))
    return pl.pallas_call(
        _ffn_kernel,
        grid=(b, S_ALL // tm),
        in_specs=[tok(),
                  pl.BlockSpec((None, N_MOD, 2, 1, D_MODEL), lambda i, t: (i, 0, 0, 0, 0)),
                  pl.BlockSpec((1, D_MODEL), lambda i, t: (0, 0)),
                  resident(w1), resident(w3), resident(w2)],
        out_specs=tok(),
        out_shape=jax.ShapeDtypeStruct((b, S_ALL, D_MODEL), F32),
        scratch_shapes=[pltpu.VMEM((tm, f), BF16)],
        compiler_params=_params(("parallel", "parallel")),
        name="ffn",
    )(x1, modp, g2, w1, w3, w2)


def _moe_kernel(te_ref, nv_ref, x_ref, w1_ref, w3_ref, w2_ref, o_ref, xb_ref, acc_ref):
    i, j = pl.program_id(0), pl.program_id(1)
    last = pl.num_programs(1) - 1
    valid = i < nv_ref[0]

    @pl.when(valid)
    def _():
        @pl.when(j == 0)
        def _():
            xb_ref[...] = x_ref[...].astype(BF16)

        _accumulate(acc_ref, _swiglu_partial(xb_ref[...], w1_ref, w3_ref, w2_ref), j)

        @pl.when(j == last)
        def _():
            o_ref[...] = acc_ref[...]

    @pl.when(jnp.logical_and(jnp.logical_not(valid), j == last))
    def _():
        o_ref[...] = jnp.zeros_like(o_ref)


def grouped_swiglu(xs, tile_expert, n_valid, w1, w3, w2, tm, tf):
    n_rows, d = xs.shape
    f = w1.shape[-1]
    assert n_rows % tm == 0 and f % tf == 0
    n_f = f // tf

    def col(i, j, nv):
        return jnp.where(i < nv[0], j, n_f - 1)

    grid_spec = pltpu.PrefetchScalarGridSpec(
        num_scalar_prefetch=2,
        grid=(n_rows // tm, n_f),
        in_specs=[pl.BlockSpec((tm, d), lambda i, j, te, nv: (jnp.minimum(i, nv[0] - 1), 0)),
                  pl.BlockSpec((None, d, tf), lambda i, j, te, nv: (te[i], 0, col(i, j, nv))),
                  pl.BlockSpec((None, d, tf), lambda i, j, te, nv: (te[i], 0, col(i, j, nv))),
                  pl.BlockSpec((None, tf, d), lambda i, j, te, nv: (te[i], col(i, j, nv), 0))],
        out_specs=pl.BlockSpec((tm, d), lambda i, j, te, nv: (i, 0)),
        scratch_shapes=[pltpu.VMEM((tm, d), BF16), pltpu.VMEM((tm, d), F32)],
    )
    return pl.pallas_call(
        _moe_kernel,
        grid_spec=grid_spec,
        out_shape=jax.ShapeDtypeStruct((n_rows, d), F32),
        compiler_params=_params(("parallel", "arbitrary")),
        name="moe",
    )(tile_expert, n_valid, xs, w1, w3, w2)


def route(logits, tm):
    n_tok = logits.shape[0]
    n_assign = n_tok * TOP_K
    idx = jnp.arange(N_EXPERTS, dtype=jnp.int32)[None, :]
    m1 = jnp.max(logits, axis=-1, keepdims=True)
    e1 = jnp.min(jnp.where(logits == m1, idx, N_EXPERTS), axis=-1, keepdims=True)
    rest = jnp.where(idx == e1, jnp.finfo(F32).min, logits)
    m2 = jnp.max(rest, axis=-1, keepdims=True)
    e2 = jnp.min(jnp.where(rest == m2, idx, N_EXPERTS), axis=-1, keepdims=True)
    t = jnp.exp(m2 - m1)
    gates = jnp.concatenate([1.0 / (1.0 + t), t / (1.0 + t)], axis=-1)
    flat_e = jnp.concatenate([e1, e2], axis=-1).reshape(n_assign, 1)
    onehot = (flat_e == idx).astype(F32)
    blk = MOE_ROUTE_BLOCK
    nb = n_assign // blk
    oh = onehot.reshape(nb, blk, N_EXPERTS)
    lower = lambda n: (jnp.arange(n)[:, None] > jnp.arange(n)[None, :]).astype(F32)
    within = jnp.einsum('ij,bjk->bik', lower(blk), oh)
    blk_cnt = jnp.sum(oh, axis=1)
    blk_off = jnp.dot(lower(nb), blk_cnt, precision=lax.Precision.HIGHEST)
    rank = jnp.sum((within + blk_off[:, None, :]) * oh, axis=-1).reshape(n_assign)
    counts = jnp.sum(blk_cnt, axis=0).astype(jnp.int32)
    padded = (counts + tm - 1) // tm * tm
    ends = []
    for e in range(N_EXPERTS):
        ends.append(padded[e] + (ends[-1] if ends else 0))
    pend = jnp.stack(ends)
    pstart = pend - padded
    dest = (jnp.sum(onehot * pstart.astype(F32)[None, :], axis=-1) + rank).astype(jnp.int32)
    n_tiles = n_assign // tm + N_EXPERTS
    starts = jnp.arange(n_tiles, dtype=jnp.int32) * tm
    tile_expert = jnp.minimum(jnp.sum((pend[None, :] <= starts[:, None]).astype(jnp.int32), axis=-1),
                              N_EXPERTS - 1)
    n_valid = (pend[-1] // tm).reshape(1)
    meta = jnp.concatenate([pend, padded, n_valid])
    return gates, dest, tile_expert, n_valid, meta


def _row_copies(dest_ref, src, dst, sem, rows, gather, side_work=()):
    n_groups = rows // SUBLANES
    for g in range(n_groups):
        for r in range(g * SUBLANES, (g + 1) * SUBLANES):
            for kk in range(TOP_K):
                d = dest_ref[0, TOP_K * r + kk]
                if gather:
                    pltpu.make_async_copy(src.at[pl.ds(d, 1), :], dst.at[kk, pl.ds(r, 1), :], sem).start()
                else:
                    pltpu.make_async_copy(src.at[pl.ds(r, 1), :], dst.at[pl.ds(d, 1), :], sem).start()
        for thunk in side_work[g * len(side_work) // n_groups:(g + 1) * len(side_work) // n_groups]:
            thunk()
    if gather:
        pltpu.make_async_copy(dst, dst, sem).wait()
    else:
        for _ in range(TOP_K):
            pltpu.make_async_copy(src, src, sem).wait()


def _dispatch_kernel(meta_ref, dest_ref, h_ref, w_ref, xs_ref, wb_ref, zero_ref, sem, zsem):
    tm_moe = zero_ref.shape[0]

    @pl.when(pl.program_id(0) == 0)
    def _():
        zero_ref[...] = jnp.zeros_like(zero_ref)
        n_tiles = xs_ref.shape[0] // tm_moe
        fills = []
        for e in range(N_EXPERTS):
            start = pl.multiple_of(jnp.maximum(meta_ref[e] - tm_moe, 0), tm_moe)
            fills.append((meta_ref[N_EXPERTS + e] > 0, start))
        for i in range(n_tiles - N_EXPERTS, n_tiles):
            fills.append((i >= meta_ref[2 * N_EXPERTS], i * tm_moe))
        for cond, start in fills:
            @pl.when(cond)
            def _():
                pltpu.make_async_copy(zero_ref, xs_ref.at[pl.ds(start, tm_moe), :], zsem).start()
        for cond, start in fills:
            @pl.when(cond)
            def _():
                pltpu.make_async_copy(zero_ref, xs_ref.at[pl.ds(start, tm_moe), :], zsem).wait()

    def cast_rows(c):
        def thunk():
            wb_ref[c * 16:(c + 1) * 16, :] = w_ref[c * 16:(c + 1) * 16, :].astype(BF16)
        return thunk

    casts = [cast_rows(c) for c in range(w_ref.shape[0] // 16)]
    _row_copies(dest_ref, h_ref, xs_ref, sem, h_ref.shape[0], gather=False, side_work=casts)


def moe_dispatch(h2, dest, meta, n_slots, w):
    n_tok, d = h2.shape
    tm = MOE_DISPATCH_TILE
    n_steps = n_tok // tm
    wf = w.reshape(-1, w.shape[-1])
    assert wf.shape[0] % (16 * n_steps) == 0
    slab = pl.BlockSpec((wf.shape[0] // n_steps, wf.shape[1]), lambda t, m: (t, 0))
    grid_spec = pltpu.PrefetchScalarGridSpec(
        num_scalar_prefetch=1,
        grid=(n_steps,),
        in_specs=[pl.BlockSpec((None, 1, TOP_K * tm), lambda t, m: (t, 0, 0), memory_space=pltpu.SMEM),
                  pl.BlockSpec((tm, d), lambda t, m: (t, 0)), slab],
        out_specs=[pl.BlockSpec(memory_space=pl.ANY), slab],
        scratch_shapes=[pltpu.VMEM((MOE_TILE_M, d), h2.dtype), pltpu.SemaphoreType.DMA,
                        pltpu.SemaphoreType.DMA],
    )
    xs, wb = pl.pallas_call(
        _dispatch_kernel,
        grid_spec=grid_spec,
        out_shape=[jax.ShapeDtypeStruct((n_slots, d), h2.dtype), jax.ShapeDtypeStruct(wf.shape, BF16)],
        compiler_params=_params(("arbitrary",)),
        name="moe_dispatch",
    )(meta, dest.reshape(n_steps, 1, TOP_K * tm), h2, wf)
    return xs, wb.reshape(w.shape)


def _combine_kernel(dest_ref, x1_ref, g_ref, mod_ref, ys_ref, o_ref, buf_ref, sem):
    tm = x1_ref.shape[0]
    _row_copies(dest_ref, ys_ref, buf_ref, sem, tm, gather=True)
    g = g_ref[...]
    moe = g[:, 0:1] * buf_ref[0] + g[:, 1:2] * buf_ref[1]
    o_ref[...] = x1_ref[...] + mod_ref[5, 0] * moe


def moe_combine(x1, gates, dest, ys, modp):
    b, s, d = x1.shape
    tm = MOE_COMBINE_TILE
    nt = s // tm
    return pl.pallas_call(
        _combine_kernel,
        grid=(b, nt),
        in_specs=[pl.BlockSpec((None, 1, TOP_K * tm), lambda i, t: (i * nt + t, 0, 0), memory_space=pltpu.SMEM),
                  pl.BlockSpec((None, tm, d), lambda i, t: (i, t, 0)),
                  pl.BlockSpec((None, tm, TOP_K), lambda i, t: (i, t, 0)),
                  pl.BlockSpec((None, N_MOD, 2, 1, d), lambda i, t: (i, 0, 0, 0, 0)),
                  pl.BlockSpec(memory_space=pl.ANY)],
        out_specs=pl.BlockSpec((None, tm, d), lambda i, t: (i, t, 0)),
        out_shape=jax.ShapeDtypeStruct((b, s, d), F32),
        scratch_shapes=[pltpu.VMEM((TOP_K, tm, d), F32), pltpu.SemaphoreType.DMA],
        compiler_params=_params(("arbitrary", "arbitrary")),
        name="moe_combine",
    )(dest.reshape(b * nt, 1, TOP_K * tm), x1, gates.reshape(b, s, TOP_K), modp, ys)


def moe_residual(x1, h2, logits, modp, w1, w3, w2):
    b, s, d = x1.shape
    n_tok = b * s
    tm = MOE_TILE_M
    gates, dest, tile_expert, n_valid, meta = route(logits.reshape(n_tok, N_EXPERTS), tm)
    n_slots = (n_tok * TOP_K // tm + N_EXPERTS) * tm
    xs, w2 = moe_dispatch(h2.reshape(n_tok, d), dest, meta, n_slots, w2)
    ys = grouped_swiglu(xs, tile_expert, n_valid, w1, w3, w2, tm, MOE_TILE_F)
    return moe_combine(x1, gates, dest, ys, modp)


def _proj1_kernel(x_ref, g_ref, mod_ref, w_ref, qan_ref, qb_ref, qbs_ref, kvan_ref, kbk_ref, kbv_ref,
                  cos_ref, sin_ref, gains_ref, ones_ref, hy_ref, q_ref, k_ref, v_ref):
    tm = x_ref.shape[0]
    row0 = pl.program_id(1) * tm
    h = _norm_mod(x_ref[...], g_ref[...], mod_ref, row0, 0, 1)
    z = jnp.dot(h.astype(BF16), w_ref[...], preferred_element_type=F32)
    hy_ref[...] = z[:, :HY_IN]

    def rms(a, g):
        return a * lax.rsqrt(jnp.mean(a * a, axis=-1, keepdims=True) + EPS) * g

    c_q, c_kv = HY_IN, HY_IN + MLA_Q_RANK
    c_r = c_kv + MLA_KV_RANK
    qa = rms(z[:, c_q:c_kv], qan_ref[...]).astype(BF16)
    kva = rms(z[:, c_kv:c_r], kvan_ref[...]).astype(BF16)
    q = jnp.dot(qa, qb_ref[...], preferred_element_type=F32)
    q_sw = jnp.dot(qa, qbs_ref[...], preferred_element_type=F32)
    kk = jnp.dot(kva, kbk_ref[...], preferred_element_type=F32)
    v_ref[...] = jnp.dot(kva, kbv_ref[...], preferred_element_type=F32).astype(BF16)
    k_rope = pltpu.roll(z[:, c_r:c_r + LANES], MLA_NOPE, 1)
    first_half = lax.broadcasted_iota(jnp.int32, (tm, LANES), 1) < MLA_NOPE + MLA_ROPE // 2
    sh = MLA_ROPE // 2
    k_rope_sw = jnp.where(first_half, pltpu.roll(k_rope, LANES - sh, 1), pltpu.roll(k_rope, sh, 1))
    cos, sin = cos_ref[...], sin_ref[...]
    qg, qg_sw, kg, kg_sw = (gains_ref[i:i + 1, :] for i in range(4))

    def head(a, a_sw, gain, gain_sw):
        inv = lax.rsqrt(_group_sums(a, ones_ref) / MLA_QK + EPS)
        return (a * (inv * gain) * cos + a_sw * (inv * gain_sw) * sin).astype(BF16)

    for hh in range(MLA_HEADS):
        sl = slice(hh * LANES, (hh + 1) * LANES)
        q_ref[:, sl] = head(q[:, sl], q_sw[:, sl], qg, qg_sw)
        k_ref[:, sl] = head(kk[:, sl] + k_rope, k_rope_sw, kg, kg_sw)


def _swap_rope_halves(a):
    lead = a.shape[:-1]
    g = a.reshape(*lead, -1, LANES)
    lo, mid = MLA_NOPE, MLA_NOPE + MLA_ROPE // 2
    g = jnp.concatenate([g[..., :lo], g[..., mid:MLA_QK], g[..., lo:mid], g[..., MLA_QK:]], axis=-1)
    return g.reshape(a.shape)


def proj1(xa, g, modp, w_in, qan, qb, kvan, kbk, kbv, cos, sin, q_gain, k_gain):
    b = xa.shape[0]
    tm = TOKEN_TILE
    tok = lambda width: pl.BlockSpec((None, tm, width), lambda i, t: (i, t, 0))
    full = lambda a: pl.BlockSpec(a.shape, lambda i, t: (0, 0))
    hw = MLA_HEADS * LANES
    qbs = _swap_rope_halves(qb)
    gains = jnp.concatenate([q_gain, _swap_rope_halves(q_gain), k_gain, _swap_rope_halves(k_gain)], axis=0)
    ones = jnp.ones((LANES, LANES), BF16)
    return pl.pallas_call(
        _proj1_kernel,
        grid=(b, S_ALL // tm),
        in_specs=[tok(D_MODEL), full(g),
                  pl.BlockSpec((None, N_MOD, 2, 1, D_MODEL), lambda i, t: (i, 0, 0, 0, 0)),
                  full(w_in), full(qan), full(qb), full(qbs), full(kvan), full(kbk), full(kbv),
                  pl.BlockSpec((tm, LANES), lambda i, t: (t, 0)),
                  pl.BlockSpec((tm, LANES), lambda i, t: (t, 0)),
                  full(gains), full(ones)],
        out_specs=[tok(HY_IN), tok(hw), tok(hw), tok(MLA_HEADS * MLA_V)],
        out_shape=[jax.ShapeDtypeStruct((b, S_ALL, HY_IN), F32),
                   jax.ShapeDtypeStruct((b, S_ALL, hw), BF16),
                   jax.ShapeDtypeStruct((b, S_ALL, hw), BF16),
                   jax.ShapeDtypeStruct((b, S_ALL, MLA_HEADS * MLA_V), BF16)],
        compiler_params=_params(("parallel", "parallel")),
        name="proj1",
    )(xa, g, modp, w_in, qan, qb, qbs, kvan, kbk, kbv, cos, sin, gains, ones)


def _merge1_kernel(hy_ref, att_ref, x_ref, mod_ref, g2_ref, w_hy_ref, w_att_ref, r_ref,
                   x1_ref, h2_ref, lg_ref):
    y = (jnp.dot(hy_ref[...].astype(BF16), w_hy_ref[...], preferred_element_type=F32)
         + jnp.dot(att_ref[...], w_att_ref[...], preferred_element_type=F32))
    x1 = x_ref[...] + mod_ref[2, 0] * y
    x1_ref[...] = x1
    h2 = _norm_mod(x1, g2_ref[...], mod_ref, 0, 3, 4)
    h2_ref[...] = h2
    lane = lax.broadcasted_iota(jnp.int32, lg_ref.shape, 1)
    logits = jnp.zeros(lg_ref.shape, F32)
    for e in range(N_EXPERTS):
        col = jnp.sum(h2 * r_ref[e:e + 1, :], axis=-1, keepdims=True)
        logits = jnp.where(lane == e, col, logits)
    lg_ref[...] = logits


def merge1(hy, att, xa, modp, g2, w_hy, w_att, router):
    b = xa.shape[0]
    tm = MERGE1_TILE
    tok = lambda width: pl.BlockSpec((None, tm, width), lambda i, t: (i, t, 0))
    full = lambda a: pl.BlockSpec(a.shape, lambda i, t: (0, 0))
    return pl.pallas_call(
        _merge1_kernel,
        grid=(b, SEQ // tm),
        in_specs=[tok(HY_WIDTH), tok(HALF_MIX), tok(D_MODEL),
                  pl.BlockSpec((None, N_MOD, 2, 1, D_MODEL), lambda i, t: (i, 0, 0, 0, 0)),
                  full(g2), full(w_hy), full(w_att), full(router)],
        out_specs=[tok(D_MODEL), tok(D_MODEL), tok(N_EXPERTS)],
        out_shape=[jax.ShapeDtypeStruct((b, SEQ, D_MODEL), F32),
                   jax.ShapeDtypeStruct((b, SEQ, D_MODEL), F32),
                   jax.ShapeDtypeStruct((b, SEQ, N_EXPERTS), F32)],
        compiler_params=_params(("parallel", "parallel")),
        name="merge1",
    )(hy, att, xa, modp, g2, w_hy, w_att, router)


HY_N = 2 * SEQ
HY_TW_ROWS = 256
HY_SPEC_FBLK = 256
HY_CONV_FBLK = 512
HY_LANE_TILES = SEQ // LANES


def _twiddle_kernel(ca_ref, sa_ref, cb_ref, sb_ref, fre_ref, fim_ref, ic_ref, is_ref):
    rows = fre_ref.shape[0]
    r = pl.program_id(0) * rows + lax.broadcasted_iota(jnp.int32, (rows, LANES), 0)
    alt_r = (1 - 2 * (r & 1)).astype(F32)
    cb, sb = cb_ref[...], sb_ref[...]
    for a in range(HY_LANE_TILES):
        ca, sa = ca_ref[:, a:a + 1], sa_ref[:, a:a + 1]
        c = ca * cb - sa * sb
        s = sa * cb + ca * sb
        col = a * LANES + lax.broadcasted_iota(jnp.int32, (rows, LANES), 1)
        alt_c = (1 - 2 * (col & 1)).astype(F32)
        w = jnp.where(col == 0, 1.0 / HY_N, 2.0 / HY_N)
        sl = slice(a * LANES, (a + 1) * LANES)
        fre_ref[:, sl] = c.astype(BF16)
        fim_ref[:, sl] = jnp.where(r == 0, alt_c, -s).astype(BF16)
        ic_ref[:, sl] = (c * w).astype(BF16)
        is_ref[:, sl] = jnp.where(col == 0, alt_r / HY_N, -s * w).astype(BF16)


def dft_matrices():
    idx = jnp.arange(SEQ, dtype=jnp.int32)[:, None]
    step = 2.0 * math.pi / HY_N
    ph_a = ((idx * (LANES * jnp.arange(HY_LANE_TILES, dtype=jnp.int32))[None, :]) % HY_N).astype(F32) * step
    ph_b = ((idx * jnp.arange(LANES, dtype=jnp.int32)[None, :]) % HY_N).astype(F32) * step
    rows = HY_TW_ROWS
    tab = lambda width: pl.BlockSpec((rows, width), lambda i: (i, 0))
    out = jax.ShapeDtypeStruct((SEQ, SEQ), BF16)
    return pl.pallas_call(
        _twiddle_kernel,
        grid=(SEQ // rows,),
        in_specs=[tab(HY_LANE_TILES), tab(HY_LANE_TILES), tab(LANES), tab(LANES)],
        out_specs=[tab(SEQ)] * 4,
        out_shape=[out] * 4,
        compiler_params=_params(("parallel",)),
        name="twiddle",
    )(jnp.cos(ph_a), jnp.sin(ph_a), jnp.cos(ph_b), jnp.sin(ph_b))


def _spec_kernel(fre_ref, fim_ref, h_ref, k_ref):
    w = HY_WIDTH
    h = h_ref[...].astype(BF16)
    re = jnp.dot(fre_ref[...], h, preferred_element_type=F32)
    im = jnp.dot(fim_ref[...], h, preferred_element_type=F32)
    hb0 = h[0:1, w:].astype(F32)
    first = (pl.program_id(1) * re.shape[0] + lax.broadcasted_iota(jnp.int32, (re.shape[0], 1), 0)) == 0
    k_ref[0] = re[:, :w] + re[:, w:] - hb0
    k_ref[1] = jnp.where(first, im[:, :w] + im[:, w:] - hb0, im[:, :w] - im[:, w:])


def filter_spectra(filt, fre, fim):
    fb = HY_SPEC_FBLK
    return pl.pallas_call(
        _spec_kernel,
        grid=(HY_ORDER, SEQ // fb),
        in_specs=[pl.BlockSpec((fb, SEQ), lambda o, f: (f, 0)),
                  pl.BlockSpec((fb, SEQ), lambda o, f: (f, 0)),
                  pl.BlockSpec((SEQ, 2 * HY_WIDTH), lambda o, f: (0, o))],
        out_specs=pl.BlockSpec((None, 2, fb, HY_WIDTH), lambda o, f: (o, 0, f, 0)),
        out_shape=jax.ShapeDtypeStruct((HY_ORDER, 2, SEQ, HY_WIDTH), F32),
        compiler_params=_params(("parallel", "parallel")),
        name="filter_spectra",
    )(fre, fim, filt)


def _short_conv(z_ref, w_ref, b_ref, part):
    z = z_ref[...]
    l = z.shape[0]
    t = lax.broadcasted_iota(jnp.int32, (l, 1), 0)
    prev = jnp.where(t == 0, 0.0, pltpu.roll(z, 1, 0))
    nxt = jnp.where(t == l - 1, 0.0, pltpu.roll(z, l - 1, 0))
    cs = slice(part * HY_WIDTH, (part + 1) * HY_WIDTH)
    return (w_ref[0:1, cs] * prev + w_ref[1:2, cs] * z + w_ref[2:3, cs] * nxt) + b_ref[:, cs]


def _hyconv_kernel(u_ref, xg_ref, cw_ref, cb_ref, fre_ref, fim_ref, ic_ref, is_ref, k_ref, skip_ref,
                   y_ref, ub_ref, acc_ref, *, order):
    f = pl.program_id(1)

    def u_f32():
        return _short_conv(u_ref, cw_ref, cb_ref, 0) if order == 0 else u_ref[...]

    @pl.when(f == 0)
    def _():
        ub_ref[...] = u_f32().astype(BF16)

    x_re = jnp.dot(fre_ref[...], ub_ref[...], preferred_element_type=F32)
    x_im = jnp.dot(fim_ref[...], ub_ref[...], preferred_element_type=F32)
    k_re, k_im = k_ref[0], k_ref[1]
    first = (f * x_re.shape[0] + lax.broadcasted_iota(jnp.int32, (x_re.shape[0], 1), 0)) == 0
    y_re = x_re * k_re - jnp.where(first, 0.0, x_im * k_im)
    y_im = x_im * jnp.where(first, k_im, k_re) + jnp.where(first, 0.0, x_re * k_im)
    part = (jnp.dot(ic_ref[...], y_re.astype(BF16), preferred_element_type=F32)
            + jnp.dot(is_ref[...], y_im.astype(BF16), preferred_element_type=F32))
    _accumulate(acc_ref, part, f)

    @pl.when(f == pl.num_programs(1) - 1)
    def _():
        xg = _short_conv(xg_ref, cw_ref, cb_ref, order + 1)
        y_ref[...] = xg * (acc_ref[...] + skip_ref[order:order + 1, :] * u_f32())


def hyena_conv(order, u, hyz, conv_w, conv_b, fre, fim, ic, is_, spectra, skip):
    b = hyz.shape[0]
    fb = HY_CONV_FBLK
    w = HY_WIDTH
    lat = lambda part: pl.BlockSpec((None, SEQ, w), lambda i, f: (i, 0, part))
    full = lambda a: pl.BlockSpec(a.shape, lambda i, f: (0, 0))
    return pl.pallas_call(
        functools.partial(_hyconv_kernel, order=order),
        grid=(b, SEQ // fb),
        in_specs=[lat(0), lat(order + 1), full(conv_w), full(conv_b),
                  pl.BlockSpec((fb, SEQ), lambda i, f: (f, 0)),
                  pl.BlockSpec((fb, SEQ), lambda i, f: (f, 0)),
                  pl.BlockSpec((SEQ, fb), lambda i, f: (0, f)),
                  pl.BlockSpec((SEQ, fb), lambda i, f: (0, f)),
                  pl.BlockSpec((None, 2, fb, w), lambda i, f: (order, 0, f, 0)),
                  full(skip)],
        out_specs=pl.BlockSpec((None, SEQ, w), lambda i, f: (i, 0, 0)),
        out_shape=jax.ShapeDtypeStruct((b, SEQ, w), F32),
        scratch_shapes=[pltpu.VMEM((SEQ, w), BF16), pltpu.VMEM((SEQ, w), F32)],
        compiler_params=_params(("parallel", "arbitrary")),
        name="hyena_conv",
    )(u, hyz, conv_w, conv_b, fre, fim, ic, is_, spectra, skip)


def hyena_filters(length, w1, b1, w2, b2, w3, freq):
    hp = lax.Precision.HIGHEST
    t = jnp.arange(length, dtype=F32)[:, None]
    t_norm = t / max(length - 1, 1)
    bands = jnp.linspace(1e-4, HY_BANDS - 1, HY_BANDS, dtype=F32)
    ang = 2.0 * math.pi * t * bands / length
    z = jnp.concatenate([t_norm, jnp.cos(ang), -jnp.sin(ang)], axis=-1)
    h = jnp.sin(freq * (jnp.dot(z, w1, precision=hp) + b1))
    h = jnp.sin(freq * (jnp.dot(h, w2, precision=hp) + b2))
    h = jnp.dot(h, w3, precision=hp)
    deltas = jnp.abs(jnp.linspace(HY_MIN_DECAY, HY_MAX_DECAY, HY_WIDTH, dtype=F32))
    window = jnp.exp(-t_norm * deltas) + HY_SHIFT
    return h * jnp.tile(window, (1, HY_ORDER * 2))


def hyena(hyz, conv_w, conv_b, fw1, fb1, fw2, fb2, fw3, freq, skip):
    fre, fim, ic, is_ = dft_matrices()
    filt = hyena_filters(SEQ, fw1, fb1, fw2, fb2, fw3, freq)
    spectra = filter_spectra(filt, fre, fim)
    y = hyz
    for o in range(HY_ORDER):
        y = hyena_conv(o, y, hyz, conv_w, conv_b.reshape(1, -1), fre, fim, ic, is_, spectra, skip)
    return y


def _grid_angles(rot_dim):
    n_freq = rot_dim // 4
    inv_freq = ROPE_THETA ** (-jnp.arange(n_freq, dtype=F32) / n_freq)
    t = jnp.arange(SEQ)
    r = (t // GRID_W).astype(F32)
    c_ = (t % GRID_W).astype(F32)
    return jnp.concatenate([r[:, None] * inv_freq, c_[:, None] * inv_freq], axis=-1)


def _rope_tables(rot_dim, lane_offsets):
    ang = _grid_angles(rot_dim)
    c, s = jnp.cos(ang), jnp.sin(ang)
    cos_parts, sin_parts, lane = [], [], 0
    for off in lane_offsets:
        cos_parts += [jnp.ones((SEQ, off - lane), F32), c, c]
        sin_parts += [jnp.zeros((SEQ, off - lane), F32), -s, s]
        lane = off + rot_dim
    cos_parts.append(jnp.ones((SEQ, LANES - lane), F32))
    sin_parts.append(jnp.zeros((SEQ, LANES - lane), F32))
    cos = jnp.concatenate([jnp.concatenate(cos_parts, axis=1), jnp.ones((CTX_LEN, LANES), F32)], axis=0)
    sin = jnp.concatenate([jnp.concatenate(sin_parts, axis=1), jnp.zeros((CTX_LEN, LANES), F32)], axis=0)
    return cos, sin


def _block_diag(w):
    nb, bs, _ = w.shape
    eye = jnp.eye(nb, dtype=w.dtype)
    return (eye[:, None, :, None] * w[:, :, None, :]).reshape(nb * bs, nb * bs)


def _gqa_pair_order():
    g = GQA_Q_HEADS // GQA_KV_HEADS
    heads = []
    for p in range(g):
        heads += [p, g + p]
    return np.concatenate([np.arange(h * HEAD_DIM, (h + 1) * HEAD_DIM) for h in heads])


def _pad_heads(w, n_heads, width):
    k = w.shape[0]
    w = w.reshape(k, n_heads, width)
    return jnp.pad(w, ((0, 0), (0, 0), (0, LANES - width))).reshape(k, n_heads * LANES)


def kernel(x, c, ctx, c_ctx, mod_w, mod_b, norm1_g, norm2_g, ab_w_in, ab_w_out, lru_conv_w, lru_conv_b, lru_w_a, lru_b_a, lru_w_x, lru_b_x, lru_lambda, gqa_q_norm, gqa_k_norm, ffn_w1, ffn_w3, ffn_w2, cd_w_in, cd_w_out, hy_conv_w, hy_conv_b, hy_filt_w1, hy_filt_b1, hy_filt_w2, hy_filt_b2, hy_filt_w3, hy_sin_freq, hy_skip, mla_q_a_norm, mla_q_b, mla_kv_a_norm, mla_kv_b, mla_q_norm, mla_k_norm, moe_router, moe_w1, moe_w3, moe_w2):
    batch = x.shape[0]
    bf = lambda w: w.astype(BF16)
    row = lambda v: v.reshape(1, -1)

    silu_all = jnp.concatenate([jax.nn.silu(c), jax.nn.silu(c_ctx)[None, :],
                                jnp.zeros((16 - batch - 1, D_MODEL), F32)], axis=0)
    mods = []
    mod_all = stacked_matmul(silu_all, mod_w, 1536) + mod_b[:, None, :]
    for layer in range(DEPTH):
        m = mod_all[layer]
        lat = m[:batch].reshape(batch, N_MOD, 1, D_MODEL)
        cx = jnp.broadcast_to(m[batch].reshape(1, N_MOD, 1, D_MODEL), lat.shape)
        mods.append(jnp.stack([lat, cx], axis=2))

    xa = jnp.concatenate([x, ctx], axis=1)

    perm = _gqa_pair_order()
    q0 = 2 * LRU_WIDTH
    w_in0 = ab_w_in[0]
    w_in0 = jnp.concatenate([w_in0[:, :q0], w_in0[:, q0 + perm], w_in0[:, q0 + GQA_Q_HEADS * HEAD_DIM:]], axis=1)
    cos_g, sin_g = _rope_tables(HEAD_DIM, (0, HEAD_DIM))
    q_gain = row(jnp.tile(gqa_q_norm[0], 2) * HEAD_DIM ** -0.5)
    k_gain = row(jnp.tile(gqa_k_norm[0], 2))
    xr, gate, q, k, v = proj0(xa, row(norm1_g[0]), mods[0], bf(w_in0), cos_g, sin_g, q_gain, k_gain)

    w_gates = jnp.concatenate([_block_diag(lru_w_a[0, 0]), _block_diag(lru_w_x[0, 0]),
                               _block_diag(lru_w_a[0, 1]), _block_diag(lru_w_x[0, 1])], axis=1)
    b_gates = jnp.concatenate([lru_b_a[0, 0].reshape(-1), lru_b_x[0, 0].reshape(-1),
                               lru_b_a[0, 1].reshape(-1), lru_b_x[0, 1].reshape(-1)])
    rec = rglru(xr, lru_conv_w[0], row(lru_conv_b[0]), bf(w_gates), row(b_gates), lru_lambda[0])

    att_l, moe_w1b, moe_w3b = attention_latent(
        q, k, v, GQA_HEADS, GQA_TILE_Q,
        cast=(moe_w1[0].reshape(-1, moe_w1.shape[-1]), moe_w3[0].reshape(-1, moe_w3.shape[-1])))
    att_c = attention_context(q, k, v, GQA_HEADS)
    w_out0 = ab_w_out[0]
    x1 = merge0(rec, gate, att_l, att_c, xa, mods[0], bf(w_out0[:LRU_WIDTH]), bf(w_out0[LRU_WIDTH:][perm]))
    xa = ffn_residual(x1, mods[0], row(norm2_g[0]), bf(ffn_w1[0]), bf(ffn_w3[0]), bf(ffn_w2[0]))

    w_in1 = jnp.pad(cd_w_in[0], ((0, 0), (0, LANES - MLA_ROPE)))
    cos_m, sin_m = _rope_tables(MLA_ROPE, (MLA_NOPE,))
    qb = _pad_heads(mla_q_b[0], MLA_HEADS, MLA_QK)
    kvb = mla_kv_b[0].reshape(MLA_KV_RANK, MLA_HEADS, MLA_NOPE + MLA_V)
    kbk = _pad_heads(kvb[:, :, :MLA_NOPE].reshape(MLA_KV_RANK, -1), MLA_HEADS, MLA_NOPE)
    kbv = kvb[:, :, MLA_NOPE:].reshape(MLA_KV_RANK, -1)
    pad_gain = lambda g_: row(jnp.pad(g_, (0, LANES - MLA_QK)))
    hyz, q, k, v = proj1(xa, row(norm1_g[1]), mods[1], bf(w_in1), row(mla_q_a_norm[0]), bf(qb),
                         row(mla_kv_a_norm[0]), bf(kbk), bf(kbv), cos_m, sin_m,
                         pad_gain(mla_q_norm[0] * MLA_QK ** -0.5), pad_gain(mla_k_norm[0]))
    att = attention_latent(q, k, v, MLA_HEAD_GROUPS, MLA_TILE_Q)
    hy = hyena(hyz, hy_conv_w[0], hy_conv_b[0], hy_filt_w1[0], hy_filt_b1[0], hy_filt_w2[0],
               hy_filt_b2[0], hy_filt_w3[0], hy_sin_freq[0], hy_skip[0])
    w_out1 = cd_w_out[0]
    x1, h2, logits = merge1(hy, att, xa, mods[1], row(norm2_g[1]),
                            bf(w_out1[:HY_WIDTH]), bf(w_out1[HY_WIDTH:]), moe_router[0].T)
    return moe_residual(x1, h2, logits, mods[1], moe_w1b.reshape(moe_w1.shape[1:]),
                        moe_w3b.reshape(moe_w3.shape[1:]), moe_w2[0])
```

```python
import functools
import math

import jax
import jax.numpy as jnp
import numpy as np
from jax import lax
from jax.experimental import pallas as pl
from jax.experimental.pallas import tpu as pltpu

F32 = jnp.float32
BF16 = jnp.bfloat16

D_MODEL = 1024
SEQ = 2048
CTX_LEN = 256
S_ALL = SEQ + CTX_LEN
DEPTH = 2
GRID_W = 64
HEAD_DIM = 64
HALF_MIX = D_MODEL // 2
ROPE_THETA = 10000.0
EPS = 1e-6
N_MOD = 6
LRU_WIDTH = HALF_MIX
LRU_BLOCKS = LRU_WIDTH // HEAD_DIM
LRU_CONV = 4
LRU_C = 8.0
GQA_Q_HEADS = HALF_MIX // HEAD_DIM
GQA_KV_HEADS = 2
HY_WIDTH = HALF_MIX
HY_ORDER = 2
HY_CONV = 3
HY_BANDS = 16
HY_TARGET = 1e-2
HY_FAST_PCT = 0.3
HY_SLOW_PCT = 1.5
HY_MIN_DECAY = math.log(HY_TARGET) / HY_SLOW_PCT
HY_MAX_DECAY = math.log(HY_TARGET) / HY_FAST_PCT
HY_SHIFT = 0.05
HY_IN = (HY_ORDER + 1) * HY_WIDTH
MLA_HEADS = HALF_MIX // HEAD_DIM
MLA_Q_RANK = D_MODEL // 4
MLA_KV_RANK = D_MODEL // 8
MLA_NOPE = HEAD_DIM
MLA_ROPE = HEAD_DIM // 2
MLA_V = HEAD_DIM
MLA_QK = MLA_NOPE + MLA_ROPE
N_EXPERTS = 8
TOP_K = 2

LANES = 128
SUBLANES = 8
V7X_VMEM_LIMIT_BYTES = 56 * 1024 * 1024

TOKEN_TILE = 768
FFN_TILE_M = 768
FFN_TILE_F = 256
GQA_TILE_Q = 512
MLA_TILE_Q = 512
MERGE1_TILE = 512
MOE_TILE_M = 512
MOE_TILE_F = 1792
MOE_ROUTE_BLOCK = 256
MOE_DISPATCH_TILE = 256
MOE_COMBINE_TILE = 256

LRU_GAP = SUBLANES
LRU_LAT0 = CTX_LEN + LRU_GAP
LRU_CHUNK = 464
LRU_ROWS = 2320
LRU_PAD_FRONT = SUBLANES

assert S_ALL % TOKEN_TILE == 0 and S_ALL % FFN_TILE_M == 0
assert SEQ % GQA_TILE_Q == 0 and SEQ % MLA_TILE_Q == 0
assert LRU_ROWS % LRU_CHUNK == 0 and LRU_ROWS >= LRU_LAT0 + SEQ + SUBLANES


def _params(sem):
    return pltpu.CompilerParams(dimension_semantics=sem, vmem_limit_bytes=V7X_VMEM_LIMIT_BYTES)


def _norm_mod(x, g, mod_pair, row0, shift_idx, scale_idx):
    y = x * lax.rsqrt(jnp.mean(x * x, axis=-1, keepdims=True) + EPS) * g
    shift = _mod_rows(mod_pair, shift_idx, row0, x.shape[0])
    scale = _mod_rows(mod_pair, scale_idx, row0, x.shape[0])
    return y * (1.0 + scale) + shift


def _mod_rows(mod_pair, idx, row0, rows):
    row = row0 + lax.broadcasted_iota(jnp.int32, (rows, 1), 0)
    return jnp.where(row >= SEQ, mod_pair[idx, 1], mod_pair[idx, 0])


def _mm_kernel(x_ref, w_ref, o_ref):
    o_ref[...] = jnp.dot(x_ref[...].astype(BF16), w_ref[...].astype(BF16),
                         preferred_element_type=F32)


def stacked_matmul(x, w, tn):
    m, k = x.shape
    n_l, _, n = w.shape
    assert n % tn == 0
    return pl.pallas_call(
        _mm_kernel,
        grid=(n_l, n // tn),
        in_specs=[pl.BlockSpec((m, k), lambda l, j: (0, 0)),
                  pl.BlockSpec((None, k, tn), lambda l, j: (l, 0, j))],
        out_specs=pl.BlockSpec((None, m, tn), lambda l, j: (l, 0, j)),
        out_shape=jax.ShapeDtypeStruct((n_l, m, n), F32),
        compiler_params=_params(("parallel", "parallel")),
        name="matmul",
    )(x, w)


def _rope_pair(a, cos, sin, first_half, shift):
    rot = jnp.where(first_half, pltpu.roll(a, LANES - shift, 1), pltpu.roll(a, shift, 1))
    return a * cos + rot * sin


def _group_sums(a, ones_ref):
    return jnp.dot((a * a).astype(BF16), ones_ref[...], preferred_element_type=F32)


def _proj0_kernel(x_ref, g_ref, mod_ref, w_ref, cos_ref, sin_ref, qg_ref, kg_ref, ones_ref,
                  xr_ref, gate_ref, q_ref, k_ref, v_ref):
    tm = x_ref.shape[0]
    row0 = pl.program_id(1) * tm
    h = _norm_mod(x_ref[...], g_ref[...], mod_ref, row0, 0, 1)
    z = jnp.dot(h.astype(BF16), w_ref[...], preferred_element_type=F32)
    xr_ref[...] = z[:, :LRU_WIDTH]
    gate_ref[...] = z[:, LRU_WIDTH:2 * LRU_WIDTH]
    cos, sin = cos_ref[...], sin_ref[...]
    first_half = (lax.broadcasted_iota(jnp.int32, (tm, LANES), 1) & (HEAD_DIM // 2)) == 0

    def head_pair(a, gain):
        inv = lax.rsqrt(_group_sums(a, ones_ref) / HEAD_DIM + EPS)
        return _rope_pair(a * inv * gain, cos, sin, first_half, HEAD_DIM // 2).astype(BF16)

    q0 = 2 * LRU_WIDTH
    for p in range(GQA_Q_HEADS // 2):
        q_ref[:, p * LANES:(p + 1) * LANES] = head_pair(z[:, q0 + p * LANES:q0 + (p + 1) * LANES], qg_ref[...])
    k0 = q0 + GQA_Q_HEADS * HEAD_DIM
    k_ref[...] = head_pair(z[:, k0:k0 + LANES], kg_ref[...])
    v_ref[...] = z[:, k0 + LANES:k0 + 2 * LANES].astype(BF16)


def proj0(xa, g, modp, w_in, cos, sin, q_gain, k_gain):
    b = xa.shape[0]
    tm = TOKEN_TILE
    n = w_in.shape[1]
    tok = lambda width: pl.BlockSpec((None, tm, width), lambda i, t: (i, t, 0))
    full = lambda r, c_: pl.BlockSpec((r, c_), lambda i, t: (0, 0))
    half = jnp.arange(LANES) // HEAD_DIM
    ones_pair = (half[:, None] == half[None, :]).astype(BF16)
    return pl.pallas_call(
        _proj0_kernel,
        grid=(b, S_ALL // tm),
        in_specs=[tok(D_MODEL), full(1, D_MODEL),
                  pl.BlockSpec((None, N_MOD, 2, 1, D_MODEL), lambda i, t: (i, 0, 0, 0, 0)),
                  full(D_MODEL, n),
                  pl.BlockSpec((tm, LANES), lambda i, t: (t, 0)),
                  pl.BlockSpec((tm, LANES), lambda i, t: (t, 0)),
                  full(1, LANES), full(1, LANES), full(LANES, LANES)],
        out_specs=[tok(LRU_WIDTH), tok(LRU_WIDTH), tok(GQA_Q_HEADS * HEAD_DIM), tok(LANES), tok(LANES)],
        out_shape=[jax.ShapeDtypeStruct((b, S_ALL, LRU_WIDTH), F32),
                   jax.ShapeDtypeStruct((b, S_ALL, LRU_WIDTH), F32),
                   jax.ShapeDtypeStruct((b, S_ALL, GQA_Q_HEADS * HEAD_DIM), BF16),
                   jax.ShapeDtypeStruct((b, S_ALL, LANES), BF16),
                   jax.ShapeDtypeStruct((b, S_ALL, LANES), BF16)],
        compiler_params=_params(("parallel", "parallel")),
        name="proj0",
    )(xa, g, modp, w_in, cos, sin, q_gain, k_gain, ones_pair)


def _lru_kernel(xr_ref, cw_ref, cb_ref, wg_ref, bg_ref, lam_ref, rec_ref,
                pad_ref, af_ref, hf_ref, ab_ref, hb_ref):
    w = LRU_WIDTH
    pad_ref[...] = jnp.zeros_like(pad_ref)
    pad_ref[LRU_PAD_FRONT:LRU_PAD_FRONT + CTX_LEN, :] = xr_ref[SEQ:S_ALL, :]
    pad_ref[LRU_PAD_FRONT + LRU_LAT0:LRU_PAD_FRONT + LRU_LAT0 + SEQ, :] = xr_ref[0:SEQ, :]
    lam = lam_ref[...]
    neg_c_softplus = -LRU_C * (jnp.maximum(-lam, 0.0) + jnp.log(1.0 + jnp.exp(-jnp.abs(lam))))
    left = LRU_CONV // 2
    for ch in range(LRU_ROWS // LRU_CHUNK):
        r0 = ch * LRU_CHUNK
        u = cb_ref[...]
        for j in range(LRU_CONV):
            start = LRU_PAD_FRONT + r0 - left + j
            u = u + cw_ref[j:j + 1, :] * pad_ref[start:start + LRU_CHUNK, :]
        gz = jnp.dot(u.astype(BF16), wg_ref[...], preferred_element_type=F32) + bg_ref[...]
        for d, (a_ref, h_ref) in enumerate(((af_ref, hf_ref), (ab_ref, hb_ref))):
            r = jax.nn.sigmoid(gz[:, 2 * d * w:(2 * d + 1) * w])
            i = jax.nn.sigmoid(gz[:, (2 * d + 1) * w:(2 * d + 2) * w])
            a = jnp.exp(neg_c_softplus[d:d + 1, :] * r)
            a_ref[r0:r0 + LRU_CHUNK, :] = a
            h_ref[r0:r0 + LRU_CHUNK, :] = jnp.sqrt(1.0 - a * a) * (i * u)

    n_ctx_groups = CTX_LEN // SUBLANES
    sub = lax.broadcasted_iota(jnp.int32, (SUBLANES, w), 0)

    def tile_scan(a, b, descending):
        for s in (1, 2, 4):
            if descending:
                ok, shift = sub < SUBLANES - s, SUBLANES - s
            else:
                ok, shift = sub >= s, s
            a_s = jnp.where(ok, pltpu.roll(a, shift, 0), 1.0)
            b_s = jnp.where(ok, pltpu.roll(b, shift, 0), 0.0)
            a, b = a * a_s, a * b_s + b
        return a, b

    def group(tt, carry):
        hf, hb = carry
        is_lat = tt >= n_ctx_groups
        rf = pl.ds(pl.multiple_of(tt * SUBLANES + jnp.where(is_lat, LRU_GAP, 0), SUBLANES), SUBLANES)
        rb = pl.ds(pl.multiple_of(
            jnp.where(is_lat, LRU_LAT0 + SEQ + CTX_LEN, CTX_LEN) - (tt + 1) * SUBLANES, SUBLANES), SUBLANES)
        a, b = tile_scan(af_ref[rf, :], hf_ref[rf, :], False)
        h = a * hf + b
        hf_ref[rf, :] = h
        hf = jnp.broadcast_to(h[SUBLANES - 1:SUBLANES, :], (SUBLANES, w))
        a, b = tile_scan(ab_ref[rb, :], hb_ref[rb, :], True)
        h = a * hb + b
        hb_ref[rb, :] = h
        hb = jnp.broadcast_to(h[0:1, :], (SUBLANES, w))
        return hf, hb

    zero = jnp.zeros((SUBLANES, w), F32)
    lax.fori_loop(0, S_ALL // SUBLANES, group, (zero, zero))
    rec_ref[0:SEQ, :] = hf_ref[LRU_LAT0:LRU_LAT0 + SEQ, :] + hb_ref[LRU_LAT0:LRU_LAT0 + SEQ, :]
    rec_ref[SEQ:S_ALL, :] = hf_ref[0:CTX_LEN, :] + hb_ref[0:CTX_LEN, :]


def rglru(xr, conv_w, conv_b, w_gates, b_gates, lam):
    b = xr.shape[0]
    w = LRU_WIDTH
    full = lambda r, c_: pl.BlockSpec((r, c_), lambda i: (0, 0))
    rows = pltpu.VMEM((LRU_ROWS, w), F32)
    return pl.pallas_call(
        _lru_kernel,
        grid=(b,),
        in_specs=[pl.BlockSpec((None, S_ALL, w), lambda i: (i, 0, 0)),
                  full(LRU_CONV, w), full(1, w), full(w, 4 * w), full(1, 4 * w), full(2, w)],
        out_specs=pl.BlockSpec((None, S_ALL, w), lambda i: (i, 0, 0)),
        out_shape=jax.ShapeDtypeStruct((b, S_ALL, w), F32),
        scratch_shapes=[pltpu.VMEM((LRU_ROWS + 2 * LRU_PAD_FRONT, w), F32), rows, rows, rows, rows],
        compiler_params=_params(("parallel",)),
        name="rglru",
    )(xr, conv_w, conv_b, w_gates, b_gates, lam)


def _attn_kernel(q_ref, k_ref, v_ref, *rest, heads):
    n_cast = (len(rest) - 1) // 2
    o_ref = rest[n_cast]
    cast_pairs = list(zip(rest[:n_cast], rest[n_cast + 1:]))
    tq, sk = q_ref.shape[0], k_ref.shape[0]
    q_lo = lax.broadcasted_iota(jnp.int32, (tq, LANES), 1) < HEAD_DIM
    v_lo = lax.broadcasted_iota(jnp.int32, (sk, LANES), 1) < HEAD_DIM

    def keep(x, is_lo, half):
        if half is None:
            return x
        return jnp.where(is_lo if half == 0 else jnp.logical_not(is_lo), x, jnp.zeros_like(x))

    for og, members in enumerate(heads):
        acc = None
        for qg, qh, kg, vg, vh in members:
            q = keep(q_ref[:, qg * LANES:(qg + 1) * LANES], q_lo, qh)
            s = lax.dot_general(q, k_ref[:, kg * LANES:(kg + 1) * LANES],
                                (((1,), (1,)), ((), ())), preferred_element_type=F32)
            p = jnp.exp(s - jnp.max(s, axis=-1, keepdims=True))
            l = jnp.sum(p, axis=-1, keepdims=True)
            v = keep(v_ref[:, vg * LANES:(vg + 1) * LANES], v_lo, vh)
            o = jnp.dot(p.astype(BF16), v, preferred_element_type=F32) / l
            acc = o if acc is None else acc + o
        o_ref[:, og * LANES:(og + 1) * LANES] = acc.astype(o_ref.dtype)
        for src, dst in cast_pairs:
            r0, r1 = (og * src.shape[0] // len(heads), (og + 1) * src.shape[0] // len(heads))
            dst[r0:r1, :] = src[r0:r1, :].astype(BF16)


def attention(q, k, v, heads, q_block, n_q_blocks, tq, k_rows, cast=()):
    b, s, qw = q.shape
    kw, vw = k.shape[-1], v.shape[-1]
    ow = len(heads) * LANES
    n_steps = b * n_q_blocks
    kv_spec = lambda width: pl.BlockSpec((None, k_rows, width), lambda i, t: (i, s // k_rows - 1, 0))
    slab = lambda w: pl.BlockSpec((w.shape[0] // n_steps, w.shape[1]), lambda i, t: (i * n_q_blocks + t, 0))
    assert all(w.shape[0] % (16 * len(heads) * n_steps) == 0 for w in cast)
    out = pl.pallas_call(
        functools.partial(_attn_kernel, heads=heads),
        grid=(b, n_q_blocks),
        in_specs=[pl.BlockSpec((None, tq, qw), lambda i, t: (i, q_block + t, 0)), kv_spec(kw), kv_spec(vw)]
        + [slab(w) for w in cast],
        out_specs=[pl.BlockSpec((None, tq, ow), lambda i, t: (i, t, 0))] + [slab(w) for w in cast],
        out_shape=[jax.ShapeDtypeStruct((b, n_q_blocks * tq, ow), BF16)]
        + [jax.ShapeDtypeStruct(w.shape, BF16) for w in cast],
        compiler_params=_params(("parallel", "parallel")),
        name="attention",
    )(q, k, v, *cast)
    return out if cast else out[0]


GQA_HEADS = tuple(((p, 0, 0, 0, 0), (p, 1, 0, 0, 1)) for p in range(GQA_Q_HEADS // 2))
MLA_HEAD_GROUPS = tuple(((2 * p, None, 2 * p, p, 0), (2 * p + 1, None, 2 * p + 1, p, 1))
                        for p in range(MLA_HEADS // 2))


def attention_latent(q, k, v, heads, tq, cast=()):
    return attention(q, k, v, heads, 0, SEQ // tq, tq, S_ALL, cast)


def attention_context(q, k, v, heads):
    return attention(q, k, v, heads, SEQ // CTX_LEN, 1, CTX_LEN, CTX_LEN)


def _merge0_kernel(rec_ref, gate_ref, att_l_ref, att_c_ref, x_ref, mod_ref,
                   w_rec_ref, w_att_ref, x1_ref):
    tm = x_ref.shape[0]
    row0 = pl.program_id(1) * tm
    rg = (rec_ref[...] * jax.nn.gelu(gate_ref[...])).astype(BF16)
    att = jnp.where(row0 >= SEQ, att_c_ref[...], att_l_ref[...])
    y = (jnp.dot(rg, w_rec_ref[...], preferred_element_type=F32)
         + jnp.dot(att, w_att_ref[...], preferred_element_type=F32))
    x1_ref[...] = x_ref[...] + _mod_rows(mod_ref, 2, row0, tm) * y


def merge0(rec, gate, att_l, att_c, xa, modp, w_rec, w_att):
    b = xa.shape[0]
    tm = CTX_LEN
    n_lat = SEQ // tm
    tok = lambda width: pl.BlockSpec((None, tm, width), lambda i, t: (i, t, 0))
    full = lambda r, c_: pl.BlockSpec((r, c_), lambda i, t: (0, 0))
    return pl.pallas_call(
        _merge0_kernel,
        grid=(b, S_ALL // tm),
        in_specs=[tok(LRU_WIDTH), tok(LRU_WIDTH),
                  pl.BlockSpec((None, tm, HALF_MIX), lambda i, t: (i, jnp.minimum(t, n_lat - 1), 0)),
                  pl.BlockSpec((None, tm, HALF_MIX), lambda i, t: (i, 0, 0)),
                  tok(D_MODEL),
                  pl.BlockSpec((None, N_MOD, 2, 1, D_MODEL), lambda i, t: (i, 0, 0, 0, 0)),
                  full(LRU_WIDTH, D_MODEL), full(HALF_MIX, D_MODEL)],
        out_specs=tok(D_MODEL),
        out_shape=jax.ShapeDtypeStruct((b, S_ALL, D_MODEL), F32),
        compiler_params=_params(("parallel", "parallel")),
        name="merge0",
    )(rec, gate, att_l, att_c, xa, modp, w_rec, w_att)


def _swiglu_partial(x, w1_ref, w3_ref, w2_ref):
    h1 = jnp.dot(x, w1_ref[...], preferred_element_type=F32)
    h3 = jnp.dot(x, w3_ref[...], preferred_element_type=F32)
    act = (h1 * jax.nn.sigmoid(h1) * h3).astype(BF16)
    return jnp.dot(act, w2_ref[...], preferred_element_type=F32)


def _accumulate(acc_ref, part, j):
    @pl.when(j == 0)
    def _():
        acc_ref[...] = part

    @pl.when(j > 0)
    def _():
        acc_ref[...] += part


def _ffn_kernel(x1_ref, mod_ref, g2_ref, w1_ref, w3_ref, w2_ref, o_ref, act_ref):
    tm = x1_ref.shape[0]
    row0 = pl.program_id(1) * tm
    x = _norm_mod(x1_ref[...], g2_ref[...], mod_ref, row0, 3, 4).astype(BF16)
    for c in range(w1_ref.shape[1] // FFN_TILE_F):
        sl = slice(c * FFN_TILE_F, (c + 1) * FFN_TILE_F)
        h1 = jnp.dot(x, w1_ref[:, sl], preferred_element_type=F32)
        h3 = jnp.dot(x, w3_ref[:, sl], preferred_element_type=F32)
        act_ref[:, sl] = (h1 * jax.nn.sigmoid(h1) * h3).astype(BF16)
    y = jnp.dot(act_ref[...], w2_ref[...], preferred_element_type=F32)
    o_ref[...] = x1_ref[...] + _mod_rows(mod_ref, 5, row0, tm) * y


def ffn_residual(x1, modp, g2, w1, w3, w2):
    b = x1.shape[0]
    tm = FFN_TILE_M
    f = w1.shape[1]
    assert f % FFN_TILE_F == 0
    tok = lambda: pl.BlockSpec((None, tm, D_MODEL), lambda i, t: (i, t, 0))
    resident = lambda a: pl.BlockSpec(a.shape, lambda i, t: (0, 0), pipeline_mode=pl.Buffered(1))
    return pl.pallas_call(
        _ffn_kernel,
        grid=(b, S_ALL // tm),
        in_specs=[tok(),
                  pl.BlockSpec((None, N_MOD, 2, 1, D_MODEL), lambda i, t: (i, 0, 0, 0, 0)),
                  pl.BlockSpec((1, D_MODEL), lambda i, t: (0, 0)),
                  resident(w1), resident(w3), resident(w2)],
        out_specs=tok(),
        out_shape=jax.ShapeDtypeStruct((b, S_ALL, D_MODEL), F32),
        scratch_shapes=[pltpu.VMEM((tm, f), BF16)],
        compiler_params=_params(("parallel", "parallel")),
        name="ffn",
    )(x1, modp, g2, w1, w3, w2)


def _moe_kernel(te_ref, nv_ref, x_ref, w1_ref, w3_ref, w2_ref, o_ref, xb_ref, acc_ref):
    i, j = pl.program_id(0), pl.program_id(1)
    last = pl.num_programs(1) - 1
    valid = i < nv_ref[0]

    @pl.when(valid)
    def _():
        @pl.when(j == 0)
        def _():
            xb_ref[...] = x_ref[...].astype(BF16)

        _accumulate(acc_ref, _swiglu_partial(xb_ref[...], w1_ref, w3_ref, w2_ref), j)

        @pl.when(j == last)
        def _():
            o_ref[...] = acc_ref[...]

    @pl.when(jnp.logical_and(jnp.logical_not(valid), j == last))
    def _():
        o_ref[...] = jnp.zeros_like(o_ref)


def grouped_swiglu(xs, tile_expert, n_valid, w1, w3, w2, tm, tf):
    n_rows, d = xs.shape
    f = w1.shape[-1]
    assert n_rows % tm == 0 and f % tf == 0
    n_f = f // tf

    def col(i, j, nv):
        return jnp.where(i < nv[0], j, n_f - 1)

    grid_spec = pltpu.PrefetchScalarGridSpec(
        num_scalar_prefetch=2,
        grid=(n_rows // tm, n_f),
        in_specs=[pl.BlockSpec((tm, d), lambda i, j, te, nv: (jnp.minimum(i, nv[0] - 1), 0)),
                  pl.BlockSpec((None, d, tf), lambda i, j, te, nv: (te[i], 0, col(i, j, nv))),
                  pl.BlockSpec((None, d, tf), lambda i, j, te, nv: (te[i], 0, col(i, j, nv))),
                  pl.BlockSpec((None, tf, d), lambda i, j, te, nv: (te[i], col(i, j, nv), 0))],
        out_specs=pl.BlockSpec((tm, d), lambda i, j, te, nv: (i, 0)),
        scratch_shapes=[pltpu.VMEM((tm, d), BF16), pltpu.VMEM((tm, d), F32)],
    )
    return pl.pallas_call(
        _moe_kernel,
        grid_spec=grid_spec,
        out_shape=jax.ShapeDtypeStruct((n_rows, d), F32),
        compiler_params=_params(("parallel", "arbitrary")),
        name="moe",
    )(tile_expert, n_valid, xs, w1, w3, w2)


def route(logits, tm):
    n_tok = logits.shape[0]
    n_assign = n_tok * TOP_K
    idx = jnp.arange(N_EXPERTS, dtype=jnp.int32)[None, :]
    m1 = jnp.max(logits, axis=-1, keepdims=True)
    e1 = jnp.min(jnp.where(logits == m1, idx, N_EXPERTS), axis=-1, keepdims=True)
    rest = jnp.where(idx == e1, jnp.finfo(F32).min, logits)
    m2 = jnp.max(rest, axis=-1, keepdims=True)
    e2 = jnp.min(jnp.where(rest == m2, idx, N_EXPERTS), axis=-1, keepdims=True)
    t = jnp.exp(m2 - m1)
    gates = jnp.concatenate([1.0 / (1.0 + t), t / (1.0 + t)], axis=-1)
    flat_e = jnp.concatenate([e1, e2], axis=-1).reshape(n_assign, 1)
    onehot = (flat_e == idx).astype(F32)
    blk = MOE_ROUTE_BLOCK
    nb = n_assign // blk
    oh = onehot.reshape(nb, blk, N_EXPERTS)
    lower = lambda n: (jnp.arange(n)[:, None] > jnp.arange(n)[None, :]).astype(F32)
    within = jnp.einsum('ij,bjk->bik', lower(blk), oh)
    blk_cnt = jnp.sum(oh, axis=1)
    blk_off = jnp.dot(lower(nb), blk_cnt, precision=lax.Precision.HIGHEST)
    rank = jnp.sum((within + blk_off[:, None, :]) * oh, axis=-1).reshape(n_assign)
    counts = jnp.sum(blk_cnt, axis=0).astype(jnp.int32)
    padded = (counts + tm - 1) // tm * tm
    ends = []
    for e in range(N_EXPERTS):
        ends.append(padded[e] + (ends[-1] if ends else 0))
    pend = jnp.stack(ends)
    pstart = pend - padded
    dest = (jnp.sum(onehot * pstart.astype(F32)[None, :], axis=-1) + rank).astype(jnp.int32)
    n_tiles = n_assign // tm + N_EXPERTS
    starts = jnp.arange(n_tiles, dtype=jnp.int32) * tm
    tile_expert = jnp.minimum(jnp.sum((pend[None, :] <= starts[:, None]).astype(jnp.int32), axis=-1),
                              N_EXPERTS - 1)
    n_valid = (pend[-1] // tm).reshape(1)
    meta = jnp.concatenate([pend, padded, n_valid])
    return gates, dest, tile_expert, n_valid, meta


def _row_copies(dest_ref, src, dst, sem, rows, gather, side_work=()):
    n_groups = rows // SUBLANES
    for g in range(n_groups):
        for r in range(g * SUBLANES, (g + 1) * SUBLANES):
            for kk in range(TOP_K):
                d = dest_ref[0, TOP_K * r + kk]
                if gather:
                    copy = pltpu.make_async_copy(src.at[pl.ds(d, 1), :], dst.at[kk, pl.ds(r, 1), :], sem)
                else:
                    copy = pltpu.make_async_copy(src.at[pl.ds(r, 1), :], dst.at[pl.ds(d, 1), :], sem)
                copy.start(priority=kk % 2)
        for thunk in side_work[g * len(side_work) // n_groups:(g + 1) * len(side_work) // n_groups]:
            thunk()
    if gather:
        pltpu.make_async_copy(dst, dst, sem).wait()
    else:
        for _ in range(TOP_K):
            pltpu.make_async_copy(src, src, sem).wait()


def _dispatch_kernel(meta_ref, dest_ref, h_ref, w_ref, xs_ref, wb_ref, zero_ref, sem, zsem):
    tm_moe = zero_ref.shape[0]

    @pl.when(pl.program_id(0) == 0)
    def _():
        zero_ref[...] = jnp.zeros_like(zero_ref)
        n_tiles = xs_ref.shape[0] // tm_moe
        fills = []
        for e in range(N_EXPERTS):
            start = pl.multiple_of(jnp.maximum(meta_ref[e] - tm_moe, 0), tm_moe)
            fills.append((meta_ref[N_EXPERTS + e] > 0, start))
        for i in range(n_tiles - N_EXPERTS, n_tiles):
            fills.append((i >= meta_ref[2 * N_EXPERTS], i * tm_moe))
        for cond, start in fills:
            @pl.when(cond)
            def _():
                pltpu.make_async_copy(zero_ref, xs_ref.at[pl.ds(start, tm_moe), :], zsem).start()
        for cond, start in fills:
            @pl.when(cond)
            def _():
                pltpu.make_async_copy(zero_ref, xs_ref.at[pl.ds(start, tm_moe), :], zsem).wait()

    def cast_rows(c):
        def thunk():
            wb_ref[c * 16:(c + 1) * 16, :] = w_ref[c * 16:(c + 1) * 16, :].astype(BF16)
        return thunk

    casts = [cast_rows(c) for c in range(w_ref.shape[0] // 16)]
    _row_copies(dest_ref, h_ref, xs_ref, sem, h_ref.shape[0], gather=False, side_work=casts)


def moe_dispatch(h2, dest, meta, n_slots, w):
    n_tok, d = h2.shape
    tm = MOE_DISPATCH_TILE
    n_steps = n_tok // tm
    wf = w.reshape(-1, w.shape[-1])
    assert wf.shape[0] % (16 * n_steps) == 0
    slab = pl.BlockSpec((wf.shape[0] // n_steps, wf.shape[1]), lambda t, m: (t, 0))
    grid_spec = pltpu.PrefetchScalarGridSpec(
        num_scalar_prefetch=1,
        grid=(n_steps,),
        in_specs=[pl.BlockSpec((None, 1, TOP_K * tm), lambda t, m: (t, 0, 0), memory_space=pltpu.SMEM),
                  pl.BlockSpec((tm, d), lambda t, m: (t, 0)), slab],
        out_specs=[pl.BlockSpec(memory_space=pl.ANY), slab],
        scratch_shapes=[pltpu.VMEM((MOE_TILE_M, d), h2.dtype), pltpu.SemaphoreType.DMA,
                        pltpu.SemaphoreType.DMA],
    )
    xs, wb = pl.pallas_call(
        _dispatch_kernel,
        grid_spec=grid_spec,
        out_shape=[jax.ShapeDtypeStruct((n_slots, d), h2.dtype), jax.ShapeDtypeStruct(wf.shape, BF16)],
        compiler_params=_params(("arbitrary",)),
        name="moe_dispatch",
    )(meta, dest.reshape(n_steps, 1, TOP_K * tm), h2, wf)
    return xs, wb.reshape(w.shape)


def _combine_kernel(dest_ref, x1_ref, g_ref, mod_ref, ys_ref, o_ref, buf_ref, sem):
    tm = x1_ref.shape[0]
    _row_copies(dest_ref, ys_ref, buf_ref, sem, tm, gather=True)
    g = g_ref[...]
    moe = g[:, 0:1] * buf_ref[0] + g[:, 1:2] * buf_ref[1]
    o_ref[...] = x1_ref[...] + mod_ref[5, 0] * moe


def moe_combine(x1, gates, dest, ys, modp):
    b, s, d = x1.shape
    tm = MOE_COMBINE_TILE
    nt = s // tm
    return pl.pallas_call(
        _combine_kernel,
        grid=(b, nt),
        in_specs=[pl.BlockSpec((None, 1, TOP_K * tm), lambda i, t: (i * nt + t, 0, 0), memory_space=pltpu.SMEM),
                  pl.BlockSpec((None, tm, d), lambda i, t: (i, t, 0)),
                  pl.BlockSpec((None, tm, TOP_K), lambda i, t: (i, t, 0)),
                  pl.BlockSpec((None, N_MOD, 2, 1, d), lambda i, t: (i, 0, 0, 0, 0)),
                  pl.BlockSpec(memory_space=pl.ANY)],
        out_specs=pl.BlockSpec((None, tm, d), lambda i, t: (i, t, 0)),
        out_shape=jax.ShapeDtypeStruct((b, s, d), F32),
        scratch_shapes=[pltpu.VMEM((TOP_K, tm, d), F32), pltpu.SemaphoreType.DMA],
        compiler_params=_params(("arbitrary", "arbitrary")),
        name="moe_combine",
    )(dest.reshape(b * nt, 1, TOP_K * tm), x1, gates.reshape(b, s, TOP_K), modp, ys)


def moe_residual(x1, h2, logits, modp, w1, w3, w2):
    b, s, d = x1.shape
    n_tok = b * s
    tm = MOE_TILE_M
    gates, dest, tile_expert, n_valid, meta = route(logits.reshape(n_tok, N_EXPERTS), tm)
    n_slots = (n_tok * TOP_K // tm + N_EXPERTS) * tm
    xs, w2 = moe_dispatch(h2.reshape(n_tok, d), dest, meta, n_slots, w2)
    ys = grouped_swiglu(xs, tile_expert, n_valid, w1, w3, w2, tm, MOE_TILE_F)
    return moe_combine(x1, gates, dest, ys, modp)


def _proj1_kernel(x_ref, g_ref, mod_ref, w_ref, qan_ref, qb_ref, qbs_ref, kvan_ref, kbk_ref, kbv_ref,
                  cos_ref, sin_ref, gains_ref, ones_ref, hy_ref, q_ref, k_ref, v_ref):
    tm = x_ref.shape[0]
    row0 = pl.program_id(1) * tm
    h = _norm_mod(x_ref[...], g_ref[...], mod_ref, row0, 0, 1)
    z = jnp.dot(h.astype(BF16), w_ref[...], preferred_element_type=F32)
    hy_ref[...] = z[:, :HY_IN]

    def rms(a, g):
        return a * lax.rsqrt(jnp.mean(a * a, axis=-1, keepdims=True) + EPS) * g

    c_q, c_kv = HY_IN, HY_IN + MLA_Q_RANK
    c_r = c_kv + MLA_KV_RANK
    qa = rms(z[:, c_q:c_kv], qan_ref[...]).astype(BF16)
    kva = rms(z[:, c_kv:c_r], kvan_ref[...]).astype(BF16)
    q = jnp.dot(qa, qb_ref[...], preferred_element_type=F32)
    q_sw = jnp.dot(qa, qbs_ref[...], preferred_element_type=F32)
    kk = jnp.dot(kva, kbk_ref[...], preferred_element_type=F32)
    v_ref[...] = jnp.dot(kva, kbv_ref[...], preferred_element_type=F32).astype(BF16)
    k_rope = pltpu.roll(z[:, c_r:c_r + LANES], MLA_NOPE, 1)
    first_half = lax.broadcasted_iota(jnp.int32, (tm, LANES), 1) < MLA_NOPE + MLA_ROPE // 2
    sh = MLA_ROPE // 2
    k_rope_sw = jnp.where(first_half, pltpu.roll(k_rope, LANES - sh, 1), pltpu.roll(k_rope, sh, 1))
    cos, sin = cos_ref[...], sin_ref[...]
    qg, qg_sw, kg, kg_sw = (gains_ref[i:i + 1, :] for i in range(4))

    def head(a, a_sw, gain, gain_sw):
        inv = lax.rsqrt(_group_sums(a, ones_ref) / MLA_QK + EPS)
        return (a * (inv * gain) * cos + a_sw * (inv * gain_sw) * sin).astype(BF16)

    for hh in range(MLA_HEADS):
        sl = slice(hh * LANES, (hh + 1) * LANES)
        q_ref[:, sl] = head(q[:, sl], q_sw[:, sl], qg, qg_sw)
        k_ref[:, sl] = head(kk[:, sl] + k_rope, k_rope_sw, kg, kg_sw)


def _swap_rope_halves(a):
    lead = a.shape[:-1]
    g = a.reshape(*lead, -1, LANES)
    lo, mid = MLA_NOPE, MLA_NOPE + MLA_ROPE // 2
    g = jnp.concatenate([g[..., :lo], g[..., mid:MLA_QK], g[..., lo:mid], g[..., MLA_QK:]], axis=-1)
    return g.reshape(a.shape)


def proj1(xa, g, modp, w_in, qan, qb, kvan, kbk, kbv, cos, sin, q_gain, k_gain):
    b = xa.shape[0]
    tm = TOKEN_TILE
    tok = lambda width: pl.BlockSpec((None, tm, width), lambda i, t: (i, t, 0))
    full = lambda a: pl.BlockSpec(a.shape, lambda i, t: (0, 0))
    hw = MLA_HEADS * LANES
    qbs = _swap_rope_halves(qb)
    gains = jnp.concatenate([q_gain, _swap_rope_halves(q_gain), k_gain, _swap_rope_halves(k_gain)], axis=0)
    ones = jnp.ones((LANES, LANES), BF16)
    return pl.pallas_call(
        _proj1_kernel,
        grid=(b, S_ALL // tm),
        in_specs=[tok(D_MODEL), full(g),
                  pl.BlockSpec((None, N_MOD, 2, 1, D_MODEL), lambda i, t: (i, 0, 0, 0, 0)),
                  full(w_in), full(qan), full(qb), full(qbs), full(kvan), full(kbk), full(kbv),
                  pl.BlockSpec((tm, LANES), lambda i, t: (t, 0)),
                  pl.BlockSpec((tm, LANES), lambda i, t: (t, 0)),
                  full(gains), full(ones)],
        out_specs=[tok(HY_IN), tok(hw), tok(hw), tok(MLA_HEADS * MLA_V)],
        out_shape=[jax.ShapeDtypeStruct((b, S_ALL, HY_IN), F32),
                   jax.ShapeDtypeStruct((b, S_ALL, hw), BF16),
                   jax.ShapeDtypeStruct((b, S_ALL, hw), BF16),
                   jax.ShapeDtypeStruct((b, S_ALL, MLA_HEADS * MLA_V), BF16)],
        compiler_params=_params(("parallel", "parallel")),
        name="proj1",
    )(xa, g, modp, w_in, qan, qb, qbs, kvan, kbk, kbv, cos, sin, gains, ones)


def _merge1_kernel(hy_ref, att_ref, x_ref, mod_ref, g2_ref, w_hy_ref, w_att_ref, r_ref,
                   x1_ref, h2_ref, lg_ref):
    y = (jnp.dot(hy_ref[...].astype(BF16), w_hy_ref[...], preferred_element_type=F32)
         + jnp.dot(att_ref[...], w_att_ref[...], preferred_element_type=F32))
    x1 = x_ref[...] + mod_ref[2, 0] * y
    x1_ref[...] = x1
    h2 = _norm_mod(x1, g2_ref[...], mod_ref, 0, 3, 4)
    h2_ref[...] = h2
    lane = lax.broadcasted_iota(jnp.int32, lg_ref.shape, 1)
    logits = jnp.zeros(lg_ref.shape, F32)
    for e in range(N_EXPERTS):
        col = jnp.sum(h2 * r_ref[e:e + 1, :], axis=-1, keepdims=True)
        logits = jnp.where(lane == e, col, logits)
    lg_ref[...] = logits


def merge1(hy, att, xa, modp, g2, w_hy, w_att, router):
    b = xa.shape[0]
    tm = MERGE1_TILE
    tok = lambda width: pl.BlockSpec((None, tm, width), lambda i, t: (i, t, 0))
    full = lambda a: pl.BlockSpec(a.shape, lambda i, t: (0, 0))
    return pl.pallas_call(
        _merge1_kernel,
        grid=(b, SEQ // tm),
        in_specs=[tok(HY_WIDTH), tok(HALF_MIX), tok(D_MODEL),
                  pl.BlockSpec((None, N_MOD, 2, 1, D_MODEL), lambda i, t: (i, 0, 0, 0, 0)),
                  full(g2), full(w_hy), full(w_att), full(router)],
        out_specs=[tok(D_MODEL), tok(D_MODEL), tok(N_EXPERTS)],
        out_shape=[jax.ShapeDtypeStruct((b, SEQ, D_MODEL), F32),
                   jax.ShapeDtypeStruct((b, SEQ, D_MODEL), F32),
                   jax.ShapeDtypeStruct((b, SEQ, N_EXPERTS), F32)],
        compiler_params=_params(("parallel", "parallel")),
        name="merge1",
    )(hy, att, xa, modp, g2, w_hy, w_att, router)


HY_N = 2 * SEQ
HY_TW_ROWS = 256
HY_SPEC_FBLK = 256
HY_CONV_FBLK = 512
HY_LANE_TILES = SEQ // LANES


def _twiddle_kernel(ca_ref, sa_ref, cb_ref, sb_ref, fre_ref, fim_ref, ic_ref, is_ref):
    rows = fre_ref.shape[0]
    r = pl.program_id(0) * rows + lax.broadcasted_iota(jnp.int32, (rows, LANES), 0)
    alt_r = (1 - 2 * (r & 1)).astype(F32)
    cb, sb = cb_ref[...], sb_ref[...]
    for a in range(HY_LANE_TILES):
        ca, sa = ca_ref[:, a:a + 1], sa_ref[:, a:a + 1]
        c = ca * cb - sa * sb
        s = sa * cb + ca * sb
        col = a * LANES + lax.broadcasted_iota(jnp.int32, (rows, LANES), 1)
        alt_c = (1 - 2 * (col & 1)).astype(F32)
        w = jnp.where(col == 0, 1.0 / HY_N, 2.0 / HY_N)
        sl = slice(a * LANES, (a + 1) * LANES)
        fre_ref[:, sl] = c.astype(BF16)
        fim_ref[:, sl] = jnp.where(r == 0, alt_c, -s).astype(BF16)
        ic_ref[:, sl] = (c * w).astype(BF16)
        is_ref[:, sl] = jnp.where(col == 0, alt_r / HY_N, -s * w).astype(BF16)


def dft_matrices():
    idx = jnp.arange(SEQ, dtype=jnp.int32)[:, None]
    step = 2.0 * math.pi / HY_N
    ph_a = ((idx * (LANES * jnp.arange(HY_LANE_TILES, dtype=jnp.int32))[None, :]) % HY_N).astype(F32) * step
    ph_b = ((idx * jnp.arange(LANES, dtype=jnp.int32)[None, :]) % HY_N).astype(F32) * step
    rows = HY_TW_ROWS
    tab = lambda width: pl.BlockSpec((rows, width), lambda i: (i, 0))
    out = jax.ShapeDtypeStruct((SEQ, SEQ), BF16)
    return pl.pallas_call(
        _twiddle_kernel,
        grid=(SEQ // rows,),
        in_specs=[tab(HY_LANE_TILES), tab(HY_LANE_TILES), tab(LANES), tab(LANES)],
        out_specs=[tab(SEQ)] * 4,
        out_shape=[out] * 4,
        compiler_params=_params(("parallel",)),
        name="twiddle",
    )(jnp.cos(ph_a), jnp.sin(ph_a), jnp.cos(ph_b), jnp.sin(ph_b))


def _spec_kernel(fre_ref, fim_ref, h_ref, k_ref):
    w = HY_WIDTH
    h = h_ref[...].astype(BF16)
    re = jnp.dot(fre_ref[...], h, preferred_element_type=F32)
    im = jnp.dot(fim_ref[...], h, preferred_element_type=F32)
    hb0 = h[0:1, w:].astype(F32)
    first = (pl.program_id(1) * re.shape[0] + lax.broadcasted_iota(jnp.int32, (re.shape[0], 1), 0)) == 0
    k_ref[0] = re[:, :w] + re[:, w:] - hb0
    k_ref[1] = jnp.where(first, im[:, :w] + im[:, w:] - hb0, im[:, :w] - im[:, w:])


def filter_spectra(filt, fre, fim):
    fb = HY_SPEC_FBLK
    return pl.pallas_call(
        _spec_kernel,
        grid=(HY_ORDER, SEQ // fb),
        in_specs=[pl.BlockSpec((fb, SEQ), lambda o, f: (f, 0)),
                  pl.BlockSpec((fb, SEQ), lambda o, f: (f, 0)),
                  pl.BlockSpec((SEQ, 2 * HY_WIDTH), lambda o, f: (0, o))],
        out_specs=pl.BlockSpec((None, 2, fb, HY_WIDTH), lambda o, f: (o, 0, f, 0)),
        out_shape=jax.ShapeDtypeStruct((HY_ORDER, 2, SEQ, HY_WIDTH), F32),
        compiler_params=_params(("parallel", "parallel")),
        name="filter_spectra",
    )(fre, fim, filt)


def _short_conv(z_ref, w_ref, b_ref, part):
    z = z_ref[...]
    l = z.shape[0]
    t = lax.broadcasted_iota(jnp.int32, (l, 1), 0)
    prev = jnp.where(t == 0, 0.0, pltpu.roll(z, 1, 0))
    nxt = jnp.where(t == l - 1, 0.0, pltpu.roll(z, l - 1, 0))
    cs = slice(part * HY_WIDTH, (part + 1) * HY_WIDTH)
    return (w_ref[0:1, cs] * prev + w_ref[1:2, cs] * z + w_ref[2:3, cs] * nxt) + b_ref[:, cs]


def _hyconv_kernel(u_ref, xg_ref, cw_ref, cb_ref, fre_ref, fim_ref, ic_ref, is_ref, k_ref, skip_ref,
                   y_ref, ub_ref, acc_ref, *, order):
    f = pl.program_id(1)

    def u_f32():
        return _short_conv(u_ref, cw_ref, cb_ref, 0) if order == 0 else u_ref[...]

    @pl.when(f == 0)
    def _():
        ub_ref[...] = u_f32().astype(BF16)

    x_re = jnp.dot(fre_ref[...], ub_ref[...], preferred_element_type=F32)
    x_im = jnp.dot(fim_ref[...], ub_ref[...], preferred_element_type=F32)
    k_re, k_im = k_ref[0], k_ref[1]
    first = (f * x_re.shape[0] + lax.broadcasted_iota(jnp.int32, (x_re.shape[0], 1), 0)) == 0
    y_re = x_re * k_re - jnp.where(first, 0.0, x_im * k_im)
    y_im = x_im * jnp.where(first, k_im, k_re) + jnp.where(first, 0.0, x_re * k_im)
    part = (jnp.dot(ic_ref[...], y_re.astype(BF16), preferred_element_type=F32)
            + jnp.dot(is_ref[...], y_im.astype(BF16), preferred_element_type=F32))
    _accumulate(acc_ref, part, f)

    @pl.when(f == pl.num_programs(1) - 1)
    def _():
        xg = _short_conv(xg_ref, cw_ref, cb_ref, order + 1)
        y_ref[...] = xg * (acc_ref[...] + skip_ref[order:order + 1, :] * u_f32())


def hyena_conv(order, u, hyz, conv_w, conv_b, fre, fim, ic, is_, spectra, skip):
    b = hyz.shape[0]
    fb = HY_CONV_FBLK
    w = HY_WIDTH
    lat = lambda part: pl.BlockSpec((None, SEQ, w), lambda i, f: (i, 0, part))
    full = lambda a: pl.BlockSpec(a.shape, lambda i, f: (0, 0))
    return pl.pallas_call(
        functools.partial(_hyconv_kernel, order=order),
        grid=(b, SEQ // fb),
        in_specs=[lat(0), lat(order + 1), full(conv_w), full(conv_b),
                  pl.BlockSpec((fb, SEQ), lambda i, f: (f, 0)),
                  pl.BlockSpec((fb, SEQ), lambda i, f: (f, 0)),
                  pl.BlockSpec((SEQ, fb), lambda i, f: (0, f)),
                  pl.BlockSpec((SEQ, fb), lambda i, f: (0, f)),
                  pl.BlockSpec((None, 2, fb, w), lambda i, f: (order, 0, f, 0)),
                  full(skip)],
        out_specs=pl.BlockSpec((None, SEQ, w), lambda i, f: (i, 0, 0)),
        out_shape=jax.ShapeDtypeStruct((b, SEQ, w), F32),
        scratch_shapes=[pltpu.VMEM((SEQ, w), BF16), pltpu.VMEM((SEQ, w), F32)],
        compiler_params=_params(("parallel", "arbitrary")),
        name="hyena_conv",
    )(u, hyz, conv_w, conv_b, fre, fim, ic, is_, spectra, skip)


def hyena_filters(length, w1, b1, w2, b2, w3, freq):
    hp = lax.Precision.HIGHEST
    t = jnp.arange(length, dtype=F32)[:, None]
    t_norm = t / max(length - 1, 1)
    bands = jnp.linspace(1e-4, HY_BANDS - 1, HY_BANDS, dtype=F32)
    ang = 2.0 * math.pi * t * bands / length
    z = jnp.concatenate([t_norm, jnp.cos(ang), -jnp.sin(ang)], axis=-1)
    h = jnp.sin(freq * (jnp.dot(z, w1, precision=hp) + b1))
    h = jnp.sin(freq * (jnp.dot(h, w2, precision=hp) + b2))
    h = jnp.dot(h, w3, precision=hp)
    deltas = jnp.abs(jnp.linspace(HY_MIN_DECAY, HY_MAX_DECAY, HY_WIDTH, dtype=F32))
    window = jnp.exp(-t_norm * deltas) + HY_SHIFT
    return h * jnp.tile(window, (1, HY_ORDER * 2))


def hyena(hyz, conv_w, conv_b, fw1, fb1, fw2, fb2, fw3, freq, skip):
    fre, fim, ic, is_ = dft_matrices()
    filt = hyena_filters(SEQ, fw1, fb1, fw2, fb2, fw3, freq)
    spectra = filter_spectra(filt, fre, fim)
    y = hyz
    for o in range(HY_ORDER):
        y = hyena_conv(o, y, hyz, conv_w, conv_b.reshape(1, -1), fre, fim, ic, is_, spectra, skip)
    return y


def _grid_angles(rot_dim):
    n_freq = rot_dim // 4
    inv_freq = ROPE_THETA ** (-jnp.arange(n_freq, dtype=F32) / n_freq)
    t = jnp.arange(SEQ)
    r = (t // GRID_W).astype(F32)
    c_ = (t % GRID_W).astype(F32)
    return jnp.concatenate([r[:, None] * inv_freq, c_[:, None] * inv_freq], axis=-1)


def _rope_tables(rot_dim, lane_offsets):
    ang = _grid_angles(rot_dim)
    c, s = jnp.cos(ang), jnp.sin(ang)
    cos_parts, sin_parts, lane = [], [], 0
    for off in lane_offsets:
        cos_parts += [jnp.ones((SEQ, off - lane), F32), c, c]
        sin_parts += [jnp.zeros((SEQ, off - lane), F32), -s, s]
        lane = off + rot_dim
    cos_parts.append(jnp.ones((SEQ, LANES - lane), F32))
    sin_parts.append(jnp.zeros((SEQ, LANES - lane), F32))
    cos = jnp.concatenate([jnp.concatenate(cos_parts, axis=1), jnp.ones((CTX_LEN, LANES), F32)], axis=0)
    sin = jnp.concatenate([jnp.concatenate(sin_parts, axis=1), jnp.zeros((CTX_LEN, LANES), F32)], axis=0)
    return cos, sin


def _block_diag(w):
    nb, bs, _ = w.shape
    eye = jnp.eye(nb, dtype=w.dtype)
    return (eye[:, None, :, None] * w[:, :, None, :]).reshape(nb * bs, nb * bs)


def _gqa_pair_order():
    g = GQA_Q_HEADS // GQA_KV_HEADS
    heads = []
    for p in range(g):
        heads += [p, g + p]
    return np.concatenate([np.arange(h * HEAD_DIM, (h + 1) * HEAD_DIM) for h in heads])


def _pad_heads(w, n_heads, width):
    k = w.shape[0]
    w = w.reshape(k, n_heads, width)
    return jnp.pad(w, ((0, 0), (0, 0), (0, LANES - width))).reshape(k, n_heads * LANES)


def kernel(x, c, ctx, c_ctx, mod_w, mod_b, norm1_g, norm2_g, ab_w_in, ab_w_out, lru_conv_w, lru_conv_b, lru_w_a, lru_b_a, lru_w_x, lru_b_x, lru_lambda, gqa_q_norm, gqa_k_norm, ffn_w1, ffn_w3, ffn_w2, cd_w_in, cd_w_out, hy_conv_w, hy_conv_b, hy_filt_w1, hy_filt_b1, hy_filt_w2, hy_filt_b2, hy_filt_w3, hy_sin_freq, hy_skip, mla_q_a_norm, mla_q_b, mla_kv_a_norm, mla_kv_b, mla_q_norm, mla_k_norm, moe_router, moe_w1, moe_w3, moe_w2):
    batch = x.shape[0]
    bf = lambda w: w.astype(BF16)
    row = lambda v: v.reshape(1, -1)

    silu_all = jnp.concatenate([jax.nn.silu(c), jax.nn.silu(c_ctx)[None, :],
                                jnp.zeros((16 - batch - 1, D_MODEL), F32)], axis=0)
    mods = []
    mod_all = stacked_matmul(silu_all, mod_w, 1536) + mod_b[:, None, :]
    for layer in range(DEPTH):
        m = mod_all[layer]
        lat = m[:batch].reshape(batch, N_MOD, 1, D_MODEL)
        cx = jnp.broadcast_to(m[batch].reshape(1, N_MOD, 1, D_MODEL), lat.shape)
        mods.append(jnp.stack([lat, cx], axis=2))

    xa = jnp.concatenate([x, ctx], axis=1)

    perm = _gqa_pair_order()
    q0 = 2 * LRU_WIDTH
    w_in0 = ab_w_in[0]
    w_in0 = jnp.concatenate([w_in0[:, :q0], w_in0[:, q0 + perm], w_in0[:, q0 + GQA_Q_HEADS * HEAD_DIM:]], axis=1)
    cos_g, sin_g = _rope_tables(HEAD_DIM, (0, HEAD_DIM))
    q_gain = row(jnp.tile(gqa_q_norm[0], 2) * HEAD_DIM ** -0.5)
    k_gain = row(jnp.tile(gqa_k_norm[0], 2))
    xr, gate, q, k, v = proj0(xa, row(norm1_g[0]), mods[0], bf(w_in0), cos_g, sin_g, q_gain, k_gain)

    w_gates = jnp.concatenate([_block_diag(lru_w_a[0, 0]), _block_diag(lru_w_x[0, 0]),
                               _block_diag(lru_w_a[0, 1]), _block_diag(lru_w_x[0, 1])], axis=1)
    b_gates = jnp.concatenate([lru_b_a[0, 0].reshape(-1), lru_b_x[0, 0].reshape(-1),
                               lru_b_a[0, 1].reshape(-1), lru_b_x[0, 1].reshape(-1)])
    rec = rglru(xr, lru_conv_w[0], row(lru_conv_b[0]), bf(w_gates), row(b_gates), lru_lambda[0])

    att_l, moe_w1b, moe_w3b = attention_latent(
        q, k, v, GQA_HEADS, GQA_TILE_Q,
        cast=(moe_w1[0].reshape(-1, moe_w1.shape[-1]), moe_w3[0].reshape(-1, moe_w3.shape[-1])))
    att_c = attention_context(q, k, v, GQA_HEADS)
    w_out0 = ab_w_out[0]
    x1 = merge0(rec, gate, att_l, att_c, xa, mods[0], bf(w_out0[:LRU_WIDTH]), bf(w_out0[LRU_WIDTH:][perm]))
    xa = ffn_residual(x1, mods[0], row(norm2_g[0]), bf(ffn_w1[0]), bf(ffn_w3[0]), bf(ffn_w2[0]))

    w_in1 = jnp.pad(cd_w_in[0], ((0, 0), (0, LANES - MLA_ROPE)))
    cos_m, sin_m = _rope_tables(MLA_ROPE, (MLA_NOPE,))
    qb = _pad_heads(mla_q_b[0], MLA_HEADS, MLA_QK)
    kvb = mla_kv_b[0].reshape(MLA_KV_RANK, MLA_HEADS, MLA_NOPE + MLA_V)
    kbk = _pad_heads(kvb[:, :, :MLA_NOPE].reshape(MLA_KV_RANK, -1), MLA_HEADS, MLA_NOPE)
    kbv = kvb[:, :, MLA_NOPE:].reshape(MLA_KV_RANK, -1)
    pad_gain = lambda g_: row(jnp.pad(g_, (0, LANES - MLA_QK)))
    hyz, q, k, v = proj1(xa, row(norm1_g[1]), mods[1], bf(w_in1), row(mla_q_a_norm[0]), bf(qb),
                         row(mla_kv_a_norm[0]), bf(kbk), bf(kbv), cos_m, sin_m,
                         pad_gain(mla_q_norm[0] * MLA_QK ** -0.5), pad_gain(mla_k_norm[0]))
    att = attention_latent(q, k, v, MLA_HEAD_GROUPS, MLA_TILE_Q)
    hy = hyena(hyz, hy_conv_w[0], hy_conv_b[0], hy_filt_w1[0], hy_filt_b1[0], hy_filt_w2[0],
               hy_filt_b2[0], hy_filt_w3[0], hy_sin_freq[0], hy_skip[0])
    w_out1 = cd_w_out[0]
    x1, h2, logits = merge1(hy, att, xa, mods[1], row(norm2_g[1]),
                            bf(w_out1[:HY_WIDTH]), bf(w_out1[HY_WIDTH:]), moe_router[0].T)
    return moe_residual(x1, h2, logits, mods[1], moe_w1b.reshape(moe_w1.shape[1:]),
                        moe_w3b.reshape(moe_w3.shape[1:]), moe_w2[0])
```

```python
import functools
import math

import jax
import jax.numpy as jnp
import numpy as np
from jax import lax
from jax.experimental import pallas as pl
from jax.experimental.pallas import tpu as pltpu

F32 = jnp.float32
BF16 = jnp.bfloat16

D_MODEL = 1024
SEQ = 2048
CTX_LEN = 256
S_ALL = SEQ + CTX_LEN
DEPTH = 2
GRID_W = 64
HEAD_DIM = 64
HALF_MIX = D_MODEL // 2
ROPE_THETA = 10000.0
EPS = 1e-6
N_MOD = 6
LRU_WIDTH = HALF_MIX
LRU_BLOCKS = LRU_WIDTH // HEAD_DIM
LRU_CONV = 4
LRU_C = 8.0
GQA_Q_HEADS = HALF_MIX // HEAD_DIM
GQA_KV_HEADS = 2
HY_WIDTH = HALF_MIX
HY_ORDER = 2
HY_CONV = 3
HY_BANDS = 16
HY_TARGET = 1e-2
HY_FAST_PCT = 0.3
HY_SLOW_PCT = 1.5
HY_MIN_DECAY = math.log(HY_TARGET) / HY_SLOW_PCT
HY_MAX_DECAY = math.log(HY_TARGET) / HY_FAST_PCT
HY_SHIFT = 0.05
HY_IN = (HY_ORDER + 1) * HY_WIDTH
MLA_HEADS = HALF_MIX // HEAD_DIM
MLA_Q_RANK = D_MODEL // 4
MLA_KV_RANK = D_MODEL // 8
MLA_NOPE = HEAD_DIM
MLA_ROPE = HEAD_DIM // 2
MLA_V = HEAD_DIM
MLA_QK = MLA_NOPE + MLA_ROPE
N_EXPERTS = 8
TOP_K = 2

LANES = 128
SUBLANES = 8
V7X_VMEM_LIMIT_BYTES = 56 * 1024 * 1024

TOKEN_TILE = 768
FFN_TILE_M = 768
FFN_TILE_F = 256
GQA_TILE_Q = 512
MLA_TILE_Q = 512
MERGE1_TILE = 512
MOE_TILE_M = 512
MOE_TILE_F = 1792
MOE_ROUTE_BLOCK = 256
MOE_DISPATCH_TILE = 256
MOE_COMBINE_TILE = 256

LRU_GAP = SUBLANES
LRU_LAT0 = CTX_LEN + LRU_GAP
LRU_CHUNK = 464
LRU_ROWS = 2320
LRU_PAD_FRONT = SUBLANES

assert S_ALL % TOKEN_TILE == 0 and S_ALL % FFN_TILE_M == 0
assert SEQ % GQA_TILE_Q == 0 and SEQ % MLA_TILE_Q == 0
assert LRU_ROWS % LRU_CHUNK == 0 and LRU_ROWS >= LRU_LAT0 + SEQ + SUBLANES


def _params(sem):
    return pltpu.CompilerParams(dimension_semantics=sem, vmem_limit_bytes=V7X_VMEM_LIMIT_BYTES)


def _norm_mod(x, g, mod_pair, row0, shift_idx, scale_idx):
    y = x * lax.rsqrt(jnp.mean(x * x, axis=-1, keepdims=True) + EPS) * g
    shift = _mod_rows(mod_pair, shift_idx, row0, x.shape[0])
    scale = _mod_rows(mod_pair, scale_idx, row0, x.shape[0])
    return y * (1.0 + scale) + shift


def _mod_rows(mod_pair, idx, row0, rows):
    row = row0 + lax.broadcasted_iota(jnp.int32, (rows, 1), 0)
    return jnp.where(row >= SEQ, mod_pair[idx, 1], mod_pair[idx, 0])


def _mm_kernel(x_ref, w_ref, o_ref):
    o_ref[...] = jnp.dot(x_ref[...].astype(BF16), w_ref[...].astype(BF16),
                         preferred_element_type=F32)


def stacked_matmul(x, w, tn):
    m, k = x.shape
    n_l, _, n = w.shape
    assert n % tn == 0
    return pl.pallas_call(
        _mm_kernel,
        grid=(n_l, n // tn),
        in_specs=[pl.BlockSpec((m, k), lambda l, j: (0, 0)),
                  pl.BlockSpec((None, k, tn), lambda l, j: (l, 0, j))],
        out_specs=pl.BlockSpec((None, m, tn), lambda l, j: (l, 0, j)),
        out_shape=jax.ShapeDtypeStruct((n_l, m, n), F32),
        compiler_params=_params(("parallel", "parallel")),
        name="matmul",
    )(x, w)


def _rope_pair(a, cos, sin, first_half, shift):
    rot = jnp.where(first_half, pltpu.roll(a, LANES - shift, 1), pltpu.roll(a, shift, 1))
    return a * cos + rot * sin


def _group_sums(a, ones_ref):
    return jnp.dot((a * a).astype(BF16), ones_ref[...], preferred_element_type=F32)


def _proj0_kernel(x_ref, xt_ref, ctx_ref, g_ref, mod_ref, w_ref, cos_ref, sin_ref, qg_ref, kg_ref, ones_ref,
                  xr_ref, gate_ref, q_ref, k_ref, v_ref):
    tm = x_ref.shape[0]
    t = pl.program_id(1)
    row0 = t * tm
    last = jnp.concatenate([xt_ref[...], ctx_ref[...]], axis=0)
    x = jnp.where(t < SEQ // tm, x_ref[...], last)
    h = _norm_mod(x, g_ref[...], mod_ref, row0, 0, 1)
    z = jnp.dot(h.astype(BF16), w_ref[...], preferred_element_type=F32)
    xr_ref[...] = z[:, :LRU_WIDTH]
    gate_ref[...] = z[:, LRU_WIDTH:2 * LRU_WIDTH].astype(BF16)
    cos, sin = cos_ref[...], sin_ref[...]
    first_half = (lax.broadcasted_iota(jnp.int32, (tm, LANES), 1) & (HEAD_DIM // 2)) == 0

    def head_pair(a, gain):
        inv = lax.rsqrt(_group_sums(a, ones_ref) / HEAD_DIM + EPS)
        return _rope_pair(a * inv * gain, cos, sin, first_half, HEAD_DIM // 2).astype(BF16)

    q0 = 2 * LRU_WIDTH
    for p in range(GQA_Q_HEADS // 2):
        q_ref[:, p * LANES:(p + 1) * LANES] = head_pair(z[:, q0 + p * LANES:q0 + (p + 1) * LANES], qg_ref[...])
    k0 = q0 + GQA_Q_HEADS * HEAD_DIM
    k_ref[...] = head_pair(z[:, k0:k0 + LANES], kg_ref[...])
    v_ref[...] = z[:, k0 + LANES:k0 + 2 * LANES].astype(BF16)


def proj0(x, ctx, g, modp, w_in, cos, sin, q_gain, k_gain):
    b = x.shape[0]
    tm = TOKEN_TILE
    n = w_in.shape[1]
    n_full = SEQ // tm
    tail = SEQ - n_full * tm
    assert tail + CTX_LEN == tm and SEQ % tail == 0
    tok = lambda width: pl.BlockSpec((None, tm, width), lambda i, t: (i, t, 0))
    full = lambda r, c_: pl.BlockSpec((r, c_), lambda i, t: (0, 0))
    half = jnp.arange(LANES) // HEAD_DIM
    ones_pair = (half[:, None] == half[None, :]).astype(BF16)
    return pl.pallas_call(
        _proj0_kernel,
        grid=(b, S_ALL // tm),
        in_specs=[pl.BlockSpec((None, tm, D_MODEL), lambda i, t: (i, jnp.minimum(t, n_full - 1), 0)),
                  pl.BlockSpec((None, tail, D_MODEL), lambda i, t: (i, SEQ // tail - 1, 0)),
                  pl.BlockSpec((None, CTX_LEN, D_MODEL), lambda i, t: (i, 0, 0)),
                  full(1, D_MODEL),
                  pl.BlockSpec((None, N_MOD, 2, 1, D_MODEL), lambda i, t: (i, 0, 0, 0, 0)),
                  full(D_MODEL, n),
                  pl.BlockSpec((tm, LANES), lambda i, t: (t, 0)),
                  pl.BlockSpec((tm, LANES), lambda i, t: (t, 0)),
                  full(1, LANES), full(1, LANES), full(LANES, LANES)],
        out_specs=[tok(LRU_WIDTH), tok(LRU_WIDTH), tok(GQA_Q_HEADS * HEAD_DIM), tok(LANES), tok(LANES)],
        out_shape=[jax.ShapeDtypeStruct((b, S_ALL, LRU_WIDTH), F32),
                   jax.ShapeDtypeStruct((b, S_ALL, LRU_WIDTH), BF16),
                   jax.ShapeDtypeStruct((b, S_ALL, GQA_Q_HEADS * HEAD_DIM), BF16),
                   jax.ShapeDtypeStruct((b, S_ALL, LANES), BF16),
                   jax.ShapeDtypeStruct((b, S_ALL, LANES), BF16)],
        compiler_params=_params(("parallel", "parallel")),
        name="proj0",
    )(x, x, ctx, g, modp, w_in, cos, sin, q_gain, k_gain, ones_pair)


def _lru_kernel(xr_ref, cw_ref, cb_ref, wg_ref, bg_ref, lam_ref, rec_ref,
                pad_ref, af_ref, hf_ref, ab_ref, hb_ref):
    w = LRU_WIDTH
    pad_ref[...] = jnp.zeros_like(pad_ref)
    pad_ref[LRU_PAD_FRONT:LRU_PAD_FRONT + CTX_LEN, :] = xr_ref[SEQ:S_ALL, :]
    pad_ref[LRU_PAD_FRONT + LRU_LAT0:LRU_PAD_FRONT + LRU_LAT0 + SEQ, :] = xr_ref[0:SEQ, :]
    lam = lam_ref[...]
    neg_c_softplus = -LRU_C * (jnp.maximum(-lam, 0.0) + jnp.log(1.0 + jnp.exp(-jnp.abs(lam))))
    left = LRU_CONV // 2
    for ch in range(LRU_ROWS // LRU_CHUNK):
        r0 = ch * LRU_CHUNK
        u = cb_ref[...]
        for j in range(LRU_CONV):
            start = LRU_PAD_FRONT + r0 - left + j
            u = u + cw_ref[j:j + 1, :] * pad_ref[start:start + LRU_CHUNK, :]
        gz = jnp.dot(u.astype(BF16), wg_ref[...], preferred_element_type=F32) + bg_ref[...]
        for d, (a_ref, h_ref) in enumerate(((af_ref, hf_ref), (ab_ref, hb_ref))):
            r = jax.nn.sigmoid(gz[:, 2 * d * w:(2 * d + 1) * w])
            i = jax.nn.sigmoid(gz[:, (2 * d + 1) * w:(2 * d + 2) * w])
            a = jnp.exp(neg_c_softplus[d:d + 1, :] * r)
            a_ref[r0:r0 + LRU_CHUNK, :] = a
            h_ref[r0:r0 + LRU_CHUNK, :] = jnp.sqrt(1.0 - a * a) * (i * u)

    n_ctx_groups = CTX_LEN // SUBLANES
    sub = lax.broadcasted_iota(jnp.int32, (SUBLANES, w), 0)

    def tile_scan(a, b, descending):
        for s in (1, 2, 4):
            if descending:
                ok, shift = sub < SUBLANES - s, SUBLANES - s
            else:
                ok, shift = sub >= s, s
            a_s = jnp.where(ok, pltpu.roll(a, shift, 0), 1.0)
            b_s = jnp.where(ok, pltpu.roll(b, shift, 0), 0.0)
            a, b = a * a_s, a * b_s + b
        return a, b

    def group(tt, carry):
        hf, hb = carry
        is_lat = tt >= n_ctx_groups
        rf = pl.ds(pl.multiple_of(tt * SUBLANES + jnp.where(is_lat, LRU_GAP, 0), SUBLANES), SUBLANES)
        rb = pl.ds(pl.multiple_of(
            jnp.where(is_lat, LRU_LAT0 + SEQ + CTX_LEN, CTX_LEN) - (tt + 1) * SUBLANES, SUBLANES), SUBLANES)
        a, b = tile_scan(af_ref[rf, :], hf_ref[rf, :], False)
        h = a * hf + b
        hf_ref[rf, :] = h
        hf = jnp.broadcast_to(h[SUBLANES - 1:SUBLANES, :], (SUBLANES, w))
        a, b = tile_scan(ab_ref[rb, :], hb_ref[rb, :], True)
        h = a * hb + b
        hb_ref[rb, :] = h
        hb = jnp.broadcast_to(h[0:1, :], (SUBLANES, w))
        return hf, hb

    zero = jnp.zeros((SUBLANES, w), F32)
    lax.fori_loop(0, S_ALL // SUBLANES, group, (zero, zero))
    dt = rec_ref.dtype
    rec_ref[0:SEQ, :] = (hf_ref[LRU_LAT0:LRU_LAT0 + SEQ, :] + hb_ref[LRU_LAT0:LRU_LAT0 + SEQ, :]).astype(dt)
    rec_ref[SEQ:S_ALL, :] = (hf_ref[0:CTX_LEN, :] + hb_ref[0:CTX_LEN, :]).astype(dt)


def rglru(xr, conv_w, conv_b, w_gates, b_gates, lam):
    b = xr.shape[0]
    w = LRU_WIDTH
    full = lambda r, c_: pl.BlockSpec((r, c_), lambda i: (0, 0))
    rows = pltpu.VMEM((LRU_ROWS, w), F32)
    return pl.pallas_call(
        _lru_kernel,
        grid=(b,),
        in_specs=[pl.BlockSpec((None, S_ALL, w), lambda i: (i, 0, 0)),
                  full(LRU_CONV, w), full(1, w), full(w, 4 * w), full(1, 4 * w), full(2, w)],
        out_specs=pl.BlockSpec((None, S_ALL, w), lambda i: (i, 0, 0)),
        out_shape=jax.ShapeDtypeStruct((b, S_ALL, w), BF16),
        scratch_shapes=[pltpu.VMEM((LRU_ROWS + 2 * LRU_PAD_FRONT, w), F32), rows, rows, rows, rows],
        compiler_params=_params(("parallel",)),
        name="rglru",
    )(xr, conv_w, conv_b, w_gates, b_gates, lam)


def _attn_kernel(q_ref, k_ref, v_ref, *rest, heads):
    n_cast = (len(rest) - 1) // 2
    o_ref = rest[n_cast]
    cast_pairs = list(zip(rest[:n_cast], rest[n_cast + 1:]))
    tq, sk = q_ref.shape[0], k_ref.shape[0]
    q_lo = lax.broadcasted_iota(jnp.int32, (tq, LANES), 1) < HEAD_DIM
    v_lo = lax.broadcasted_iota(jnp.int32, (sk, LANES), 1) < HEAD_DIM

    def keep(x, is_lo, half):
        if half is None:
            return x
        return jnp.where(is_lo if half == 0 else jnp.logical_not(is_lo), x, jnp.zeros_like(x))

    for og, members in enumerate(heads):
        acc = None
        for qg, qh, kg, vg, vh in members:
            q = keep(q_ref[:, qg * LANES:(qg + 1) * LANES], q_lo, qh)
            s = lax.dot_general(q, k_ref[:, kg * LANES:(kg + 1) * LANES],
                                (((1,), (1,)), ((), ())), preferred_element_type=F32)
            p = jnp.exp(s - jnp.max(s, axis=-1, keepdims=True))
            l = jnp.sum(p, axis=-1, keepdims=True)
            v = keep(v_ref[:, vg * LANES:(vg + 1) * LANES], v_lo, vh)
            o = jnp.dot(p.astype(BF16), v, preferred_element_type=F32) / l
            acc = o if acc is None else acc + o
        o_ref[:, og * LANES:(og + 1) * LANES] = acc.astype(o_ref.dtype)
        for src, dst in cast_pairs:
            r0, r1 = (og * src.shape[0] // len(heads), (og + 1) * src.shape[0] // len(heads))
            dst[r0:r1, :] = src[r0:r1, :].astype(BF16)


def attention(q, k, v, heads, q_block, n_q_blocks, tq, k_rows, cast=()):
    b, s, qw = q.shape
    kw, vw = k.shape[-1], v.shape[-1]
    ow = len(heads) * LANES
    n_steps = b * n_q_blocks
    kv_spec = lambda width: pl.BlockSpec((None, k_rows, width), lambda i, t: (i, s // k_rows - 1, 0))
    slab = lambda w: pl.BlockSpec((w.shape[0] // n_steps, w.shape[1]), lambda i, t: (i * n_q_blocks + t, 0))
    assert all(w.shape[0] % (16 * len(heads) * n_steps) == 0 for w in cast)
    out = pl.pallas_call(
        functools.partial(_attn_kernel, heads=heads),
        grid=(b, n_q_blocks),
        in_specs=[pl.BlockSpec((None, tq, qw), lambda i, t: (i, q_block + t, 0)), kv_spec(kw), kv_spec(vw)]
        + [slab(w) for w in cast],
        out_specs=[pl.BlockSpec((None, tq, ow), lambda i, t: (i, t, 0))] + [slab(w) for w in cast],
        out_shape=[jax.ShapeDtypeStruct((b, n_q_blocks * tq, ow), BF16)]
        + [jax.ShapeDtypeStruct(w.shape, BF16) for w in cast],
        compiler_params=_params(("parallel", "parallel")),
        name="attention",
    )(q, k, v, *cast)
    return out if cast else out[0]


GQA_HEADS = tuple(((p, 0, 0, 0, 0), (p, 1, 0, 0, 1)) for p in range(GQA_Q_HEADS // 2))
MLA_HEAD_GROUPS = tuple(((2 * p, None, 2 * p, p, 0), (2 * p + 1, None, 2 * p + 1, p, 1))
                        for p in range(MLA_HEADS // 2))


def attention_latent(q, k, v, heads, tq, cast=()):
    return attention(q, k, v, heads, 0, SEQ // tq, tq, S_ALL, cast)


def attention_context(q, k, v, heads):
    return attention(q, k, v, heads, SEQ // CTX_LEN, 1, CTX_LEN, CTX_LEN)


def _merge0_kernel(rec_ref, gate_ref, att_l_ref, att_c_ref, x_ref, ctx_ref, mod_ref,
                   w_rec_ref, w_att_ref, x1_ref):
    tm = x_ref.shape[0]
    row0 = pl.program_id(1) * tm
    is_ctx = row0 >= SEQ
    rg = (rec_ref[...].astype(F32) * jax.nn.gelu(gate_ref[...].astype(F32))).astype(BF16)
    att = jnp.where(is_ctx, att_c_ref[...], att_l_ref[...])
    y = (jnp.dot(rg, w_rec_ref[...], preferred_element_type=F32)
         + jnp.dot(att, w_att_ref[...], preferred_element_type=F32))
    x1_ref[...] = jnp.where(is_ctx, ctx_ref[...], x_ref[...]) + _mod_rows(mod_ref, 2, row0, tm) * y


def merge0(rec, gate, att_l, att_c, x, ctx, modp, w_rec, w_att):
    b = x.shape[0]
    tm = CTX_LEN
    n_lat = SEQ // tm
    tok = lambda width: pl.BlockSpec((None, tm, width), lambda i, t: (i, t, 0))
    lat = lambda width: pl.BlockSpec((None, tm, width), lambda i, t: (i, jnp.minimum(t, n_lat - 1), 0))
    cx = lambda width: pl.BlockSpec((None, tm, width), lambda i, t: (i, 0, 0))
    full = lambda r, c_: pl.BlockSpec((r, c_), lambda i, t: (0, 0))
    return pl.pallas_call(
        _merge0_kernel,
        grid=(b, S_ALL // tm),
        in_specs=[tok(LRU_WIDTH), tok(LRU_WIDTH), lat(HALF_MIX), cx(HALF_MIX), lat(D_MODEL), cx(D_MODEL),
                  pl.BlockSpec((None, N_MOD, 2, 1, D_MODEL), lambda i, t: (i, 0, 0, 0, 0)),
                  full(LRU_WIDTH, D_MODEL), full(HALF_MIX, D_MODEL)],
        out_specs=tok(D_MODEL),
        out_shape=jax.ShapeDtypeStruct((b, S_ALL, D_MODEL), F32),
        compiler_params=_params(("parallel", "parallel")),
        name="merge0",
    )(rec, gate, att_l, att_c, x, ctx, modp, w_rec, w_att)


def _swiglu_partial(x, w1_ref, w3_ref, w2_ref):
    h1 = jnp.dot(x, w1_ref[...], preferred_element_type=F32)
    h3 = jnp.dot(x, w3_ref[...], preferred_element_type=F32)
    act = (h1 * jax.nn.sigmoid(h1) * h3).astype(BF16)
    return jnp.dot(act, w2_ref[...], preferred_element_type=F32)


def _accumulate(acc_ref, part, j):
    @pl.when(j == 0)
    def _():
        acc_ref[...] = part

    @pl.when(j > 0)
    def _():
        acc_ref[...] += part


def _ffn_kernel(x1_ref, mod_ref, g2_ref, w1_ref, w3_ref, w2_ref, o_ref, act_ref):
    tm = x1_ref.shape[0]
    row0 = pl.program_id(1) * tm
    x = _norm_mod(x1_ref[...], g2_ref[...], mod_ref, row0, 3, 4).astype(BF16)
    for c in range(w1_ref.shape[1] // FFN_TILE_F):
        sl = slice(c * FFN_TILE_F, (c + 1) * FFN_TILE_F)
        h1 = jnp.dot(x, w1_ref[:, sl], preferred_element_type=F32)
        h3 = jnp.dot(x, w3_ref[:, sl], preferred_element_type=F32)
        act_ref[:, sl] = (h1 * jax.nn.sigmoid(h1) * h3).astype(BF16)
    y = jnp.dot(act_ref[...], w2_ref[...], preferred_element_type=F32)
    o_ref[...] = x1_ref[...] + _mod_rows(mod_ref, 5, row0, tm) * y


def ffn_residual(x1, modp, g2, w1, w3, w2):
    b = x1.shape[0]
    tm = FFN_TILE_M
    f = w1.shape[1]
    assert f % FFN_TILE_F == 0
    tok = lambda: pl.BlockSpec((None, tm, D_MODEL), lambda i, t: (i, t, 0))
    resident = lambda a: pl.BlockSpec(a.shape, lambda i, t: (0, 0), pipeline_mode=pl.Buffered(1))
    return pl.pallas_call(
        _ffn_kernel,
        grid=(b, S_ALL // tm),
        in_specs=[tok(),
                  pl.BlockSpec((None, N_MOD, 2, 1, D_MODEL), lambda i, t: (i, 0, 0, 0, 0)),
                  pl.BlockSpec((1, D_MODEL), lambda i, t: (0, 0)),
                  resident(w1), resident(w3), resident(w2)],
        out_specs=tok(),
        out_shape=jax.ShapeDtypeStruct((b, S_ALL, D_MODEL), F32),
        scratch_shapes=[pltpu.VMEM((tm, f), BF16)],
        compiler_params=_params(("parallel", "parallel")),
        name="ffn",
    )(x1, modp, g2, w1, w3, w2)


def _moe_kernel(te_ref, nv_ref, x_ref, w1_ref, w3_ref, w2_ref, o_ref, xb_ref, acc_ref):
    i, j = pl.program_id(0), pl.program_id(1)
    last = pl.num_programs(1) - 1
    valid = i < nv_ref[0]

    @pl.when(valid)
    def _():
        @pl.when(j == 0)
        def _():
            xb_ref[...] = x_ref[...].astype(BF16)

        _accumulate(acc_ref, _swiglu_partial(xb_ref[...], w1_ref, w3_ref, w2_ref), j)

        @pl.when(j == last)
        def _():
            o_ref[...] = acc_ref[...]

    @pl.when(jnp.logical_and(jnp.logical_not(valid), j == last))
    def _():
        o_ref[...] = jnp.zeros_like(o_ref)


def grouped_swiglu(xs, tile_expert, n_valid, w1, w3, w2, tm, tf):
    n_rows, d = xs.shape
    f = w1.shape[-1]
    assert n_rows % tm == 0 and f % tf == 0
    n_f = f // tf

    def col(i, j, nv):
        return jnp.where(i < nv[0], j, n_f - 1)

    grid_spec = pltpu.PrefetchScalarGridSpec(
        num_scalar_prefetch=2,
        grid=(n_rows // tm, n_f),
        in_specs=[pl.BlockSpec((tm, d), lambda i, j, te, nv: (jnp.minimum(i, nv[0] - 1), 0)),
                  pl.BlockSpec((None, d, tf), lambda i, j, te, nv: (te[i], 0, col(i, j, nv))),
                  pl.BlockSpec((None, d, tf), lambda i, j, te, nv: (te[i], 0, col(i, j, nv))),
                  pl.BlockSpec((None, tf, d), lambda i, j, te, nv: (te[i], col(i, j, nv), 0))],
        out_specs=pl.BlockSpec((tm, d), lambda i, j, te, nv: (i, 0)),
        scratch_shapes=[pltpu.VMEM((tm, d), BF16), pltpu.VMEM((tm, d), F32)],
    )
    return pl.pallas_call(
        _moe_kernel,
        grid_spec=grid_spec,
        out_shape=jax.ShapeDtypeStruct((n_rows, d), F32),
        compiler_params=_params(("parallel", "arbitrary")),
        name="moe",
    )(tile_expert, n_valid, xs, w1, w3, w2)


def route(logits, tm):
    n_tok = logits.shape[0]
    n_assign = n_tok * TOP_K
    idx = jnp.arange(N_EXPERTS, dtype=jnp.int32)[None, :]
    m1 = jnp.max(logits, axis=-1, keepdims=True)
    e1 = jnp.min(jnp.where(logits == m1, idx, N_EXPERTS), axis=-1, keepdims=True)
    rest = jnp.where(idx == e1, jnp.finfo(F32).min, logits)
    m2 = jnp.max(rest, axis=-1, keepdims=True)
    e2 = jnp.min(jnp.where(rest == m2, idx, N_EXPERTS), axis=-1, keepdims=True)
    t = jnp.exp(m2 - m1)
    gates = jnp.concatenate([1.0 / (1.0 + t), t / (1.0 + t)], axis=-1)
    flat_e = jnp.concatenate([e1, e2], axis=-1).reshape(n_assign, 1)
    onehot = (flat_e == idx).astype(F32)
    blk = MOE_ROUTE_BLOCK
    nb = n_assign // blk
    oh = onehot.reshape(nb, blk, N_EXPERTS)
    lower = lambda n: (jnp.arange(n)[:, None] > jnp.arange(n)[None, :]).astype(F32)
    within = jnp.einsum('ij,bjk->bik', lower(blk), oh)
    blk_cnt = jnp.sum(oh, axis=1)
    blk_off = jnp.dot(lower(nb), blk_cnt, precision=lax.Precision.HIGHEST)
    rank = jnp.sum((within + blk_off[:, None, :]) * oh, axis=-1).reshape(n_assign)
    counts = jnp.sum(blk_cnt, axis=0).astype(jnp.int32)
    padded = (counts + tm - 1) // tm * tm
    ends = []
    for e in range(N_EXPERTS):
        ends.append(padded[e] + (ends[-1] if ends else 0))
    pend = jnp.stack(ends)
    pstart = pend - padded
    dest = (jnp.sum(onehot * pstart.astype(F32)[None, :], axis=-1) + rank).astype(jnp.int32)
    n_tiles = n_assign // tm + N_EXPERTS
    starts = jnp.arange(n_tiles, dtype=jnp.int32) * tm
    tile_expert = jnp.minimum(jnp.sum((pend[None, :] <= starts[:, None]).astype(jnp.int32), axis=-1),
                              N_EXPERTS - 1)
    n_valid = (pend[-1] // tm).reshape(1)
    meta = jnp.concatenate([pend, padded, n_valid])
    return gates, dest, tile_expert, n_valid, meta


def _row_copies(dest_ref, src, dst, sem, rows, gather, side_work=()):
    n_groups = rows // SUBLANES
    for g in range(n_groups):
        for r in range(g * SUBLANES, (g + 1) * SUBLANES):
            for kk in range(TOP_K):
                d = dest_ref[0, TOP_K * r + kk]
                if gather:
                    copy = pltpu.make_async_copy(src.at[pl.ds(d, 1), :], dst.at[kk, pl.ds(r, 1), :], sem)
                else:
                    copy = pltpu.make_async_copy(src.at[pl.ds(r, 1), :], dst.at[pl.ds(d, 1), :], sem)
                copy.start(priority=kk % 2)
        for thunk in side_work[g * len(side_work) // n_groups:(g + 1) * len(side_work) // n_groups]:
            thunk()
    if gather:
        pltpu.make_async_copy(dst, dst, sem).wait()
    else:
        for _ in range(TOP_K):
            pltpu.make_async_copy(src, src, sem).wait()


def _dispatch_kernel(meta_ref, dest_ref, h_ref, w_ref, xs_ref, wb_ref, zero_ref, sem, zsem):
    tm_moe = zero_ref.shape[0]

    @pl.when(pl.program_id(0) == 0)
    def _():
        zero_ref[...] = jnp.zeros_like(zero_ref)
        n_tiles = xs_ref.shape[0] // tm_moe
        fills = []
        for e in range(N_EXPERTS):
            start = pl.multiple_of(jnp.maximum(meta_ref[e] - tm_moe, 0), tm_moe)
            fills.append((meta_ref[N_EXPERTS + e] > 0, start))
        for i in range(n_tiles - N_EXPERTS, n_tiles):
            fills.append((i >= meta_ref[2 * N_EXPERTS], i * tm_moe))
        for cond, start in fills:
            @pl.when(cond)
            def _():
                pltpu.make_async_copy(zero_ref, xs_ref.at[pl.ds(start, tm_moe), :], zsem).start()
        for cond, start in fills:
            @pl.when(cond)
            def _():
                pltpu.make_async_copy(zero_ref, xs_ref.at[pl.ds(start, tm_moe), :], zsem).wait()

    def cast_rows(c):
        def thunk():
            wb_ref[c * 16:(c + 1) * 16, :] = w_ref[c * 16:(c + 1) * 16, :].astype(BF16)
        return thunk

    casts = [cast_rows(c) for c in range(w_ref.shape[0] // 16)]
    _row_copies(dest_ref, h_ref, xs_ref, sem, h_ref.shape[0], gather=False, side_work=casts)


def moe_dispatch(h2, dest, meta, n_slots, w):
    n_tok, d = h2.shape
    tm = MOE_DISPATCH_TILE
    n_steps = n_tok // tm
    wf = w.reshape(-1, w.shape[-1])
    assert wf.shape[0] % (16 * n_steps) == 0
    slab = pl.BlockSpec((wf.shape[0] // n_steps, wf.shape[1]), lambda t, m: (t, 0))
    grid_spec = pltpu.PrefetchScalarGridSpec(
        num_scalar_prefetch=1,
        grid=(n_steps,),
        in_specs=[pl.BlockSpec((None, 1, TOP_K * tm), lambda t, m: (t, 0, 0), memory_space=pltpu.SMEM),
                  pl.BlockSpec((tm, d), lambda t, m: (t, 0)), slab],
        out_specs=[pl.BlockSpec(memory_space=pl.ANY), slab],
        scratch_shapes=[pltpu.VMEM((MOE_TILE_M, d), h2.dtype), pltpu.SemaphoreType.DMA,
                        pltpu.SemaphoreType.DMA],
    )
    xs, wb = pl.pallas_call(
        _dispatch_kernel,
        grid_spec=grid_spec,
        out_shape=[jax.ShapeDtypeStruct((n_slots, d), h2.dtype), jax.ShapeDtypeStruct(wf.shape, BF16)],
        compiler_params=_params(("arbitrary",)),
        name="moe_dispatch",
    )(meta, dest.reshape(n_steps, 1, TOP_K * tm), h2, wf)
    return xs, wb.reshape(w.shape)


def _combine_kernel(dest_ref, x1_ref, g_ref, mod_ref, ys_ref, o_ref, buf_ref, sem):
    tm = x1_ref.shape[0]
    _row_copies(dest_ref, ys_ref, buf_ref, sem, tm, gather=True)
    g = g_ref[...]
    moe = g[:, 0:1] * buf_ref[0] + g[:, 1:2] * buf_ref[1]
    o_ref[...] = x1_ref[...] + mod_ref[5, 0] * moe


def moe_combine(x1, gates, dest, ys, modp):
    b, s, d = x1.shape
    tm = MOE_COMBINE_TILE
    nt = s // tm
    return pl.pallas_call(
        _combine_kernel,
        grid=(b, nt),
        in_specs=[pl.BlockSpec((None, 1, TOP_K * tm), lambda i, t: (i * nt + t, 0, 0), memory_space=pltpu.SMEM),
                  pl.BlockSpec((None, tm, d), lambda i, t: (i, t, 0)),
                  pl.BlockSpec((None, tm, TOP_K), lambda i, t: (i, t, 0)),
                  pl.BlockSpec((None, N_MOD, 2, 1, d), lambda i, t: (i, 0, 0, 0, 0)),
                  pl.BlockSpec(memory_space=pl.ANY)],
        out_specs=pl.BlockSpec((None, tm, d), lambda i, t: (i, t, 0)),
        out_shape=jax.ShapeDtypeStruct((b, s, d), F32),
        scratch_shapes=[pltpu.VMEM((TOP_K, tm, d), F32), pltpu.SemaphoreType.DMA],
        compiler_params=_params(("arbitrary", "arbitrary")),
        name="moe_combine",
    )(dest.reshape(b * nt, 1, TOP_K * tm), x1, gates.reshape(b, s, TOP_K), modp, ys)


def moe_residual(x1, h2, logits, modp, w1, w3, w2):
    b, s, d = x1.shape
    n_tok = b * s
    tm = MOE_TILE_M
    gates, dest, tile_expert, n_valid, meta = route(logits.reshape(n_tok, N_EXPERTS), tm)
    n_slots = (n_tok * TOP_K // tm + N_EXPERTS) * tm
    xs, w2 = moe_dispatch(h2.reshape(n_tok, d), dest, meta, n_slots, w2)
    ys = grouped_swiglu(xs, tile_expert, n_valid, w1, w3, w2, tm, MOE_TILE_F)
    return moe_combine(x1, gates, dest, ys, modp)


def _proj1_kernel(x_ref, g_ref, mod_ref, w_ref, qan_ref, qb_ref, qbs_ref, kvan_ref, kbk_ref, kbv_ref,
                  cos_ref, sin_ref, gains_ref, ones_ref, hy_ref, q_ref, k_ref, v_ref):
    tm = x_ref.shape[0]
    row0 = pl.program_id(1) * tm
    h = _norm_mod(x_ref[...], g_ref[...], mod_ref, row0, 0, 1)
    z = jnp.dot(h.astype(BF16), w_ref[...], preferred_element_type=F32)
    hy_ref[...] = z[:, :HY_IN]

    def rms(a, g):
        return a * lax.rsqrt(jnp.mean(a * a, axis=-1, keepdims=True) + EPS) * g

    c_q, c_kv = HY_IN, HY_IN + MLA_Q_RANK
    c_r = c_kv + MLA_KV_RANK
    qa = rms(z[:, c_q:c_kv], qan_ref[...]).astype(BF16)
    kva = rms(z[:, c_kv:c_r], kvan_ref[...]).astype(BF16)
    q = jnp.dot(qa, qb_ref[...], preferred_element_type=F32)
    q_sw = jnp.dot(qa, qbs_ref[...], preferred_element_type=F32)
    kk = jnp.dot(kva, kbk_ref[...], preferred_element_type=F32)
    v_ref[...] = jnp.dot(kva, kbv_ref[...], preferred_element_type=F32).astype(BF16)
    k_rope = pltpu.roll(z[:, c_r:c_r + LANES], MLA_NOPE, 1)
    first_half = lax.broadcasted_iota(jnp.int32, (tm, LANES), 1) < MLA_NOPE + MLA_ROPE // 2
    sh = MLA_ROPE // 2
    k_rope_sw = jnp.where(first_half, pltpu.roll(k_rope, LANES - sh, 1), pltpu.roll(k_rope, sh, 1))
    cos, sin = cos_ref[...], sin_ref[...]
    qg, qg_sw, kg, kg_sw = (gains_ref[i:i + 1, :] for i in range(4))

    def head(a, a_sw, gain, gain_sw):
        inv = lax.rsqrt(_group_sums(a, ones_ref) / MLA_QK + EPS)
        return (a * (inv * gain) * cos + a_sw * (inv * gain_sw) * sin).astype(BF16)

    for hh in range(MLA_HEADS):
        sl = slice(hh * LANES, (hh + 1) * LANES)
        q_ref[:, sl] = head(q[:, sl], q_sw[:, sl], qg, qg_sw)
        k_ref[:, sl] = head(kk[:, sl] + k_rope, k_rope_sw, kg, kg_sw)


def _swap_rope_halves(a):
    lead = a.shape[:-1]
    g = a.reshape(*lead, -1, LANES)
    lo, mid = MLA_NOPE, MLA_NOPE + MLA_ROPE // 2
    g = jnp.concatenate([g[..., :lo], g[..., mid:MLA_QK], g[..., lo:mid], g[..., MLA_QK:]], axis=-1)
    return g.reshape(a.shape)


def proj1(xa, g, modp, w_in, qan, qb, kvan, kbk, kbv, cos, sin, q_gain, k_gain):
    b = xa.shape[0]
    tm = TOKEN_TILE
    tok = lambda width: pl.BlockSpec((None, tm, width), lambda i, t: (i, t, 0))
    full = lambda a: pl.BlockSpec(a.shape, lambda i, t: (0, 0))
    hw = MLA_HEADS * LANES
    qbs = _swap_rope_halves(qb)
    gains = jnp.concatenate([q_gain, _swap_rope_halves(q_gain), k_gain, _swap_rope_halves(k_gain)], axis=0)
    ones = jnp.ones((LANES, LANES), BF16)
    return pl.pallas_call(
        _proj1_kernel,
        grid=(b, S_ALL // tm),
        in_specs=[tok(D_MODEL), full(g),
                  pl.BlockSpec((None, N_MOD, 2, 1, D_MODEL), lambda i, t: (i, 0, 0, 0, 0)),
                  full(w_in), full(qan), full(qb), full(qbs), full(kvan), full(kbk), full(kbv),
                  pl.BlockSpec((tm, LANES), lambda i, t: (t, 0)),
                  pl.BlockSpec((tm, LANES), lambda i, t: (t, 0)),
                  full(gains), full(ones)],
        out_specs=[tok(HY_IN), tok(hw), tok(hw), tok(MLA_HEADS * MLA_V)],
        out_shape=[jax.ShapeDtypeStruct((b, S_ALL, HY_IN), F32),
                   jax.ShapeDtypeStruct((b, S_ALL, hw), BF16),
                   jax.ShapeDtypeStruct((b, S_ALL, hw), BF16),
                   jax.ShapeDtypeStruct((b, S_ALL, MLA_HEADS * MLA_V), BF16)],
        compiler_params=_params(("parallel", "parallel")),
        name="proj1",
    )(xa, g, modp, w_in, qan, qb, qbs, kvan, kbk, kbv, cos, sin, gains, ones)


def _merge1_kernel(hy_ref, att_ref, x_ref, mod_ref, g2_ref, w_hy_ref, w_att_ref, r_ref,
                   x1_ref, h2_ref, lg_ref):
    y = (jnp.dot(hy_ref[...].astype(BF16), w_hy_ref[...], preferred_element_type=F32)
         + jnp.dot(att_ref[...], w_att_ref[...], preferred_element_type=F32))
    x1 = x_ref[...] + mod_ref[2, 0] * y
    x1_ref[...] = x1
    h2 = _norm_mod(x1, g2_ref[...], mod_ref, 0, 3, 4)
    h2_ref[...] = h2
    lane = lax.broadcasted_iota(jnp.int32, lg_ref.shape, 1)
    logits = jnp.zeros(lg_ref.shape, F32)
    for e in range(N_EXPERTS):
        col = jnp.sum(h2 * r_ref[e:e + 1, :], axis=-1, keepdims=True)
        logits = jnp.where(lane == e, col, logits)
    lg_ref[...] = logits


def merge1(hy, att, xa, modp, g2, w_hy, w_att, router):
    b = xa.shape[0]
    tm = MERGE1_TILE
    tok = lambda width: pl.BlockSpec((None, tm, width), lambda i, t: (i, t, 0))
    full = lambda a: pl.BlockSpec(a.shape, lambda i, t: (0, 0))
    return pl.pallas_call(
        _merge1_kernel,
        grid=(b, SEQ // tm),
        in_specs=[tok(HY_WIDTH), tok(HALF_MIX), tok(D_MODEL),
                  pl.BlockSpec((None, N_MOD, 2, 1, D_MODEL), lambda i, t: (i, 0, 0, 0, 0)),
                  full(g2), full(w_hy), full(w_att), full(router)],
        out_specs=[tok(D_MODEL), tok(D_MODEL), tok(N_EXPERTS)],
        out_shape=[jax.ShapeDtypeStruct((b, SEQ, D_MODEL), F32),
                   jax.ShapeDtypeStruct((b, SEQ, D_MODEL), F32),
                   jax.ShapeDtypeStruct((b, SEQ, N_EXPERTS), F32)],
        compiler_params=_params(("parallel", "parallel")),
        name="merge1",
    )(hy, att, xa, modp, g2, w_hy, w_att, router)


HY_N = 2 * SEQ
HY_TW_ROWS = 256
HY_SPEC_FBLK = 256
HY_CONV_FBLK = 512
HY_LANE_TILES = SEQ // LANES


def _twiddle_kernel(ca_ref, sa_ref, cb_ref, sb_ref, fre_ref, fim_ref, ic_ref, is_ref):
    rows = fre_ref.shape[0]
    r = pl.program_id(0) * rows + lax.broadcasted_iota(jnp.int32, (rows, LANES), 0)
    alt_r = (1 - 2 * (r & 1)).astype(F32)
    cb, sb = cb_ref[...], sb_ref[...]
    for a in range(HY_LANE_TILES):
        ca, sa = ca_ref[:, a:a + 1], sa_ref[:, a:a + 1]
        c = ca * cb - sa * sb
        s = sa * cb + ca * sb
        col = a * LANES + lax.broadcasted_iota(jnp.int32, (rows, LANES), 1)
        alt_c = (1 - 2 * (col & 1)).astype(F32)
        w = jnp.where(col == 0, 1.0 / HY_N, 2.0 / HY_N)
        sl = slice(a * LANES, (a + 1) * LANES)
        fre_ref[:, sl] = c.astype(BF16)
        fim_ref[:, sl] = jnp.where(r == 0, alt_c, -s).astype(BF16)
        ic_ref[:, sl] = (c * w).astype(BF16)
        is_ref[:, sl] = jnp.where(col == 0, alt_r / HY_N, -s * w).astype(BF16)


def dft_matrices():
    idx = jnp.arange(SEQ, dtype=jnp.int32)[:, None]
    step = 2.0 * math.pi / HY_N
    ph_a = ((idx * (LANES * jnp.arange(HY_LANE_TILES, dtype=jnp.int32))[None, :]) % HY_N).astype(F32) * step
    ph_b = ((idx * jnp.arange(LANES, dtype=jnp.int32)[None, :]) % HY_N).astype(F32) * step
    rows = HY_TW_ROWS
    tab = lambda width: pl.BlockSpec((rows, width), lambda i: (i, 0))
    out = jax.ShapeDtypeStruct((SEQ, SEQ), BF16)
    return pl.pallas_call(
        _twiddle_kernel,
        grid=(SEQ // rows,),
        in_specs=[tab(HY_LANE_TILES), tab(HY_LANE_TILES), tab(LANES), tab(LANES)],
        out_specs=[tab(SEQ)] * 4,
        out_shape=[out] * 4,
        compiler_params=_params(("parallel",)),
        name="twiddle",
    )(jnp.cos(ph_a), jnp.sin(ph_a), jnp.cos(ph_b), jnp.sin(ph_b))


def _spec_kernel(fre_ref, fim_ref, h_ref, k_ref):
    w = HY_WIDTH
    h = h_ref[...].astype(BF16)
    re = jnp.dot(fre_ref[...], h, preferred_element_type=F32)
    im = jnp.dot(fim_ref[...], h, preferred_element_type=F32)
    hb0 = h[0:1, w:].astype(F32)
    first = (pl.program_id(1) * re.shape[0] + lax.broadcasted_iota(jnp.int32, (re.shape[0], 1), 0)) == 0
    k_ref[0] = re[:, :w] + re[:, w:] - hb0
    k_ref[1] = jnp.where(first, im[:, :w] + im[:, w:] - hb0, im[:, :w] - im[:, w:])


def filter_spectra(filt, fre, fim):
    fb = HY_SPEC_FBLK
    return pl.pallas_call(
        _spec_kernel,
        grid=(HY_ORDER, SEQ // fb),
        in_specs=[pl.BlockSpec((fb, SEQ), lambda o, f: (f, 0)),
                  pl.BlockSpec((fb, SEQ), lambda o, f: (f, 0)),
                  pl.BlockSpec((SEQ, 2 * HY_WIDTH), lambda o, f: (0, o))],
        out_specs=pl.BlockSpec((None, 2, fb, HY_WIDTH), lambda o, f: (o, 0, f, 0)),
        out_shape=jax.ShapeDtypeStruct((HY_ORDER, 2, SEQ, HY_WIDTH), F32),
        compiler_params=_params(("parallel", "parallel")),
        name="filter_spectra",
    )(fre, fim, filt)


def _short_conv(z_ref, w_ref, b_ref, part):
    z = z_ref[...]
    l = z.shape[0]
    t = lax.broadcasted_iota(jnp.int32, (l, 1), 0)
    prev = jnp.where(t == 0, 0.0, pltpu.roll(z, 1, 0))
    nxt = jnp.where(t == l - 1, 0.0, pltpu.roll(z, l - 1, 0))
    cs = slice(part * HY_WIDTH, (part + 1) * HY_WIDTH)
    return (w_ref[0:1, cs] * prev + w_ref[1:2, cs] * z + w_ref[2:3, cs] * nxt) + b_ref[:, cs]


def _hyconv_kernel(u_ref, xg_ref, cw_ref, cb_ref, fre_ref, fim_ref, ic_ref, is_ref, k_ref, skip_ref,
                   y_ref, ub_ref, acc_ref, *, order):
    f = pl.program_id(1)

    def u_f32():
        return _short_conv(u_ref, cw_ref, cb_ref, 0) if order == 0 else u_ref[...]

    @pl.when(f == 0)
    def _():
        ub_ref[...] = u_f32().astype(BF16)

    x_re = jnp.dot(fre_ref[...], ub_ref[...], preferred_element_type=F32)
    x_im = jnp.dot(fim_ref[...], ub_ref[...], preferred_element_type=F32)
    k_re, k_im = k_ref[0], k_ref[1]
    first = (f * x_re.shape[0] + lax.broadcasted_iota(jnp.int32, (x_re.shape[0], 1), 0)) == 0
    y_re = x_re * k_re - jnp.where(first, 0.0, x_im * k_im)
    y_im = x_im * jnp.where(first, k_im, k_re) + jnp.where(first, 0.0, x_re * k_im)
    part = (jnp.dot(ic_ref[...], y_re.astype(BF16), preferred_element_type=F32)
            + jnp.dot(is_ref[...], y_im.astype(BF16), preferred_element_type=F32))
    _accumulate(acc_ref, part, f)

    @pl.when(f == pl.num_programs(1) - 1)
    def _():
        xg = _short_conv(xg_ref, cw_ref, cb_ref, order + 1)
        y_ref[...] = xg * (acc_ref[...] + skip_ref[order:order + 1, :] * u_f32())


def hyena_conv(order, u, hyz, conv_w, conv_b, fre, fim, ic, is_, spectra, skip):
    b = hyz.shape[0]
    fb = HY_CONV_FBLK
    w = HY_WIDTH
    lat = lambda part: pl.BlockSpec((None, SEQ, w), lambda i, f: (i, 0, part))
    full = lambda a: pl.BlockSpec(a.shape, lambda i, f: (0, 0))
    return pl.pallas_call(
        functools.partial(_hyconv_kernel, order=order),
        grid=(b, SEQ // fb),
        in_specs=[lat(0), lat(order + 1), full(conv_w), full(conv_b),
                  pl.BlockSpec((fb, SEQ), lambda i, f: (f, 0)),
                  pl.BlockSpec((fb, SEQ), lambda i, f: (f, 0)),
                  pl.BlockSpec((SEQ, fb), lambda i, f: (0, f)),
                  pl.BlockSpec((SEQ, fb), lambda i, f: (0, f)),
                  pl.BlockSpec((None, 2, fb, w), lambda i, f: (order, 0, f, 0)),
                  full(skip)],
        out_specs=pl.BlockSpec((None, SEQ, w), lambda i, f: (i, 0, 0)),
        out_shape=jax.ShapeDtypeStruct((b, SEQ, w), F32),
        scratch_shapes=[pltpu.VMEM((SEQ, w), BF16), pltpu.VMEM((SEQ, w), F32)],
        compiler_params=_params(("parallel", "arbitrary")),
        name="hyena_conv",
    )(u, hyz, conv_w, conv_b, fre, fim, ic, is_, spectra, skip)


def hyena_filters(length, w1, b1, w2, b2, w3, freq):
    hp = lax.Precision.HIGHEST
    t = jnp.arange(length, dtype=F32)[:, None]
    t_norm = t / max(length - 1, 1)
    bands = jnp.linspace(1e-4, HY_BANDS - 1, HY_BANDS, dtype=F32)
    ang = 2.0 * math.pi * t * bands / length
    z = jnp.concatenate([t_norm, jnp.cos(ang), -jnp.sin(ang)], axis=-1)
    h = jnp.sin(freq * (jnp.dot(z, w1, precision=hp) + b1))
    h = jnp.sin(freq * (jnp.dot(h, w2, precision=hp) + b2))
    h = jnp.dot(h, w3, precision=hp)
    deltas = jnp.abs(jnp.linspace(HY_MIN_DECAY, HY_MAX_DECAY, HY_WIDTH, dtype=F32))
    window = jnp.exp(-t_norm * deltas) + HY_SHIFT
    return h * jnp.tile(window, (1, HY_ORDER * 2))


def hyena(hyz, conv_w, conv_b, fw1, fb1, fw2, fb2, fw3, freq, skip):
    fre, fim, ic, is_ = dft_matrices()
    filt = hyena_filters(SEQ, fw1, fb1, fw2, fb2, fw3, freq)
    spectra = filter_spectra(filt, fre, fim)
    y = hyz
    for o in range(HY_ORDER):
        y = hyena_conv(o, y, hyz, conv_w, conv_b.reshape(1, -1), fre, fim, ic, is_, spectra, skip)
    return y


def _grid_angles(rot_dim):
    n_freq = rot_dim // 4
    inv_freq = ROPE_THETA ** (-jnp.arange(n_freq, dtype=F32) / n_freq)
    t = jnp.arange(SEQ)
    r = (t // GRID_W).astype(F32)
    c_ = (t % GRID_W).astype(F32)
    return jnp.concatenate([r[:, None] * inv_freq, c_[:, None] * inv_freq], axis=-1)


def _rope_tables(rot_dim, lane_offsets):
    ang = _grid_angles(rot_dim)
    c, s = jnp.cos(ang), jnp.sin(ang)
    cos_parts, sin_parts, lane = [], [], 0
    for off in lane_offsets:
        cos_parts += [jnp.ones((SEQ, off - lane), F32), c, c]
        sin_parts += [jnp.zeros((SEQ, off - lane), F32), -s, s]
        lane = off + rot_dim
    cos_parts.append(jnp.ones((SEQ, LANES - lane), F32))
    sin_parts.append(jnp.zeros((SEQ, LANES - lane), F32))
    cos = jnp.concatenate([jnp.concatenate(cos_parts, axis=1), jnp.ones((CTX_LEN, LANES), F32)], axis=0)
    sin = jnp.concatenate([jnp.concatenate(sin_parts, axis=1), jnp.zeros((CTX_LEN, LANES), F32)], axis=0)
    return cos, sin


def _block_diag(w):
    nb, bs, _ = w.shape
    eye = jnp.eye(nb, dtype=w.dtype)
    return (eye[:, None, :, None] * w[:, :, None, :]).reshape(nb * bs, nb * bs)


def _gqa_pair_order():
    g = GQA_Q_HEADS // GQA_KV_HEADS
    heads = []
    for p in range(g):
        heads += [p, g + p]
    return np.concatenate([np.arange(h * HEAD_DIM, (h + 1) * HEAD_DIM) for h in heads])


def _pad_heads(w, n_heads, width):
    k = w.shape[0]
    w = w.reshape(k, n_heads, width)
    return jnp.pad(w, ((0, 0), (0, 0), (0, LANES - width))).reshape(k, n_heads * LANES)


def kernel(x, c, ctx, c_ctx, mod_w, mod_b, norm1_g, norm2_g, ab_w_in, ab_w_out, lru_conv_w, lru_conv_b, lru_w_a, lru_b_a, lru_w_x, lru_b_x, lru_lambda, gqa_q_norm, gqa_k_norm, ffn_w1, ffn_w3, ffn_w2, cd_w_in, cd_w_out, hy_conv_w, hy_conv_b, hy_filt_w1, hy_filt_b1, hy_filt_w2, hy_filt_b2, hy_filt_w3, hy_sin_freq, hy_skip, mla_q_a_norm, mla_q_b, mla_kv_a_norm, mla_kv_b, mla_q_norm, mla_k_norm, moe_router, moe_w1, moe_w3, moe_w2):
    batch = x.shape[0]
    bf = lambda w: w.astype(BF16)
    row = lambda v: v.reshape(1, -1)

    silu_all = jnp.concatenate([jax.nn.silu(c), jax.nn.silu(c_ctx)[None, :],
                                jnp.zeros((16 - batch - 1, D_MODEL), F32)], axis=0)
    mods = []
    mod_all = stacked_matmul(silu_all, mod_w, 1536) + mod_b[:, None, :]
    for layer in range(DEPTH):
        m = mod_all[layer]
        lat = m[:batch].reshape(batch, N_MOD, 1, D_MODEL)
        cx = jnp.broadcast_to(m[batch].reshape(1, N_MOD, 1, D_MODEL), lat.shape)
        mods.append(jnp.stack([lat, cx], axis=2))

    perm = _gqa_pair_order()
    q0 = 2 * LRU_WIDTH
    w_in0 = ab_w_in[0]
    w_in0 = jnp.concatenate([w_in0[:, :q0], w_in0[:, q0 + perm], w_in0[:, q0 + GQA_Q_HEADS * HEAD_DIM:]], axis=1)
    cos_g, sin_g = _rope_tables(HEAD_DIM, (0, HEAD_DIM))
    q_gain = row(jnp.tile(gqa_q_norm[0], 2) * HEAD_DIM ** -0.5)
    k_gain = row(jnp.tile(gqa_k_norm[0], 2))
    xr, gate, q, k, v = proj0(x, ctx, row(norm1_g[0]), mods[0], bf(w_in0), cos_g, sin_g, q_gain, k_gain)

    w_gates = jnp.concatenate([_block_diag(lru_w_a[0, 0]), _block_diag(lru_w_x[0, 0]),
                               _block_diag(lru_w_a[0, 1]), _block_diag(lru_w_x[0, 1])], axis=1)
    b_gates = jnp.concatenate([lru_b_a[0, 0].reshape(-1), lru_b_x[0, 0].reshape(-1),
                               lru_b_a[0, 1].reshape(-1), lru_b_x[0, 1].reshape(-1)])
    rec = rglru(xr, lru_conv_w[0], row(lru_conv_b[0]), bf(w_gates), row(b_gates), lru_lambda[0])

    att_l, moe_w1b, moe_w3b = attention_latent(
        q, k, v, GQA_HEADS, GQA_TILE_Q,
        cast=(moe_w1[0].reshape(-1, moe_w1.shape[-1]), moe_w3[0].reshape(-1, moe_w3.shape[-1])))
    att_c = attention_context(q, k, v, GQA_HEADS)
    w_out0 = ab_w_out[0]
    x1 = merge0(rec, gate, att_l, att_c, x, ctx, mods[0], bf(w_out0[:LRU_WIDTH]), bf(w_out0[LRU_WIDTH:][perm]))
    xa = ffn_residual(x1, mods[0], row(norm2_g[0]), bf(ffn_w1[0]), bf(ffn_w3[0]), bf(ffn_w2[0]))

    w_in1 = jnp.pad(cd_w_in[0], ((0, 0), (0, LANES - MLA_ROPE)))
    cos_m, sin_m = _rope_tables(MLA_ROPE, (MLA_NOPE,))
    qb = _pad_heads(mla_q_b[0], MLA_HEADS, MLA_QK)
    kvb = mla_kv_b[0].reshape(MLA_KV_RANK, MLA_HEADS, MLA_NOPE + MLA_V)
    kbk = _pad_heads(kvb[:, :, :MLA_NOPE].reshape(MLA_KV_RANK, -1), MLA_HEADS, MLA_NOPE)
    kbv = kvb[:, :, MLA_NOPE:].reshape(MLA_KV_RANK, -1)
    pad_gain = lambda g_: row(jnp.pad(g_, (0, LANES - MLA_QK)))
    hyz, q, k, v = proj1(xa, row(norm1_g[1]), mods[1], bf(w_in1), row(mla_q_a_norm[0]), bf(qb),
                         row(mla_kv_a_norm[0]), bf(kbk), bf(kbv), cos_m, sin_m,
                         pad_gain(mla_q_norm[0] * MLA_QK ** -0.5), pad_gain(mla_k_norm[0]))
    att = attention_latent(q, k, v, MLA_HEAD_GROUPS, MLA_TILE_Q)
    hy = hyena(hyz, hy_conv_w[0], hy_conv_b[0], hy_filt_w1[0], hy_filt_b1[0], hy_filt_w2[0],
               hy_filt_b2[0], hy_filt_w3[0], hy_sin_freq[0], hy_skip[0])
    w_out1 = cd_w_out[0]
    x1, h2, logits = merge1(hy, att, xa, mods[1], row(norm2_g[1]),
                            bf(w_out1[:HY_WIDTH]), bf(w_out1[HY_WIDTH:]), moe_router[0].T)
    return moe_residual(x1, h2, logits, mods[1], moe_w1b.reshape(moe_w1.shape[1:]),
                        moe_w3b.reshape(moe_w3.shape[1:]), moe_w2[0])
```

```python
import functools
import math

import jax
import jax.numpy as jnp
import numpy as np
from jax import lax
from jax.experimental import pallas as pl
from jax.experimental.pallas import tpu as pltpu

F32 = jnp.float32
BF16 = jnp.bfloat16

D_MODEL = 1024
SEQ = 2048
CTX_LEN = 256
S_ALL = SEQ + CTX_LEN
DEPTH = 2
GRID_W = 64
HEAD_DIM = 64
HALF_MIX = D_MODEL // 2
ROPE_THETA = 10000.0
EPS = 1e-6
N_MOD = 6
LRU_WIDTH = HALF_MIX
LRU_BLOCKS = LRU_WIDTH // HEAD_DIM
LRU_CONV = 4
LRU_C = 8.0
GQA_Q_HEADS = HALF_MIX // HEAD_DIM
GQA_KV_HEADS = 2
HY_WIDTH = HALF_MIX
HY_ORDER = 2
HY_CONV = 3
HY_BANDS = 16
HY_TARGET = 1e-2
HY_FAST_PCT = 0.3
HY_SLOW_PCT = 1.5
HY_MIN_DECAY = math.log(HY_TARGET) / HY_SLOW_PCT
HY_MAX_DECAY = math.log(HY_TARGET) / HY_FAST_PCT
HY_SHIFT = 0.05
HY_IN = (HY_ORDER + 1) * HY_WIDTH
MLA_HEADS = HALF_MIX // HEAD_DIM
MLA_Q_RANK = D_MODEL // 4
MLA_KV_RANK = D_MODEL // 8
MLA_NOPE = HEAD_DIM
MLA_ROPE = HEAD_DIM // 2
MLA_V = HEAD_DIM
MLA_QK = MLA_NOPE + MLA_ROPE
N_EXPERTS = 8
TOP_K = 2

LANES = 128
SUBLANES = 8
V7X_VMEM_LIMIT_BYTES = 56 * 1024 * 1024

TOKEN_TILE = 768
PROJ0_ROW_SPLIT = 2
PROJ1_ROW_SPLIT = 1
FFN_TILE_M = 768
FFN_TILE_F = 256
GQA_TILE_Q = 512
MLA_TILE_Q = 512
MERGE1_TILE = 512
MOE_TILE_M = 512
MOE_TILE_F = 1792
MOE_ROUTE_BLOCK = 256
MOE_DISPATCH_TILE = 512
MOE_COMBINE_TILE = 512

LRU_GAP = SUBLANES
LRU_LAT0 = CTX_LEN + LRU_GAP
LRU_CHUNK = 464
LRU_ROWS = 2320
LRU_PAD_FRONT = SUBLANES

assert S_ALL % TOKEN_TILE == 0 and S_ALL % FFN_TILE_M == 0
assert SEQ % GQA_TILE_Q == 0 and SEQ % MLA_TILE_Q == 0
assert LRU_ROWS % LRU_CHUNK == 0 and LRU_ROWS >= LRU_LAT0 + SEQ + SUBLANES


def _params(sem):
    return pltpu.CompilerParams(dimension_semantics=sem, vmem_limit_bytes=V7X_VMEM_LIMIT_BYTES)


def _norm_mod(x, g, mod_pair, row0, shift_idx, scale_idx):
    y = x * lax.rsqrt(jnp.mean(x * x, axis=-1, keepdims=True) + EPS) * g
    shift = _mod_rows(mod_pair, shift_idx, row0, x.shape[0])
    scale = _mod_rows(mod_pair, scale_idx, row0, x.shape[0])
    return y * (1.0 + scale) + shift


def _mod_rows(mod_pair, idx, row0, rows):
    row = row0 + lax.broadcasted_iota(jnp.int32, (rows, 1), 0)
    return jnp.where(row >= SEQ, mod_pair[idx, 1], mod_pair[idx, 0])


def _mm_kernel(x_ref, w_ref, o_ref):
    o_ref[...] = jnp.dot(x_ref[...].astype(BF16), w_ref[...].astype(BF16),
                         preferred_element_type=F32)


def stacked_matmul(x, w, tn):
    m, k = x.shape
    n_l, _, n = w.shape
    assert n % tn == 0
    return pl.pallas_call(
        _mm_kernel,
        grid=(n_l, n // tn),
        in_specs=[pl.BlockSpec((m, k), lambda l, j: (0, 0)),
                  pl.BlockSpec((None, k, tn), lambda l, j: (l, 0, j))],
        out_specs=pl.BlockSpec((None, m, tn), lambda l, j: (l, 0, j)),
        out_shape=jax.ShapeDtypeStruct((n_l, m, n), F32),
        compiler_params=_params(("parallel", "parallel")),
        name="matmul",
    )(x, w)


def _rope_pair(a, cos, sin, first_half, shift):
    rot = jnp.where(first_half, pltpu.roll(a, LANES - shift, 1), pltpu.roll(a, shift, 1))
    return a * cos + rot * sin


def _group_sums(a, ones_ref):
    return jnp.dot((a * a).astype(BF16), ones_ref[...], preferred_element_type=F32)


def _proj0_kernel(x_ref, xt_ref, ctx_ref, g_ref, mod_ref, w_ref, cos_ref, sin_ref, qg_ref, kg_ref, ones_ref,
                  xr_ref, gate_ref, q_ref, k_ref, v_ref):
    tm = x_ref.shape[0]
    t = pl.program_id(1)
    last = jnp.concatenate([xt_ref[...], ctx_ref[...]], axis=0)
    rs = tm // PROJ0_ROW_SPLIT
    first_half = (lax.broadcasted_iota(jnp.int32, (rs, LANES), 1) & (HEAD_DIM // 2)) == 0
    q0 = 2 * LRU_WIDTH
    k0 = q0 + GQA_Q_HEADS * HEAD_DIM
    for r0 in range(0, tm, rs):
        rows = slice(r0, r0 + rs)
        x = jnp.where(t < SEQ // tm, x_ref[rows, :], last[rows])
        h = _norm_mod(x, g_ref[...], mod_ref, t * tm + r0, 0, 1)
        z = jnp.dot(h.astype(BF16), w_ref[...], preferred_element_type=F32)
        xr_ref[rows, :] = z[:, :LRU_WIDTH]
        gate_ref[rows, :] = z[:, LRU_WIDTH:2 * LRU_WIDTH].astype(BF16)
        cos, sin = cos_ref[rows, :], sin_ref[rows, :]

        def head_pair(a, gain):
            inv = lax.rsqrt(_group_sums(a, ones_ref) / HEAD_DIM + EPS)
            return _rope_pair(a * inv * gain, cos, sin, first_half, HEAD_DIM // 2).astype(BF16)

        for p in range(GQA_Q_HEADS // 2):
            q_ref[rows, p * LANES:(p + 1) * LANES] = head_pair(
                z[:, q0 + p * LANES:q0 + (p + 1) * LANES], qg_ref[...])
        k_ref[rows, :] = head_pair(z[:, k0:k0 + LANES], kg_ref[...])
        v_ref[rows, :] = z[:, k0 + LANES:k0 + 2 * LANES].astype(BF16)


def proj0(x, ctx, g, modp, w_in, cos, sin, q_gain, k_gain):
    b = x.shape[0]
    tm = TOKEN_TILE
    n = w_in.shape[1]
    n_full = SEQ // tm
    tail = SEQ - n_full * tm
    assert tail + CTX_LEN == tm and SEQ % tail == 0
    tok = lambda width: pl.BlockSpec((None, tm, width), lambda i, t: (i, t, 0))
    full = lambda r, c_: pl.BlockSpec((r, c_), lambda i, t: (0, 0))
    half = jnp.arange(LANES) // HEAD_DIM
    ones_pair = (half[:, None] == half[None, :]).astype(BF16)
    return pl.pallas_call(
        _proj0_kernel,
        grid=(b, S_ALL // tm),
        in_specs=[pl.BlockSpec((None, tm, D_MODEL), lambda i, t: (i, jnp.minimum(t, n_full - 1), 0)),
                  pl.BlockSpec((None, tail, D_MODEL), lambda i, t: (i, SEQ // tail - 1, 0)),
                  pl.BlockSpec((None, CTX_LEN, D_MODEL), lambda i, t: (i, 0, 0)),
                  full(1, D_MODEL),
                  pl.BlockSpec((None, N_MOD, 2, 1, D_MODEL), lambda i, t: (i, 0, 0, 0, 0)),
                  full(D_MODEL, n),
                  pl.BlockSpec((tm, LANES), lambda i, t: (t, 0)),
                  pl.BlockSpec((tm, LANES), lambda i, t: (t, 0)),
                  full(1, LANES), full(1, LANES), full(LANES, LANES)],
        out_specs=[tok(LRU_WIDTH), tok(LRU_WIDTH), tok(GQA_Q_HEADS * HEAD_DIM), tok(LANES), tok(LANES)],
        out_shape=[jax.ShapeDtypeStruct((b, S_ALL, LRU_WIDTH), F32),
                   jax.ShapeDtypeStruct((b, S_ALL, LRU_WIDTH), BF16),
                   jax.ShapeDtypeStruct((b, S_ALL, GQA_Q_HEADS * HEAD_DIM), BF16),
                   jax.ShapeDtypeStruct((b, S_ALL, LANES), BF16),
                   jax.ShapeDtypeStruct((b, S_ALL, LANES), BF16)],
        compiler_params=_params(("parallel", "parallel")),
        name="proj0",
    )(x, x, ctx, g, modp, w_in, cos, sin, q_gain, k_gain, ones_pair)


def _lru_kernel(xr_ref, cw_ref, cb_ref, wg_ref, bg_ref, lam_ref, rec_ref,
                pad_ref, af_ref, hf_ref, ab_ref, hb_ref):
    w = LRU_WIDTH
    pad_ref[...] = jnp.zeros_like(pad_ref)
    pad_ref[LRU_PAD_FRONT:LRU_PAD_FRONT + CTX_LEN, :] = xr_ref[SEQ:S_ALL, :]
    pad_ref[LRU_PAD_FRONT + LRU_LAT0:LRU_PAD_FRONT + LRU_LAT0 + SEQ, :] = xr_ref[0:SEQ, :]
    lam = lam_ref[...]
    neg_c_softplus = -LRU_C * (jnp.maximum(-lam, 0.0) + jnp.log(1.0 + jnp.exp(-jnp.abs(lam))))
    left = LRU_CONV // 2
    for ch in range(LRU_ROWS // LRU_CHUNK):
        r0 = ch * LRU_CHUNK
        u = cb_ref[...]
        for j in range(LRU_CONV):
            start = LRU_PAD_FRONT + r0 - left + j
            u = u + cw_ref[j:j + 1, :] * pad_ref[start:start + LRU_CHUNK, :]
        gz = jnp.dot(u.astype(BF16), wg_ref[...], preferred_element_type=F32) + bg_ref[...]
        for d, (a_ref, h_ref) in enumerate(((af_ref, hf_ref), (ab_ref, hb_ref))):
            r = jax.nn.sigmoid(gz[:, 2 * d * w:(2 * d + 1) * w])
            i = jax.nn.sigmoid(gz[:, (2 * d + 1) * w:(2 * d + 2) * w])
            a = jnp.exp(neg_c_softplus[d:d + 1, :] * r)
            a_ref[r0:r0 + LRU_CHUNK, :] = a
            h_ref[r0:r0 + LRU_CHUNK, :] = jnp.sqrt(1.0 - a * a) * (i * u)

    n_ctx_groups = CTX_LEN // SUBLANES
    sub = lax.broadcasted_iota(jnp.int32, (SUBLANES, w), 0)

    def tile_scan(a, b, descending):
        for s in (1, 2, 4):
            if descending:
                ok, shift = sub < SUBLANES - s, SUBLANES - s
            else:
                ok, shift = sub >= s, s
            a_s = jnp.where(ok, pltpu.roll(a, shift, 0), 1.0)
            b_s = jnp.where(ok, pltpu.roll(b, shift, 0), 0.0)
            a, b = a * a_s, a * b_s + b
        return a, b

    def group(tt, carry):
        hf, hb = carry
        is_lat = tt >= n_ctx_groups
        rf = pl.ds(pl.multiple_of(tt * SUBLANES + jnp.where(is_lat, LRU_GAP, 0), SUBLANES), SUBLANES)
        rb = pl.ds(pl.multiple_of(
            jnp.where(is_lat, LRU_LAT0 + SEQ + CTX_LEN, CTX_LEN) - (tt + 1) * SUBLANES, SUBLANES), SUBLANES)
        a, b = tile_scan(af_ref[rf, :], hf_ref[rf, :], False)
        h = a * hf + b
        hf_ref[rf, :] = h
        hf = jnp.broadcast_to(h[SUBLANES - 1:SUBLANES, :], (SUBLANES, w))
        a, b = tile_scan(ab_ref[rb, :], hb_ref[rb, :], True)
        h = a * hb + b
        hb_ref[rb, :] = h
        hb = jnp.broadcast_to(h[0:1, :], (SUBLANES, w))
        return hf, hb

    zero = jnp.zeros((SUBLANES, w), F32)
    lax.fori_loop(0, S_ALL // SUBLANES, group, (zero, zero))
    dt = rec_ref.dtype
    rec_ref[0:SEQ, :] = (hf_ref[LRU_LAT0:LRU_LAT0 + SEQ, :] + hb_ref[LRU_LAT0:LRU_LAT0 + SEQ, :]).astype(dt)
    rec_ref[SEQ:S_ALL, :] = (hf_ref[0:CTX_LEN, :] + hb_ref[0:CTX_LEN, :]).astype(dt)


def rglru(xr, conv_w, conv_b, w_gates, b_gates, lam):
    b = xr.shape[0]
    w = LRU_WIDTH
    full = lambda r, c_: pl.BlockSpec((r, c_), lambda i: (0, 0))
    rows = pltpu.VMEM((LRU_ROWS, w), F32)
    return pl.pallas_call(
        _lru_kernel,
        grid=(b,),
        in_specs=[pl.BlockSpec((None, S_ALL, w), lambda i: (i, 0, 0)),
                  full(LRU_CONV, w), full(1, w), full(w, 4 * w), full(1, 4 * w), full(2, w)],
        out_specs=pl.BlockSpec((None, S_ALL, w), lambda i: (i, 0, 0)),
        out_shape=jax.ShapeDtypeStruct((b, S_ALL, w), BF16),
        scratch_shapes=[pltpu.VMEM((LRU_ROWS + 2 * LRU_PAD_FRONT, w), F32), rows, rows, rows, rows],
        compiler_params=_params(("parallel",)),
        name="rglru",
    )(xr, conv_w, conv_b, w_gates, b_gates, lam)


def _attn_kernel(q_ref, k_ref, v_ref, *rest, heads):
    n_cast = (len(rest) - 1) // 2
    o_ref = rest[n_cast]
    cast_pairs = list(zip(rest[:n_cast], rest[n_cast + 1:]))
    tq, sk = q_ref.shape[0], k_ref.shape[0]
    q_lo = lax.broadcasted_iota(jnp.int32, (tq, LANES), 1) < HEAD_DIM
    v_lo = lax.broadcasted_iota(jnp.int32, (sk, LANES), 1) < HEAD_DIM

    def keep(x, is_lo, half):
        if half is None:
            return x
        return jnp.where(is_lo if half == 0 else jnp.logical_not(is_lo), x, jnp.zeros_like(x))

    for og, members in enumerate(heads):
        acc = None
        for qg, qh, kg, vg, vh in members:
            q = keep(q_ref[:, qg * LANES:(qg + 1) * LANES], q_lo, qh)
            s = lax.dot_general(q, k_ref[:, kg * LANES:(kg + 1) * LANES],
                                (((1,), (1,)), ((), ())), preferred_element_type=F32)
            p = jnp.exp(s - jnp.max(s, axis=-1, keepdims=True))
            l = jnp.sum(p, axis=-1, keepdims=True)
            v = keep(v_ref[:, vg * LANES:(vg + 1) * LANES], v_lo, vh)
            o = jnp.dot(p.astype(BF16), v, preferred_element_type=F32) / l
            acc = o if acc is None else acc + o
        o_ref[:, og * LANES:(og + 1) * LANES] = acc.astype(o_ref.dtype)
        for src, dst in cast_pairs:
            r0, r1 = (og * src.shape[0] // len(heads), (og + 1) * src.shape[0] // len(heads))
            dst[r0:r1, :] = src[r0:r1, :].astype(BF16)


def attention(q, k, v, heads, q_block, n_q_blocks, tq, k_rows, cast=()):
    b, s, qw = q.shape
    kw, vw = k.shape[-1], v.shape[-1]
    ow = len(heads) * LANES
    n_steps = b * n_q_blocks
    kv_spec = lambda width: pl.BlockSpec((None, k_rows, width), lambda i, t: (i, s // k_rows - 1, 0))
    slab = lambda w: pl.BlockSpec((w.shape[0] // n_steps, w.shape[1]), lambda i, t: (i * n_q_blocks + t, 0))
    assert all(w.shape[0] % (16 * len(heads) * n_steps) == 0 for w in cast)
    out = pl.pallas_call(
        functools.partial(_attn_kernel, heads=heads),
        grid=(b, n_q_blocks),
        in_specs=[pl.BlockSpec((None, tq, qw), lambda i, t: (i, q_block + t, 0)), kv_spec(kw), kv_spec(vw)]
        + [slab(w) for w in cast],
        out_specs=[pl.BlockSpec((None, tq, ow), lambda i, t: (i, t, 0))] + [slab(w) for w in cast],
        out_shape=[jax.ShapeDtypeStruct((b, n_q_blocks * tq, ow), BF16)]
        + [jax.ShapeDtypeStruct(w.shape, BF16) for w in cast],
        compiler_params=_params(("parallel", "parallel")),
        name="attention",
    )(q, k, v, *cast)
    return out if cast else out[0]


GQA_HEADS = tuple(((p, 0, 0, 0, 0), (p, 1, 0, 0, 1)) for p in range(GQA_Q_HEADS // 2))
MLA_HEAD_GROUPS = tuple(((2 * p, None, 2 * p, p, 0), (2 * p + 1, None, 2 * p + 1, p, 1))
                        for p in range(MLA_HEADS // 2))


def attention_latent(q, k, v, heads, tq, cast=()):
    return attention(q, k, v, heads, 0, SEQ // tq, tq, S_ALL, cast)


def attention_context(q, k, v, heads):
    return attention(q, k, v, heads, SEQ // CTX_LEN, 1, CTX_LEN, CTX_LEN)


def _merge0_kernel(rec_ref, gate_ref, att_l_ref, att_c_ref, x_ref, ctx_ref, mod_ref,
                   w_rec_ref, w_att_ref, x1_ref):
    tm = x_ref.shape[0]
    row0 = pl.program_id(1) * tm
    is_ctx = row0 >= SEQ
    rg = (rec_ref[...].astype(F32) * jax.nn.gelu(gate_ref[...].astype(F32))).astype(BF16)
    att = jnp.where(is_ctx, att_c_ref[...], att_l_ref[...])
    y = (jnp.dot(rg, w_rec_ref[...], preferred_element_type=F32)
         + jnp.dot(att, w_att_ref[...], preferred_element_type=F32))
    x1_ref[...] = jnp.where(is_ctx, ctx_ref[...], x_ref[...]) + _mod_rows(mod_ref, 2, row0, tm) * y


def merge0(rec, gate, att_l, att_c, x, ctx, modp, w_rec, w_att):
    b = x.shape[0]
    tm = CTX_LEN
    n_lat = SEQ // tm
    tok = lambda width: pl.BlockSpec((None, tm, width), lambda i, t: (i, t, 0))
    lat = lambda width: pl.BlockSpec((None, tm, width), lambda i, t: (i, jnp.minimum(t, n_lat - 1), 0))
    cx = lambda width: pl.BlockSpec((None, tm, width), lambda i, t: (i, 0, 0))
    full = lambda r, c_: pl.BlockSpec((r, c_), lambda i, t: (0, 0))
    return pl.pallas_call(
        _merge0_kernel,
        grid=(b, S_ALL // tm),
        in_specs=[tok(LRU_WIDTH), tok(LRU_WIDTH), lat(HALF_MIX), cx(HALF_MIX), lat(D_MODEL), cx(D_MODEL),
                  pl.BlockSpec((None, N_MOD, 2, 1, D_MODEL), lambda i, t: (i, 0, 0, 0, 0)),
                  full(LRU_WIDTH, D_MODEL), full(HALF_MIX, D_MODEL)],
        out_specs=tok(D_MODEL),
        out_shape=jax.ShapeDtypeStruct((b, S_ALL, D_MODEL), F32),
        compiler_params=_params(("parallel", "parallel")),
        name="merge0",
    )(rec, gate, att_l, att_c, x, ctx, modp, w_rec, w_att)


def _swiglu(x, w1_ref, w3_ref, w2_ref, act_ref):
    for c in range(w1_ref.shape[1] // FFN_TILE_F):
        sl = slice(c * FFN_TILE_F, (c + 1) * FFN_TILE_F)
        h1 = jnp.dot(x, w1_ref[:, sl], preferred_element_type=F32)
        h3 = jnp.dot(x, w3_ref[:, sl], preferred_element_type=F32)
        act_ref[:, sl] = (h1 * jax.nn.sigmoid(h1) * h3).astype(BF16)
    return jnp.dot(act_ref[...], w2_ref[...], preferred_element_type=F32)


def _accumulate(acc_ref, part, j):
    @pl.when(j == 0)
    def _():
        acc_ref[...] = part

    @pl.when(j > 0)
    def _():
        acc_ref[...] += part


def _ffn_kernel(x1_ref, mod_ref, g2_ref, w1_ref, w3_ref, w2_ref, o_ref, act_ref):
    tm = x1_ref.shape[0]
    row0 = pl.program_id(1) * tm
    x = _norm_mod(x1_ref[...], g2_ref[...], mod_ref, row0, 3, 4).astype(BF16)
    y = _swiglu(x, w1_ref, w3_ref, w2_ref, act_ref)
    o_ref[...] = x1_ref[...] + _mod_rows(mod_ref, 5, row0, tm) * y


def ffn_residual(x1, modp, g2, w1, w3, w2):
    b = x1.shape[0]
    tm = FFN_TILE_M
    f = w1.shape[1]
    assert f % FFN_TILE_F == 0
    tok = lambda: pl.BlockSpec((None, tm, D_MODEL), lambda i, t: (i, t, 0))
    resident = lambda a: pl.BlockSpec(a.shape, lambda i, t: (0, 0), pipeline_mode=pl.Buffered(1))
    return pl.pallas_call(
        _ffn_kernel,
        grid=(b, S_ALL // tm),
        in_specs=[tok(),
                  pl.BlockSpec((None, N_MOD, 2, 1, D_MODEL), lambda i, t: (i, 0, 0, 0, 0)),
                  pl.BlockSpec((1, D_MODEL), lambda i, t: (0, 0)),
                  resident(w1), resident(w3), resident(w2)],
        out_specs=tok(),
        out_shape=jax.ShapeDtypeStruct((b, S_ALL, D_MODEL), F32),
        scratch_shapes=[pltpu.VMEM((tm, f), BF16)],
        compiler_params=_params(("parallel", "parallel")),
        name="ffn",
    )(x1, modp, g2, w1, w3, w2)


def _moe_kernel(te_ref, nv_ref, x_ref, w1_ref, w3_ref, w2_ref, o_ref, xb_ref, acc_ref, act_ref):
    i, j = pl.program_id(0), pl.program_id(1)
    last = pl.num_programs(1) - 1
    valid = i < nv_ref[0]

    @pl.when(valid)
    def _():
        @pl.when(j == 0)
        def _():
            xb_ref[...] = x_ref[...].astype(BF16)

        _accumulate(acc_ref, _swiglu(xb_ref[...], w1_ref, w3_ref, w2_ref, act_ref), j)

        @pl.when(j == last)
        def _():
            o_ref[...] = acc_ref[...]

    @pl.when(jnp.logical_and(jnp.logical_not(valid), j == last))
    def _():
        o_ref[...] = jnp.zeros_like(o_ref)


def grouped_swiglu(xs, tile_expert, n_valid, w1, w3, w2, tm, tf):
    n_rows, d = xs.shape
    f = w1.shape[-1]
    assert n_rows % tm == 0 and f % tf == 0
    n_f = f // tf

    def col(i, j, nv):
        return jnp.where(i < nv[0], j, n_f - 1)

    grid_spec = pltpu.PrefetchScalarGridSpec(
        num_scalar_prefetch=2,
        grid=(n_rows // tm, n_f),
        in_specs=[pl.BlockSpec((tm, d), lambda i, j, te, nv: (jnp.minimum(i, nv[0] - 1), 0)),
                  pl.BlockSpec((None, d, tf), lambda i, j, te, nv: (te[i], 0, col(i, j, nv))),
                  pl.BlockSpec((None, d, tf), lambda i, j, te, nv: (te[i], 0, col(i, j, nv))),
                  pl.BlockSpec((None, tf, d), lambda i, j, te, nv: (te[i], col(i, j, nv), 0))],
        out_specs=pl.BlockSpec((tm, d), lambda i, j, te, nv: (i, 0)),
        scratch_shapes=[pltpu.VMEM((tm, d), BF16), pltpu.VMEM((tm, d), F32), pltpu.VMEM((tm, tf), BF16)],
    )
    return pl.pallas_call(
        _moe_kernel,
        grid_spec=grid_spec,
        out_shape=jax.ShapeDtypeStruct((n_rows, d), F32),
        compiler_params=_params(("parallel", "arbitrary")),
        name="moe",
    )(tile_expert, n_valid, xs, w1, w3, w2)


def route(logits, tm):
    n_tok = logits.shape[0]
    n_assign = n_tok * TOP_K
    idx = jnp.arange(N_EXPERTS, dtype=jnp.int32)[None, :]
    m1 = jnp.max(logits, axis=-1, keepdims=True)
    e1 = jnp.min(jnp.where(logits == m1, idx, N_EXPERTS), axis=-1, keepdims=True)
    rest = jnp.where(idx == e1, jnp.finfo(F32).min, logits)
    m2 = jnp.max(rest, axis=-1, keepdims=True)
    e2 = jnp.min(jnp.where(rest == m2, idx, N_EXPERTS), axis=-1, keepdims=True)
    t = jnp.exp(m2 - m1)
    gates = jnp.concatenate([1.0 / (1.0 + t), t / (1.0 + t)], axis=-1)
    flat_e = jnp.concatenate([e1, e2], axis=-1).reshape(n_assign, 1)
    onehot = (flat_e == idx).astype(F32)
    blk = MOE_ROUTE_BLOCK
    nb = n_assign // blk
    oh = onehot.reshape(nb, blk, N_EXPERTS)
    lower = lambda n: (jnp.arange(n)[:, None] > jnp.arange(n)[None, :]).astype(F32)
    within = jnp.einsum('ij,bjk->bik', lower(blk), oh)
    blk_cnt = jnp.sum(oh, axis=1)
    blk_off = jnp.dot(lower(nb), blk_cnt, precision=lax.Precision.HIGHEST)
    rank = jnp.sum((within + blk_off[:, None, :]) * oh, axis=-1).reshape(n_assign)
    counts = jnp.sum(blk_cnt, axis=0).astype(jnp.int32)
    padded = (counts + tm - 1) // tm * tm
    ends = []
    for e in range(N_EXPERTS):
        ends.append(padded[e] + (ends[-1] if ends else 0))
    pend = jnp.stack(ends)
    pstart = pend - padded
    dest = (jnp.sum(onehot * pstart.astype(F32)[None, :], axis=-1) + rank).astype(jnp.int32)
    n_tiles = n_assign // tm + N_EXPERTS
    starts = jnp.arange(n_tiles, dtype=jnp.int32) * tm
    tile_expert = jnp.minimum(jnp.sum((pend[None, :] <= starts[:, None]).astype(jnp.int32), axis=-1),
                              N_EXPERTS - 1)
    n_valid = (pend[-1] // tm).reshape(1)
    meta = jnp.concatenate([pend, padded, n_valid])
    return gates, dest, tile_expert, n_valid, meta


def _row_copies(dest_ref, src, dst, sem, rows, gather, side_work=()):
    n_groups = rows // SUBLANES
    for g in range(n_groups):
        for r in range(g * SUBLANES, (g + 1) * SUBLANES):
            for kk in range(TOP_K):
                d = dest_ref[0, TOP_K * r + kk]
                if gather:
                    copy = pltpu.make_async_copy(src.at[pl.ds(d, 1), :], dst.at[kk, pl.ds(r, 1), :], sem)
                else:
                    copy = pltpu.make_async_copy(src.at[pl.ds(r, 1), :], dst.at[pl.ds(d, 1), :], sem)
                copy.start(priority=kk % 2)
        for thunk in side_work[g * len(side_work) // n_groups:(g + 1) * len(side_work) // n_groups]:
            thunk()
    if gather:
        pltpu.make_async_copy(dst, dst, sem).wait()
    else:
        for _ in range(TOP_K):
            pltpu.make_async_copy(src, src, sem).wait()


def _dispatch_kernel(meta_ref, dest_ref, h_ref, w_ref, xs_ref, wb_ref, zero_ref, sem, zsem):
    tm_moe = zero_ref.shape[0]

    @pl.when(pl.program_id(0) == 0)
    def _():
        zero_ref[...] = jnp.zeros_like(zero_ref)
        n_tiles = xs_ref.shape[0] // tm_moe
        fills = []
        for e in range(N_EXPERTS):
            start = pl.multiple_of(jnp.maximum(meta_ref[e] - tm_moe, 0), tm_moe)
            fills.append((meta_ref[N_EXPERTS + e] > 0, start))
        for i in range(n_tiles - N_EXPERTS, n_tiles):
            fills.append((i >= meta_ref[2 * N_EXPERTS], i * tm_moe))
        for cond, start in fills:
            @pl.when(cond)
            def _():
                pltpu.make_async_copy(zero_ref, xs_ref.at[pl.ds(start, tm_moe), :], zsem).start()
        for cond, start in fills:
            @pl.when(cond)
            def _():
                pltpu.make_async_copy(zero_ref, xs_ref.at[pl.ds(start, tm_moe), :], zsem).wait()

    def cast_rows(c):
        def thunk():
            wb_ref[c * 16:(c + 1) * 16, :] = w_ref[c * 16:(c + 1) * 16, :].astype(BF16)
        return thunk

    casts = [cast_rows(c) for c in range(w_ref.shape[0] // 16)]
    _row_copies(dest_ref, h_ref, xs_ref, sem, h_ref.shape[0], gather=False, side_work=casts)


def moe_dispatch(h2, dest, meta, n_slots, w):
    n_tok, d = h2.shape
    tm = MOE_DISPATCH_TILE
    n_steps = n_tok // tm
    wf = w.reshape(-1, w.shape[-1])
    assert wf.shape[0] % (16 * n_steps) == 0
    slab = pl.BlockSpec((wf.shape[0] // n_steps, wf.shape[1]), lambda t, m: (t, 0))
    grid_spec = pltpu.PrefetchScalarGridSpec(
        num_scalar_prefetch=1,
        grid=(n_steps,),
        in_specs=[pl.BlockSpec((None, 1, TOP_K * tm), lambda t, m: (t, 0, 0), memory_space=pltpu.SMEM),
                  pl.BlockSpec((tm, d), lambda t, m: (t, 0)), slab],
        out_specs=[pl.BlockSpec(memory_space=pl.ANY), slab],
        scratch_shapes=[pltpu.VMEM((MOE_TILE_M, d), h2.dtype), pltpu.SemaphoreType.DMA,
                        pltpu.SemaphoreType.DMA],
    )
    xs, wb = pl.pallas_call(
        _dispatch_kernel,
        grid_spec=grid_spec,
        out_shape=[jax.ShapeDtypeStruct((n_slots, d), h2.dtype), jax.ShapeDtypeStruct(wf.shape, BF16)],
        compiler_params=_params(("arbitrary",)),
        name="moe_dispatch",
    )(meta, dest.reshape(n_steps, 1, TOP_K * tm), h2, wf)
    return xs, wb.reshape(w.shape)


def _combine_kernel(dest_ref, x1_ref, g_ref, mod_ref, ys_ref, o_ref, buf_ref, sem):
    tm = x1_ref.shape[0]
    _row_copies(dest_ref, ys_ref, buf_ref, sem, tm, gather=True)
    g = g_ref[...]
    moe = g[:, 0:1] * buf_ref[0] + g[:, 1:2] * buf_ref[1]
    o_ref[...] = x1_ref[...] + mod_ref[5, 0] * moe


def moe_combine(x1, gates, dest, ys, modp):
    b, s, d = x1.shape
    tm = MOE_COMBINE_TILE
    nt = s // tm
    return pl.pallas_call(
        _combine_kernel,
        grid=(b, nt),
        in_specs=[pl.BlockSpec((None, 1, TOP_K * tm), lambda i, t: (i * nt + t, 0, 0), memory_space=pltpu.SMEM),
                  pl.BlockSpec((None, tm, d), lambda i, t: (i, t, 0)),
                  pl.BlockSpec((None, tm, TOP_K), lambda i, t: (i, t, 0)),
                  pl.BlockSpec((None, N_MOD, 2, 1, d), lambda i, t: (i, 0, 0, 0, 0)),
                  pl.BlockSpec(memory_space=pl.ANY)],
        out_specs=pl.BlockSpec((None, tm, d), lambda i, t: (i, t, 0)),
        out_shape=jax.ShapeDtypeStruct((b, s, d), F32),
        scratch_shapes=[pltpu.VMEM((TOP_K, tm, d), F32), pltpu.SemaphoreType.DMA],
        compiler_params=_params(("arbitrary", "arbitrary")),
        name="moe_combine",
    )(dest.reshape(b * nt, 1, TOP_K * tm), x1, gates.reshape(b, s, TOP_K), modp, ys)


def moe_residual(x1, h2, logits, modp, w1, w3, w2):
    b, s, d = x1.shape
    n_tok = b * s
    tm = MOE_TILE_M
    gates, dest, tile_expert, n_valid, meta = route(logits.reshape(n_tok, N_EXPERTS), tm)
    n_slots = (n_tok * TOP_K // tm + N_EXPERTS) * tm
    xs, w2 = moe_dispatch(h2.reshape(n_tok, d), dest, meta, n_slots, w2)
    ys = grouped_swiglu(xs, tile_expert, n_valid, w1, w3, w2, tm, MOE_TILE_F)
    return moe_combine(x1, gates, dest, ys, modp)


def _proj1_kernel(x_ref, g_ref, mod_ref, w_ref, qan_ref, qb_ref, qbs_ref, kvan_ref, kbk_ref, kbv_ref,
                  cos_ref, sin_ref, gains_ref, ones_ref, hy_ref, q_ref, k_ref, v_ref):
    tm = x_ref.shape[0]
    rs = tm // PROJ1_ROW_SPLIT
    c_q, c_kv = HY_IN, HY_IN + MLA_Q_RANK
    c_r = c_kv + MLA_KV_RANK
    first_half = lax.broadcasted_iota(jnp.int32, (rs, LANES), 1) < MLA_NOPE + MLA_ROPE // 2
    sh = MLA_ROPE // 2
    qg, qg_sw, kg, kg_sw = (gains_ref[i:i + 1, :] for i in range(4))

    def rms(a, g):
        return a * lax.rsqrt(jnp.mean(a * a, axis=-1, keepdims=True) + EPS) * g

    for r0 in range(0, tm, rs):
        rows = slice(r0, r0 + rs)
        h = _norm_mod(x_ref[rows, :], g_ref[...], mod_ref, pl.program_id(1) * tm + r0, 0, 1)
        z = jnp.dot(h.astype(BF16), w_ref[...], preferred_element_type=F32)
        hy_ref[rows, :] = z[:, :HY_IN]
        qa = rms(z[:, c_q:c_kv], qan_ref[...]).astype(BF16)
        kva = rms(z[:, c_kv:c_r], kvan_ref[...]).astype(BF16)
        q = jnp.dot(qa, qb_ref[...], preferred_element_type=F32)
        q_sw = jnp.dot(qa, qbs_ref[...], preferred_element_type=F32)
        kk = jnp.dot(kva, kbk_ref[...], preferred_element_type=F32)
        v_ref[rows, :] = jnp.dot(kva, kbv_ref[...], preferred_element_type=F32).astype(BF16)
        k_rope = pltpu.roll(z[:, c_r:c_r + LANES], MLA_NOPE, 1)
        k_rope_sw = jnp.where(first_half, pltpu.roll(k_rope, LANES - sh, 1), pltpu.roll(k_rope, sh, 1))
        cos, sin = cos_ref[rows, :], sin_ref[rows, :]

        def head(a, a_sw, gain, gain_sw):
            inv = lax.rsqrt(_group_sums(a, ones_ref) / MLA_QK + EPS)
            return (a * (inv * gain) * cos + a_sw * (inv * gain_sw) * sin).astype(BF16)

        for hh in range(MLA_HEADS):
            sl = slice(hh * LANES, (hh + 1) * LANES)
            q_ref[rows, sl] = head(q[:, sl], q_sw[:, sl], qg, qg_sw)
            k_ref[rows, sl] = head(kk[:, sl] + k_rope, k_rope_sw, kg, kg_sw)


def _swap_rope_halves(a):
    lead = a.shape[:-1]
    g = a.reshape(*lead, -1, LANES)
    lo, mid = MLA_NOPE, MLA_NOPE + MLA_ROPE // 2
    g = jnp.concatenate([g[..., :lo], g[..., mid:MLA_QK], g[..., lo:mid], g[..., MLA_QK:]], axis=-1)
    return g.reshape(a.shape)


def proj1(xa, g, modp, w_in, qan, qb, kvan, kbk, kbv, cos, sin, q_gain, k_gain):
    b = xa.shape[0]
    tm = TOKEN_TILE
    tok = lambda width: pl.BlockSpec((None, tm, width), lambda i, t: (i, t, 0))
    full = lambda a: pl.BlockSpec(a.shape, lambda i, t: (0, 0))
    hw = MLA_HEADS * LANES
    qbs = _swap_rope_halves(qb)
    gains = jnp.concatenate([q_gain, _swap_rope_halves(q_gain), k_gain, _swap_rope_halves(k_gain)], axis=0)
    ones = jnp.ones((LANES, LANES), BF16)
    return pl.pallas_call(
        _proj1_kernel,
        grid=(b, S_ALL // tm),
        in_specs=[tok(D_MODEL), full(g),
                  pl.BlockSpec((None, N_MOD, 2, 1, D_MODEL), lambda i, t: (i, 0, 0, 0, 0)),
                  full(w_in), full(qan), full(qb), full(qbs), full(kvan), full(kbk), full(kbv),
                  pl.BlockSpec((tm, LANES), lambda i, t: (t, 0)),
                  pl.BlockSpec((tm, LANES), lambda i, t: (t, 0)),
                  full(gains), full(ones)],
        out_specs=[tok(HY_IN), tok(hw), tok(hw), tok(MLA_HEADS * MLA_V)],
        out_shape=[jax.ShapeDtypeStruct((b, S_ALL, HY_IN), F32),
                   jax.ShapeDtypeStruct((b, S_ALL, hw), BF16),
                   jax.ShapeDtypeStruct((b, S_ALL, hw), BF16),
                   jax.ShapeDtypeStruct((b, S_ALL, MLA_HEADS * MLA_V), BF16)],
        compiler_params=_params(("parallel", "parallel")),
        name="proj1",
    )(xa, g, modp, w_in, qan, qb, qbs, kvan, kbk, kbv, cos, sin, gains, ones)


def _merge1_kernel(hy_ref, att_ref, x_ref, mod_ref, g2_ref, w_hy_ref, w_att_ref, r_ref,
                   x1_ref, h2_ref, lg_ref):
    y = (jnp.dot(hy_ref[...].astype(BF16), w_hy_ref[...], preferred_element_type=F32)
         + jnp.dot(att_ref[...], w_att_ref[...], preferred_element_type=F32))
    x1 = x_ref[...] + mod_ref[2, 0] * y
    x1_ref[...] = x1
    h2 = _norm_mod(x1, g2_ref[...], mod_ref, 0, 3, 4)
    h2_ref[...] = h2
    lane = lax.broadcasted_iota(jnp.int32, lg_ref.shape, 1)
    logits = jnp.zeros(lg_ref.shape, F32)
    for e in range(N_EXPERTS):
        col = jnp.sum(h2 * r_ref[e:e + 1, :], axis=-1, keepdims=True)
        logits = jnp.where(lane == e, col, logits)
    lg_ref[...] = logits


def merge1(hy, att, xa, modp, g2, w_hy, w_att, router):
    b = xa.shape[0]
    tm = MERGE1_TILE
    tok = lambda width: pl.BlockSpec((None, tm, width), lambda i, t: (i, t, 0))
    full = lambda a: pl.BlockSpec(a.shape, lambda i, t: (0, 0))
    return pl.pallas_call(
        _merge1_kernel,
        grid=(b, SEQ // tm),
        in_specs=[tok(HY_WIDTH), tok(HALF_MIX), tok(D_MODEL),
                  pl.BlockSpec((None, N_MOD, 2, 1, D_MODEL), lambda i, t: (i, 0, 0, 0, 0)),
                  full(g2), full(w_hy), full(w_att), full(router)],
        out_specs=[tok(D_MODEL), tok(D_MODEL), tok(N_EXPERTS)],
        out_shape=[jax.ShapeDtypeStruct((b, SEQ, D_MODEL), F32),
                   jax.ShapeDtypeStruct((b, SEQ, D_MODEL), F32),
                   jax.ShapeDtypeStruct((b, SEQ, N_EXPERTS), F32)],
        compiler_params=_params(("parallel", "parallel")),
        name="merge1",
    )(hy, att, xa, modp, g2, w_hy, w_att, router)


HY_N = 2 * SEQ
HY_TW_ROWS = 256
HY_SPEC_FBLK = 256
HY_CONV_FBLK = 512
HY_LANE_TILES = SEQ // LANES


def _twiddle_kernel(ca_ref, sa_ref, cb_ref, sb_ref, fre_ref, fim_ref, ic_ref, is_ref):
    rows = fre_ref.shape[0]
    r = pl.program_id(0) * rows + lax.broadcasted_iota(jnp.int32, (rows, LANES), 0)
    alt_r = (1 - 2 * (r & 1)).astype(F32)
    cb, sb = cb_ref[...], sb_ref[...]
    for a in range(HY_LANE_TILES):
        ca, sa = ca_ref[:, a:a + 1], sa_ref[:, a:a + 1]
        c = ca * cb - sa * sb
        s = sa * cb + ca * sb
        col = a * LANES + lax.broadcasted_iota(jnp.int32, (rows, LANES), 1)
        alt_c = (1 - 2 * (col & 1)).astype(F32)
        w = jnp.where(col == 0, 1.0 / HY_N, 2.0 / HY_N)
        sl = slice(a * LANES, (a + 1) * LANES)
        fre_ref[:, sl] = c.astype(BF16)
        fim_ref[:, sl] = jnp.where(r == 0, alt_c, -s).astype(BF16)
        ic_ref[:, sl] = (c * w).astype(BF16)
        is_ref[:, sl] = jnp.where(col == 0, alt_r / HY_N, -s * w).astype(BF16)


def dft_matrices():
    idx = jnp.arange(SEQ, dtype=jnp.int32)[:, None]
    step = 2.0 * math.pi / HY_N
    ph_a = ((idx * (LANES * jnp.arange(HY_LANE_TILES, dtype=jnp.int32))[None, :]) % HY_N).astype(F32) * step
    ph_b = ((idx * jnp.arange(LANES, dtype=jnp.int32)[None, :]) % HY_N).astype(F32) * step
    rows = HY_TW_ROWS
    tab = lambda width: pl.BlockSpec((rows, width), lambda i: (i, 0))
    out = jax.ShapeDtypeStruct((SEQ, SEQ), BF16)
    return pl.pallas_call(
        _twiddle_kernel,
        grid=(SEQ // rows,),
        in_specs=[tab(HY_LANE_TILES), tab(HY_LANE_TILES), tab(LANES), tab(LANES)],
        out_specs=[tab(SEQ)] * 4,
        out_shape=[out] * 4,
        compiler_params=_params(("parallel",)),
        name="twiddle",
    )(jnp.cos(ph_a), jnp.sin(ph_a), jnp.cos(ph_b), jnp.sin(ph_b))


def _spec_kernel(fre_ref, fim_ref, h_ref, k_ref):
    w = HY_WIDTH
    h = h_ref[...].astype(BF16)
    re = jnp.dot(fre_ref[...], h, preferred_element_type=F32)
    im = jnp.dot(fim_ref[...], h, preferred_element_type=F32)
    hb0 = h[0:1, w:].astype(F32)
    first = (pl.program_id(1) * re.shape[0] + lax.broadcasted_iota(jnp.int32, (re.shape[0], 1), 0)) == 0
    k_ref[0] = re[:, :w] + re[:, w:] - hb0
    k_ref[1] = jnp.where(first, im[:, :w] + im[:, w:] - hb0, im[:, :w] - im[:, w:])


def filter_spectra(filt, fre, fim):
    fb = HY_SPEC_FBLK
    return pl.pallas_call(
        _spec_kernel,
        grid=(HY_ORDER, SEQ // fb),
        in_specs=[pl.BlockSpec((fb, SEQ), lambda o, f: (f, 0)),
                  pl.BlockSpec((fb, SEQ), lambda o, f: (f, 0)),
                  pl.BlockSpec((SEQ, 2 * HY_WIDTH), lambda o, f: (0, o))],
        out_specs=pl.BlockSpec((None, 2, fb, HY_WIDTH), lambda o, f: (o, 0, f, 0)),
        out_shape=jax.ShapeDtypeStruct((HY_ORDER, 2, SEQ, HY_WIDTH), F32),
        compiler_params=_params(("parallel", "parallel")),
        name="filter_spectra",
    )(fre, fim, filt)


def _short_conv(z_ref, w_ref, b_ref, part):
    z = z_ref[...]
    l = z.shape[0]
    t = lax.broadcasted_iota(jnp.int32, (l, 1), 0)
    prev = jnp.where(t == 0, 0.0, pltpu.roll(z, 1, 0))
    nxt = jnp.where(t == l - 1, 0.0, pltpu.roll(z, l - 1, 0))
    cs = slice(part * HY_WIDTH, (part + 1) * HY_WIDTH)
    return (w_ref[0:1, cs] * prev + w_ref[1:2, cs] * z + w_ref[2:3, cs] * nxt) + b_ref[:, cs]


def _hyconv_kernel(u_ref, xg_ref, cw_ref, cb_ref, fre_ref, fim_ref, ic_ref, is_ref, k_ref, skip_ref,
                   y_ref, ub_ref, acc_ref, *, order):
    f = pl.program_id(1)

    def u_f32():
        return _short_conv(u_ref, cw_ref, cb_ref, 0) if order == 0 else u_ref[...]

    @pl.when(f == 0)
    def _():
        ub_ref[...] = u_f32().astype(BF16)

    x_re = jnp.dot(fre_ref[...], ub_ref[...], preferred_element_type=F32)
    x_im = jnp.dot(fim_ref[...], ub_ref[...], preferred_element_type=F32)
    k_re, k_im = k_ref[0], k_ref[1]
    first = (f * x_re.shape[0] + lax.broadcasted_iota(jnp.int32, (x_re.shape[0], 1), 0)) == 0
    y_re = x_re * k_re - jnp.where(first, 0.0, x_im * k_im)
    y_im = x_im * jnp.where(first, k_im, k_re) + jnp.where(first, 0.0, x_re * k_im)
    part = (jnp.dot(ic_ref[...], y_re.astype(BF16), preferred_element_type=F32)
            + jnp.dot(is_ref[...], y_im.astype(BF16), preferred_element_type=F32))
    _accumulate(acc_ref, part, f)

    @pl.when(f == pl.num_programs(1) - 1)
    def _():
        xg = _short_conv(xg_ref, cw_ref, cb_ref, order + 1)
        y_ref[...] = xg * (acc_ref[...] + skip_ref[order:order + 1, :] * u_f32())


def hyena_conv(order, u, hyz, conv_w, conv_b, fre, fim, ic, is_, spectra, skip):
    b = hyz.shape[0]
    fb = HY_CONV_FBLK
    w = HY_WIDTH
    lat = lambda part: pl.BlockSpec((None, SEQ, w), lambda i, f: (i, 0, part))
    full = lambda a: pl.BlockSpec(a.shape, lambda i, f: (0, 0))
    return pl.pallas_call(
        functools.partial(_hyconv_kernel, order=order),
        grid=(b, SEQ // fb),
        in_specs=[lat(0), lat(order + 1), full(conv_w), full(conv_b),
                  pl.BlockSpec((fb, SEQ), lambda i, f: (f, 0)),
                  pl.BlockSpec((fb, SEQ), lambda i, f: (f, 0)),
                  pl.BlockSpec((SEQ, fb), lambda i, f: (0, f)),
                  pl.BlockSpec((SEQ, fb), lambda i, f: (0, f)),
                  pl.BlockSpec((None, 2, fb, w), lambda i, f: (order, 0, f, 0)),
                  full(skip)],
        out_specs=pl.BlockSpec((None, SEQ, w), lambda i, f: (i, 0, 0)),
        out_shape=jax.ShapeDtypeStruct((b, SEQ, w), F32),
        scratch_shapes=[pltpu.VMEM((SEQ, w), BF16), pltpu.VMEM((SEQ, w), F32)],
        compiler_params=_params(("parallel", "arbitrary")),
        name="hyena_conv",
    )(u, hyz, conv_w, conv_b, fre, fim, ic, is_, spectra, skip)


def hyena_filters(length, w1, b1, w2, b2, w3, freq):
    hp = lax.Precision.HIGHEST
    t = jnp.arange(length, dtype=F32)[:, None]
    t_norm = t / max(length - 1, 1)
    bands = jnp.linspace(1e-4, HY_BANDS - 1, HY_BANDS, dtype=F32)
    ang = 2.0 * math.pi * t * bands / length
    z = jnp.concatenate([t_norm, jnp.cos(ang), -jnp.sin(ang)], axis=-1)
    h = jnp.sin(freq * (jnp.dot(z, w1, precision=hp) + b1))
    h = jnp.sin(freq * (jnp.dot(h, w2, precision=hp) + b2))
    h = jnp.dot(h, w3, precision=hp)
    deltas = jnp.abs(jnp.linspace(HY_MIN_DECAY, HY_MAX_DECAY, HY_WIDTH, dtype=F32))
    window = jnp.exp(-t_norm * deltas) + HY_SHIFT
    return h * jnp.tile(window, (1, HY_ORDER * 2))


def hyena(hyz, conv_w, conv_b, fw1, fb1, fw2, fb2, fw3, freq, skip):
    fre, fim, ic, is_ = dft_matrices()
    filt = hyena_filters(SEQ, fw1, fb1, fw2, fb2, fw3, freq)
    spectra = filter_spectra(filt, fre, fim)
    y = hyz
    for o in range(HY_ORDER):
        y = hyena_conv(o, y, hyz, conv_w, conv_b.reshape(1, -1), fre, fim, ic, is_, spectra, skip)
    return y


def _grid_angles(rot_dim):
    n_freq = rot_dim // 4
    inv_freq = ROPE_THETA ** (-jnp.arange(n_freq, dtype=F32) / n_freq)
    t = jnp.arange(SEQ)
    r = (t // GRID_W).astype(F32)
    c_ = (t % GRID_W).astype(F32)
    return jnp.concatenate([r[:, None] * inv_freq, c_[:, None] * inv_freq], axis=-1)


def _rope_tables(rot_dim, lane_offsets):
    ang = _grid_angles(rot_dim)
    c, s = jnp.cos(ang), jnp.sin(ang)
    cos_parts, sin_parts, lane = [], [], 0
    for off in lane_offsets:
        cos_parts += [jnp.ones((SEQ, off - lane), F32), c, c]
        sin_parts += [jnp.zeros((SEQ, off - lane), F32), -s, s]
        lane = off + rot_dim
    cos_parts.append(jnp.ones((SEQ, LANES - lane), F32))
    sin_parts.append(jnp.zeros((SEQ, LANES - lane), F32))
    cos = jnp.concatenate([jnp.concatenate(cos_parts, axis=1), jnp.ones((CTX_LEN, LANES), F32)], axis=0)
    sin = jnp.concatenate([jnp.concatenate(sin_parts, axis=1), jnp.zeros((CTX_LEN, LANES), F32)], axis=0)
    return cos, sin


def _block_diag(w):
    nb, bs, _ = w.shape
    eye = jnp.eye(nb, dtype=w.dtype)
    return (eye[:, None, :, None] * w[:, :, None, :]).reshape(nb * bs, nb * bs)


def _gqa_pair_order():
    g = GQA_Q_HEADS // GQA_KV_HEADS
    heads = []
    for p in range(g):
        heads += [p, g + p]
    return np.concatenate([np.arange(h * HEAD_DIM, (h + 1) * HEAD_DIM) for h in heads])


def _pad_heads(w, n_heads, width):
    k = w.shape[0]
    w = w.reshape(k, n_heads, width)
    return jnp.pad(w, ((0, 0), (0, 0), (0, LANES - width))).reshape(k, n_heads * LANES)


def kernel(x, c, ctx, c_ctx, mod_w, mod_b, norm1_g, norm2_g, ab_w_in, ab_w_out, lru_conv_w, lru_conv_b, lru_w_a, lru_b_a, lru_w_x, lru_b_x, lru_lambda, gqa_q_norm, gqa_k_norm, ffn_w1, ffn_w3, ffn_w2, cd_w_in, cd_w_out, hy_conv_w, hy_conv_b, hy_filt_w1, hy_filt_b1, hy_filt_w2, hy_filt_b2, hy_filt_w3, hy_sin_freq, hy_skip, mla_q_a_norm, mla_q_b, mla_kv_a_norm, mla_kv_b, mla_q_norm, mla_k_norm, moe_router, moe_w1, moe_w3, moe_w2):
    batch = x.shape[0]
    bf = lambda w: w.astype(BF16)
    row = lambda v: v.reshape(1, -1)

    silu_all = jnp.concatenate([jax.nn.silu(c), jax.nn.silu(c_ctx)[None, :],
                                jnp.zeros((16 - batch - 1, D_MODEL), F32)], axis=0)
    mods = []
    mod_all = stacked_matmul(silu_all, mod_w, 1536) + mod_b[:, None, :]
    for layer in range(DEPTH):
        m = mod_all[layer]
        lat = m[:batch].reshape(batch, N_MOD, 1, D_MODEL)
        cx = jnp.broadcast_to(m[batch].reshape(1, N_MOD, 1, D_MODEL), lat.shape)
        mods.append(jnp.stack([lat, cx], axis=2))

    perm = _gqa_pair_order()
    q0 = 2 * LRU_WIDTH
    w_in0 = ab_w_in[0]
    w_in0 = jnp.concatenate([w_in0[:, :q0], w_in0[:, q0 + perm], w_in0[:, q0 + GQA_Q_HEADS * HEAD_DIM:]], axis=1)
    cos_g, sin_g = _rope_tables(HEAD_DIM, (0, HEAD_DIM))
    q_gain = row(jnp.tile(gqa_q_norm[0], 2) * HEAD_DIM ** -0.5)
    k_gain = row(jnp.tile(gqa_k_norm[0], 2))
    xr, gate, q, k, v = proj0(x, ctx, row(norm1_g[0]), mods[0], bf(w_in0), cos_g, sin_g, q_gain, k_gain)

    w_gates = jnp.concatenate([_block_diag(lru_w_a[0, 0]), _block_diag(lru_w_x[0, 0]),
                               _block_diag(lru_w_a[0, 1]), _block_diag(lru_w_x[0, 1])], axis=1)
    b_gates = jnp.concatenate([lru_b_a[0, 0].reshape(-1), lru_b_x[0, 0].reshape(-1),
                               lru_b_a[0, 1].reshape(-1), lru_b_x[0, 1].reshape(-1)])
    rec = rglru(xr, lru_conv_w[0], row(lru_conv_b[0]), bf(w_gates), row(b_gates), lru_lambda[0])

    att_l, moe_w1b, moe_w3b = attention_latent(
        q, k, v, GQA_HEADS, GQA_TILE_Q,
        cast=(moe_w1[0].reshape(-1, moe_w1.shape[-1]), moe_w3[0].reshape(-1, moe_w3.shape[-1])))
    att_c = attention_context(q, k, v, GQA_HEADS)
    w_out0 = ab_w_out[0]
    x1 = merge0(rec, gate, att_l, att_c, x, ctx, mods[0], bf(w_out0[:LRU_WIDTH]), bf(w_out0[LRU_WIDTH:][perm]))
    xa = ffn_residual(x1, mods[0], row(norm2_g[0]), bf(ffn_w1[0]), bf(ffn_w3[0]), bf(ffn_w2[0]))

    w_in1 = jnp.pad(cd_w_in[0], ((0, 0), (0, LANES - MLA_ROPE)))
    cos_m, sin_m = _rope_tables(MLA_ROPE, (MLA_NOPE,))
    qb = _pad_heads(mla_q_b[0], MLA_HEADS, MLA_QK)
    kvb = mla_kv_b[0].reshape(MLA_KV_RANK, MLA_HEADS, MLA_NOPE + MLA_V)
    kbk = _pad_heads(kvb[:, :, :MLA_NOPE].reshape(MLA_KV_RANK, -1), MLA_HEADS, MLA_NOPE)
    kbv = kvb[:, :, MLA_NOPE:].reshape(MLA_KV_RANK, -1)
    pad_gain = lambda g_: row(jnp.pad(g_, (0, LANES - MLA_QK)))
    hyz, q, k, v = proj1(xa, row(norm1_g[1]), mods[1], bf(w_in1), row(mla_q_a_norm[0]), bf(qb),
                         row(mla_kv_a_norm[0]), bf(kbk), bf(kbv), cos_m, sin_m,
                         pad_gain(mla_q_norm[0] * MLA_QK ** -0.5), pad_gain(mla_k_norm[0]))
    att = attention_latent(q, k, v, MLA_HEAD_GROUPS, MLA_TILE_Q)
    hy = hyena(hyz, hy_conv_w[0], hy_conv_b[0], hy_filt_w1[0], hy_filt_b1[0], hy_filt_w2[0],
               hy_filt_b2[0], hy_filt_w3[0], hy_sin_freq[0], hy_skip[0])
    w_out1 = cd_w_out[0]
    x1, h2, logits = merge1(hy, att, xa, mods[1], row(norm2_g[1]),
                            bf(w_out1[:HY_WIDTH]), bf(w_out1[HY_WIDTH:]), moe_router[0].T)
    return moe_residual(x1, h2, logits, mods[1], moe_w1b.reshape(moe_w1.shape[1:]),
                        moe_w3b.reshape(moe_w3.shape[1:]), moe_w2[0])
```

```python
import functools
import math

import jax
import jax.numpy as jnp
import numpy as np
from jax import lax
from jax.experimental import pallas as pl
from jax.experimental.pallas import tpu as pltpu

F32 = jnp.float32
BF16 = jnp.bfloat16

D_MODEL = 1024
SEQ = 2048
CTX_LEN = 256
S_ALL = SEQ + CTX_LEN
DEPTH = 2
GRID_W = 64
HEAD_DIM = 64
HALF_MIX = D_MODEL // 2
ROPE_THETA = 10000.0
EPS = 1e-6
N_MOD = 6
LRU_WIDTH = HALF_MIX
LRU_BLOCKS = LRU_WIDTH // HEAD_DIM
LRU_CONV = 4
LRU_C = 8.0
GQA_Q_HEADS = HALF_MIX // HEAD_DIM
GQA_KV_HEADS = 2
HY_WIDTH = HALF_MIX
HY_ORDER = 2
HY_CONV = 3
HY_BANDS = 16
HY_TARGET = 1e-2
HY_FAST_PCT = 0.3
HY_SLOW_PCT = 1.5
HY_MIN_DECAY = math.log(HY_TARGET) / HY_SLOW_PCT
HY_MAX_DECAY = math.log(HY_TARGET) / HY_FAST_PCT
HY_SHIFT = 0.05
HY_IN = (HY_ORDER + 1) * HY_WIDTH
MLA_HEADS = HALF_MIX // HEAD_DIM
MLA_Q_RANK = D_MODEL // 4
MLA_KV_RANK = D_MODEL // 8
MLA_NOPE = HEAD_DIM
MLA_ROPE = HEAD_DIM // 2
MLA_V = HEAD_DIM
MLA_QK = MLA_NOPE + MLA_ROPE
N_EXPERTS = 8
TOP_K = 2

LANES = 128
SUBLANES = 8
V7X_VMEM_LIMIT_BYTES = 56 * 1024 * 1024

TOKEN_TILE = 768
PROJ0_ROW_SPLIT = 2
PROJ1_ROW_SPLIT = 1
FFN_TILE_M = 768
FFN_TILE_F = 256
GQA_TILE_Q = 512
MLA_TILE_Q = 512
MERGE1_TILE = 1024
MOE_TILE_M = 512
MOE_TILE_F = 1792
MOE_ROUTE_BLOCK = 256
MOE_DISPATCH_TILE = 1024
MOE_COMBINE_TILE = 1024

LRU_GAP = SUBLANES
LRU_LAT0 = CTX_LEN + LRU_GAP
LRU_CHUNK = 464
LRU_ROWS = 2320
LRU_PAD_FRONT = SUBLANES

assert S_ALL % TOKEN_TILE == 0 and S_ALL % FFN_TILE_M == 0
assert SEQ % GQA_TILE_Q == 0 and SEQ % MLA_TILE_Q == 0
assert LRU_ROWS % LRU_CHUNK == 0 and LRU_ROWS >= LRU_LAT0 + SEQ + SUBLANES


def _params(sem):
    return pltpu.CompilerParams(dimension_semantics=sem, vmem_limit_bytes=V7X_VMEM_LIMIT_BYTES)


def _norm_mod(x, g, mod_pair, row0, shift_idx, scale_idx):
    y = x * lax.rsqrt(jnp.mean(x * x, axis=-1, keepdims=True) + EPS) * g
    shift = _mod_rows(mod_pair, shift_idx, row0, x.shape[0])
    scale = _mod_rows(mod_pair, scale_idx, row0, x.shape[0])
    return y * (1.0 + scale) + shift


def _mod_rows(mod_pair, idx, row0, rows):
    row = row0 + lax.broadcasted_iota(jnp.int32, (rows, 1), 0)
    return jnp.where(row >= SEQ, mod_pair[idx, 1], mod_pair[idx, 0])


def _mm_kernel(x_ref, w_ref, o_ref):
    o_ref[...] = jnp.dot(x_ref[...].astype(BF16), w_ref[...].astype(BF16),
                         preferred_element_type=F32)


def stacked_matmul(x, w, tn):
    m, k = x.shape
    n_l, _, n = w.shape
    assert n % tn == 0
    return pl.pallas_call(
        _mm_kernel,
        grid=(n_l, n // tn),
        in_specs=[pl.BlockSpec((m, k), lambda l, j: (0, 0)),
                  pl.BlockSpec((None, k, tn), lambda l, j: (l, 0, j))],
        out_specs=pl.BlockSpec((None, m, tn), lambda l, j: (l, 0, j)),
        out_shape=jax.ShapeDtypeStruct((n_l, m, n), F32),
        compiler_params=_params(("parallel", "parallel")),
        name="matmul",
    )(x, w)


def _rope_pair(a, cos, sin, first_half, shift):
    rot = jnp.where(first_half, pltpu.roll(a, LANES - shift, 1), pltpu.roll(a, shift, 1))
    return a * cos + rot * sin


def _group_sums(a, ones_ref):
    return jnp.dot((a * a).astype(BF16), ones_ref[...], preferred_element_type=F32)


def _proj0_kernel(x_ref, xt_ref, ctx_ref, g_ref, mod_ref, w_ref, cos_ref, sin_ref, qg_ref, kg_ref, ones_ref,
                  xr_ref, gate_ref, q_ref, k_ref, v_ref):
    tm = x_ref.shape[0]
    t = pl.program_id(1)
    last = jnp.concatenate([xt_ref[...], ctx_ref[...]], axis=0)
    rs = tm // PROJ0_ROW_SPLIT
    first_half = (lax.broadcasted_iota(jnp.int32, (rs, LANES), 1) & (HEAD_DIM // 2)) == 0
    q0 = 2 * LRU_WIDTH
    k0 = q0 + GQA_Q_HEADS * HEAD_DIM
    for r0 in range(0, tm, rs):
        rows = slice(r0, r0 + rs)
        x = jnp.where(t < SEQ // tm, x_ref[rows, :], last[rows])
        h = _norm_mod(x, g_ref[...], mod_ref, t * tm + r0, 0, 1)
        z = jnp.dot(h.astype(BF16), w_ref[...], preferred_element_type=F32)
        xr_ref[rows, :] = z[:, :LRU_WIDTH]
        gate_ref[rows, :] = z[:, LRU_WIDTH:2 * LRU_WIDTH].astype(BF16)
        cos, sin = cos_ref[rows, :], sin_ref[rows, :]

        def head_pair(a, gain):
            inv = lax.rsqrt(_group_sums(a, ones_ref) / HEAD_DIM + EPS)
            return _rope_pair(a * inv * gain, cos, sin, first_half, HEAD_DIM // 2).astype(BF16)

        for p in range(GQA_Q_HEADS // 2):
            q_ref[rows, p * LANES:(p + 1) * LANES] = head_pair(
                z[:, q0 + p * LANES:q0 + (p + 1) * LANES], qg_ref[...])
        k_ref[rows, :] = head_pair(z[:, k0:k0 + LANES], kg_ref[...])
        v_ref[rows, :] = z[:, k0 + LANES:k0 + 2 * LANES].astype(BF16)


def proj0(x, ctx, g, modp, w_in, cos, sin, q_gain, k_gain):
    b = x.shape[0]
    tm = TOKEN_TILE
    n = w_in.shape[1]
    n_full = SEQ // tm
    tail = SEQ - n_full * tm
    assert tail + CTX_LEN == tm and SEQ % tail == 0
    tok = lambda width: pl.BlockSpec((None, tm, width), lambda i, t: (i, t, 0))
    full = lambda r, c_: pl.BlockSpec((r, c_), lambda i, t: (0, 0))
    half = jnp.arange(LANES) // HEAD_DIM
    ones_pair = (half[:, None] == half[None, :]).astype(BF16)
    return pl.pallas_call(
        _proj0_kernel,
        grid=(b, S_ALL // tm),
        in_specs=[pl.BlockSpec((None, tm, D_MODEL), lambda i, t: (i, jnp.minimum(t, n_full - 1), 0)),
                  pl.BlockSpec((None, tail, D_MODEL), lambda i, t: (i, SEQ // tail - 1, 0)),
                  pl.BlockSpec((None, CTX_LEN, D_MODEL), lambda i, t: (i, 0, 0)),
                  full(1, D_MODEL),
                  pl.BlockSpec((None, N_MOD, 2, 1, D_MODEL), lambda i, t: (i, 0, 0, 0, 0)),
                  full(D_MODEL, n),
                  pl.BlockSpec((tm, LANES), lambda i, t: (t, 0)),
                  pl.BlockSpec((tm, LANES), lambda i, t: (t, 0)),
                  full(1, LANES), full(1, LANES), full(LANES, LANES)],
        out_specs=[tok(LRU_WIDTH), tok(LRU_WIDTH), tok(GQA_Q_HEADS * HEAD_DIM), tok(LANES), tok(LANES)],
        out_shape=[jax.ShapeDtypeStruct((b, S_ALL, LRU_WIDTH), F32),
                   jax.ShapeDtypeStruct((b, S_ALL, LRU_WIDTH), BF16),
                   jax.ShapeDtypeStruct((b, S_ALL, GQA_Q_HEADS * HEAD_DIM), BF16),
                   jax.ShapeDtypeStruct((b, S_ALL, LANES), BF16),
                   jax.ShapeDtypeStruct((b, S_ALL, LANES), BF16)],
        compiler_params=_params(("parallel", "parallel")),
        name="proj0",
    )(x, x, ctx, g, modp, w_in, cos, sin, q_gain, k_gain, ones_pair)


def _lru_kernel(xr_ref, cw_ref, cb_ref, wg_ref, bg_ref, lam_ref, rec_ref,
                pad_ref, af_ref, hf_ref, ab_ref, hb_ref):
    w = LRU_WIDTH
    pad_ref[...] = jnp.zeros_like(pad_ref)
    pad_ref[LRU_PAD_FRONT:LRU_PAD_FRONT + CTX_LEN, :] = xr_ref[SEQ:S_ALL, :]
    pad_ref[LRU_PAD_FRONT + LRU_LAT0:LRU_PAD_FRONT + LRU_LAT0 + SEQ, :] = xr_ref[0:SEQ, :]
    lam = lam_ref[...]
    neg_c_softplus = -LRU_C * (jnp.maximum(-lam, 0.0) + jnp.log(1.0 + jnp.exp(-jnp.abs(lam))))
    left = LRU_CONV // 2
    for ch in range(LRU_ROWS // LRU_CHUNK):
        r0 = ch * LRU_CHUNK
        u = cb_ref[...]
        for j in range(LRU_CONV):
            start = LRU_PAD_FRONT + r0 - left + j
            u = u + cw_ref[j:j + 1, :] * pad_ref[start:start + LRU_CHUNK, :]
        gz = jnp.dot(u.astype(BF16), wg_ref[...], preferred_element_type=F32) + bg_ref[...]
        for d, (a_ref, h_ref) in enumerate(((af_ref, hf_ref), (ab_ref, hb_ref))):
            r = jax.nn.sigmoid(gz[:, 2 * d * w:(2 * d + 1) * w])
            i = jax.nn.sigmoid(gz[:, (2 * d + 1) * w:(2 * d + 2) * w])
            a = jnp.exp(neg_c_softplus[d:d + 1, :] * r)
            a_ref[r0:r0 + LRU_CHUNK, :] = a
            h_ref[r0:r0 + LRU_CHUNK, :] = jnp.sqrt(1.0 - a * a) * (i * u)

    n_ctx_groups = CTX_LEN // SUBLANES
    sub = lax.broadcasted_iota(jnp.int32, (SUBLANES, w), 0)

    def tile_scan(a, b, descending):
        for s in (1, 2, 4):
            if descending:
                ok, shift = sub < SUBLANES - s, SUBLANES - s
            else:
                ok, shift = sub >= s, s
            a_s = jnp.where(ok, pltpu.roll(a, shift, 0), 1.0)
            b_s = jnp.where(ok, pltpu.roll(b, shift, 0), 0.0)
            a, b = a * a_s, a * b_s + b
        return a, b

    def group(tt, carry):
        hf, hb = carry
        is_lat = tt >= n_ctx_groups
        rf = pl.ds(pl.multiple_of(tt * SUBLANES + jnp.where(is_lat, LRU_GAP, 0), SUBLANES), SUBLANES)
        rb = pl.ds(pl.multiple_of(
            jnp.where(is_lat, LRU_LAT0 + SEQ + CTX_LEN, CTX_LEN) - (tt + 1) * SUBLANES, SUBLANES), SUBLANES)
        a, b = tile_scan(af_ref[rf, :], hf_ref[rf, :], False)
        h = a * hf + b
        hf_ref[rf, :] = h
        hf = jnp.broadcast_to(h[SUBLANES - 1:SUBLANES, :], (SUBLANES, w))
        a, b = tile_scan(ab_ref[rb, :], hb_ref[rb, :], True)
        h = a * hb + b
        hb_ref[rb, :] = h
        hb = jnp.broadcast_to(h[0:1, :], (SUBLANES, w))
        return hf, hb

    zero = jnp.zeros((SUBLANES, w), F32)
    lax.fori_loop(0, S_ALL // SUBLANES, group, (zero, zero), unroll=2)
    dt = rec_ref.dtype
    rec_ref[0:SEQ, :] = (hf_ref[LRU_LAT0:LRU_LAT0 + SEQ, :] + hb_ref[LRU_LAT0:LRU_LAT0 + SEQ, :]).astype(dt)
    rec_ref[SEQ:S_ALL, :] = (hf_ref[0:CTX_LEN, :] + hb_ref[0:CTX_LEN, :]).astype(dt)


def rglru(xr, conv_w, conv_b, w_gates, b_gates, lam):
    b = xr.shape[0]
    w = LRU_WIDTH
    full = lambda r, c_: pl.BlockSpec((r, c_), lambda i: (0, 0))
    rows = pltpu.VMEM((LRU_ROWS, w), F32)
    return pl.pallas_call(
        _lru_kernel,
        grid=(b,),
        in_specs=[pl.BlockSpec((None, S_ALL, w), lambda i: (i, 0, 0)),
                  full(LRU_CONV, w), full(1, w), full(w, 4 * w), full(1, 4 * w), full(2, w)],
        out_specs=pl.BlockSpec((None, S_ALL, w), lambda i: (i, 0, 0)),
        out_shape=jax.ShapeDtypeStruct((b, S_ALL, w), BF16),
        scratch_shapes=[pltpu.VMEM((LRU_ROWS + 2 * LRU_PAD_FRONT, w), F32), rows, rows, rows, rows],
        compiler_params=_params(("parallel",)),
        name="rglru",
    )(xr, conv_w, conv_b, w_gates, b_gates, lam)


def _attn_kernel(q_ref, k_ref, v_ref, *rest, heads):
    n_cast = (len(rest) - 1) // 2
    o_ref = rest[n_cast]
    cast_pairs = list(zip(rest[:n_cast], rest[n_cast + 1:]))
    tq, sk = q_ref.shape[0], k_ref.shape[0]
    q_lo = lax.broadcasted_iota(jnp.int32, (tq, LANES), 1) < HEAD_DIM
    v_lo = lax.broadcasted_iota(jnp.int32, (sk, LANES), 1) < HEAD_DIM

    def keep(x, is_lo, half):
        if half is None:
            return x
        return jnp.where(is_lo if half == 0 else jnp.logical_not(is_lo), x, jnp.zeros_like(x))

    for og, members in enumerate(heads):
        acc = None
        for qg, qh, kg, vg, vh in members:
            q = keep(q_ref[:, qg * LANES:(qg + 1) * LANES], q_lo, qh)
            s = lax.dot_general(q, k_ref[:, kg * LANES:(kg + 1) * LANES],
                                (((1,), (1,)), ((), ())), preferred_element_type=F32)
            p = jnp.exp(s - jnp.max(s, axis=-1, keepdims=True))
            l = jnp.sum(p, axis=-1, keepdims=True)
            v = keep(v_ref[:, vg * LANES:(vg + 1) * LANES], v_lo, vh)
            o = jnp.dot(p.astype(BF16), v, preferred_element_type=F32) / l
            acc = o if acc is None else acc + o
        o_ref[:, og * LANES:(og + 1) * LANES] = acc.astype(o_ref.dtype)
        for src, dst in cast_pairs:
            r0, r1 = (og * src.shape[0] // len(heads), (og + 1) * src.shape[0] // len(heads))
            dst[r0:r1, :] = src[r0:r1, :].astype(BF16)


def attention(q, k, v, heads, q_block, n_q_blocks, tq, k_rows, cast=()):
    b, s, qw = q.shape
    kw, vw = k.shape[-1], v.shape[-1]
    ow = len(heads) * LANES
    n_steps = b * n_q_blocks
    kv_spec = lambda width: pl.BlockSpec((None, k_rows, width), lambda i, t: (i, s // k_rows - 1, 0))
    slab = lambda w: pl.BlockSpec((w.shape[0] // n_steps, w.shape[1]), lambda i, t: (i * n_q_blocks + t, 0))
    assert all(w.shape[0] % (16 * len(heads) * n_steps) == 0 for w in cast)
    out = pl.pallas_call(
        functools.partial(_attn_kernel, heads=heads),
        grid=(b, n_q_blocks),
        in_specs=[pl.BlockSpec((None, tq, qw), lambda i, t: (i, q_block + t, 0)), kv_spec(kw), kv_spec(vw)]
        + [slab(w) for w in cast],
        out_specs=[pl.BlockSpec((None, tq, ow), lambda i, t: (i, t, 0))] + [slab(w) for w in cast],
        out_shape=[jax.ShapeDtypeStruct((b, n_q_blocks * tq, ow), BF16)]
        + [jax.ShapeDtypeStruct(w.shape, BF16) for w in cast],
        compiler_params=_params(("parallel", "parallel")),
        name="attention",
    )(q, k, v, *cast)
    return out if cast else out[0]


GQA_HEADS = tuple(((p, 0, 0, 0, 0), (p, 1, 0, 0, 1)) for p in range(GQA_Q_HEADS // 2))
MLA_HEAD_GROUPS = tuple(((2 * p, None, 2 * p, p, 0), (2 * p + 1, None, 2 * p + 1, p, 1))
                        for p in range(MLA_HEADS // 2))


def attention_latent(q, k, v, heads, tq, cast=()):
    return attention(q, k, v, heads, 0, SEQ // tq, tq, S_ALL, cast)


def attention_context(q, k, v, heads):
    return attention(q, k, v, heads, SEQ // CTX_LEN, 1, CTX_LEN, CTX_LEN)


def _merge0_kernel(rec_ref, gate_ref, att_l_ref, att_c_ref, x_ref, ctx_ref, mod_ref,
                   w_rec_ref, w_att_ref, x1_ref):
    tm = x_ref.shape[0]
    row0 = pl.program_id(1) * tm
    is_ctx = row0 >= SEQ
    rg = (rec_ref[...].astype(F32) * jax.nn.gelu(gate_ref[...].astype(F32))).astype(BF16)
    att = jnp.where(is_ctx, att_c_ref[...], att_l_ref[...])
    y = (jnp.dot(rg, w_rec_ref[...], preferred_element_type=F32)
         + jnp.dot(att, w_att_ref[...], preferred_element_type=F32))
    x1_ref[...] = jnp.where(is_ctx, ctx_ref[...], x_ref[...]) + _mod_rows(mod_ref, 2, row0, tm) * y


def merge0(rec, gate, att_l, att_c, x, ctx, modp, w_rec, w_att):
    b = x.shape[0]
    tm = CTX_LEN
    n_lat = SEQ // tm
    tok = lambda width: pl.BlockSpec((None, tm, width), lambda i, t: (i, t, 0))
    lat = lambda width: pl.BlockSpec((None, tm, width), lambda i, t: (i, jnp.minimum(t, n_lat - 1), 0))
    cx = lambda width: pl.BlockSpec((None, tm, width), lambda i, t: (i, 0, 0))
    full = lambda r, c_: pl.BlockSpec((r, c_), lambda i, t: (0, 0))
    return pl.pallas_call(
        _merge0_kernel,
        grid=(b, S_ALL // tm),
        in_specs=[tok(LRU_WIDTH), tok(LRU_WIDTH), lat(HALF_MIX), cx(HALF_MIX), lat(D_MODEL), cx(D_MODEL),
                  pl.BlockSpec((None, N_MOD, 2, 1, D_MODEL), lambda i, t: (i, 0, 0, 0, 0)),
                  full(LRU_WIDTH, D_MODEL), full(HALF_MIX, D_MODEL)],
        out_specs=tok(D_MODEL),
        out_shape=jax.ShapeDtypeStruct((b, S_ALL, D_MODEL), F32),
        compiler_params=_params(("parallel", "parallel")),
        name="merge0",
    )(rec, gate, att_l, att_c, x, ctx, modp, w_rec, w_att)


def _swiglu(x, w1_ref, w3_ref, w2_ref, act_ref):
    for c in range(w1_ref.shape[1] // FFN_TILE_F):
        sl = slice(c * FFN_TILE_F, (c + 1) * FFN_TILE_F)
        h1 = jnp.dot(x, w1_ref[:, sl], preferred_element_type=F32)
        h3 = jnp.dot(x, w3_ref[:, sl], preferred_element_type=F32)
        act_ref[:, sl] = (h1 * jax.nn.sigmoid(h1) * h3).astype(BF16)
    return jnp.dot(act_ref[...], w2_ref[...], preferred_element_type=F32)


def _accumulate(acc_ref, part, j):
    @pl.when(j == 0)
    def _():
        acc_ref[...] = part

    @pl.when(j > 0)
    def _():
        acc_ref[...] += part


def _ffn_kernel(x1_ref, mod_ref, g2_ref, w1_ref, w3_ref, w2_ref, o_ref, act_ref):
    tm = x1_ref.shape[0]
    row0 = pl.program_id(1) * tm
    x = _norm_mod(x1_ref[...], g2_ref[...], mod_ref, row0, 3, 4).astype(BF16)
    y = _swiglu(x, w1_ref, w3_ref, w2_ref, act_ref)
    o_ref[...] = x1_ref[...] + _mod_rows(mod_ref, 5, row0, tm) * y


def ffn_residual(x1, modp, g2, w1, w3, w2):
    b = x1.shape[0]
    tm = FFN_TILE_M
    f = w1.shape[1]
    assert f % FFN_TILE_F == 0
    tok = lambda: pl.BlockSpec((None, tm, D_MODEL), lambda i, t: (i, t, 0))
    resident = lambda a: pl.BlockSpec(a.shape, lambda i, t: (0, 0), pipeline_mode=pl.Buffered(1))
    return pl.pallas_call(
        _ffn_kernel,
        grid=(b, S_ALL // tm),
        in_specs=[tok(),
                  pl.BlockSpec((None, N_MOD, 2, 1, D_MODEL), lambda i, t: (i, 0, 0, 0, 0)),
                  pl.BlockSpec((1, D_MODEL), lambda i, t: (0, 0)),
                  resident(w1), resident(w3), resident(w2)],
        out_specs=tok(),
        out_shape=jax.ShapeDtypeStruct((b, S_ALL, D_MODEL), F32),
        scratch_shapes=[pltpu.VMEM((tm, f), BF16)],
        compiler_params=_params(("parallel", "parallel")),
        name="ffn",
    )(x1, modp, g2, w1, w3, w2)


def _moe_kernel(te_ref, nv_ref, x_ref, w1_ref, w3_ref, w2_ref, o_ref, xb_ref, acc_ref, act_ref):
    i, j = pl.program_id(0), pl.program_id(1)
    last = pl.num_programs(1) - 1
    valid = i < nv_ref[0]

    @pl.when(valid)
    def _():
        @pl.when(j == 0)
        def _():
            xb_ref[...] = x_ref[...].astype(BF16)

        _accumulate(acc_ref, _swiglu(xb_ref[...], w1_ref, w3_ref, w2_ref, act_ref), j)

        @pl.when(j == last)
        def _():
            o_ref[...] = acc_ref[...]

    @pl.when(jnp.logical_and(jnp.logical_not(valid), j == last))
    def _():
        o_ref[...] = jnp.zeros_like(o_ref)


def grouped_swiglu(xs, tile_expert, n_valid, w1, w3, w2, tm, tf):
    n_rows, d = xs.shape
    f = w1.shape[-1]
    assert n_rows % tm == 0 and f % tf == 0
    n_f = f // tf

    def col(i, j, nv):
        return jnp.where(i < nv[0], j, n_f - 1)

    grid_spec = pltpu.PrefetchScalarGridSpec(
        num_scalar_prefetch=2,
        grid=(n_rows // tm, n_f),
        in_specs=[pl.BlockSpec((tm, d), lambda i, j, te, nv: (jnp.minimum(i, nv[0] - 1), 0)),
                  pl.BlockSpec((None, d, tf), lambda i, j, te, nv: (te[i], 0, col(i, j, nv))),
                  pl.BlockSpec((None, d, tf), lambda i, j, te, nv: (te[i], 0, col(i, j, nv))),
                  pl.BlockSpec((None, tf, d), lambda i, j, te, nv: (te[i], col(i, j, nv), 0))],
        out_specs=pl.BlockSpec((tm, d), lambda i, j, te, nv: (i, 0)),
        scratch_shapes=[pltpu.VMEM((tm, d), BF16), pltpu.VMEM((tm, d), F32), pltpu.VMEM((tm, tf), BF16)],
    )
    return pl.pallas_call(
        _moe_kernel,
        grid_spec=grid_spec,
        out_shape=jax.ShapeDtypeStruct((n_rows, d), F32),
        compiler_params=_params(("parallel", "arbitrary")),
        name="moe",
    )(tile_expert, n_valid, xs, w1, w3, w2)


def route(logits, tm):
    n_tok = logits.shape[0]
    n_assign = n_tok * TOP_K
    idx = jnp.arange(N_EXPERTS, dtype=jnp.int32)[None, :]
    m1 = jnp.max(logits, axis=-1, keepdims=True)
    e1 = jnp.min(jnp.where(logits == m1, idx, N_EXPERTS), axis=-1, keepdims=True)
    rest = jnp.where(idx == e1, jnp.finfo(F32).min, logits)
    m2 = jnp.max(rest, axis=-1, keepdims=True)
    e2 = jnp.min(jnp.where(rest == m2, idx, N_EXPERTS), axis=-1, keepdims=True)
    t = jnp.exp(m2 - m1)
    gates = jnp.concatenate([1.0 / (1.0 + t), t / (1.0 + t)], axis=-1)
    flat_e = jnp.concatenate([e1, e2], axis=-1).reshape(n_assign, 1)
    onehot = (flat_e == idx).astype(F32)
    blk = MOE_ROUTE_BLOCK
    nb = n_assign // blk
    oh = onehot.reshape(nb, blk, N_EXPERTS)
    lower = lambda n: (jnp.arange(n)[:, None] > jnp.arange(n)[None, :]).astype(F32)
    within = jnp.einsum('ij,bjk->bik', lower(blk), oh)
    blk_cnt = jnp.sum(oh, axis=1)
    blk_off = jnp.dot(lower(nb), blk_cnt, precision=lax.Precision.HIGHEST)
    rank = jnp.sum((within + blk_off[:, None, :]) * oh, axis=-1).reshape(n_assign)
    counts = jnp.sum(blk_cnt, axis=0).astype(jnp.int32)
    padded = (counts + tm - 1) // tm * tm
    ends = []
    for e in range(N_EXPERTS):
        ends.append(padded[e] + (ends[-1] if ends else 0))
    pend = jnp.stack(ends)
    pstart = pend - padded
    dest = (jnp.sum(onehot * pstart.astype(F32)[None, :], axis=-1) + rank).astype(jnp.int32)
    n_tiles = n_assign // tm + N_EXPERTS
    starts = jnp.arange(n_tiles, dtype=jnp.int32) * tm
    tile_expert = jnp.minimum(jnp.sum((pend[None, :] <= starts[:, None]).astype(jnp.int32), axis=-1),
                              N_EXPERTS - 1)
    n_valid = (pend[-1] // tm).reshape(1)
    meta = jnp.concatenate([pend, padded, n_valid])
    return gates, dest, tile_expert, n_valid, meta


def _row_copies(dest_ref, src, dst, sem, rows, gather, side_work=()):
    n_groups = rows // SUBLANES
    for g in range(n_groups):
        for r in range(g * SUBLANES, (g + 1) * SUBLANES):
            for kk in range(TOP_K):
                d = dest_ref[0, TOP_K * r + kk]
                if gather:
                    copy = pltpu.make_async_copy(src.at[pl.ds(d, 1), :], dst.at[kk, pl.ds(r, 1), :], sem)
                else:
                    copy = pltpu.make_async_copy(src.at[pl.ds(r, 1), :], dst.at[pl.ds(d, 1), :], sem)
                copy.start(priority=kk % 2)
        for thunk in side_work[g * len(side_work) // n_groups:(g + 1) * len(side_work) // n_groups]:
            thunk()
    if gather:
        pltpu.make_async_copy(dst, dst, sem).wait()
    else:
        for _ in range(TOP_K):
            pltpu.make_async_copy(src, src, sem).wait()


def _dispatch_kernel(meta_ref, dest_ref, h_ref, w_ref, xs_ref, wb_ref, zero_ref, sem, zsem):
    tm_moe = zero_ref.shape[0]

    @pl.when(pl.program_id(0) == 0)
    def _():
        zero_ref[...] = jnp.zeros_like(zero_ref)
        n_tiles = xs_ref.shape[0] // tm_moe
        fills = []
        for e in range(N_EXPERTS):
            start = pl.multiple_of(jnp.maximum(meta_ref[e] - tm_moe, 0), tm_moe)
            fills.append((meta_ref[N_EXPERTS + e] > 0, start))
        for i in range(n_tiles - N_EXPERTS, n_tiles):
            fills.append((i >= meta_ref[2 * N_EXPERTS], i * tm_moe))
        for cond, start in fills:
            @pl.when(cond)
            def _():
                pltpu.make_async_copy(zero_ref, xs_ref.at[pl.ds(start, tm_moe), :], zsem).start()
        for cond, start in fills:
            @pl.when(cond)
            def _():
                pltpu.make_async_copy(zero_ref, xs_ref.at[pl.ds(start, tm_moe), :], zsem).wait()

    def cast_rows(c):
        def thunk():
            wb_ref[c * 16:(c + 1) * 16, :] = w_ref[c * 16:(c + 1) * 16, :].astype(BF16)
        return thunk

    casts = [cast_rows(c) for c in range(w_ref.shape[0] // 16)]
    _row_copies(dest_ref, h_ref, xs_ref, sem, h_ref.shape[0], gather=False, side_work=casts)


def moe_dispatch(h2, dest, meta, n_slots, w):
    n_tok, d = h2.shape
    tm = MOE_DISPATCH_TILE
    n_steps = n_tok // tm
    wf = w.reshape(-1, w.shape[-1])
    assert wf.shape[0] % (16 * n_steps) == 0
    slab = pl.BlockSpec((wf.shape[0] // n_steps, wf.shape[1]), lambda t, m: (t, 0))
    grid_spec = pltpu.PrefetchScalarGridSpec(
        num_scalar_prefetch=1,
        grid=(n_steps,),
        in_specs=[pl.BlockSpec((None, 1, TOP_K * tm), lambda t, m: (t, 0, 0), memory_space=pltpu.SMEM),
                  pl.BlockSpec((tm, d), lambda t, m: (t, 0)), slab],
        out_specs=[pl.BlockSpec(memory_space=pl.ANY), slab],
        scratch_shapes=[pltpu.VMEM((MOE_TILE_M, d), h2.dtype), pltpu.SemaphoreType.DMA,
                        pltpu.SemaphoreType.DMA],
    )
    xs, wb = pl.pallas_call(
        _dispatch_kernel,
        grid_spec=grid_spec,
        out_shape=[jax.ShapeDtypeStruct((n_slots, d), h2.dtype), jax.ShapeDtypeStruct(wf.shape, BF16)],
        compiler_params=_params(("arbitrary",)),
        name="moe_dispatch",
    )(meta, dest.reshape(n_steps, 1, TOP_K * tm), h2, wf)
    return xs, wb.reshape(w.shape)


def _combine_kernel(dest_ref, x1_ref, g_ref, mod_ref, ys_ref, o_ref, buf_ref, sem):
    tm = x1_ref.shape[0]
    _row_copies(dest_ref, ys_ref, buf_ref, sem, tm, gather=True)
    g = g_ref[...]
    moe = g[:, 0:1] * buf_ref[0] + g[:, 1:2] * buf_ref[1]
    o_ref[...] = x1_ref[...] + mod_ref[5, 0] * moe


def moe_combine(x1, gates, dest, ys, modp):
    b, s, d = x1.shape
    tm = MOE_COMBINE_TILE
    nt = s // tm
    return pl.pallas_call(
        _combine_kernel,
        grid=(b, nt),
        in_specs=[pl.BlockSpec((None, 1, TOP_K * tm), lambda i, t: (i * nt + t, 0, 0), memory_space=pltpu.SMEM),
                  pl.BlockSpec((None, tm, d), lambda i, t: (i, t, 0)),
                  pl.BlockSpec((None, tm, TOP_K), lambda i, t: (i, t, 0)),
                  pl.BlockSpec((None, N_MOD, 2, 1, d), lambda i, t: (i, 0, 0, 0, 0)),
                  pl.BlockSpec(memory_space=pl.ANY)],
        out_specs=pl.BlockSpec((None, tm, d), lambda i, t: (i, t, 0)),
        out_shape=jax.ShapeDtypeStruct((b, s, d), F32),
        scratch_shapes=[pltpu.VMEM((TOP_K, tm, d), F32), pltpu.SemaphoreType.DMA],
        compiler_params=_params(("arbitrary", "arbitrary")),
        name="moe_combine",
    )(dest.reshape(b * nt, 1, TOP_K * tm), x1, gates.reshape(b, s, TOP_K), modp, ys)


def moe_residual(x1, h2, logits, modp, w1, w3, w2):
    b, s, d = x1.shape
    n_tok = b * s
    tm = MOE_TILE_M
    gates, dest, tile_expert, n_valid, meta = route(logits.reshape(n_tok, N_EXPERTS), tm)
    n_slots = (n_tok * TOP_K // tm + N_EXPERTS) * tm
    xs, w2 = moe_dispatch(h2.reshape(n_tok, d), dest, meta, n_slots, w2)
    ys = grouped_swiglu(xs, tile_expert, n_valid, w1, w3, w2, tm, MOE_TILE_F)
    return moe_combine(x1, gates, dest, ys, modp)


def _proj1_kernel(x_ref, g_ref, mod_ref, w_ref, qan_ref, qb_ref, qbs_ref, kvan_ref, kbk_ref, kbv_ref,
                  cos_ref, sin_ref, gains_ref, ones_ref, hy_ref, q_ref, k_ref, v_ref):
    tm = x_ref.shape[0]
    rs = tm // PROJ1_ROW_SPLIT
    c_q, c_kv = HY_IN, HY_IN + MLA_Q_RANK
    c_r = c_kv + MLA_KV_RANK
    first_half = lax.broadcasted_iota(jnp.int32, (rs, LANES), 1) < MLA_NOPE + MLA_ROPE // 2
    sh = MLA_ROPE // 2
    qg, qg_sw, kg, kg_sw = (gains_ref[i:i + 1, :] for i in range(4))

    def rms(a, g):
        return a * lax.rsqrt(jnp.mean(a * a, axis=-1, keepdims=True) + EPS) * g

    for r0 in range(0, tm, rs):
        rows = slice(r0, r0 + rs)
        h = _norm_mod(x_ref[rows, :], g_ref[...], mod_ref, pl.program_id(1) * tm + r0, 0, 1)
        z = jnp.dot(h.astype(BF16), w_ref[...], preferred_element_type=F32)
        hy_ref[rows, :] = z[:, :HY_IN]
        qa = rms(z[:, c_q:c_kv], qan_ref[...]).astype(BF16)
        kva = rms(z[:, c_kv:c_r], kvan_ref[...]).astype(BF16)
        q = jnp.dot(qa, qb_ref[...], preferred_element_type=F32)
        q_sw = jnp.dot(qa, qbs_ref[...], preferred_element_type=F32)
        kk = jnp.dot(kva, kbk_ref[...], preferred_element_type=F32)
        v_ref[rows, :] = jnp.dot(kva, kbv_ref[...], preferred_element_type=F32).astype(BF16)
        k_rope = pltpu.roll(z[:, c_r:c_r + LANES], MLA_NOPE, 1)
        k_rope_sw = jnp.where(first_half, pltpu.roll(k_rope, LANES - sh, 1), pltpu.roll(k_rope, sh, 1))
        cos, sin = cos_ref[rows, :], sin_ref[rows, :]

        def head(a, a_sw, gain, gain_sw):
            inv = lax.rsqrt(_group_sums(a, ones_ref) / MLA_QK + EPS)
            return (a * (inv * gain) * cos + a_sw * (inv * gain_sw) * sin).astype(BF16)

        for hh in range(MLA_HEADS):
            sl = slice(hh * LANES, (hh + 1) * LANES)
            q_ref[rows, sl] = head(q[:, sl], q_sw[:, sl], qg, qg_sw)
            k_ref[rows, sl] = head(kk[:, sl] + k_rope, k_rope_sw, kg, kg_sw)


def _swap_rope_halves(a):
    lead = a.shape[:-1]
    g = a.reshape(*lead, -1, LANES)
    lo, mid = MLA_NOPE, MLA_NOPE + MLA_ROPE // 2
    g = jnp.concatenate([g[..., :lo], g[..., mid:MLA_QK], g[..., lo:mid], g[..., MLA_QK:]], axis=-1)
    return g.reshape(a.shape)


def proj1(xa, g, modp, w_in, qan, qb, kvan, kbk, kbv, cos, sin, q_gain, k_gain):
    b = xa.shape[0]
    tm = TOKEN_TILE
    tok = lambda width: pl.BlockSpec((None, tm, width), lambda i, t: (i, t, 0))
    full = lambda a: pl.BlockSpec(a.shape, lambda i, t: (0, 0))
    hw = MLA_HEADS * LANES
    qbs = _swap_rope_halves(qb)
    gains = jnp.concatenate([q_gain, _swap_rope_halves(q_gain), k_gain, _swap_rope_halves(k_gain)], axis=0)
    ones = jnp.ones((LANES, LANES), BF16)
    return pl.pallas_call(
        _proj1_kernel,
        grid=(b, S_ALL // tm),
        in_specs=[tok(D_MODEL), full(g),
                  pl.BlockSpec((None, N_MOD, 2, 1, D_MODEL), lambda i, t: (i, 0, 0, 0, 0)),
                  full(w_in), full(qan), full(qb), full(qbs), full(kvan), full(kbk), full(kbv),
                  pl.BlockSpec((tm, LANES), lambda i, t: (t, 0)),
                  pl.BlockSpec((tm, LANES), lambda i, t: (t, 0)),
                  full(gains), full(ones)],
        out_specs=[tok(HY_IN), tok(hw), tok(hw), tok(MLA_HEADS * MLA_V)],
        out_shape=[jax.ShapeDtypeStruct((b, S_ALL, HY_IN), F32),
                   jax.ShapeDtypeStruct((b, S_ALL, hw), BF16),
                   jax.ShapeDtypeStruct((b, S_ALL, hw), BF16),
                   jax.ShapeDtypeStruct((b, S_ALL, MLA_HEADS * MLA_V), BF16)],
        compiler_params=_params(("parallel", "parallel")),
        name="proj1",
    )(xa, g, modp, w_in, qan, qb, qbs, kvan, kbk, kbv, cos, sin, gains, ones)


def _merge1_kernel(hy_ref, att_ref, x_ref, mod_ref, g2_ref, w_hy_ref, w_att_ref, r_ref,
                   x1_ref, h2_ref, lg_ref):
    y = (jnp.dot(hy_ref[...].astype(BF16), w_hy_ref[...], preferred_element_type=F32)
         + jnp.dot(att_ref[...], w_att_ref[...], preferred_element_type=F32))
    x1 = x_ref[...] + mod_ref[2, 0] * y
    x1_ref[...] = x1
    h2 = _norm_mod(x1, g2_ref[...], mod_ref, 0, 3, 4)
    h2_ref[...] = h2
    lane = lax.broadcasted_iota(jnp.int32, lg_ref.shape, 1)
    logits = jnp.zeros(lg_ref.shape, F32)
    for e in range(N_EXPERTS):
        col = jnp.sum(h2 * r_ref[e:e + 1, :], axis=-1, keepdims=True)
        logits = jnp.where(lane == e, col, logits)
    lg_ref[...] = logits


def merge1(hy, att, xa, modp, g2, w_hy, w_att, router):
    b = xa.shape[0]
    tm = MERGE1_TILE
    tok = lambda width: pl.BlockSpec((None, tm, width), lambda i, t: (i, t, 0))
    full = lambda a: pl.BlockSpec(a.shape, lambda i, t: (0, 0))
    return pl.pallas_call(
        _merge1_kernel,
        grid=(b, SEQ // tm),
        in_specs=[tok(HY_WIDTH), tok(HALF_MIX), tok(D_MODEL),
                  pl.BlockSpec((None, N_MOD, 2, 1, D_MODEL), lambda i, t: (i, 0, 0, 0, 0)),
                  full(g2), full(w_hy), full(w_att), full(router)],
        out_specs=[tok(D_MODEL), tok(D_MODEL), tok(N_EXPERTS)],
        out_shape=[jax.ShapeDtypeStruct((b, SEQ, D_MODEL), F32),
                   jax.ShapeDtypeStruct((b, SEQ, D_MODEL), F32),
                   jax.ShapeDtypeStruct((b, SEQ, N_EXPERTS), F32)],
        compiler_params=_params(("parallel", "parallel")),
        name="merge1",
    )(hy, att, xa, modp, g2, w_hy, w_att, router)


HY_N = 2 * SEQ
HY_TW_ROWS = 256
HY_SPEC_FBLK = 256
HY_CONV_FBLK = 512
HY_LANE_TILES = SEQ // LANES


def _twiddle_kernel(ca_ref, sa_ref, cb_ref, sb_ref, fre_ref, fim_ref, ic_ref, is_ref):
    rows = fre_ref.shape[0]
    r = pl.program_id(0) * rows + lax.broadcasted_iota(jnp.int32, (rows, LANES), 0)
    alt_r = (1 - 2 * (r & 1)).astype(F32)
    cb, sb = cb_ref[...], sb_ref[...]
    for a in range(HY_LANE_TILES):
        ca, sa = ca_ref[:, a:a + 1], sa_ref[:, a:a + 1]
        c = ca * cb - sa * sb
        s = sa * cb + ca * sb
        col = a * LANES + lax.broadcasted_iota(jnp.int32, (rows, LANES), 1)
        alt_c = (1 - 2 * (col & 1)).astype(F32)
        w = jnp.where(col == 0, 1.0 / HY_N, 2.0 / HY_N)
        sl = slice(a * LANES, (a + 1) * LANES)
        fre_ref[:, sl] = c.astype(BF16)
        fim_ref[:, sl] = jnp.where(r == 0, alt_c, -s).astype(BF16)
        ic_ref[:, sl] = (c * w).astype(BF16)
        is_ref[:, sl] = jnp.where(col == 0, alt_r / HY_N, -s * w).astype(BF16)


def dft_matrices():
    idx = jnp.arange(SEQ, dtype=jnp.int32)[:, None]
    step = 2.0 * math.pi / HY_N
    ph_a = ((idx * (LANES * jnp.arange(HY_LANE_TILES, dtype=jnp.int32))[None, :]) % HY_N).astype(F32) * step
    ph_b = ((idx * jnp.arange(LANES, dtype=jnp.int32)[None, :]) % HY_N).astype(F32) * step
    rows = HY_TW_ROWS
    tab = lambda width: pl.BlockSpec((rows, width), lambda i: (i, 0))
    out = jax.ShapeDtypeStruct((SEQ, SEQ), BF16)
    return pl.pallas_call(
        _twiddle_kernel,
        grid=(SEQ // rows,),
        in_specs=[tab(HY_LANE_TILES), tab(HY_LANE_TILES), tab(LANES), tab(LANES)],
        out_specs=[tab(SEQ)] * 4,
        out_shape=[out] * 4,
        compiler_params=_params(("parallel",)),
        name="twiddle",
    )(jnp.cos(ph_a), jnp.sin(ph_a), jnp.cos(ph_b), jnp.sin(ph_b))


def _spec_kernel(fre_ref, fim_ref, h_ref, k_ref):
    w = HY_WIDTH
    h = h_ref[...].astype(BF16)
    re = jnp.dot(fre_ref[...], h, preferred_element_type=F32)
    im = jnp.dot(fim_ref[...], h, preferred_element_type=F32)
    hb0 = h[0:1, w:].astype(F32)
    first = (pl.program_id(1) * re.shape[0] + lax.broadcasted_iota(jnp.int32, (re.shape[0], 1), 0)) == 0
    k_ref[0] = re[:, :w] + re[:, w:] - hb0
    k_ref[1] = jnp.where(first, im[:, :w] + im[:, w:] - hb0, im[:, :w] - im[:, w:])


def filter_spectra(filt, fre, fim):
    fb = HY_SPEC_FBLK
    return pl.pallas_call(
        _spec_kernel,
        grid=(HY_ORDER, SEQ // fb),
        in_specs=[pl.BlockSpec((fb, SEQ), lambda o, f: (f, 0)),
                  pl.BlockSpec((fb, SEQ), lambda o, f: (f, 0)),
                  pl.BlockSpec((SEQ, 2 * HY_WIDTH), lambda o, f: (0, o))],
        out_specs=pl.BlockSpec((None, 2, fb, HY_WIDTH), lambda o, f: (o, 0, f, 0)),
        out_shape=jax.ShapeDtypeStruct((HY_ORDER, 2, SEQ, HY_WIDTH), F32),
        compiler_params=_params(("parallel", "parallel")),
        name="filter_spectra",
    )(fre, fim, filt)


def _short_conv(z_ref, w_ref, b_ref, part):
    z = z_ref[...]
    l = z.shape[0]
    t = lax.broadcasted_iota(jnp.int32, (l, 1), 0)
    prev = jnp.where(t == 0, 0.0, pltpu.roll(z, 1, 0))
    nxt = jnp.where(t == l - 1, 0.0, pltpu.roll(z, l - 1, 0))
    cs = slice(part * HY_WIDTH, (part + 1) * HY_WIDTH)
    return (w_ref[0:1, cs] * prev + w_ref[1:2, cs] * z + w_ref[2:3, cs] * nxt) + b_ref[:, cs]


def _hyconv_kernel(u_ref, xg_ref, cw_ref, cb_ref, fre_ref, fim_ref, ic_ref, is_ref, k_ref, skip_ref,
                   y_ref, ub_ref, acc_ref, *, order):
    f = pl.program_id(1)

    def u_f32():
        return _short_conv(u_ref, cw_ref, cb_ref, 0) if order == 0 else u_ref[...]

    @pl.when(f == 0)
    def _():
        ub_ref[...] = u_f32().astype(BF16)

    x_re = jnp.dot(fre_ref[...], ub_ref[...], preferred_element_type=F32)
    x_im = jnp.dot(fim_ref[...], ub_ref[...], preferred_element_type=F32)
    k_re, k_im = k_ref[0], k_ref[1]
    first = (f * x_re.shape[0] + lax.broadcasted_iota(jnp.int32, (x_re.shape[0], 1), 0)) == 0
    y_re = x_re * k_re - jnp.where(first, 0.0, x_im * k_im)
    y_im = x_im * jnp.where(first, k_im, k_re) + jnp.where(first, 0.0, x_re * k_im)
    part = (jnp.dot(ic_ref[...], y_re.astype(BF16), preferred_element_type=F32)
            + jnp.dot(is_ref[...], y_im.astype(BF16), preferred_element_type=F32))
    _accumulate(acc_ref, part, f)

    @pl.when(f == pl.num_programs(1) - 1)
    def _():
        xg = _short_conv(xg_ref, cw_ref, cb_ref, order + 1)
        y_ref[...] = xg * (acc_ref[...] + skip_ref[order:order + 1, :] * u_f32())


def hyena_conv(order, u, hyz, conv_w, conv_b, fre, fim, ic, is_, spectra, skip):
    b = hyz.shape[0]
    fb = HY_CONV_FBLK
    w = HY_WIDTH
    lat = lambda part: pl.BlockSpec((None, SEQ, w), lambda i, f: (i, 0, part))
    full = lambda a: pl.BlockSpec(a.shape, lambda i, f: (0, 0))
    return pl.pallas_call(
        functools.partial(_hyconv_kernel, order=order),
        grid=(b, SEQ // fb),
        in_specs=[lat(0), lat(order + 1), full(conv_w), full(conv_b),
                  pl.BlockSpec((fb, SEQ), lambda i, f: (f, 0)),
                  pl.BlockSpec((fb, SEQ), lambda i, f: (f, 0)),
                  pl.BlockSpec((SEQ, fb), lambda i, f: (0, f)),
                  pl.BlockSpec((SEQ, fb), lambda i, f: (0, f)),
                  pl.BlockSpec((None, 2, fb, w), lambda i, f: (order, 0, f, 0)),
                  full(skip)],
        out_specs=pl.BlockSpec((None, SEQ, w), lambda i, f: (i, 0, 0)),
        out_shape=jax.ShapeDtypeStruct((b, SEQ, w), F32),
        scratch_shapes=[pltpu.VMEM((SEQ, w), BF16), pltpu.VMEM((SEQ, w), F32)],
        compiler_params=_params(("parallel", "arbitrary")),
        name="hyena_conv",
    )(u, hyz, conv_w, conv_b, fre, fim, ic, is_, spectra, skip)


def hyena_filters(length, w1, b1, w2, b2, w3, freq):
    hp = lax.Precision.HIGHEST
    t = jnp.arange(length, dtype=F32)[:, None]
    t_norm = t / max(length - 1, 1)
    bands = jnp.linspace(1e-4, HY_BANDS - 1, HY_BANDS, dtype=F32)
    ang = 2.0 * math.pi * t * bands / length
    z = jnp.concatenate([t_norm, jnp.cos(ang), -jnp.sin(ang)], axis=-1)
    h = jnp.sin(freq * (jnp.dot(z, w1, precision=hp) + b1))
    h = jnp.sin(freq * (jnp.dot(h, w2, precision=hp) + b2))
    h = jnp.dot(h, w3, precision=hp)
    deltas = jnp.abs(jnp.linspace(HY_MIN_DECAY, HY_MAX_DECAY, HY_WIDTH, dtype=F32))
    window = jnp.exp(-t_norm * deltas) + HY_SHIFT
    return h * jnp.tile(window, (1, HY_ORDER * 2))


def hyena(hyz, conv_w, conv_b, fw1, fb1, fw2, fb2, fw3, freq, skip):
    fre, fim, ic, is_ = dft_matrices()
    filt = hyena_filters(SEQ, fw1, fb1, fw2, fb2, fw3, freq)
    spectra = filter_spectra(filt, fre, fim)
    y = hyz
    for o in range(HY_ORDER):
        y = hyena_conv(o, y, hyz, conv_w, conv_b.reshape(1, -1), fre, fim, ic, is_, spectra, skip)
    return y


def _grid_angles(rot_dim):
    n_freq = rot_dim // 4
    inv_freq = ROPE_THETA ** (-jnp.arange(n_freq, dtype=F32) / n_freq)
    t = jnp.arange(SEQ)
    r = (t // GRID_W).astype(F32)
    c_ = (t % GRID_W).astype(F32)
    return jnp.concatenate([r[:, None] * inv_freq, c_[:, None] * inv_freq], axis=-1)


def _rope_tables(rot_dim, lane_offsets):
    ang = _grid_angles(rot_dim)
    c, s = jnp.cos(ang), jnp.sin(ang)
    cos_parts, sin_parts, lane = [], [], 0
    for off in lane_offsets:
        cos_parts += [jnp.ones((SEQ, off - lane), F32), c, c]
        sin_parts += [jnp.zeros((SEQ, off - lane), F32), -s, s]
        lane = off + rot_dim
    cos_parts.append(jnp.ones((SEQ, LANES - lane), F32))
    sin_parts.append(jnp.zeros((SEQ, LANES - lane), F32))
    cos = jnp.concatenate([jnp.concatenate(cos_parts, axis=1), jnp.ones((CTX_LEN, LANES), F32)], axis=0)
    sin = jnp.concatenate([jnp.concatenate(sin_parts, axis=1), jnp.zeros((CTX_LEN, LANES), F32)], axis=0)
    return cos, sin


def _block_diag(w):
    nb, bs, _ = w.shape
    eye = jnp.eye(nb, dtype=w.dtype)
    return (eye[:, None, :, None] * w[:, :, None, :]).reshape(nb * bs, nb * bs)


def _gqa_pair_order():
    g = GQA_Q_HEADS // GQA_KV_HEADS
    heads = []
    for p in range(g):
        heads += [p, g + p]
    return np.concatenate([np.arange(h * HEAD_DIM, (h + 1) * HEAD_DIM) for h in heads])


def _pad_heads(w, n_heads, width):
    k = w.shape[0]
    w = w.reshape(k, n_heads, width)
    return jnp.pad(w, ((0, 0), (0, 0), (0, LANES - width))).reshape(k, n_heads * LANES)


def kernel(x, c, ctx, c_ctx, mod_w, mod_b, norm1_g, norm2_g, ab_w_in, ab_w_out, lru_conv_w, lru_conv_b, lru_w_a, lru_b_a, lru_w_x, lru_b_x, lru_lambda, gqa_q_norm, gqa_k_norm, ffn_w1, ffn_w3, ffn_w2, cd_w_in, cd_w_out, hy_conv_w, hy_conv_b, hy_filt_w1, hy_filt_b1, hy_filt_w2, hy_filt_b2, hy_filt_w3, hy_sin_freq, hy_skip, mla_q_a_norm, mla_q_b, mla_kv_a_norm, mla_kv_b, mla_q_norm, mla_k_norm, moe_router, moe_w1, moe_w3, moe_w2):
    batch = x.shape[0]
    bf = lambda w: w.astype(BF16)
    row = lambda v: v.reshape(1, -1)

    silu_all = jnp.concatenate([jax.nn.silu(c), jax.nn.silu(c_ctx)[None, :],
                                jnp.zeros((16 - batch - 1, D_MODEL), F32)], axis=0)
    mods = []
    mod_all = stacked_matmul(silu_all, mod_w, 1536) + mod_b[:, None, :]
    for layer in range(DEPTH):
        m = mod_all[layer]
        lat = m[:batch].reshape(batch, N_MOD, 1, D_MODEL)
        cx = jnp.broadcast_to(m[batch].reshape(1, N_MOD, 1, D_MODEL), lat.shape)
        mods.append(jnp.stack([lat, cx], axis=2))

    perm = _gqa_pair_order()
    q0 = 2 * LRU_WIDTH
    w_in0 = ab_w_in[0]
    w_in0 = jnp.concatenate([w_in0[:, :q0], w_in0[:, q0 + perm], w_in0[:, q0 + GQA_Q_HEADS * HEAD_DIM:]], axis=1)
    cos_g, sin_g = _rope_tables(HEAD_DIM, (0, HEAD_DIM))
    q_gain = row(jnp.tile(gqa_q_norm[0], 2) * HEAD_DIM ** -0.5)
    k_gain = row(jnp.tile(gqa_k_norm[0], 2))
    xr, gate, q, k, v = proj0(x, ctx, row(norm1_g[0]), mods[0], bf(w_in0), cos_g, sin_g, q_gain, k_gain)

    w_gates = jnp.concatenate([_block_diag(lru_w_a[0, 0]), _block_diag(lru_w_x[0, 0]),
                               _block_diag(lru_w_a[0, 1]), _block_diag(lru_w_x[0, 1])], axis=1)
    b_gates = jnp.concatenate([lru_b_a[0, 0].reshape(-1), lru_b_x[0, 0].reshape(-1),
                               lru_b_a[0, 1].reshape(-1), lru_b_x[0, 1].reshape(-1)])
    rec = rglru(xr, lru_conv_w[0], row(lru_conv_b[0]), bf(w_gates), row(b_gates), lru_lambda[0])

    att_l, moe_w1b, moe_w3b = attention_latent(
        q, k, v, GQA_HEADS, GQA_TILE_Q,
        cast=(moe_w1[0].reshape(-1, moe_w1.shape[-1]), moe_w3[0].reshape(-1, moe_w3.shape[-1])))
    att_c = attention_context(q, k, v, GQA_HEADS)
    w_out0 = ab_w_out[0]
    x1 = merge0(rec, gate, att_l, att_c, x, ctx, mods[0], bf(w_out0[:LRU_WIDTH]), bf(w_out0[LRU_WIDTH:][perm]))
    xa = ffn_residual(x1, mods[0], row(norm2_g[0]), bf(ffn_w1[0]), bf(ffn_w3[0]), bf(ffn_w2[0]))

    w_in1 = jnp.pad(cd_w_in[0], ((0, 0), (0, LANES - MLA_ROPE)))
    cos_m, sin_m = _rope_tables(MLA_ROPE, (MLA_NOPE,))
    qb = _pad_heads(mla_q_b[0], MLA_HEADS, MLA_QK)
    kvb = mla_kv_b[0].reshape(MLA_KV_RANK, MLA_HEADS, MLA_NOPE + MLA_V)
    kbk = _pad_heads(kvb[:, :, :MLA_NOPE].reshape(MLA_KV_RANK, -1), MLA_HEADS, MLA_NOPE)
    kbv = kvb[:, :, MLA_NOPE:].reshape(MLA_KV_RANK, -1)
    pad_gain = lambda g_: row(jnp.pad(g_, (0, LANES - MLA_QK)))
    hyz, q, k, v = proj1(xa, row(norm1_g[1]), mods[1], bf(w_in1), row(mla_q_a_norm[0]), bf(qb),
                         row(mla_kv_a_norm[0]), bf(kbk), bf(kbv), cos_m, sin_m,
                         pad_gain(mla_q_norm[0] * MLA_QK ** -0.5), pad_gain(mla_k_norm[0]))
    att = attention_latent(q, k, v, MLA_HEAD_GROUPS, MLA_TILE_Q)
    hy = hyena(hyz, hy_conv_w[0], hy_conv_b[0], hy_filt_w1[0], hy_filt_b1[0], hy_filt_w2[0],
               hy_filt_b2[0], hy_filt_w3[0], hy_sin_freq[0], hy_skip[0])
    w_out1 = cd_w_out[0]
    x1, h2, logits = merge1(hy, att, xa, mods[1], row(norm2_g[1]),
                            bf(w_out1[:HY_WIDTH]), bf(w_out1[HY_WIDTH:]), moe_router[0].T)
    return moe_residual(x1, h2, logits, mods[1], moe_w1b.reshape(moe_w1.shape[1:]),
                        moe_w3b.reshape(moe_w3.shape[1:]), moe_w2[0])
```

```python
import functools
import math

import jax
import jax.numpy as jnp
import numpy as np
from jax import lax
from jax.experimental import pallas as pl
from jax.experimental.pallas import tpu as pltpu

F32 = jnp.float32
BF16 = jnp.bfloat16

D_MODEL = 1024
SEQ = 2048
CTX_LEN = 256
S_ALL = SEQ + CTX_LEN
DEPTH = 2
GRID_W = 64
HEAD_DIM = 64
HALF_MIX = D_MODEL // 2
ROPE_THETA = 10000.0
EPS = 1e-6
N_MOD = 6
LRU_WIDTH = HALF_MIX
LRU_BLOCKS = LRU_WIDTH // HEAD_DIM
LRU_CONV = 4
LRU_C = 8.0
GQA_Q_HEADS = HALF_MIX // HEAD_DIM
GQA_KV_HEADS = 2
HY_WIDTH = HALF_MIX
HY_ORDER = 2
HY_CONV = 3
HY_BANDS = 16
HY_TARGET = 1e-2
HY_FAST_PCT = 0.3
HY_SLOW_PCT = 1.5
HY_MIN_DECAY = math.log(HY_TARGET) / HY_SLOW_PCT
HY_MAX_DECAY = math.log(HY_TARGET) / HY_FAST_PCT
HY_SHIFT = 0.05
HY_IN = (HY_ORDER + 1) * HY_WIDTH
MLA_HEADS = HALF_MIX // HEAD_DIM
MLA_Q_RANK = D_MODEL // 4
MLA_KV_RANK = D_MODEL // 8
MLA_NOPE = HEAD_DIM
MLA_ROPE = HEAD_DIM // 2
MLA_V = HEAD_DIM
MLA_QK = MLA_NOPE + MLA_ROPE
N_EXPERTS = 8
TOP_K = 2

LANES = 128
SUBLANES = 8
V7X_VMEM_LIMIT_BYTES = 56 * 1024 * 1024

TOKEN_TILE = 768
PROJ0_ROW_SPLIT = 2
PROJ1_ROW_SPLIT = 1
FFN_TILE_M = 768
FFN_TILE_F = 256
GQA_TILE_Q = 512
MLA_TILE_Q = 512
MERGE1_TILE = 1024
MOE_TILE_M = 512
MOE_TILE_F = 1792
MOE_ROUTE_BLOCK = 256
MOE_DISPATCH_TILE = 1024
MOE_COMBINE_TILE = 1024

LRU_GAP = SUBLANES
LRU_LAT0 = CTX_LEN + LRU_GAP
LRU_CHUNK = 464
LRU_ROWS = 2320
LRU_PAD_FRONT = SUBLANES

assert S_ALL % TOKEN_TILE == 0 and S_ALL % FFN_TILE_M == 0
assert SEQ % GQA_TILE_Q == 0 and SEQ % MLA_TILE_Q == 0
assert LRU_ROWS % LRU_CHUNK == 0 and LRU_ROWS >= LRU_LAT0 + SEQ + SUBLANES


def _params(sem):
    return pltpu.CompilerParams(dimension_semantics=sem, vmem_limit_bytes=V7X_VMEM_LIMIT_BYTES)


def _norm_mod(x, g, mod_pair, row0, shift_idx, scale_idx):
    y = x * lax.rsqrt(jnp.mean(x * x, axis=-1, keepdims=True) + EPS) * g
    shift = _mod_rows(mod_pair, shift_idx, row0, x.shape[0])
    scale = _mod_rows(mod_pair, scale_idx, row0, x.shape[0])
    return y * (1.0 + scale) + shift


def _mod_rows(mod_pair, idx, row0, rows):
    row = row0 + lax.broadcasted_iota(jnp.int32, (rows, 1), 0)
    return jnp.where(row >= SEQ, mod_pair[idx, 1], mod_pair[idx, 0])


def _mm_kernel(x_ref, w_ref, o_ref):
    o_ref[...] = jnp.dot(x_ref[...].astype(BF16), w_ref[...].astype(BF16),
                         preferred_element_type=F32)


def stacked_matmul(x, w, tn):
    m, k = x.shape
    n_l, _, n = w.shape
    assert n % tn == 0
    return pl.pallas_call(
        _mm_kernel,
        grid=(n_l, n // tn),
        in_specs=[pl.BlockSpec((m, k), lambda l, j: (0, 0)),
                  pl.BlockSpec((None, k, tn), lambda l, j: (l, 0, j))],
        out_specs=pl.BlockSpec((None, m, tn), lambda l, j: (l, 0, j)),
        out_shape=jax.ShapeDtypeStruct((n_l, m, n), F32),
        compiler_params=_params(("parallel", "parallel")),
        name="matmul",
    )(x, w)


def _rope_pair(a, cos, sin, first_half, shift):
    rot = jnp.where(first_half, pltpu.roll(a, LANES - shift, 1), pltpu.roll(a, shift, 1))
    return a * cos + rot * sin


def _group_sums(a, ones_ref):
    return jnp.dot((a * a).astype(BF16), ones_ref[...], preferred_element_type=F32)


def _proj0_kernel(x_ref, xt_ref, ctx_ref, g_ref, mod_ref, w_ref, cos_ref, sin_ref, qg_ref, kg_ref, ones_ref,
                  xr_ref, gate_ref, q_ref, k_ref, v_ref):
    tm = x_ref.shape[0]
    t = pl.program_id(1)
    last = jnp.concatenate([xt_ref[...], ctx_ref[...]], axis=0)
    rs = tm // PROJ0_ROW_SPLIT
    first_half = (lax.broadcasted_iota(jnp.int32, (rs, LANES), 1) & (HEAD_DIM // 2)) == 0
    q0 = 2 * LRU_WIDTH
    k0 = q0 + GQA_Q_HEADS * HEAD_DIM
    for r0 in range(0, tm, rs):
        rows = slice(r0, r0 + rs)
        x = jnp.where(t < SEQ // tm, x_ref[rows, :], last[rows])
        h = _norm_mod(x, g_ref[...], mod_ref, t * tm + r0, 0, 1)
        z = jnp.dot(h.astype(BF16), w_ref[...], preferred_element_type=F32)
        xr_ref[rows, :] = z[:, :LRU_WIDTH]
        gate_ref[rows, :] = z[:, LRU_WIDTH:2 * LRU_WIDTH].astype(BF16)
        cos, sin = cos_ref[rows, :], sin_ref[rows, :]

        def head_pair(a, gain):
            inv = lax.rsqrt(_group_sums(a, ones_ref) / HEAD_DIM + EPS)
            return _rope_pair(a * inv * gain, cos, sin, first_half, HEAD_DIM // 2).astype(BF16)

        for p in range(GQA_Q_HEADS // 2):
            q_ref[rows, p * LANES:(p + 1) * LANES] = head_pair(
                z[:, q0 + p * LANES:q0 + (p + 1) * LANES], qg_ref[...])
        k_ref[rows, :] = head_pair(z[:, k0:k0 + LANES], kg_ref[...])
        v_ref[rows, :] = z[:, k0 + LANES:k0 + 2 * LANES].astype(BF16)


def proj0(x, ctx, g, modp, w_in, cos, sin, q_gain, k_gain):
    b = x.shape[0]
    tm = TOKEN_TILE
    n = w_in.shape[1]
    n_full = SEQ // tm
    tail = SEQ - n_full * tm
    assert tail + CTX_LEN == tm and SEQ % tail == 0
    tok = lambda width: pl.BlockSpec((None, tm, width), lambda i, t: (i, t, 0))
    full = lambda r, c_: pl.BlockSpec((r, c_), lambda i, t: (0, 0))
    half = jnp.arange(LANES) // HEAD_DIM
    ones_pair = (half[:, None] == half[None, :]).astype(BF16)
    return pl.pallas_call(
        _proj0_kernel,
        grid=(b, S_ALL // tm),
        in_specs=[pl.BlockSpec((None, tm, D_MODEL), lambda i, t: (i, jnp.minimum(t, n_full - 1), 0)),
                  pl.BlockSpec((None, tail, D_MODEL), lambda i, t: (i, SEQ // tail - 1, 0)),
                  pl.BlockSpec((None, CTX_LEN, D_MODEL), lambda i, t: (i, 0, 0)),
                  full(1, D_MODEL),
                  pl.BlockSpec((None, N_MOD, 2, 1, D_MODEL), lambda i, t: (i, 0, 0, 0, 0)),
                  full(D_MODEL, n),
                  pl.BlockSpec((tm, LANES), lambda i, t: (t, 0)),
                  pl.BlockSpec((tm, LANES), lambda i, t: (t, 0)),
                  full(1, LANES), full(1, LANES), full(LANES, LANES)],
        out_specs=[tok(LRU_WIDTH), tok(LRU_WIDTH), tok(GQA_Q_HEADS * HEAD_DIM), tok(LANES), tok(LANES)],
        out_shape=[jax.ShapeDtypeStruct((b, S_ALL, LRU_WIDTH), F32),
                   jax.ShapeDtypeStruct((b, S_ALL, LRU_WIDTH), BF16),
                   jax.ShapeDtypeStruct((b, S_ALL, GQA_Q_HEADS * HEAD_DIM), BF16),
                   jax.ShapeDtypeStruct((b, S_ALL, LANES), BF16),
                   jax.ShapeDtypeStruct((b, S_ALL, LANES), BF16)],
        compiler_params=_params(("parallel", "parallel")),
        name="proj0",
    )(x, x, ctx, g, modp, w_in, cos, sin, q_gain, k_gain, ones_pair)


def _lru_kernel(xr_ref, cw_ref, cb_ref, wg_ref, bg_ref, lam_ref, rec_ref,
                pad_ref, af_ref, hf_ref, ab_ref, hb_ref):
    w = LRU_WIDTH
    pad_ref[...] = jnp.zeros_like(pad_ref)
    pad_ref[LRU_PAD_FRONT:LRU_PAD_FRONT + CTX_LEN, :] = xr_ref[SEQ:S_ALL, :]
    pad_ref[LRU_PAD_FRONT + LRU_LAT0:LRU_PAD_FRONT + LRU_LAT0 + SEQ, :] = xr_ref[0:SEQ, :]
    lam = lam_ref[...]
    neg_c_softplus = -LRU_C * (jnp.maximum(-lam, 0.0) + jnp.log(1.0 + jnp.exp(-jnp.abs(lam))))
    left = LRU_CONV // 2
    for ch in range(LRU_ROWS // LRU_CHUNK):
        r0 = ch * LRU_CHUNK
        u = cb_ref[...]
        for j in range(LRU_CONV):
            start = LRU_PAD_FRONT + r0 - left + j
            u = u + cw_ref[j:j + 1, :] * pad_ref[start:start + LRU_CHUNK, :]
        gz = jnp.dot(u.astype(BF16), wg_ref[...], preferred_element_type=F32) + bg_ref[...]
        for d, (a_ref, h_ref) in enumerate(((af_ref, hf_ref), (ab_ref, hb_ref))):
            r = jax.nn.sigmoid(gz[:, 2 * d * w:(2 * d + 1) * w])
            i = jax.nn.sigmoid(gz[:, (2 * d + 1) * w:(2 * d + 2) * w])
            a = jnp.exp(neg_c_softplus[d:d + 1, :] * r)
            a_ref[r0:r0 + LRU_CHUNK, :] = a
            h_ref[r0:r0 + LRU_CHUNK, :] = jnp.sqrt(1.0 - a * a) * (i * u)

    n_ctx_groups = CTX_LEN // SUBLANES
    sub = lax.broadcasted_iota(jnp.int32, (SUBLANES, w), 0)

    def tile_scan(a, b, descending):
        for s in (1, 2, 4):
            if descending:
                ok, shift = sub < SUBLANES - s, SUBLANES - s
            else:
                ok, shift = sub >= s, s
            a_s = jnp.where(ok, pltpu.roll(a, shift, 0), 1.0)
            b_s = jnp.where(ok, pltpu.roll(b, shift, 0), 0.0)
            a, b = a * a_s, a * b_s + b
        return a, b

    def group(tt, carry):
        hf, hb = carry
        is_lat = tt >= n_ctx_groups
        rf = pl.ds(pl.multiple_of(tt * SUBLANES + jnp.where(is_lat, LRU_GAP, 0), SUBLANES), SUBLANES)
        rb = pl.ds(pl.multiple_of(
            jnp.where(is_lat, LRU_LAT0 + SEQ + CTX_LEN, CTX_LEN) - (tt + 1) * SUBLANES, SUBLANES), SUBLANES)
        a, b = tile_scan(af_ref[rf, :], hf_ref[rf, :], False)
        h = a * hf + b
        hf_ref[rf, :] = h
        hf = jnp.broadcast_to(h[SUBLANES - 1:SUBLANES, :], (SUBLANES, w))
        a, b = tile_scan(ab_ref[rb, :], hb_ref[rb, :], True)
        h = a * hb + b
        hb_ref[rb, :] = h
        hb = jnp.broadcast_to(h[0:1, :], (SUBLANES, w))
        return hf, hb

    zero = jnp.zeros((SUBLANES, w), F32)
    lax.fori_loop(0, S_ALL // SUBLANES, group, (zero, zero), unroll=2)
    dt = rec_ref.dtype
    rec_ref[0:SEQ, :] = (hf_ref[LRU_LAT0:LRU_LAT0 + SEQ, :] + hb_ref[LRU_LAT0:LRU_LAT0 + SEQ, :]).astype(dt)
    rec_ref[SEQ:S_ALL, :] = (hf_ref[0:CTX_LEN, :] + hb_ref[0:CTX_LEN, :]).astype(dt)


def rglru(xr, conv_w, conv_b, w_gates, b_gates, lam):
    b = xr.shape[0]
    w = LRU_WIDTH
    full = lambda r, c_: pl.BlockSpec((r, c_), lambda i: (0, 0))
    rows = pltpu.VMEM((LRU_ROWS, w), F32)
    return pl.pallas_call(
        _lru_kernel,
        grid=(b,),
        in_specs=[pl.BlockSpec((None, S_ALL, w), lambda i: (i, 0, 0)),
                  full(LRU_CONV, w), full(1, w), full(w, 4 * w), full(1, 4 * w), full(2, w)],
        out_specs=pl.BlockSpec((None, S_ALL, w), lambda i: (i, 0, 0)),
        out_shape=jax.ShapeDtypeStruct((b, S_ALL, w), BF16),
        scratch_shapes=[pltpu.VMEM((LRU_ROWS + 2 * LRU_PAD_FRONT, w), F32), rows, rows, rows, rows],
        compiler_params=_params(("parallel",)),
        name="rglru",
    )(xr, conv_w, conv_b, w_gates, b_gates, lam)


def _attn_kernel(q_ref, k_ref, v_ref, *rest, heads):
    n_cast = (len(rest) - 1) // 2
    o_ref = rest[n_cast]
    cast_pairs = list(zip(rest[:n_cast], rest[n_cast + 1:]))
    tq, sk = q_ref.shape[0], k_ref.shape[0]
    q_lo = lax.broadcasted_iota(jnp.int32, (tq, LANES), 1) < HEAD_DIM
    v_lo = lax.broadcasted_iota(jnp.int32, (sk, LANES), 1) < HEAD_DIM

    def keep(x, is_lo, half):
        if half is None:
            return x
        return jnp.where(is_lo if half == 0 else jnp.logical_not(is_lo), x, jnp.zeros_like(x))

    for og, members in enumerate(heads):
        acc = None
        for qg, qh, kg, vg, vh in members:
            q = keep(q_ref[:, qg * LANES:(qg + 1) * LANES], q_lo, qh)
            s = lax.dot_general(q, k_ref[:, kg * LANES:(kg + 1) * LANES],
                                (((1,), (1,)), ((), ())), preferred_element_type=F32)
            p = jnp.exp(s - jnp.max(s, axis=-1, keepdims=True))
            l = jnp.sum(p, axis=-1, keepdims=True)
            v = keep(v_ref[:, vg * LANES:(vg + 1) * LANES], v_lo, vh)
            o = jnp.dot(p.astype(BF16), v, preferred_element_type=F32) / l
            acc = o if acc is None else acc + o
        o_ref[:, og * LANES:(og + 1) * LANES] = acc.astype(o_ref.dtype)
        for src, dst in cast_pairs:
            r0, r1 = (og * src.shape[0] // len(heads), (og + 1) * src.shape[0] // len(heads))
            dst[r0:r1, :] = src[r0:r1, :].astype(BF16)


def attention(q, k, v, heads, q_block, n_q_blocks, tq, k_rows, cast=()):
    b, s, qw = q.shape
    kw, vw = k.shape[-1], v.shape[-1]
    ow = len(heads) * LANES
    n_steps = b * n_q_blocks
    kv_spec = lambda width: pl.BlockSpec((None, k_rows, width), lambda i, t: (i, s // k_rows - 1, 0))
    slab = lambda w: pl.BlockSpec((w.shape[0] // n_steps, w.shape[1]), lambda i, t: (i * n_q_blocks + t, 0))
    assert all(w.shape[0] % (16 * len(heads) * n_steps) == 0 for w in cast)
    out = pl.pallas_call(
        functools.partial(_attn_kernel, heads=heads),
        grid=(b, n_q_blocks),
        in_specs=[pl.BlockSpec((None, tq, qw), lambda i, t: (i, q_block + t, 0)), kv_spec(kw), kv_spec(vw)]
        + [slab(w) for w in cast],
        out_specs=[pl.BlockSpec((None, tq, ow), lambda i, t: (i, t, 0))] + [slab(w) for w in cast],
        out_shape=[jax.ShapeDtypeStruct((b, n_q_blocks * tq, ow), BF16)]
        + [jax.ShapeDtypeStruct(w.shape, BF16) for w in cast],
        compiler_params=_params(("parallel", "parallel")),
        name="attention",
    )(q, k, v, *cast)
    return out if cast else out[0]


GQA_HEADS = tuple(((p, 0, 0, 0, 0), (p, 1, 0, 0, 1)) for p in range(GQA_Q_HEADS // 2))
MLA_HEAD_GROUPS = tuple(((2 * p, None, 2 * p, p, 0), (2 * p + 1, None, 2 * p + 1, p, 1))
                        for p in range(MLA_HEADS // 2))


def attention_latent(q, k, v, heads, tq, cast=()):
    return attention(q, k, v, heads, 0, SEQ // tq, tq, S_ALL, cast)


def attention_context(q, k, v, heads):
    return attention(q, k, v, heads, SEQ // CTX_LEN, 1, CTX_LEN, CTX_LEN)


def _merge0_kernel(rec_ref, gate_ref, att_l_ref, att_c_ref, x_ref, ctx_ref, mod_ref,
                   w_rec_ref, w_att_ref, x1_ref):
    tm = x_ref.shape[0]
    row0 = pl.program_id(1) * tm
    is_ctx = row0 >= SEQ
    rg = (rec_ref[...].astype(F32) * jax.nn.gelu(gate_ref[...].astype(F32))).astype(BF16)
    att = jnp.where(is_ctx, att_c_ref[...], att_l_ref[...])
    y = (jnp.dot(rg, w_rec_ref[...], preferred_element_type=F32)
         + jnp.dot(att, w_att_ref[...], preferred_element_type=F32))
    x1_ref[...] = jnp.where(is_ctx, ctx_ref[...], x_ref[...]) + _mod_rows(mod_ref, 2, row0, tm) * y


def merge0(rec, gate, att_l, att_c, x, ctx, modp, w_rec, w_att):
    b = x.shape[0]
    tm = CTX_LEN
    n_lat = SEQ // tm
    tok = lambda width: pl.BlockSpec((None, tm, width), lambda i, t: (i, t, 0))
    lat = lambda width: pl.BlockSpec((None, tm, width), lambda i, t: (i, jnp.minimum(t, n_lat - 1), 0))
    cx = lambda width: pl.BlockSpec((None, tm, width), lambda i, t: (i, 0, 0))
    full = lambda r, c_: pl.BlockSpec((r, c_), lambda i, t: (0, 0))
    return pl.pallas_call(
        _merge0_kernel,
        grid=(b, S_ALL // tm),
        in_specs=[tok(LRU_WIDTH), tok(LRU_WIDTH), lat(HALF_MIX), cx(HALF_MIX), lat(D_MODEL), cx(D_MODEL),
                  pl.BlockSpec((None, N_MOD, 2, 1, D_MODEL), lambda i, t: (i, 0, 0, 0, 0)),
                  full(LRU_WIDTH, D_MODEL), full(HALF_MIX, D_MODEL)],
        out_specs=tok(D_MODEL),
        out_shape=jax.ShapeDtypeStruct((b, S_ALL, D_MODEL), F32),
        compiler_params=_params(("parallel", "parallel")),
        name="merge0",
    )(rec, gate, att_l, att_c, x, ctx, modp, w_rec, w_att)


def _swiglu(x, w1_ref, w3_ref, w2_ref, act_ref):
    for c in range(w1_ref.shape[1] // FFN_TILE_F):
        sl = slice(c * FFN_TILE_F, (c + 1) * FFN_TILE_F)
        h1 = jnp.dot(x, w1_ref[:, sl], preferred_element_type=F32)
        h3 = jnp.dot(x, w3_ref[:, sl], preferred_element_type=F32)
        act_ref[:, sl] = (h1 * jax.nn.sigmoid(h1) * h3).astype(BF16)
    return jnp.dot(act_ref[...], w2_ref[...], preferred_element_type=F32)


def _accumulate(acc_ref, part, j):
    @pl.when(j == 0)
    def _():
        acc_ref[...] = part

    @pl.when(j > 0)
    def _():
        acc_ref[...] += part


def _ffn_kernel(x1_ref, x1n_ref, mod_ref, modn_ref, g2_ref, w1_ref, w3_ref, w2_ref, o_ref, act_ref, xn_ref):
    tm = x1_ref.shape[0]
    nt = pl.num_programs(1)
    t = pl.program_id(1)
    step = pl.program_id(0) * nt + t
    slot = step % 2

    @pl.when(step == 0)
    def _():
        xn_ref[0] = _norm_mod(x1_ref[...], g2_ref[...], mod_ref, 0, 3, 4).astype(BF16)

    y = _swiglu(xn_ref[slot], w1_ref, w3_ref, w2_ref, act_ref)
    o_ref[...] = x1_ref[...] + _mod_rows(mod_ref, 5, t * tm, tm) * y
    t_next = jnp.where(t + 1 == nt, 0, t + 1)
    xn_ref[1 - slot] = _norm_mod(x1n_ref[...], g2_ref[...], modn_ref, t_next * tm, 3, 4).astype(BF16)


def ffn_residual(x1, modp, g2, w1, w3, w2):
    b = x1.shape[0]
    tm = FFN_TILE_M
    nt = S_ALL // tm
    f = w1.shape[1]
    assert f % FFN_TILE_F == 0

    def nxt(i, t):
        flat = jnp.minimum(i * nt + t + 1, b * nt - 1)
        return flat // nt, flat % nt

    tok = lambda: pl.BlockSpec((None, tm, D_MODEL), lambda i, t: (i, t, 0))
    mod_spec = lambda which: pl.BlockSpec((None, N_MOD, 2, 1, D_MODEL), lambda i, t: (which(i, t), 0, 0, 0, 0))
    resident = lambda a: pl.BlockSpec(a.shape, lambda i, t: (0, 0), pipeline_mode=pl.Buffered(1))
    return pl.pallas_call(
        _ffn_kernel,
        grid=(b, nt),
        in_specs=[tok(),
                  pl.BlockSpec((None, tm, D_MODEL), lambda i, t: (*nxt(i, t), 0)),
                  mod_spec(lambda i, t: i), mod_spec(lambda i, t: nxt(i, t)[0]),
                  pl.BlockSpec((1, D_MODEL), lambda i, t: (0, 0)),
                  resident(w1), resident(w3), resident(w2)],
        out_specs=tok(),
        out_shape=jax.ShapeDtypeStruct((b, S_ALL, D_MODEL), F32),
        scratch_shapes=[pltpu.VMEM((tm, f), BF16), pltpu.VMEM((2, tm, D_MODEL), BF16)],
        compiler_params=_params(("arbitrary", "arbitrary")),
        name="ffn",
    )(x1, x1, modp, modp, g2, w1, w3, w2)


def _moe_kernel(te_ref, nv_ref, x_ref, w1_ref, w3_ref, w2_ref, o_ref, xb_ref, acc_ref, act_ref):
    i, j = pl.program_id(0), pl.program_id(1)
    last = pl.num_programs(1) - 1
    valid = i < nv_ref[0]

    @pl.when(valid)
    def _():
        @pl.when(j == 0)
        def _():
            xb_ref[...] = x_ref[...].astype(BF16)

        _accumulate(acc_ref, _swiglu(xb_ref[...], w1_ref, w3_ref, w2_ref, act_ref), j)

        @pl.when(j == last)
        def _():
            o_ref[...] = acc_ref[...]

    @pl.when(jnp.logical_and(jnp.logical_not(valid), j == last))
    def _():
        o_ref[...] = jnp.zeros_like(o_ref)


def grouped_swiglu(xs, tile_expert, n_valid, w1, w3, w2, tm, tf):
    n_rows, d = xs.shape
    f = w1.shape[-1]
    assert n_rows % tm == 0 and f % tf == 0
    n_f = f // tf

    def col(i, j, nv):
        return jnp.where(i < nv[0], j, n_f - 1)

    grid_spec = pltpu.PrefetchScalarGridSpec(
        num_scalar_prefetch=2,
        grid=(n_rows // tm, n_f),
        in_specs=[pl.BlockSpec((tm, d), lambda i, j, te, nv: (jnp.minimum(i, nv[0] - 1), 0)),
                  pl.BlockSpec((None, d, tf), lambda i, j, te, nv: (te[i], 0, col(i, j, nv))),
                  pl.BlockSpec((None, d, tf), lambda i, j, te, nv: (te[i], 0, col(i, j, nv))),
                  pl.BlockSpec((None, tf, d), lambda i, j, te, nv: (te[i], col(i, j, nv), 0))],
        out_specs=pl.BlockSpec((tm, d), lambda i, j, te, nv: (i, 0)),
        scratch_shapes=[pltpu.VMEM((tm, d), BF16), pltpu.VMEM((tm, d), F32), pltpu.VMEM((tm, tf), BF16)],
    )
    return pl.pallas_call(
        _moe_kernel,
        grid_spec=grid_spec,
        out_shape=jax.ShapeDtypeStruct((n_rows, d), F32),
        compiler_params=_params(("parallel", "arbitrary")),
        name="moe",
    )(tile_expert, n_valid, xs, w1, w3, w2)


def route(logits, tm):
    n_tok = logits.shape[0]
    n_assign = n_tok * TOP_K
    idx = jnp.arange(N_EXPERTS, dtype=jnp.int32)[None, :]
    m1 = jnp.max(logits, axis=-1, keepdims=True)
    e1 = jnp.min(jnp.where(logits == m1, idx, N_EXPERTS), axis=-1, keepdims=True)
    rest = jnp.where(idx == e1, jnp.finfo(F32).min, logits)
    m2 = jnp.max(rest, axis=-1, keepdims=True)
    e2 = jnp.min(jnp.where(rest == m2, idx, N_EXPERTS), axis=-1, keepdims=True)
    t = jnp.exp(m2 - m1)
    gates = jnp.concatenate([1.0 / (1.0 + t), t / (1.0 + t)], axis=-1)
    flat_e = jnp.concatenate([e1, e2], axis=-1).reshape(n_assign, 1)
    onehot = (flat_e == idx).astype(F32)
    blk = MOE_ROUTE_BLOCK
    nb = n_assign // blk
    oh = onehot.reshape(nb, blk, N_EXPERTS)
    lower = lambda n: (jnp.arange(n)[:, None] > jnp.arange(n)[None, :]).astype(F32)
    within = jnp.einsum('ij,bjk->bik', lower(blk), oh)
    blk_cnt = jnp.sum(oh, axis=1)
    blk_off = jnp.dot(lower(nb), blk_cnt, precision=lax.Precision.HIGHEST)
    rank = jnp.sum((within + blk_off[:, None, :]) * oh, axis=-1).reshape(n_assign)
    counts = jnp.sum(blk_cnt, axis=0).astype(jnp.int32)
    padded = (counts + tm - 1) // tm * tm
    ends = []
    for e in range(N_EXPERTS):
        ends.append(padded[e] + (ends[-1] if ends else 0))
    pend = jnp.stack(ends)
    pstart = pend - padded
    dest = (jnp.sum(onehot * pstart.astype(F32)[None, :], axis=-1) + rank).astype(jnp.int32)
    n_tiles = n_assign // tm + N_EXPERTS
    starts = jnp.arange(n_tiles, dtype=jnp.int32) * tm
    tile_expert = jnp.minimum(jnp.sum((pend[None, :] <= starts[:, None]).astype(jnp.int32), axis=-1),
                              N_EXPERTS - 1)
    n_valid = (pend[-1] // tm).reshape(1)
    meta = jnp.concatenate([pend, padded, n_valid])
    return gates, dest, tile_expert, n_valid, meta


def _row_copies(dest_ref, src, dst, sem, rows, gather, side_work=()):
    n_groups = rows // SUBLANES
    for g in range(n_groups):
        for r in range(g * SUBLANES, (g + 1) * SUBLANES):
            for kk in range(TOP_K):
                d = dest_ref[0, TOP_K * r + kk]
                if gather:
                    copy = pltpu.make_async_copy(src.at[pl.ds(d, 1), :], dst.at[kk, pl.ds(r, 1), :], sem)
                else:
                    copy = pltpu.make_async_copy(src.at[pl.ds(r, 1), :], dst.at[pl.ds(d, 1), :], sem)
                copy.start(priority=kk % 2)
        for thunk in side_work[g * len(side_work) // n_groups:(g + 1) * len(side_work) // n_groups]:
            thunk()
    if gather:
        pltpu.make_async_copy(dst, dst, sem).wait()
    else:
        for _ in range(TOP_K):
            pltpu.make_async_copy(src, src, sem).wait()


def _dispatch_kernel(meta_ref, dest_ref, h_ref, w_ref, xs_ref, wb_ref, zero_ref, sem, zsem):
    tm_moe = zero_ref.shape[0]

    @pl.when(pl.program_id(0) == 0)
    def _():
        zero_ref[...] = jnp.zeros_like(zero_ref)
        n_tiles = xs_ref.shape[0] // tm_moe
        fills = []
        for e in range(N_EXPERTS):
            start = pl.multiple_of(jnp.maximum(meta_ref[e] - tm_moe, 0), tm_moe)
            fills.append((meta_ref[N_EXPERTS + e] > 0, start))
        for i in range(n_tiles - N_EXPERTS, n_tiles):
            fills.append((i >= meta_ref[2 * N_EXPERTS], i * tm_moe))
        for cond, start in fills:
            @pl.when(cond)
            def _():
                pltpu.make_async_copy(zero_ref, xs_ref.at[pl.ds(start, tm_moe), :], zsem).start()
        for cond, start in fills:
            @pl.when(cond)
            def _():
                pltpu.make_async_copy(zero_ref, xs_ref.at[pl.ds(start, tm_moe), :], zsem).wait()

    def cast_rows(c):
        def thunk():
            wb_ref[c * 16:(c + 1) * 16, :] = w_ref[c * 16:(c + 1) * 16, :].astype(BF16)
        return thunk

    casts = [cast_rows(c) for c in range(w_ref.shape[0] // 16)]
    _row_copies(dest_ref, h_ref, xs_ref, sem, h_ref.shape[0], gather=False, side_work=casts)


def moe_dispatch(h2, dest, meta, n_slots, w):
    n_tok, d = h2.shape
    tm = MOE_DISPATCH_TILE
    n_steps = n_tok // tm
    wf = w.reshape(-1, w.shape[-1])
    assert wf.shape[0] % (16 * n_steps) == 0
    slab = pl.BlockSpec((wf.shape[0] // n_steps, wf.shape[1]), lambda t, m: (t, 0))
    grid_spec = pltpu.PrefetchScalarGridSpec(
        num_scalar_prefetch=1,
        grid=(n_steps,),
        in_specs=[pl.BlockSpec((None, 1, TOP_K * tm), lambda t, m: (t, 0, 0), memory_space=pltpu.SMEM),
                  pl.BlockSpec((tm, d), lambda t, m: (t, 0)), slab],
        out_specs=[pl.BlockSpec(memory_space=pl.ANY), slab],
        scratch_shapes=[pltpu.VMEM((MOE_TILE_M, d), h2.dtype), pltpu.SemaphoreType.DMA,
                        pltpu.SemaphoreType.DMA],
    )
    xs, wb = pl.pallas_call(
        _dispatch_kernel,
        grid_spec=grid_spec,
        out_shape=[jax.ShapeDtypeStruct((n_slots, d), h2.dtype), jax.ShapeDtypeStruct(wf.shape, BF16)],
        compiler_params=_params(("arbitrary",)),
        name="moe_dispatch",
    )(meta, dest.reshape(n_steps, 1, TOP_K * tm), h2, wf)
    return xs, wb.reshape(w.shape)


def _combine_kernel(dest_ref, x1_ref, g_ref, mod_ref, ys_ref, o_ref, buf_ref, sem):
    tm = x1_ref.shape[0]
    _row_copies(dest_ref, ys_ref, buf_ref, sem, tm, gather=True)
    g = g_ref[...]
    moe = g[:, 0:1] * buf_ref[0] + g[:, 1:2] * buf_ref[1]
    o_ref[...] = x1_ref[...] + mod_ref[5, 0] * moe


def moe_combine(x1, gates, dest, ys, modp):
    b, s, d = x1.shape
    tm = MOE_COMBINE_TILE
    nt = s // tm
    return pl.pallas_call(
        _combine_kernel,
        grid=(b, nt),
        in_specs=[pl.BlockSpec((None, 1, TOP_K * tm), lambda i, t: (i * nt + t, 0, 0), memory_space=pltpu.SMEM),
                  pl.BlockSpec((None, tm, d), lambda i, t: (i, t, 0)),
                  pl.BlockSpec((None, tm, TOP_K), lambda i, t: (i, t, 0)),
                  pl.BlockSpec((None, N_MOD, 2, 1, d), lambda i, t: (i, 0, 0, 0, 0)),
                  pl.BlockSpec(memory_space=pl.ANY)],
        out_specs=pl.BlockSpec((None, tm, d), lambda i, t: (i, t, 0)),
        out_shape=jax.ShapeDtypeStruct((b, s, d), F32),
        scratch_shapes=[pltpu.VMEM((TOP_K, tm, d), F32), pltpu.SemaphoreType.DMA],
        compiler_params=_params(("arbitrary", "arbitrary")),
        name="moe_combine",
    )(dest.reshape(b * nt, 1, TOP_K * tm), x1, gates.reshape(b, s, TOP_K), modp, ys)


def moe_residual(x1, h2, logits, modp, w1, w3, w2):
    b, s, d = x1.shape
    n_tok = b * s
    tm = MOE_TILE_M
    gates, dest, tile_expert, n_valid, meta = route(logits.reshape(n_tok, N_EXPERTS), tm)
    n_slots = (n_tok * TOP_K // tm + N_EXPERTS) * tm
    xs, w2 = moe_dispatch(h2.reshape(n_tok, d), dest, meta, n_slots, w2)
    ys = grouped_swiglu(xs, tile_expert, n_valid, w1, w3, w2, tm, MOE_TILE_F)
    return moe_combine(x1, gates, dest, ys, modp)


def _proj1_kernel(x_ref, g_ref, mod_ref, w_ref, qan_ref, qb_ref, qbs_ref, kvan_ref, kbk_ref, kbv_ref,
                  cos_ref, sin_ref, gains_ref, ones_ref, hy_ref, q_ref, k_ref, v_ref):
    tm = x_ref.shape[0]
    rs = tm // PROJ1_ROW_SPLIT
    c_q, c_kv = HY_IN, HY_IN + MLA_Q_RANK
    c_r = c_kv + MLA_KV_RANK
    first_half = lax.broadcasted_iota(jnp.int32, (rs, LANES), 1) < MLA_NOPE + MLA_ROPE // 2
    sh = MLA_ROPE // 2
    qg, qg_sw, kg, kg_sw = (gains_ref[i:i + 1, :] for i in range(4))

    def rms(a, g):
        return a * lax.rsqrt(jnp.mean(a * a, axis=-1, keepdims=True) + EPS) * g

    for r0 in range(0, tm, rs):
        rows = slice(r0, r0 + rs)
        h = _norm_mod(x_ref[rows, :], g_ref[...], mod_ref, pl.program_id(1) * tm + r0, 0, 1)
        z = jnp.dot(h.astype(BF16), w_ref[...], preferred_element_type=F32)
        hy_ref[rows, :] = z[:, :HY_IN]
        qa = rms(z[:, c_q:c_kv], qan_ref[...]).astype(BF16)
        kva = rms(z[:, c_kv:c_r], kvan_ref[...]).astype(BF16)
        q = jnp.dot(qa, qb_ref[...], preferred_element_type=F32)
        q_sw = jnp.dot(qa, qbs_ref[...], preferred_element_type=F32)
        kk = jnp.dot(kva, kbk_ref[...], preferred_element_type=F32)
        v_ref[rows, :] = jnp.dot(kva, kbv_ref[...], preferred_element_type=F32).astype(BF16)
        k_rope = pltpu.roll(z[:, c_r:c_r + LANES], MLA_NOPE, 1)
        k_rope_sw = jnp.where(first_half, pltpu.roll(k_rope, LANES - sh, 1), pltpu.roll(k_rope, sh, 1))
        cos, sin = cos_ref[rows, :], sin_ref[rows, :]

        def head(a, a_sw, gain, gain_sw):
            inv = lax.rsqrt(_group_sums(a, ones_ref) / MLA_QK + EPS)
            return (a * (inv * gain) * cos + a_sw * (inv * gain_sw) * sin).astype(BF16)

        for hh in range(MLA_HEADS):
            sl = slice(hh * LANES, (hh + 1) * LANES)
            q_ref[rows, sl] = head(q[:, sl], q_sw[:, sl], qg, qg_sw)
            k_ref[rows, sl] = head(kk[:, sl] + k_rope, k_rope_sw, kg, kg_sw)


def _swap_rope_halves(a):
    lead = a.shape[:-1]
    g = a.reshape(*lead, -1, LANES)
    lo, mid = MLA_NOPE, MLA_NOPE + MLA_ROPE // 2
    g = jnp.concatenate([g[..., :lo], g[..., mid:MLA_QK], g[..., lo:mid], g[..., MLA_QK:]], axis=-1)
    return g.reshape(a.shape)


def proj1(xa, g, modp, w_in, qan, qb, kvan, kbk, kbv, cos, sin, q_gain, k_gain):
    b = xa.shape[0]
    tm = TOKEN_TILE
    tok = lambda width: pl.BlockSpec((None, tm, width), lambda i, t: (i, t, 0))
    full = lambda a: pl.BlockSpec(a.shape, lambda i, t: (0, 0))
    hw = MLA_HEADS * LANES
    qbs = _swap_rope_halves(qb)
    gains = jnp.concatenate([q_gain, _swap_rope_halves(q_gain), k_gain, _swap_rope_halves(k_gain)], axis=0)
    ones = jnp.ones((LANES, LANES), BF16)
    return pl.pallas_call(
        _proj1_kernel,
        grid=(b, S_ALL // tm),
        in_specs=[tok(D_MODEL), full(g),
                  pl.BlockSpec((None, N_MOD, 2, 1, D_MODEL), lambda i, t: (i, 0, 0, 0, 0)),
                  full(w_in), full(qan), full(qb), full(qbs), full(kvan), full(kbk), full(kbv),
                  pl.BlockSpec((tm, LANES), lambda i, t: (t, 0)),
                  pl.BlockSpec((tm, LANES), lambda i, t: (t, 0)),
                  full(gains), full(ones)],
        out_specs=[tok(HY_IN), tok(hw), tok(hw), tok(MLA_HEADS * MLA_V)],
        out_shape=[jax.ShapeDtypeStruct((b, S_ALL, HY_IN), F32),
                   jax.ShapeDtypeStruct((b, S_ALL, hw), BF16),
                   jax.ShapeDtypeStruct((b, S_ALL, hw), BF16),
                   jax.ShapeDtypeStruct((b, S_ALL, MLA_HEADS * MLA_V), BF16)],
        compiler_params=_params(("parallel", "parallel")),
        name="proj1",
    )(xa, g, modp, w_in, qan, qb, qbs, kvan, kbk, kbv, cos, sin, gains, ones)


def _merge1_kernel(hy_ref, att_ref, x_ref, mod_ref, g2_ref, w_hy_ref, w_att_ref, r_ref,
                   x1_ref, h2_ref, lg_ref):
    y = (jnp.dot(hy_ref[...].astype(BF16), w_hy_ref[...], preferred_element_type=F32)
         + jnp.dot(att_ref[...], w_att_ref[...], preferred_element_type=F32))
    x1 = x_ref[...] + mod_ref[2, 0] * y
    x1_ref[...] = x1
    h2 = _norm_mod(x1, g2_ref[...], mod_ref, 0, 3, 4)
    h2_ref[...] = h2
    lane = lax.broadcasted_iota(jnp.int32, lg_ref.shape, 1)
    logits = jnp.zeros(lg_ref.shape, F32)
    for e in range(N_EXPERTS):
        col = jnp.sum(h2 * r_ref[e:e + 1, :], axis=-1, keepdims=True)
        logits = jnp.where(lane == e, col, logits)
    lg_ref[...] = logits


def merge1(hy, att, xa, modp, g2, w_hy, w_att, router):
    b = xa.shape[0]
    tm = MERGE1_TILE
    tok = lambda width: pl.BlockSpec((None, tm, width), lambda i, t: (i, t, 0))
    full = lambda a: pl.BlockSpec(a.shape, lambda i, t: (0, 0))
    return pl.pallas_call(
        _merge1_kernel,
        grid=(b, SEQ // tm),
        in_specs=[tok(HY_WIDTH), tok(HALF_MIX), tok(D_MODEL),
                  pl.BlockSpec((None, N_MOD, 2, 1, D_MODEL), lambda i, t: (i, 0, 0, 0, 0)),
                  full(g2), full(w_hy), full(w_att), full(router)],
        out_specs=[tok(D_MODEL), tok(D_MODEL), tok(N_EXPERTS)],
        out_shape=[jax.ShapeDtypeStruct((b, SEQ, D_MODEL), F32),
                   jax.ShapeDtypeStruct((b, SEQ, D_MODEL), F32),
                   jax.ShapeDtypeStruct((b, SEQ, N_EXPERTS), F32)],
        compiler_params=_params(("parallel", "parallel")),
        name="merge1",
    )(hy, att, xa, modp, g2, w_hy, w_att, router)


HY_N = 2 * SEQ
HY_TW_ROWS = 256
HY_SPEC_FBLK = 256
HY_CONV_FBLK = 512
HY_LANE_TILES = SEQ // LANES


def _twiddle_kernel(ca_ref, sa_ref, cb_ref, sb_ref, fre_ref, fim_ref, ic_ref, is_ref):
    rows = fre_ref.shape[0]
    r = pl.program_id(0) * rows + lax.broadcasted_iota(jnp.int32, (rows, LANES), 0)
    alt_r = (1 - 2 * (r & 1)).astype(F32)
    cb, sb = cb_ref[...], sb_ref[...]
    for a in range(HY_LANE_TILES):
        ca, sa = ca_ref[:, a:a + 1], sa_ref[:, a:a + 1]
        c = ca * cb - sa * sb
        s = sa * cb + ca * sb
        col = a * LANES + lax.broadcasted_iota(jnp.int32, (rows, LANES), 1)
        alt_c = (1 - 2 * (col & 1)).astype(F32)
        w = jnp.where(col == 0, 1.0 / HY_N, 2.0 / HY_N)
        sl = slice(a * LANES, (a + 1) * LANES)
        fre_ref[:, sl] = c.astype(BF16)
        fim_ref[:, sl] = jnp.where(r == 0, alt_c, -s).astype(BF16)
        ic_ref[:, sl] = (c * w).astype(BF16)
        is_ref[:, sl] = jnp.where(col == 0, alt_r / HY_N, -s * w).astype(BF16)


def dft_matrices():
    idx = jnp.arange(SEQ, dtype=jnp.int32)[:, None]
    step = 2.0 * math.pi / HY_N
    ph_a = ((idx * (LANES * jnp.arange(HY_LANE_TILES, dtype=jnp.int32))[None, :]) % HY_N).astype(F32) * step
    ph_b = ((idx * jnp.arange(LANES, dtype=jnp.int32)[None, :]) % HY_N).astype(F32) * step
    rows = HY_TW_ROWS
    tab = lambda width: pl.BlockSpec((rows, width), lambda i: (i, 0))
    out = jax.ShapeDtypeStruct((SEQ, SEQ), BF16)
    return pl.pallas_call(
        _twiddle_kernel,
        grid=(SEQ // rows,),
        in_specs=[tab(HY_LANE_TILES), tab(HY_LANE_TILES), tab(LANES), tab(LANES)],
        out_specs=[tab(SEQ)] * 4,
        out_shape=[out] * 4,
        compiler_params=_params(("parallel",)),
        name="twiddle",
    )(jnp.cos(ph_a), jnp.sin(ph_a), jnp.cos(ph_b), jnp.sin(ph_b))


def _spec_kernel(fre_ref, fim_ref, h_ref, k_ref):
    w = HY_WIDTH
    h = h_ref[...].astype(BF16)
    re = jnp.dot(fre_ref[...], h, preferred_element_type=F32)
    im = jnp.dot(fim_ref[...], h, preferred_element_type=F32)
    hb0 = h[0:1, w:].astype(F32)
    first = (pl.program_id(1) * re.shape[0] + lax.broadcasted_iota(jnp.int32, (re.shape[0], 1), 0)) == 0
    k_ref[0] = re[:, :w] + re[:, w:] - hb0
    k_ref[1] = jnp.where(first, im[:, :w] + im[:, w:] - hb0, im[:, :w] - im[:, w:])


def filter_spectra(filt, fre, fim):
    fb = HY_SPEC_FBLK
    return pl.pallas_call(
        _spec_kernel,
        grid=(HY_ORDER, SEQ // fb),
        in_specs=[pl.BlockSpec((fb, SEQ), lambda o, f: (f, 0)),
                  pl.BlockSpec((fb, SEQ), lambda o, f: (f, 0)),
                  pl.BlockSpec((SEQ, 2 * HY_WIDTH), lambda o, f: (0, o))],
        out_specs=pl.BlockSpec((None, 2, fb, HY_WIDTH), lambda o, f: (o, 0, f, 0)),
        out_shape=jax.ShapeDtypeStruct((HY_ORDER, 2, SEQ, HY_WIDTH), F32),
        compiler_params=_params(("parallel", "parallel")),
        name="filter_spectra",
    )(fre, fim, filt)


def _short_conv(z_ref, w_ref, b_ref, part):
    z = z_ref[...]
    l = z.shape[0]
    t = lax.broadcasted_iota(jnp.int32, (l, 1), 0)
    prev = jnp.where(t == 0, 0.0, pltpu.roll(z, 1, 0))
    nxt = jnp.where(t == l - 1, 0.0, pltpu.roll(z, l - 1, 0))
    cs = slice(part * HY_WIDTH, (part + 1) * HY_WIDTH)
    return (w_ref[0:1, cs] * prev + w_ref[1:2, cs] * z + w_ref[2:3, cs] * nxt) + b_ref[:, cs]


def _hyconv_kernel(u_ref, xg_ref, cw_ref, cb_ref, fre_ref, fim_ref, ic_ref, is_ref, k_ref, skip_ref,
                   y_ref, ub_ref, acc_ref, *, order):
    f = pl.program_id(1)

    def u_f32():
        return _short_conv(u_ref, cw_ref, cb_ref, 0) if order == 0 else u_ref[...]

    @pl.when(f == 0)
    def _():
        ub_ref[...] = u_f32().astype(BF16)

    x_re = jnp.dot(fre_ref[...], ub_ref[...], preferred_element_type=F32)
    x_im = jnp.dot(fim_ref[...], ub_ref[...], preferred_element_type=F32)
    k_re, k_im = k_ref[0], k_ref[1]
    first = (f * x_re.shape[0] + lax.broadcasted_iota(jnp.int32, (x_re.shape[0], 1), 0)) == 0
    y_re = x_re * k_re - jnp.where(first, 0.0, x_im * k_im)
    y_im = x_im * jnp.where(first, k_im, k_re) + jnp.where(first, 0.0, x_re * k_im)
    part = (jnp.dot(ic_ref[...], y_re.astype(BF16), preferred_element_type=F32)
            + jnp.dot(is_ref[...], y_im.astype(BF16), preferred_element_type=F32))
    _accumulate(acc_ref, part, f)

    @pl.when(f == pl.num_programs(1) - 1)
    def _():
        xg = _short_conv(xg_ref, cw_ref, cb_ref, order + 1)
        y_ref[...] = xg * (acc_ref[...] + skip_ref[order:order + 1, :] * u_f32())


def hyena_conv(order, u, hyz, conv_w, conv_b, fre, fim, ic, is_, spectra, skip):
    b = hyz.shape[0]
    fb = HY_CONV_FBLK
    w = HY_WIDTH
    lat = lambda part: pl.BlockSpec((None, SEQ, w), lambda i, f: (i, 0, part))
    full = lambda a: pl.BlockSpec(a.shape, lambda i, f: (0, 0))
    return pl.pallas_call(
        functools.partial(_hyconv_kernel, order=order),
        grid=(b, SEQ // fb),
        in_specs=[lat(0), lat(order + 1), full(conv_w), full(conv_b),
                  pl.BlockSpec((fb, SEQ), lambda i, f: (f, 0)),
                  pl.BlockSpec((fb, SEQ), lambda i, f: (f, 0)),
                  pl.BlockSpec((SEQ, fb), lambda i, f: (0, f)),
                  pl.BlockSpec((SEQ, fb), lambda i, f: (0, f)),
                  pl.BlockSpec((None, 2, fb, w), lambda i, f: (order, 0, f, 0)),
                  full(skip)],
        out_specs=pl.BlockSpec((None, SEQ, w), lambda i, f: (i, 0, 0)),
        out_shape=jax.ShapeDtypeStruct((b, SEQ, w), F32),
        scratch_shapes=[pltpu.VMEM((SEQ, w), BF16), pltpu.VMEM((SEQ, w), F32)],
        compiler_params=_params(("parallel", "arbitrary")),
        name="hyena_conv",
    )(u, hyz, conv_w, conv_b, fre, fim, ic, is_, spectra, skip)


def hyena_filters(length, w1, b1, w2, b2, w3, freq):
    hp = lax.Precision.HIGHEST
    t = jnp.arange(length, dtype=F32)[:, None]
    t_norm = t / max(length - 1, 1)
    bands = jnp.linspace(1e-4, HY_BANDS - 1, HY_BANDS, dtype=F32)
    ang = 2.0 * math.pi * t * bands / length
    z = jnp.concatenate([t_norm, jnp.cos(ang), -jnp.sin(ang)], axis=-1)
    h = jnp.sin(freq * (jnp.dot(z, w1, precision=hp) + b1))
    h = jnp.sin(freq * (jnp.dot(h, w2, precision=hp) + b2))
    h = jnp.dot(h, w3, precision=hp)
    deltas = jnp.abs(jnp.linspace(HY_MIN_DECAY, HY_MAX_DECAY, HY_WIDTH, dtype=F32))
    window = jnp.exp(-t_norm * deltas) + HY_SHIFT
    return h * jnp.tile(window, (1, HY_ORDER * 2))


def hyena(hyz, conv_w, conv_b, fw1, fb1, fw2, fb2, fw3, freq, skip):
    fre, fim, ic, is_ = dft_matrices()
    filt = hyena_filters(SEQ, fw1, fb1, fw2, fb2, fw3, freq)
    spectra = filter_spectra(filt, fre, fim)
    y = hyz
    for o in range(HY_ORDER):
        y = hyena_conv(o, y, hyz, conv_w, conv_b.reshape(1, -1), fre, fim, ic, is_, spectra, skip)
    return y


def _grid_angles(rot_dim):
    n_freq = rot_dim // 4
    inv_freq = ROPE_THETA ** (-jnp.arange(n_freq, dtype=F32) / n_freq)
    t = jnp.arange(SEQ)
    r = (t // GRID_W).astype(F32)
    c_ = (t % GRID_W).astype(F32)
    return jnp.concatenate([r[:, None] * inv_freq, c_[:, None] * inv_freq], axis=-1)


def _rope_tables(rot_dim, lane_offsets):
    ang = _grid_angles(rot_dim)
    c, s = jnp.cos(ang), jnp.sin(ang)
    cos_parts, sin_parts, lane = [], [], 0
    for off in lane_offsets:
        cos_parts += [jnp.ones((SEQ, off - lane), F32), c, c]
        sin_parts += [jnp.zeros((SEQ, off - lane), F32), -s, s]
        lane = off + rot_dim
    cos_parts.append(jnp.ones((SEQ, LANES - lane), F32))
    sin_parts.append(jnp.zeros((SEQ, LANES - lane), F32))
    cos = jnp.concatenate([jnp.concatenate(cos_parts, axis=1), jnp.ones((CTX_LEN, LANES), F32)], axis=0)
    sin = jnp.concatenate([jnp.concatenate(sin_parts, axis=1), jnp.zeros((CTX_LEN, LANES), F32)], axis=0)
    return cos, sin


def _block_diag(w):
    nb, bs, _ = w.shape
    eye = jnp.eye(nb, dtype=w.dtype)
    return (eye[:, None, :, None] * w[:, :, None, :]).reshape(nb * bs, nb * bs)


def _gqa_pair_order():
    g = GQA_Q_HEADS // GQA_KV_HEADS
    heads = []
    for p in range(g):
        heads += [p, g + p]
    return np.concatenate([np.arange(h * HEAD_DIM, (h + 1) * HEAD_DIM) for h in heads])


def _pad_heads(w, n_heads, width):
    k = w.shape[0]
    w = w.reshape(k, n_heads, width)
    return jnp.pad(w, ((0, 0), (0, 0), (0, LANES - width))).reshape(k, n_heads * LANES)


def kernel(x, c, ctx, c_ctx, mod_w, mod_b, norm1_g, norm2_g, ab_w_in, ab_w_out, lru_conv_w, lru_conv_b, lru_w_a, lru_b_a, lru_w_x, lru_b_x, lru_lambda, gqa_q_norm, gqa_k_norm, ffn_w1, ffn_w3, ffn_w2, cd_w_in, cd_w_out, hy_conv_w, hy_conv_b, hy_filt_w1, hy_filt_b1, hy_filt_w2, hy_filt_b2, hy_filt_w3, hy_sin_freq, hy_skip, mla_q_a_norm, mla_q_b, mla_kv_a_norm, mla_kv_b, mla_q_norm, mla_k_norm, moe_router, moe_w1, moe_w3, moe_w2):
    batch = x.shape[0]
    bf = lambda w: w.astype(BF16)
    row = lambda v: v.reshape(1, -1)

    silu_all = jnp.concatenate([jax.nn.silu(c), jax.nn.silu(c_ctx)[None, :],
                                jnp.zeros((16 - batch - 1, D_MODEL), F32)], axis=0)
    mods = []
    mod_all = stacked_matmul(silu_all, mod_w, 1536) + mod_b[:, None, :]
    for layer in range(DEPTH):
        m = mod_all[layer]
        lat = m[:batch].reshape(batch, N_MOD, 1, D_MODEL)
        cx = jnp.broadcast_to(m[batch].reshape(1, N_MOD, 1, D_MODEL), lat.shape)
        mods.append(jnp.stack([lat, cx], axis=2))

    perm = _gqa_pair_order()
    q0 = 2 * LRU_WIDTH
    w_in0 = ab_w_in[0]
    w_in0 = jnp.concatenate([w_in0[:, :q0], w_in0[:, q0 + perm], w_in0[:, q0 + GQA_Q_HEADS * HEAD_DIM:]], axis=1)
    cos_g, sin_g = _rope_tables(HEAD_DIM, (0, HEAD_DIM))
    q_gain = row(jnp.tile(gqa_q_norm[0], 2) * HEAD_DIM ** -0.5)
    k_gain = row(jnp.tile(gqa_k_norm[0], 2))
    xr, gate, q, k, v = proj0(x, ctx, row(norm1_g[0]), mods[0], bf(w_in0), cos_g, sin_g, q_gain, k_gain)

    w_gates = jnp.concatenate([_block_diag(lru_w_a[0, 0]), _block_diag(lru_w_x[0, 0]),
                               _block_diag(lru_w_a[0, 1]), _block_diag(lru_w_x[0, 1])], axis=1)
    b_gates = jnp.concatenate([lru_b_a[0, 0].reshape(-1), lru_b_x[0, 0].reshape(-1),
                               lru_b_a[0, 1].reshape(-1), lru_b_x[0, 1].reshape(-1)])
    rec = rglru(xr, lru_conv_w[0], row(lru_conv_b[0]), bf(w_gates), row(b_gates), lru_lambda[0])

    att_l, moe_w1b, moe_w3b = attention_latent(
        q, k, v, GQA_HEADS, GQA_TILE_Q,
        cast=(moe_w1[0].reshape(-1, moe_w1.shape[-1]), moe_w3[0].reshape(-1, moe_w3.shape[-1])))
    att_c = attention_context(q, k, v, GQA_HEADS)
    w_out0 = ab_w_out[0]
    x1 = merge0(rec, gate, att_l, att_c, x, ctx, mods[0], bf(w_out0[:LRU_WIDTH]), bf(w_out0[LRU_WIDTH:][perm]))
    xa = ffn_residual(x1, mods[0], row(norm2_g[0]), bf(ffn_w1[0]), bf(ffn_w3[0]), bf(ffn_w2[0]))

    w_in1 = jnp.pad(cd_w_in[0], ((0, 0), (0, LANES - MLA_ROPE)))
    cos_m, sin_m = _rope_tables(MLA_ROPE, (MLA_NOPE,))
    qb = _pad_heads(mla_q_b[0], MLA_HEADS, MLA_QK)
    kvb = mla_kv_b[0].reshape(MLA_KV_RANK, MLA_HEADS, MLA_NOPE + MLA_V)
    kbk = _pad_heads(kvb[:, :, :MLA_NOPE].reshape(MLA_KV_RANK, -1), MLA_HEADS, MLA_NOPE)
    kbv = kvb[:, :, MLA_NOPE:].reshape(MLA_KV_RANK, -1)
    pad_gain = lambda g_: row(jnp.pad(g_, (0, LANES - MLA_QK)))
    hyz, q, k, v = proj1(xa, row(norm1_g[1]), mods[1], bf(w_in1), row(mla_q_a_norm[0]), bf(qb),
                         row(mla_kv_a_norm[0]), bf(kbk), bf(kbv), cos_m, sin_m,
                         pad_gain(mla_q_norm[0] * MLA_QK ** -0.5), pad_gain(mla_k_norm[0]))
    att = attention_latent(q, k, v, MLA_HEAD_GROUPS, MLA_TILE_Q)
    hy = hyena(hyz, hy_conv_w[0], hy_conv_b[0], hy_filt_w1[0], hy_filt_b1[0], hy_filt_w2[0],
               hy_filt_b2[0], hy_filt_w3[0], hy_sin_freq[0], hy_skip[0])
    w_out1 = cd_w_out[0]
    x1, h2, logits = merge1(hy, att, xa, mods[1], row(norm2_g[1]),
                            bf(w_out1[:HY_WIDTH]), bf(w_out1[HY_WIDTH:]), moe_router[0].T)
    return moe_residual(x1, h2, logits, mods[1], moe_w1b.reshape(moe_w1.shape[1:]),
                        moe_w3b.reshape(moe_w3.shape[1:]), moe_w2[0])
```

```python
import functools
import math

import jax
import jax.numpy as jnp
import numpy as np
from jax import lax
from jax.experimental import pallas as pl
from jax.experimental.pallas import tpu as pltpu

F32 = jnp.float32
BF16 = jnp.bfloat16

D_MODEL = 1024
SEQ = 2048
CTX_LEN = 256
S_ALL = SEQ + CTX_LEN
DEPTH = 2
GRID_W = 64
HEAD_DIM = 64
HALF_MIX = D_MODEL // 2
ROPE_THETA = 10000.0
EPS = 1e-6
N_MOD = 6
LRU_WIDTH = HALF_MIX
LRU_BLOCKS = LRU_WIDTH // HEAD_DIM
LRU_CONV = 4
LRU_C = 8.0
GQA_Q_HEADS = HALF_MIX // HEAD_DIM
GQA_KV_HEADS = 2
HY_WIDTH = HALF_MIX
HY_ORDER = 2
HY_CONV = 3
HY_BANDS = 16
HY_TARGET = 1e-2
HY_FAST_PCT = 0.3
HY_SLOW_PCT = 1.5
HY_MIN_DECAY = math.log(HY_TARGET) / HY_SLOW_PCT
HY_MAX_DECAY = math.log(HY_TARGET) / HY_FAST_PCT
HY_SHIFT = 0.05
HY_IN = (HY_ORDER + 1) * HY_WIDTH
MLA_HEADS = HALF_MIX // HEAD_DIM
MLA_Q_RANK = D_MODEL // 4
MLA_KV_RANK = D_MODEL // 8
MLA_NOPE = HEAD_DIM
MLA_ROPE = HEAD_DIM // 2
MLA_V = HEAD_DIM
MLA_QK = MLA_NOPE + MLA_ROPE
N_EXPERTS = 8
TOP_K = 2

LANES = 128
SUBLANES = 8
V7X_VMEM_LIMIT_BYTES = 56 * 1024 * 1024

TOKEN_TILE = 768
PROJ0_ROW_SPLIT = 2
PROJ1_ROW_SPLIT = 1
FFN_TILE_M = 768
FFN_TILE_F = 256
GQA_TILE_Q = 512
MLA_TILE_Q = 512
MERGE1_TILE = 1024
MOE_TILE_M = 512
MOE_TILE_F = 1792
MOE_ROUTE_BLOCK = 256
MOE_DISPATCH_TILE = 1024
MOE_COMBINE_TILE = 1024

LRU_GAP = SUBLANES
LRU_LAT0 = CTX_LEN + LRU_GAP
LRU_CHUNK = 464
LRU_ROWS = 2320
LRU_PAD_FRONT = SUBLANES

assert S_ALL % TOKEN_TILE == 0 and S_ALL % FFN_TILE_M == 0
assert SEQ % GQA_TILE_Q == 0 and SEQ % MLA_TILE_Q == 0
assert LRU_ROWS % LRU_CHUNK == 0 and LRU_ROWS >= LRU_LAT0 + SEQ + SUBLANES


def _params(sem):
    return pltpu.CompilerParams(dimension_semantics=sem, vmem_limit_bytes=V7X_VMEM_LIMIT_BYTES)


def _norm_mod(x, g, mod_pair, row0, shift_idx, scale_idx):
    y = x * lax.rsqrt(jnp.mean(x * x, axis=-1, keepdims=True) + EPS) * g
    shift = _mod_rows(mod_pair, shift_idx, row0, x.shape[0])
    scale = _mod_rows(mod_pair, scale_idx, row0, x.shape[0])
    return y * (1.0 + scale) + shift


def _mod_rows(mod_pair, idx, row0, rows):
    row = row0 + lax.broadcasted_iota(jnp.int32, (rows, 1), 0)
    return jnp.where(row >= SEQ, mod_pair[idx, 1], mod_pair[idx, 0])


def _mm_kernel(x_ref, w_ref, o_ref):
    o_ref[...] = jnp.dot(x_ref[...].astype(BF16), w_ref[...].astype(BF16),
                         preferred_element_type=F32)


def stacked_matmul(x, w, tn):
    m, k = x.shape
    n_l, _, n = w.shape
    assert n % tn == 0
    return pl.pallas_call(
        _mm_kernel,
        grid=(n_l, n // tn),
        in_specs=[pl.BlockSpec((m, k), lambda l, j: (0, 0)),
                  pl.BlockSpec((None, k, tn), lambda l, j: (l, 0, j))],
        out_specs=pl.BlockSpec((None, m, tn), lambda l, j: (l, 0, j)),
        out_shape=jax.ShapeDtypeStruct((n_l, m, n), F32),
        compiler_params=_params(("parallel", "parallel")),
        name="matmul",
    )(x, w)


def _rope_pair(a, cos, sin, first_half, shift):
    rot = jnp.where(first_half, pltpu.roll(a, LANES - shift, 1), pltpu.roll(a, shift, 1))
    return a * cos + rot * sin


def _group_sums(a, ones_ref):
    return jnp.dot((a * a).astype(BF16), ones_ref[...], preferred_element_type=F32)


def _proj0_kernel(x_ref, xt_ref, ctx_ref, g_ref, mod_ref, w_ref, cos_ref, sin_ref, qg_ref, kg_ref, ones_ref,
                  xr_ref, gate_ref, q_ref, k_ref, v_ref):
    tm = x_ref.shape[0]
    t = pl.program_id(1)
    last = jnp.concatenate([xt_ref[...], ctx_ref[...]], axis=0)
    rs = tm // PROJ0_ROW_SPLIT
    first_half = (lax.broadcasted_iota(jnp.int32, (rs, LANES), 1) & (HEAD_DIM // 2)) == 0
    q0 = 2 * LRU_WIDTH
    k0 = q0 + GQA_Q_HEADS * HEAD_DIM
    for r0 in range(0, tm, rs):
        rows = slice(r0, r0 + rs)
        x = jnp.where(t < SEQ // tm, x_ref[rows, :], last[rows])
        h = _norm_mod(x, g_ref[...], mod_ref, t * tm + r0, 0, 1)
        z = jnp.dot(h.astype(BF16), w_ref[...], preferred_element_type=F32)
        xr_ref[rows, :] = z[:, :LRU_WIDTH]
        gate_ref[rows, :] = z[:, LRU_WIDTH:2 * LRU_WIDTH].astype(BF16)
        cos, sin = cos_ref[rows, :], sin_ref[rows, :]

        def head_pair(a, gain):
            inv = lax.rsqrt(_group_sums(a, ones_ref) / HEAD_DIM + EPS)
            return _rope_pair(a * inv * gain, cos, sin, first_half, HEAD_DIM // 2).astype(BF16)

        for p in range(GQA_Q_HEADS // 2):
            q_ref[rows, p * LANES:(p + 1) * LANES] = head_pair(
                z[:, q0 + p * LANES:q0 + (p + 1) * LANES], qg_ref[...])
        k_ref[rows, :] = head_pair(z[:, k0:k0 + LANES], kg_ref[...])
        v_ref[rows, :] = z[:, k0 + LANES:k0 + 2 * LANES].astype(BF16)


def proj0(x, ctx, g, modp, w_in, cos, sin, q_gain, k_gain):
    b = x.shape[0]
    tm = TOKEN_TILE
    n = w_in.shape[1]
    n_full = SEQ // tm
    tail = SEQ - n_full * tm
    assert tail + CTX_LEN == tm and SEQ % tail == 0
    tok = lambda width: pl.BlockSpec((None, tm, width), lambda i, t: (i, t, 0))
    full = lambda r, c_: pl.BlockSpec((r, c_), lambda i, t: (0, 0))
    half = jnp.arange(LANES) // HEAD_DIM
    ones_pair = (half[:, None] == half[None, :]).astype(BF16)
    return pl.pallas_call(
        _proj0_kernel,
        grid=(b, S_ALL // tm),
        in_specs=[pl.BlockSpec((None, tm, D_MODEL), lambda i, t: (i, jnp.minimum(t, n_full - 1), 0)),
                  pl.BlockSpec((None, tail, D_MODEL), lambda i, t: (i, SEQ // tail - 1, 0)),
                  pl.BlockSpec((None, CTX_LEN, D_MODEL), lambda i, t: (i, 0, 0)),
                  full(1, D_MODEL),
                  pl.BlockSpec((None, N_MOD, 2, 1, D_MODEL), lambda i, t: (i, 0, 0, 0, 0)),
                  full(D_MODEL, n),
                  pl.BlockSpec((tm, LANES), lambda i, t: (t, 0)),
                  pl.BlockSpec((tm, LANES), lambda i, t: (t, 0)),
                  full(1, LANES), full(1, LANES), full(LANES, LANES)],
        out_specs=[tok(LRU_WIDTH), tok(LRU_WIDTH), tok(GQA_Q_HEADS * HEAD_DIM), tok(LANES), tok(LANES)],
        out_shape=[jax.ShapeDtypeStruct((b, S_ALL, LRU_WIDTH), F32),
                   jax.ShapeDtypeStruct((b, S_ALL, LRU_WIDTH), BF16),
                   jax.ShapeDtypeStruct((b, S_ALL, GQA_Q_HEADS * HEAD_DIM), BF16),
                   jax.ShapeDtypeStruct((b, S_ALL, LANES), BF16),
                   jax.ShapeDtypeStruct((b, S_ALL, LANES), BF16)],
        compiler_params=_params(("parallel", "parallel")),
        name="proj0",
    )(x, x, ctx, g, modp, w_in, cos, sin, q_gain, k_gain, ones_pair)


def _lru_kernel(xr_ref, cw_ref, cb_ref, wg_ref, bg_ref, lam_ref, rec_ref,
                pad_ref, af_ref, hf_ref, ab_ref, hb_ref):
    w = LRU_WIDTH
    pad_ref[...] = jnp.zeros_like(pad_ref)
    pad_ref[LRU_PAD_FRONT:LRU_PAD_FRONT + CTX_LEN, :] = xr_ref[SEQ:S_ALL, :]
    pad_ref[LRU_PAD_FRONT + LRU_LAT0:LRU_PAD_FRONT + LRU_LAT0 + SEQ, :] = xr_ref[0:SEQ, :]
    lam = lam_ref[...]
    neg_c_softplus = -LRU_C * (jnp.maximum(-lam, 0.0) + jnp.log(1.0 + jnp.exp(-jnp.abs(lam))))
    left = LRU_CONV // 2
    for ch in range(LRU_ROWS // LRU_CHUNK):
        r0 = ch * LRU_CHUNK
        u = cb_ref[...]
        for j in range(LRU_CONV):
            start = LRU_PAD_FRONT + r0 - left + j
            u = u + cw_ref[j:j + 1, :] * pad_ref[start:start + LRU_CHUNK, :]
        gz = jnp.dot(u.astype(BF16), wg_ref[...], preferred_element_type=F32) + bg_ref[...]
        for d, (a_ref, h_ref) in enumerate(((af_ref, hf_ref), (ab_ref, hb_ref))):
            r = jax.nn.sigmoid(gz[:, 2 * d * w:(2 * d + 1) * w])
            i = jax.nn.sigmoid(gz[:, (2 * d + 1) * w:(2 * d + 2) * w])
            a = jnp.exp(neg_c_softplus[d:d + 1, :] * r)
            a_ref[r0:r0 + LRU_CHUNK, :] = a
            h_ref[r0:r0 + LRU_CHUNK, :] = jnp.sqrt(1.0 - a * a) * (i * u)

    n_ctx_groups = CTX_LEN // SUBLANES
    sub = lax.broadcasted_iota(jnp.int32, (SUBLANES, w), 0)

    def tile_scan(a, b, descending):
        for s in (1, 2, 4):
            if descending:
                ok, shift = sub < SUBLANES - s, SUBLANES - s
            else:
                ok, shift = sub >= s, s
            a_s = jnp.where(ok, pltpu.roll(a, shift, 0), 1.0)
            b_s = jnp.where(ok, pltpu.roll(b, shift, 0), 0.0)
            a, b = a * a_s, a * b_s + b
        return a, b

    def group(tt, carry):
        hf, hb = carry
        is_lat = tt >= n_ctx_groups
        rf = pl.ds(pl.multiple_of(tt * SUBLANES + jnp.where(is_lat, LRU_GAP, 0), SUBLANES), SUBLANES)
        rb = pl.ds(pl.multiple_of(
            jnp.where(is_lat, LRU_LAT0 + SEQ + CTX_LEN, CTX_LEN) - (tt + 1) * SUBLANES, SUBLANES), SUBLANES)
        a, b = tile_scan(af_ref[rf, :], hf_ref[rf, :], False)
        h = a * hf + b
        hf_ref[rf, :] = h
        hf = jnp.broadcast_to(h[SUBLANES - 1:SUBLANES, :], (SUBLANES, w))
        a, b = tile_scan(ab_ref[rb, :], hb_ref[rb, :], True)
        h = a * hb + b
        hb_ref[rb, :] = h
        hb = jnp.broadcast_to(h[0:1, :], (SUBLANES, w))
        return hf, hb

    zero = jnp.zeros((SUBLANES, w), F32)
    lax.fori_loop(0, S_ALL // SUBLANES, group, (zero, zero), unroll=2)
    dt = rec_ref.dtype
    rec_ref[0:SEQ, :] = (hf_ref[LRU_LAT0:LRU_LAT0 + SEQ, :] + hb_ref[LRU_LAT0:LRU_LAT0 + SEQ, :]).astype(dt)
    rec_ref[SEQ:S_ALL, :] = (hf_ref[0:CTX_LEN, :] + hb_ref[0:CTX_LEN, :]).astype(dt)


def rglru(xr, conv_w, conv_b, w_gates, b_gates, lam):
    b = xr.shape[0]
    w = LRU_WIDTH
    full = lambda r, c_: pl.BlockSpec((r, c_), lambda i: (0, 0))
    rows = pltpu.VMEM((LRU_ROWS, w), F32)
    return pl.pallas_call(
        _lru_kernel,
        grid=(b,),
        in_specs=[pl.BlockSpec((None, S_ALL, w), lambda i: (i, 0, 0)),
                  full(LRU_CONV, w), full(1, w), full(w, 4 * w), full(1, 4 * w), full(2, w)],
        out_specs=pl.BlockSpec((None, S_ALL, w), lambda i: (i, 0, 0)),
        out_shape=jax.ShapeDtypeStruct((b, S_ALL, w), BF16),
        scratch_shapes=[pltpu.VMEM((LRU_ROWS + 2 * LRU_PAD_FRONT, w), F32), rows, rows, rows, rows],
        compiler_params=_params(("parallel",)),
        name="rglru",
    )(xr, conv_w, conv_b, w_gates, b_gates, lam)


def _attn_kernel(q_ref, k_ref, v_ref, *rest, heads):
    n_cast = (len(rest) - 1) // 2
    o_ref = rest[n_cast]
    cast_pairs = list(zip(rest[:n_cast], rest[n_cast + 1:]))
    tq, sk = q_ref.shape[0], k_ref.shape[0]
    q_lo = lax.broadcasted_iota(jnp.int32, (tq, LANES), 1) < HEAD_DIM
    v_lo = lax.broadcasted_iota(jnp.int32, (sk, LANES), 1) < HEAD_DIM

    def keep(x, is_lo, half):
        if half is None:
            return x
        return jnp.where(is_lo if half == 0 else jnp.logical_not(is_lo), x, jnp.zeros_like(x))

    for og, members in enumerate(heads):
        acc = None
        for qg, qh, kg, vg, vh in members:
            q = keep(q_ref[:, qg * LANES:(qg + 1) * LANES], q_lo, qh)
            s = lax.dot_general(q, k_ref[:, kg * LANES:(kg + 1) * LANES],
                                (((1,), (1,)), ((), ())), preferred_element_type=F32)
            p = jnp.exp(s - jnp.max(s, axis=-1, keepdims=True))
            l = jnp.sum(p, axis=-1, keepdims=True)
            v = keep(v_ref[:, vg * LANES:(vg + 1) * LANES], v_lo, vh)
            o = jnp.dot(p.astype(BF16), v, preferred_element_type=F32) / l
            acc = o if acc is None else acc + o
        o_ref[:, og * LANES:(og + 1) * LANES] = acc.astype(o_ref.dtype)
        for src, dst in cast_pairs:
            r0, r1 = (og * src.shape[0] // len(heads), (og + 1) * src.shape[0] // len(heads))
            dst[r0:r1, :] = src[r0:r1, :].astype(BF16)


def attention(q, k, v, heads, q_block, n_q_blocks, tq, k_rows, cast=()):
    b, s, qw = q.shape
    kw, vw = k.shape[-1], v.shape[-1]
    ow = len(heads) * LANES
    n_steps = b * n_q_blocks
    kv_spec = lambda width: pl.BlockSpec((None, k_rows, width), lambda i, t: (i, s // k_rows - 1, 0))
    slab = lambda w: pl.BlockSpec((w.shape[0] // n_steps, w.shape[1]), lambda i, t: (i * n_q_blocks + t, 0))
    assert all(w.shape[0] % (16 * len(heads) * n_steps) == 0 for w in cast)
    out = pl.pallas_call(
        functools.partial(_attn_kernel, heads=heads),
        grid=(b, n_q_blocks),
        in_specs=[pl.BlockSpec((None, tq, qw), lambda i, t: (i, q_block + t, 0)), kv_spec(kw), kv_spec(vw)]
        + [slab(w) for w in cast],
        out_specs=[pl.BlockSpec((None, tq, ow), lambda i, t: (i, t, 0))] + [slab(w) for w in cast],
        out_shape=[jax.ShapeDtypeStruct((b, n_q_blocks * tq, ow), BF16)]
        + [jax.ShapeDtypeStruct(w.shape, BF16) for w in cast],
        compiler_params=_params(("parallel", "parallel")),
        name="attention",
    )(q, k, v, *cast)
    return out if cast else out[0]


GQA_HEADS = tuple(((p, 0, 0, 0, 0), (p, 1, 0, 0, 1)) for p in range(GQA_Q_HEADS // 2))
MLA_HEAD_GROUPS = tuple(((2 * p, None, 2 * p, p, 0), (2 * p + 1, None, 2 * p + 1, p, 1))
                        for p in range(MLA_HEADS // 2))


def attention_latent(q, k, v, heads, tq, cast=()):
    return attention(q, k, v, heads, 0, SEQ // tq, tq, S_ALL, cast)


def attention_context(q, k, v, heads):
    return attention(q, k, v, heads, SEQ // CTX_LEN, 1, CTX_LEN, CTX_LEN)


def _merge0_kernel(rec_ref, gate_ref, att_l_ref, att_c_ref, x_ref, ctx_ref, mod_ref,
                   w_rec_ref, w_att_ref, x1_ref):
    tm = x_ref.shape[0]
    row0 = pl.program_id(1) * tm
    is_ctx = row0 >= SEQ
    rg = (rec_ref[...].astype(F32) * jax.nn.gelu(gate_ref[...].astype(F32))).astype(BF16)
    att = jnp.where(is_ctx, att_c_ref[...], att_l_ref[...])
    y = (jnp.dot(rg, w_rec_ref[...], preferred_element_type=F32)
         + jnp.dot(att, w_att_ref[...], preferred_element_type=F32))
    x1_ref[...] = jnp.where(is_ctx, ctx_ref[...], x_ref[...]) + _mod_rows(mod_ref, 2, row0, tm) * y


def merge0(rec, gate, att_l, att_c, x, ctx, modp, w_rec, w_att):
    b = x.shape[0]
    tm = CTX_LEN
    n_lat = SEQ // tm
    tok = lambda width: pl.BlockSpec((None, tm, width), lambda i, t: (i, t, 0))
    lat = lambda width: pl.BlockSpec((None, tm, width), lambda i, t: (i, jnp.minimum(t, n_lat - 1), 0))
    cx = lambda width: pl.BlockSpec((None, tm, width), lambda i, t: (i, 0, 0))
    full = lambda r, c_: pl.BlockSpec((r, c_), lambda i, t: (0, 0))
    return pl.pallas_call(
        _merge0_kernel,
        grid=(b, S_ALL // tm),
        in_specs=[tok(LRU_WIDTH), tok(LRU_WIDTH), lat(HALF_MIX), cx(HALF_MIX), lat(D_MODEL), cx(D_MODEL),
                  pl.BlockSpec((None, N_MOD, 2, 1, D_MODEL), lambda i, t: (i, 0, 0, 0, 0)),
                  full(LRU_WIDTH, D_MODEL), full(HALF_MIX, D_MODEL)],
        out_specs=tok(D_MODEL),
        out_shape=jax.ShapeDtypeStruct((b, S_ALL, D_MODEL), F32),
        compiler_params=_params(("parallel", "parallel")),
        name="merge0",
    )(rec, gate, att_l, att_c, x, ctx, modp, w_rec, w_att)


def _swiglu(x, w1_ref, w3_ref, w2_ref, act_ref):
    for c in range(w1_ref.shape[1] // FFN_TILE_F):
        sl = slice(c * FFN_TILE_F, (c + 1) * FFN_TILE_F)
        h1 = jnp.dot(x, w1_ref[:, sl], preferred_element_type=F32)
        h3 = jnp.dot(x, w3_ref[:, sl], preferred_element_type=F32)
        act_ref[:, sl] = (h1 * jax.nn.sigmoid(h1) * h3).astype(BF16)
    return jnp.dot(act_ref[...], w2_ref[...], preferred_element_type=F32)


def _accumulate(acc_ref, part, j):
    @pl.when(j == 0)
    def _():
        acc_ref[...] = part

    @pl.when(j > 0)
    def _():
        acc_ref[...] += part


def _ffn_kernel(x1_ref, mod_ref, g2_ref, w1_ref, w3_ref, w2_ref, o_ref, act_ref):
    tm = x1_ref.shape[0]
    row0 = pl.program_id(1) * tm
    x = _norm_mod(x1_ref[...], g2_ref[...], mod_ref, row0, 3, 4).astype(BF16)
    y = _swiglu(x, w1_ref, w3_ref, w2_ref, act_ref)
    o_ref[...] = x1_ref[...] + _mod_rows(mod_ref, 5, row0, tm) * y


def ffn_residual(x1, modp, g2, w1, w3, w2):
    b = x1.shape[0]
    tm = FFN_TILE_M
    f = w1.shape[1]
    assert f % FFN_TILE_F == 0
    tok = lambda: pl.BlockSpec((None, tm, D_MODEL), lambda i, t: (i, t, 0))
    resident = lambda a: pl.BlockSpec(a.shape, lambda i, t: (0, 0), pipeline_mode=pl.Buffered(1))
    return pl.pallas_call(
        _ffn_kernel,
        grid=(b, S_ALL // tm),
        in_specs=[tok(),
                  pl.BlockSpec((None, N_MOD, 2, 1, D_MODEL), lambda i, t: (i, 0, 0, 0, 0)),
                  pl.BlockSpec((1, D_MODEL), lambda i, t: (0, 0)),
                  resident(w1), resident(w3), resident(w2)],
        out_specs=tok(),
        out_shape=jax.ShapeDtypeStruct((b, S_ALL, D_MODEL), F32),
        scratch_shapes=[pltpu.VMEM((tm, f), BF16)],
        compiler_params=_params(("parallel", "parallel")),
        name="ffn",
    )(x1, modp, g2, w1, w3, w2)


def _moe_kernel(te_ref, nv_ref, x_ref, w1_ref, w3_ref, w2_ref, o_ref, xb_ref, acc_ref, act_ref):
    i, j = pl.program_id(0), pl.program_id(1)
    last = pl.num_programs(1) - 1
    valid = i < nv_ref[0]

    @pl.when(valid)
    def _():
        @pl.when(j == 0)
        def _():
            xb_ref[...] = x_ref[...].astype(BF16)

        _accumulate(acc_ref, _swiglu(xb_ref[...], w1_ref, w3_ref, w2_ref, act_ref), j)

        @pl.when(j == last)
        def _():
            o_ref[...] = acc_ref[...]

    @pl.when(jnp.logical_and(jnp.logical_not(valid), j == last))
    def _():
        o_ref[...] = jnp.zeros_like(o_ref)


def grouped_swiglu(xs, tile_expert, n_valid, w1, w3, w2, tm, tf):
    n_rows, d = xs.shape
    f = w1.shape[-1]
    assert n_rows % tm == 0 and f % tf == 0
    n_f = f // tf

    def col(i, j, nv):
        return jnp.where(i < nv[0], j, n_f - 1)

    grid_spec = pltpu.PrefetchScalarGridSpec(
        num_scalar_prefetch=2,
        grid=(n_rows // tm, n_f),
        in_specs=[pl.BlockSpec((tm, d), lambda i, j, te, nv: (jnp.minimum(i, nv[0] - 1), 0)),
                  pl.BlockSpec((None, d, tf), lambda i, j, te, nv: (te[i], 0, col(i, j, nv))),
                  pl.BlockSpec((None, d, tf), lambda i, j, te, nv: (te[i], 0, col(i, j, nv))),
                  pl.BlockSpec((None, tf, d), lambda i, j, te, nv: (te[i], col(i, j, nv), 0))],
        out_specs=pl.BlockSpec((tm, d), lambda i, j, te, nv: (i, 0)),
        scratch_shapes=[pltpu.VMEM((tm, d), BF16), pltpu.VMEM((tm, d), F32), pltpu.VMEM((tm, tf), BF16)],
    )
    return pl.pallas_call(
        _moe_kernel,
        grid_spec=grid_spec,
        out_shape=jax.ShapeDtypeStruct((n_rows, d), F32),
        compiler_params=_params(("parallel", "arbitrary")),
        name="moe",
    )(tile_expert, n_valid, xs, w1, w3, w2)


def route(logits, tm):
    n_tok = logits.shape[0]
    n_assign = n_tok * TOP_K
    idx = jnp.arange(N_EXPERTS, dtype=jnp.int32)[None, :]
    m1 = jnp.max(logits, axis=-1, keepdims=True)
    e1 = jnp.min(jnp.where(logits == m1, idx, N_EXPERTS), axis=-1, keepdims=True)
    rest = jnp.where(idx == e1, jnp.finfo(F32).min, logits)
    m2 = jnp.max(rest, axis=-1, keepdims=True)
    e2 = jnp.min(jnp.where(rest == m2, idx, N_EXPERTS), axis=-1, keepdims=True)
    t = jnp.exp(m2 - m1)
    gates = jnp.concatenate([1.0 / (1.0 + t), t / (1.0 + t)], axis=-1)
    flat_e = jnp.concatenate([e1, e2], axis=-1).reshape(n_assign, 1)
    onehot = (flat_e == idx).astype(F32)
    blk = MOE_ROUTE_BLOCK
    nb = n_assign // blk
    oh = onehot.reshape(nb, blk, N_EXPERTS)
    lower = lambda n: (jnp.arange(n)[:, None] > jnp.arange(n)[None, :]).astype(F32)
    within = jnp.einsum('ij,bjk->bik', lower(blk), oh)
    blk_cnt = jnp.sum(oh, axis=1)
    blk_off = jnp.dot(lower(nb), blk_cnt, precision=lax.Precision.HIGHEST)
    rank = jnp.sum((within + blk_off[:, None, :]) * oh, axis=-1).reshape(n_assign)
    counts = jnp.sum(blk_cnt, axis=0).astype(jnp.int32)
    padded = (counts + tm - 1) // tm * tm
    ends = []
    for e in range(N_EXPERTS):
        ends.append(padded[e] + (ends[-1] if ends else 0))
    pend = jnp.stack(ends)
    pstart = pend - padded
    dest = (jnp.sum(onehot * pstart.astype(F32)[None, :], axis=-1) + rank).astype(jnp.int32)
    n_tiles = n_assign // tm + N_EXPERTS
    starts = jnp.arange(n_tiles, dtype=jnp.int32) * tm
    tile_expert = jnp.minimum(jnp.sum((pend[None, :] <= starts[:, None]).astype(jnp.int32), axis=-1),
                              N_EXPERTS - 1)
    n_valid = (pend[-1] // tm).reshape(1)
    meta = jnp.concatenate([pend, padded, n_valid])
    return gates, dest, tile_expert, n_valid, meta


def _row_copies(dest_ref, src, dst, sem, rows, gather, side_work=()):
    n_groups = rows // SUBLANES
    for g in range(n_groups):
        for r in range(g * SUBLANES, (g + 1) * SUBLANES):
            for kk in range(TOP_K):
                d = dest_ref[0, TOP_K * r + kk]
                if gather:
                    copy = pltpu.make_async_copy(src.at[pl.ds(d, 1), :], dst.at[kk, pl.ds(r, 1), :], sem)
                else:
                    copy = pltpu.make_async_copy(src.at[pl.ds(r, 1), :], dst.at[pl.ds(d, 1), :], sem)
                copy.start(priority=kk % 2)
        for thunk in side_work[g * len(side_work) // n_groups:(g + 1) * len(side_work) // n_groups]:
            thunk()
    if gather:
        pltpu.make_async_copy(dst, dst, sem).wait()
    else:
        for _ in range(TOP_K):
            pltpu.make_async_copy(src, src, sem).wait()


def _dispatch_kernel(meta_ref, dest_ref, h_ref, w_ref, xs_ref, wb_ref, zero_ref, sem, zsem):
    tm_moe = zero_ref.shape[0]

    @pl.when(pl.program_id(0) == 0)
    def _():
        zero_ref[...] = jnp.zeros_like(zero_ref)
        n_tiles = xs_ref.shape[0] // tm_moe
        fills = []
        for e in range(N_EXPERTS):
            start = pl.multiple_of(jnp.maximum(meta_ref[e] - tm_moe, 0), tm_moe)
            fills.append((meta_ref[N_EXPERTS + e] > 0, start))
        for i in range(n_tiles - N_EXPERTS, n_tiles):
            fills.append((i >= meta_ref[2 * N_EXPERTS], i * tm_moe))
        for cond, start in fills:
            @pl.when(cond)
            def _():
                pltpu.make_async_copy(zero_ref, xs_ref.at[pl.ds(start, tm_moe), :], zsem).start()
        for cond, start in fills:
            @pl.when(cond)
            def _():
                pltpu.make_async_copy(zero_ref, xs_ref.at[pl.ds(start, tm_moe), :], zsem).wait()

    def cast_rows(c):
        def thunk():
            wb_ref[c * 16:(c + 1) * 16, :] = w_ref[c * 16:(c + 1) * 16, :].astype(BF16)
        return thunk

    casts = [cast_rows(c) for c in range(w_ref.shape[0] // 16)]
    _row_copies(dest_ref, h_ref, xs_ref, sem, h_ref.shape[0], gather=False, side_work=casts)


def moe_dispatch(h2, dest, meta, n_slots, w):
    n_tok, d = h2.shape
    tm = MOE_DISPATCH_TILE
    n_steps = n_tok // tm
    wf = w.reshape(-1, w.shape[-1])
    assert wf.shape[0] % (16 * n_steps) == 0
    slab = pl.BlockSpec((wf.shape[0] // n_steps, wf.shape[1]), lambda t, m: (t, 0))
    grid_spec = pltpu.PrefetchScalarGridSpec(
        num_scalar_prefetch=1,
        grid=(n_steps,),
        in_specs=[pl.BlockSpec((None, 1, TOP_K * tm), lambda t, m: (t, 0, 0), memory_space=pltpu.SMEM),
                  pl.BlockSpec((tm, d), lambda t, m: (t, 0)), slab],
        out_specs=[pl.BlockSpec(memory_space=pl.ANY), slab],
        scratch_shapes=[pltpu.VMEM((MOE_TILE_M, d), h2.dtype), pltpu.SemaphoreType.DMA,
                        pltpu.SemaphoreType.DMA],
    )
    xs, wb = pl.pallas_call(
        _dispatch_kernel,
        grid_spec=grid_spec,
        out_shape=[jax.ShapeDtypeStruct((n_slots, d), h2.dtype), jax.ShapeDtypeStruct(wf.shape, BF16)],
        compiler_params=_params(("arbitrary",)),
        name="moe_dispatch",
    )(meta, dest.reshape(n_steps, 1, TOP_K * tm), h2, wf)
    return xs, wb.reshape(w.shape)


def _combine_kernel(dest_ref, x1_ref, g_ref, mod_ref, ys_ref, o_ref, buf_ref, sem):
    tm = x1_ref.shape[0]
    _row_copies(dest_ref, ys_ref, buf_ref, sem, tm, gather=True)
    g = g_ref[...]
    moe = g[:, 0:1] * buf_ref[0] + g[:, 1:2] * buf_ref[1]
    o_ref[...] = x1_ref[...] + mod_ref[5, 0] * moe


def moe_combine(x1, gates, dest, ys, modp):
    b, s, d = x1.shape
    tm = MOE_COMBINE_TILE
    nt = s // tm
    return pl.pallas_call(
        _combine_kernel,
        grid=(b, nt),
        in_specs=[pl.BlockSpec((None, 1, TOP_K * tm), lambda i, t: (i * nt + t, 0, 0), memory_space=pltpu.SMEM),
                  pl.BlockSpec((None, tm, d), lambda i, t: (i, t, 0)),
                  pl.BlockSpec((None, tm, TOP_K), lambda i, t: (i, t, 0)),
                  pl.BlockSpec((None, N_MOD, 2, 1, d), lambda i, t: (i, 0, 0, 0, 0)),
                  pl.BlockSpec(memory_space=pl.ANY)],
        out_specs=pl.BlockSpec((None, tm, d), lambda i, t: (i, t, 0)),
        out_shape=jax.ShapeDtypeStruct((b, s, d), F32),
        scratch_shapes=[pltpu.VMEM((TOP_K, tm, d), F32), pltpu.SemaphoreType.DMA],
        compiler_params=_params(("arbitrary", "arbitrary")),
        name="moe_combine",
    )(dest.reshape(b * nt, 1, TOP_K * tm), x1, gates.reshape(b, s, TOP_K), modp, ys)


def moe_residual(x1, h2, logits, modp, w1, w3, w2):
    b, s, d = x1.shape
    n_tok = b * s
    tm = MOE_TILE_M
    gates, dest, tile_expert, n_valid, meta = route(logits.reshape(n_tok, N_EXPERTS), tm)
    n_slots = (n_tok * TOP_K // tm + N_EXPERTS) * tm
    xs, w2 = moe_dispatch(h2.reshape(n_tok, d), dest, meta, n_slots, w2)
    ys = grouped_swiglu(xs, tile_expert, n_valid, w1, w3, w2, tm, MOE_TILE_F)
    return moe_combine(x1, gates, dest, ys, modp)


def _proj1_kernel(x_ref, g_ref, mod_ref, w_ref, qan_ref, qb_ref, qbs_ref, kvan_ref, kbk_ref, kbv_ref,
                  cos_ref, sin_ref, gains_ref, ones_ref, hy_ref, q_ref, k_ref, v_ref):
    tm = x_ref.shape[0]
    rs = tm // PROJ1_ROW_SPLIT
    c_q, c_kv = HY_IN, HY_IN + MLA_Q_RANK
    c_r = c_kv + MLA_KV_RANK
    first_half = lax.broadcasted_iota(jnp.int32, (rs, LANES), 1) < MLA_NOPE + MLA_ROPE // 2
    sh = MLA_ROPE // 2
    qg, qg_sw, kg, kg_sw = (gains_ref[i:i + 1, :] for i in range(4))

    def rms(a, g):
        return a * lax.rsqrt(jnp.mean(a * a, axis=-1, keepdims=True) + EPS) * g

    for r0 in range(0, tm, rs):
        rows = slice(r0, r0 + rs)
        h = _norm_mod(x_ref[rows, :], g_ref[...], mod_ref, pl.program_id(1) * tm + r0, 0, 1)
        z = jnp.dot(h.astype(BF16), w_ref[...], preferred_element_type=F32)
        hy_ref[rows, :] = z[:, :HY_IN]
        qa = rms(z[:, c_q:c_kv], qan_ref[...]).astype(BF16)
        kva = rms(z[:, c_kv:c_r], kvan_ref[...]).astype(BF16)
        q = jnp.dot(qa, qb_ref[...], preferred_element_type=F32)
        q_sw = jnp.dot(qa, qbs_ref[...], preferred_element_type=F32)
        kk = jnp.dot(kva, kbk_ref[...], preferred_element_type=F32)
        v_ref[rows, :] = jnp.dot(kva, kbv_ref[...], preferred_element_type=F32).astype(BF16)
        k_rope = pltpu.roll(z[:, c_r:c_r + LANES], MLA_NOPE, 1)
        k_rope_sw = jnp.where(first_half, pltpu.roll(k_rope, LANES - sh, 1), pltpu.roll(k_rope, sh, 1))
        cos, sin = cos_ref[rows, :], sin_ref[rows, :]

        def head(a, a_sw, gain, gain_sw):
            inv = lax.rsqrt(_group_sums(a, ones_ref) / MLA_QK + EPS)
            return (a * (inv * gain) * cos + a_sw * (inv * gain_sw) * sin).astype(BF16)

        for hh in range(MLA_HEADS):
            sl = slice(hh * LANES, (hh + 1) * LANES)
            q_ref[rows, sl] = head(q[:, sl], q_sw[:, sl], qg, qg_sw)
            k_ref[rows, sl] = head(kk[:, sl] + k_rope, k_rope_sw, kg, kg_sw)


def _swap_rope_halves(a):
    lead = a.shape[:-1]
    g = a.reshape(*lead, -1, LANES)
    lo, mid = MLA_NOPE, MLA_NOPE + MLA_ROPE // 2
    g = jnp.concatenate([g[..., :lo], g[..., mid:MLA_QK], g[..., lo:mid], g[..., MLA_QK:]], axis=-1)
    return g.reshape(a.shape)


def proj1(xa, g, modp, w_in, qan, qb, kvan, kbk, kbv, cos, sin, q_gain, k_gain):
    b = xa.shape[0]
    tm = TOKEN_TILE
    tok = lambda width: pl.BlockSpec((None, tm, width), lambda i, t: (i, t, 0))
    full = lambda a: pl.BlockSpec(a.shape, lambda i, t: (0, 0))
    hw = MLA_HEADS * LANES
    qbs = _swap_rope_halves(qb)
    gains = jnp.concatenate([q_gain, _swap_rope_halves(q_gain), k_gain, _swap_rope_halves(k_gain)], axis=0)
    ones = jnp.ones((LANES, LANES), BF16)
    return pl.pallas_call(
        _proj1_kernel,
        grid=(b, S_ALL // tm),
        in_specs=[tok(D_MODEL), full(g),
                  pl.BlockSpec((None, N_MOD, 2, 1, D_MODEL), lambda i, t: (i, 0, 0, 0, 0)),
                  full(w_in), full(qan), full(qb), full(qbs), full(kvan), full(kbk), full(kbv),
                  pl.BlockSpec((tm, LANES), lambda i, t: (t, 0)),
                  pl.BlockSpec((tm, LANES), lambda i, t: (t, 0)),
                  full(gains), full(ones)],
        out_specs=[tok(HY_IN), tok(hw), tok(hw), tok(MLA_HEADS * MLA_V)],
        out_shape=[jax.ShapeDtypeStruct((b, S_ALL, HY_IN), F32),
                   jax.ShapeDtypeStruct((b, S_ALL, hw), BF16),
                   jax.ShapeDtypeStruct((b, S_ALL, hw), BF16),
                   jax.ShapeDtypeStruct((b, S_ALL, MLA_HEADS * MLA_V), BF16)],
        compiler_params=_params(("parallel", "parallel")),
        name="proj1",
    )(xa, g, modp, w_in, qan, qb, qbs, kvan, kbk, kbv, cos, sin, gains, ones)


def _merge1_kernel(hy_ref, att_ref, x_ref, mod_ref, g2_ref, w_hy_ref, w_att_ref, r_ref,
                   x1_ref, h2_ref, lg_ref):
    y = (jnp.dot(hy_ref[...].astype(BF16), w_hy_ref[...], preferred_element_type=F32)
         + jnp.dot(att_ref[...], w_att_ref[...], preferred_element_type=F32))
    x1 = x_ref[...] + mod_ref[2, 0] * y
    x1_ref[...] = x1
    h2 = _norm_mod(x1, g2_ref[...], mod_ref, 0, 3, 4)
    h2_ref[...] = h2
    lane = lax.broadcasted_iota(jnp.int32, lg_ref.shape, 1)
    logits = jnp.zeros(lg_ref.shape, F32)
    for e in range(N_EXPERTS):
        col = jnp.sum(h2 * r_ref[e:e + 1, :], axis=-1, keepdims=True)
        logits = jnp.where(lane == e, col, logits)
    lg_ref[...] = logits


def merge1(hy, att, xa, modp, g2, w_hy, w_att, router):
    b = xa.shape[0]
    tm = MERGE1_TILE
    tok = lambda width: pl.BlockSpec((None, tm, width), lambda i, t: (i, t, 0))
    full = lambda a: pl.BlockSpec(a.shape, lambda i, t: (0, 0))
    return pl.pallas_call(
        _merge1_kernel,
        grid=(b, SEQ // tm),
        in_specs=[tok(HY_WIDTH), tok(HALF_MIX), tok(D_MODEL),
                  pl.BlockSpec((None, N_MOD, 2, 1, D_MODEL), lambda i, t: (i, 0, 0, 0, 0)),
                  full(g2), full(w_hy), full(w_att), full(router)],
        out_specs=[tok(D_MODEL), tok(D_MODEL), tok(N_EXPERTS)],
        out_shape=[jax.ShapeDtypeStruct((b, SEQ, D_MODEL), F32),
                   jax.ShapeDtypeStruct((b, SEQ, D_MODEL), F32),
                   jax.ShapeDtypeStruct((b, SEQ, N_EXPERTS), F32)],
        compiler_params=_params(("parallel", "parallel")),
        name="merge1",
    )(hy, att, xa, modp, g2, w_hy, w_att, router)


HY_N = 2 * SEQ
HY_TW_ROWS = 256
HY_SPEC_FBLK = 512
HY_CONV_FBLK = 512
HY_LANE_TILES = SEQ // LANES


def _twiddle_kernel(ca_ref, sa_ref, cb_ref, sb_ref, fre_ref, fim_ref, ic_ref, is_ref):
    rows = fre_ref.shape[0]
    r = pl.program_id(0) * rows + lax.broadcasted_iota(jnp.int32, (rows, LANES), 0)
    alt_r = (1 - 2 * (r & 1)).astype(F32)
    cb, sb = cb_ref[...], sb_ref[...]
    for a in range(HY_LANE_TILES):
        ca, sa = ca_ref[:, a:a + 1], sa_ref[:, a:a + 1]
        c = ca * cb - sa * sb
        s = sa * cb + ca * sb
        col = a * LANES + lax.broadcasted_iota(jnp.int32, (rows, LANES), 1)
        alt_c = (1 - 2 * (col & 1)).astype(F32)
        w = jnp.where(col == 0, 1.0 / HY_N, 2.0 / HY_N)
        sl = slice(a * LANES, (a + 1) * LANES)
        fre_ref[:, sl] = c.astype(BF16)
        fim_ref[:, sl] = jnp.where(r == 0, alt_c, -s).astype(BF16)
        blk, off = divmod(a * LANES, HY_CONV_FBLK)
        ic_ref[blk, :, off:off + LANES] = (c * w).astype(BF16)
        is_ref[blk, :, off:off + LANES] = jnp.where(col == 0, alt_r / HY_N, -s * w).astype(BF16)


def dft_matrices():
    idx = jnp.arange(SEQ, dtype=jnp.int32)[:, None]
    step = 2.0 * math.pi / HY_N
    ph_a = ((idx * (LANES * jnp.arange(HY_LANE_TILES, dtype=jnp.int32))[None, :]) % HY_N).astype(F32) * step
    ph_b = ((idx * jnp.arange(LANES, dtype=jnp.int32)[None, :]) % HY_N).astype(F32) * step
    rows = HY_TW_ROWS
    tab = lambda width: pl.BlockSpec((rows, width), lambda i: (i, 0))
    fwd = jax.ShapeDtypeStruct((SEQ, SEQ), BF16)
    n_blk = SEQ // HY_CONV_FBLK
    inv = jax.ShapeDtypeStruct((n_blk, SEQ, HY_CONV_FBLK), BF16)
    inv_spec = pl.BlockSpec((n_blk, rows, HY_CONV_FBLK), lambda i: (0, i, 0))
    return pl.pallas_call(
        _twiddle_kernel,
        grid=(SEQ // rows,),
        in_specs=[tab(HY_LANE_TILES), tab(HY_LANE_TILES), tab(LANES), tab(LANES)],
        out_specs=[tab(SEQ), tab(SEQ), inv_spec, inv_spec],
        out_shape=[fwd, fwd, inv, inv],
        compiler_params=_params(("parallel",)),
        name="twiddle",
    )(jnp.cos(ph_a), jnp.sin(ph_a), jnp.cos(ph_b), jnp.sin(ph_b))


def _spec_kernel(fre_ref, fim_ref, h_ref, k_ref):
    w = HY_WIDTH
    h = h_ref[...].astype(BF16)
    re = jnp.dot(fre_ref[...], h, preferred_element_type=F32)
    im = jnp.dot(fim_ref[...], h, preferred_element_type=F32)
    hb0 = h[0:1, w:].astype(F32)
    first = (pl.program_id(1) * re.shape[0] + lax.broadcasted_iota(jnp.int32, (re.shape[0], 1), 0)) == 0
    k_ref[0] = re[:, :w] + re[:, w:] - hb0
    k_ref[1] = jnp.where(first, im[:, :w] + im[:, w:] - hb0, im[:, :w] - im[:, w:])


def filter_spectra(filt, fre, fim):
    fb = HY_SPEC_FBLK
    return pl.pallas_call(
        _spec_kernel,
        grid=(HY_ORDER, SEQ // fb),
        in_specs=[pl.BlockSpec((fb, SEQ), lambda o, f: (f, 0)),
                  pl.BlockSpec((fb, SEQ), lambda o, f: (f, 0)),
                  pl.BlockSpec((SEQ, 2 * HY_WIDTH), lambda o, f: (0, o))],
        out_specs=pl.BlockSpec((None, 2, fb, HY_WIDTH), lambda o, f: (o, 0, f, 0)),
        out_shape=jax.ShapeDtypeStruct((HY_ORDER, 2, SEQ, HY_WIDTH), F32),
        compiler_params=_params(("parallel", "parallel")),
        name="filter_spectra",
    )(fre, fim, filt)


def _short_conv(z_ref, w_ref, b_ref, part):
    z = z_ref[...]
    l = z.shape[0]
    t = lax.broadcasted_iota(jnp.int32, (l, 1), 0)
    prev = jnp.where(t == 0, 0.0, pltpu.roll(z, 1, 0))
    nxt = jnp.where(t == l - 1, 0.0, pltpu.roll(z, l - 1, 0))
    cs = slice(part * HY_WIDTH, (part + 1) * HY_WIDTH)
    return (w_ref[0:1, cs] * prev + w_ref[1:2, cs] * z + w_ref[2:3, cs] * nxt) + b_ref[:, cs]


def _hyconv_kernel(u_ref, xg_ref, cw_ref, cb_ref, fre_ref, fim_ref, ic_ref, is_ref, k_ref, skip_ref,
                   y_ref, ub_ref, acc_ref, *, order):
    f = pl.program_id(1)

    def u_f32():
        return _short_conv(u_ref, cw_ref, cb_ref, 0) if order == 0 else u_ref[...]

    @pl.when(f == 0)
    def _():
        ub_ref[...] = u_f32().astype(BF16)

    x_re = jnp.dot(fre_ref[...], ub_ref[...], preferred_element_type=F32)
    x_im = jnp.dot(fim_ref[...], ub_ref[...], preferred_element_type=F32)
    k_re, k_im = k_ref[0], k_ref[1]
    first = (f * x_re.shape[0] + lax.broadcasted_iota(jnp.int32, (x_re.shape[0], 1), 0)) == 0
    y_re = x_re * k_re - jnp.where(first, 0.0, x_im * k_im)
    y_im = x_im * jnp.where(first, k_im, k_re) + jnp.where(first, 0.0, x_re * k_im)
    part = (jnp.dot(ic_ref[...], y_re.astype(BF16), preferred_element_type=F32)
            + jnp.dot(is_ref[...], y_im.astype(BF16), preferred_element_type=F32))
    _accumulate(acc_ref, part, f)

    @pl.when(f == pl.num_programs(1) - 1)
    def _():
        xg = _short_conv(xg_ref, cw_ref, cb_ref, order + 1)
        y_ref[...] = xg * (acc_ref[...] + skip_ref[order:order + 1, :] * u_f32())


def hyena_conv(order, u, hyz, conv_w, conv_b, fre, fim, ic, is_, spectra, skip):
    b = hyz.shape[0]
    fb = HY_CONV_FBLK
    w = HY_WIDTH
    lat = lambda part: pl.BlockSpec((None, SEQ, w), lambda i, f: (i, 0, part))
    full = lambda a: pl.BlockSpec(a.shape, lambda i, f: (0, 0))
    return pl.pallas_call(
        functools.partial(_hyconv_kernel, order=order),
        grid=(b, SEQ // fb),
        in_specs=[lat(0), lat(order + 1), full(conv_w), full(conv_b),
                  pl.BlockSpec((fb, SEQ), lambda i, f: (f, 0)),
                  pl.BlockSpec((fb, SEQ), lambda i, f: (f, 0)),
                  pl.BlockSpec((None, SEQ, fb), lambda i, f: (f, 0, 0)),
                  pl.BlockSpec((None, SEQ, fb), lambda i, f: (f, 0, 0)),
                  pl.BlockSpec((None, 2, fb, w), lambda i, f: (order, 0, f, 0)),
                  full(skip)],
        out_specs=pl.BlockSpec((None, SEQ, w), lambda i, f: (i, 0, 0)),
        out_shape=jax.ShapeDtypeStruct((b, SEQ, w), F32),
        scratch_shapes=[pltpu.VMEM((SEQ, w), BF16), pltpu.VMEM((SEQ, w), F32)],
        compiler_params=_params(("parallel", "arbitrary")),
        name="hyena_conv",
    )(u, hyz, conv_w, conv_b, fre, fim, ic, is_, spectra, skip)


def hyena_filters(length, w1, b1, w2, b2, w3, freq):
    hp = lax.Precision.HIGHEST
    t = jnp.arange(length, dtype=F32)[:, None]
    t_norm = t / max(length - 1, 1)
    bands = jnp.linspace(1e-4, HY_BANDS - 1, HY_BANDS, dtype=F32)
    ang = 2.0 * math.pi * t * bands / length
    z = jnp.concatenate([t_norm, jnp.cos(ang), -jnp.sin(ang)], axis=-1)
    h = jnp.sin(freq * (jnp.dot(z, w1, precision=hp) + b1))
    h = jnp.sin(freq * (jnp.dot(h, w2, precision=hp) + b2))
    h = jnp.dot(h, w3, precision=hp)
    deltas = jnp.abs(jnp.linspace(HY_MIN_DECAY, HY_MAX_DECAY, HY_WIDTH, dtype=F32))
    window = jnp.exp(-t_norm * deltas) + HY_SHIFT
    return h * jnp.tile(window, (1, HY_ORDER * 2))


def hyena(hyz, conv_w, conv_b, fw1, fb1, fw2, fb2, fw3, freq, skip):
    fre, fim, ic, is_ = dft_matrices()
    filt = hyena_filters(SEQ, fw1, fb1, fw2, fb2, fw3, freq)
    spectra = filter_spectra(filt, fre, fim)
    y = hyz
    for o in range(HY_ORDER):
        y = hyena_conv(o, y, hyz, conv_w, conv_b.reshape(1, -1), fre, fim, ic, is_, spectra, skip)
    return y


def _grid_angles(rot_dim):
    n_freq = rot_dim // 4
    inv_freq = ROPE_THETA ** (-jnp.arange(n_freq, dtype=F32) / n_freq)
    t = jnp.arange(SEQ)
    r = (t // GRID_W).astype(F32)
    c_ = (t % GRID_W).astype(F32)
    return jnp.concatenate([r[:, None] * inv_freq, c_[:, None] * inv_freq], axis=-1)


def _rope_tables(rot_dim, lane_offsets):
    ang = _grid_angles(rot_dim)
    c, s = jnp.cos(ang), jnp.sin(ang)
    cos_parts, sin_parts, lane = [], [], 0
    for off in lane_offsets:
        cos_parts += [jnp.ones((SEQ, off - lane), F32), c, c]
        sin_parts += [jnp.zeros((SEQ, off - lane), F32), -s, s]
        lane = off + rot_dim
    cos_parts.append(jnp.ones((SEQ, LANES - lane), F32))
    sin_parts.append(jnp.zeros((SEQ, LANES - lane), F32))
    cos = jnp.concatenate([jnp.concatenate(cos_parts, axis=1), jnp.ones((CTX_LEN, LANES), F32)], axis=0)
    sin = jnp.concatenate([jnp.concatenate(sin_parts, axis=1), jnp.zeros((CTX_LEN, LANES), F32)], axis=0)
    return cos, sin


def _block_diag(w):
    nb, bs, _ = w.shape
    eye = jnp.eye(nb, dtype=w.dtype)
    return (eye[:, None, :, None] * w[:, :, None, :]).reshape(nb * bs, nb * bs)


def _gqa_pair_order():
    g = GQA_Q_HEADS // GQA_KV_HEADS
    heads = []
    for p in range(g):
        heads += [p, g + p]
    return np.concatenate([np.arange(h * HEAD_DIM, (h + 1) * HEAD_DIM) for h in heads])


def _pad_heads(w, n_heads, width):
    k = w.shape[0]
    w = w.reshape(k, n_heads, width)
    return jnp.pad(w, ((0, 0), (0, 0), (0, LANES - width))).reshape(k, n_heads * LANES)


def kernel(x, c, ctx, c_ctx, mod_w, mod_b, norm1_g, norm2_g, ab_w_in, ab_w_out, lru_conv_w, lru_conv_b, lru_w_a, lru_b_a, lru_w_x, lru_b_x, lru_lambda, gqa_q_norm, gqa_k_norm, ffn_w1, ffn_w3, ffn_w2, cd_w_in, cd_w_out, hy_conv_w, hy_conv_b, hy_filt_w1, hy_filt_b1, hy_filt_w2, hy_filt_b2, hy_filt_w3, hy_sin_freq, hy_skip, mla_q_a_norm, mla_q_b, mla_kv_a_norm, mla_kv_b, mla_q_norm, mla_k_norm, moe_router, moe_w1, moe_w3, moe_w2):
    batch = x.shape[0]
    bf = lambda w: w.astype(BF16)
    row = lambda v: v.reshape(1, -1)

    silu_all = jnp.concatenate([jax.nn.silu(c), jax.nn.silu(c_ctx)[None, :],
                                jnp.zeros((16 - batch - 1, D_MODEL), F32)], axis=0)
    mods = []
    mod_all = stacked_matmul(silu_all, mod_w, 1536) + mod_b[:, None, :]
    for layer in range(DEPTH):
        m = mod_all[layer]
        lat = m[:batch].reshape(batch, N_MOD, 1, D_MODEL)
        cx = jnp.broadcast_to(m[batch].reshape(1, N_MOD, 1, D_MODEL), lat.shape)
        mods.append(jnp.stack([lat, cx], axis=2))

    perm = _gqa_pair_order()
    q0 = 2 * LRU_WIDTH
    w_in0 = ab_w_in[0]
    w_in0 = jnp.concatenate([w_in0[:, :q0], w_in0[:, q0 + perm], w_in0[:, q0 + GQA_Q_HEADS * HEAD_DIM:]], axis=1)
    cos_g, sin_g = _rope_tables(HEAD_DIM, (0, HEAD_DIM))
    q_gain = row(jnp.tile(gqa_q_norm[0], 2) * HEAD_DIM ** -0.5)
    k_gain = row(jnp.tile(gqa_k_norm[0], 2))
    xr, gate, q, k, v = proj0(x, ctx, row(norm1_g[0]), mods[0], bf(w_in0), cos_g, sin_g, q_gain, k_gain)

    w_gates = jnp.concatenate([_block_diag(lru_w_a[0, 0]), _block_diag(lru_w_x[0, 0]),
                               _block_diag(lru_w_a[0, 1]), _block_diag(lru_w_x[0, 1])], axis=1)
    b_gates = jnp.concatenate([lru_b_a[0, 0].reshape(-1), lru_b_x[0, 0].reshape(-1),
                               lru_b_a[0, 1].reshape(-1), lru_b_x[0, 1].reshape(-1)])
    rec = rglru(xr, lru_conv_w[0], row(lru_conv_b[0]), bf(w_gates), row(b_gates), lru_lambda[0])

    att_l, moe_w1b, moe_w3b = attention_latent(
        q, k, v, GQA_HEADS, GQA_TILE_Q,
        cast=(moe_w1[0].reshape(-1, moe_w1.shape[-1]), moe_w3[0].reshape(-1, moe_w3.shape[-1])))
    att_c = attention_context(q, k, v, GQA_HEADS)
    w_out0 = ab_w_out[0]
    x1 = merge0(rec, gate, att_l, att_c, x, ctx, mods[0], bf(w_out0[:LRU_WIDTH]), bf(w_out0[LRU_WIDTH:][perm]))
    xa = ffn_residual(x1, mods[0], row(norm2_g[0]), bf(ffn_w1[0]), bf(ffn_w3[0]), bf(ffn_w2[0]))

    w_in1 = jnp.pad(cd_w_in[0], ((0, 0), (0, LANES - MLA_ROPE)))
    cos_m, sin_m = _rope_tables(MLA_ROPE, (MLA_NOPE,))
    qb = _pad_heads(mla_q_b[0], MLA_HEADS, MLA_QK)
    kvb = mla_kv_b[0].reshape(MLA_KV_RANK, MLA_HEADS, MLA_NOPE + MLA_V)
    kbk = _pad_heads(kvb[:, :, :MLA_NOPE].reshape(MLA_KV_RANK, -1), MLA_HEADS, MLA_NOPE)
    kbv = kvb[:, :, MLA_NOPE:].reshape(MLA_KV_RANK, -1)
    pad_gain = lambda g_: row(jnp.pad(g_, (0, LANES - MLA_QK)))
    hyz, q, k, v = proj1(xa, row(norm1_g[1]), mods[1], bf(w_in1), row(mla_q_a_norm[0]), bf(qb),
                         row(mla_kv_a_norm[0]), bf(kbk), bf(kbv), cos_m, sin_m,
                         pad_gain(mla_q_norm[0] * MLA_QK ** -0.5), pad_gain(mla_k_norm[0]))
    att = attention_latent(q, k, v, MLA_HEAD_GROUPS, MLA_TILE_Q)
    hy = hyena(hyz, hy_conv_w[0], hy_conv_b[0], hy_filt_w1[0], hy_filt_b1[0], hy_filt_w2[0],
               hy_filt_b2[0], hy_filt_w3[0], hy_sin_freq[0], hy_skip[0])
    w_out1 = cd_w_out[0]
    x1, h2, logits = merge1(hy, att, xa, mods[1], row(norm2_g[1]),
                            bf(w_out1[:HY_WIDTH]), bf(w_out1[HY_WIDTH:]), moe_router[0].T)
    return moe_residual(x1, h2, logits, mods[1], moe_w1b.reshape(moe_w1.shape[1:]),
                        moe_w3b.reshape(moe_w3.shape[1:]), moe_w2[0])
```

```python
import functools
import math

import jax
import jax.numpy as jnp
import numpy as np
from jax import lax
from jax.experimental import pallas as pl
from jax.experimental.pallas import tpu as pltpu

F32 = jnp.float32
BF16 = jnp.bfloat16

D_MODEL = 1024
SEQ = 2048
CTX_LEN = 256
S_ALL = SEQ + CTX_LEN
DEPTH = 2
GRID_W = 64
HEAD_DIM = 64
HALF_MIX = D_MODEL // 2
ROPE_THETA = 10000.0
EPS = 1e-6
N_MOD = 6
LRU_WIDTH = HALF_MIX
LRU_BLOCKS = LRU_WIDTH // HEAD_DIM
LRU_CONV = 4
LRU_C = 8.0
GQA_Q_HEADS = HALF_MIX // HEAD_DIM
GQA_KV_HEADS = 2
HY_WIDTH = HALF_MIX
HY_ORDER = 2
HY_CONV = 3
HY_BANDS = 16
HY_TARGET = 1e-2
HY_FAST_PCT = 0.3
HY_SLOW_PCT = 1.5
HY_MIN_DECAY = math.log(HY_TARGET) / HY_SLOW_PCT
HY_MAX_DECAY = math.log(HY_TARGET) / HY_FAST_PCT
HY_SHIFT = 0.05
HY_IN = (HY_ORDER + 1) * HY_WIDTH
MLA_HEADS = HALF_MIX // HEAD_DIM
MLA_Q_RANK = D_MODEL // 4
MLA_KV_RANK = D_MODEL // 8
MLA_NOPE = HEAD_DIM
MLA_ROPE = HEAD_DIM // 2
MLA_V = HEAD_DIM
MLA_QK = MLA_NOPE + MLA_ROPE
N_EXPERTS = 8
TOP_K = 2
LOG2_E = math.log2(math.e)

LANES = 128
SUBLANES = 8
V7X_VMEM_LIMIT_BYTES = 56 * 1024 * 1024

TOKEN_TILE = 768
PROJ0_ROW_SPLIT = 2
PROJ1_ROW_SPLIT = 1
FFN_TILE_M = 768
FFN_TILE_F = 256
GQA_TILE_Q = 512
MLA_TILE_Q = 512
MERGE1_TILE = 1024
MOE_TILE_M = 512
MOE_TILE_F = 1792
MOE_ROUTE_BLOCK = 256
MOE_DISPATCH_TILE = 1024
MOE_COMBINE_TILE = 1024

LRU_GAP = SUBLANES
LRU_LAT0 = CTX_LEN + LRU_GAP
LRU_CHUNK = 464
LRU_ROWS = 2320
LRU_PAD_FRONT = SUBLANES

assert S_ALL % TOKEN_TILE == 0 and S_ALL % FFN_TILE_M == 0
assert SEQ % GQA_TILE_Q == 0 and SEQ % MLA_TILE_Q == 0
assert LRU_ROWS % LRU_CHUNK == 0 and LRU_ROWS >= LRU_LAT0 + SEQ + SUBLANES


def _params(sem):
    return pltpu.CompilerParams(dimension_semantics=sem, vmem_limit_bytes=V7X_VMEM_LIMIT_BYTES)


def _norm_mod(x, g, mod_pair, row0, shift_idx, scale_idx):
    y = x * lax.rsqrt(jnp.mean(x * x, axis=-1, keepdims=True) + EPS) * g
    shift = _mod_rows(mod_pair, shift_idx, row0, x.shape[0])
    scale = _mod_rows(mod_pair, scale_idx, row0, x.shape[0])
    return y * (1.0 + scale) + shift


def _mod_rows(mod_pair, idx, row0, rows):
    row = row0 + lax.broadcasted_iota(jnp.int32, (rows, 1), 0)
    return jnp.where(row >= SEQ, mod_pair[idx, 1], mod_pair[idx, 0])


def _mm_kernel(x_ref, w_ref, o_ref):
    o_ref[...] = jnp.dot(x_ref[...].astype(BF16), w_ref[...].astype(BF16),
                         preferred_element_type=F32)


def stacked_matmul(x, w, tn):
    m, k = x.shape
    n_l, _, n = w.shape
    assert n % tn == 0
    return pl.pallas_call(
        _mm_kernel,
        grid=(n_l, n // tn),
        in_specs=[pl.BlockSpec((m, k), lambda l, j: (0, 0)),
                  pl.BlockSpec((None, k, tn), lambda l, j: (l, 0, j))],
        out_specs=pl.BlockSpec((None, m, tn), lambda l, j: (l, 0, j)),
        out_shape=jax.ShapeDtypeStruct((n_l, m, n), F32),
        compiler_params=_params(("parallel", "parallel")),
        name="matmul",
    )(x, w)


def _rope_pair(a, cos, sin, first_half, shift):
    rot = jnp.where(first_half, pltpu.roll(a, LANES - shift, 1), pltpu.roll(a, shift, 1))
    return a * cos + rot * sin


def _group_sums(a, ones_ref):
    return jnp.dot((a * a).astype(BF16), ones_ref[...], preferred_element_type=F32)


def _proj0_kernel(x_ref, xt_ref, ctx_ref, g_ref, mod_ref, w_ref, cos_ref, sin_ref, qg_ref, kg_ref, ones_ref,
                  xr_ref, gate_ref, q_ref, k_ref, v_ref):
    tm = x_ref.shape[0]
    t = pl.program_id(1)
    last = jnp.concatenate([xt_ref[...], ctx_ref[...]], axis=0)
    rs = tm // PROJ0_ROW_SPLIT
    first_half = (lax.broadcasted_iota(jnp.int32, (rs, LANES), 1) & (HEAD_DIM // 2)) == 0
    q0 = 2 * LRU_WIDTH
    k0 = q0 + GQA_Q_HEADS * HEAD_DIM
    for r0 in range(0, tm, rs):
        rows = slice(r0, r0 + rs)
        x = jnp.where(t < SEQ // tm, x_ref[rows, :], last[rows])
        h = _norm_mod(x, g_ref[...], mod_ref, t * tm + r0, 0, 1)
        z = jnp.dot(h.astype(BF16), w_ref[...], preferred_element_type=F32)
        xr_ref[rows, :] = z[:, :LRU_WIDTH]
        gate_ref[rows, :] = z[:, LRU_WIDTH:2 * LRU_WIDTH].astype(BF16)
        cos, sin = cos_ref[rows, :], sin_ref[rows, :]

        def head_pair(a, gain):
            inv = lax.rsqrt(_group_sums(a, ones_ref) / HEAD_DIM + EPS)
            return _rope_pair(a * inv * gain, cos, sin, first_half, HEAD_DIM // 2).astype(BF16)

        for p in range(GQA_Q_HEADS // 2):
            q_ref[rows, p * LANES:(p + 1) * LANES] = head_pair(
                z[:, q0 + p * LANES:q0 + (p + 1) * LANES], qg_ref[...])
        k_ref[rows, :] = head_pair(z[:, k0:k0 + LANES], kg_ref[...])
        v_ref[rows, :] = z[:, k0 + LANES:k0 + 2 * LANES].astype(BF16)


def proj0(x, ctx, g, modp, w_in, cos, sin, q_gain, k_gain):
    b = x.shape[0]
    tm = TOKEN_TILE
    n = w_in.shape[1]
    n_full = SEQ // tm
    tail = SEQ - n_full * tm
    assert tail + CTX_LEN == tm and SEQ % tail == 0
    tok = lambda width: pl.BlockSpec((None, tm, width), lambda i, t: (i, t, 0))
    full = lambda r, c_: pl.BlockSpec((r, c_), lambda i, t: (0, 0))
    half = jnp.arange(LANES) // HEAD_DIM
    ones_pair = (half[:, None] == half[None, :]).astype(BF16)
    return pl.pallas_call(
        _proj0_kernel,
        grid=(b, S_ALL // tm),
        in_specs=[pl.BlockSpec((None, tm, D_MODEL), lambda i, t: (i, jnp.minimum(t, n_full - 1), 0)),
                  pl.BlockSpec((None, tail, D_MODEL), lambda i, t: (i, SEQ // tail - 1, 0)),
                  pl.BlockSpec((None, CTX_LEN, D_MODEL), lambda i, t: (i, 0, 0)),
                  full(1, D_MODEL),
                  pl.BlockSpec((None, N_MOD, 2, 1, D_MODEL), lambda i, t: (i, 0, 0, 0, 0)),
                  full(D_MODEL, n),
                  pl.BlockSpec((tm, LANES), lambda i, t: (t, 0)),
                  pl.BlockSpec((tm, LANES), lambda i, t: (t, 0)),
                  full(1, LANES), full(1, LANES), full(LANES, LANES)],
        out_specs=[tok(LRU_WIDTH), tok(LRU_WIDTH), tok(GQA_Q_HEADS * HEAD_DIM), tok(LANES), tok(LANES)],
        out_shape=[jax.ShapeDtypeStruct((b, S_ALL, LRU_WIDTH), F32),
                   jax.ShapeDtypeStruct((b, S_ALL, LRU_WIDTH), BF16),
                   jax.ShapeDtypeStruct((b, S_ALL, GQA_Q_HEADS * HEAD_DIM), BF16),
                   jax.ShapeDtypeStruct((b, S_ALL, LANES), BF16),
                   jax.ShapeDtypeStruct((b, S_ALL, LANES), BF16)],
        compiler_params=_params(("parallel", "parallel")),
        name="proj0",
    )(x, x, ctx, g, modp, w_in, cos, sin, q_gain, k_gain, ones_pair)


def _lru_kernel(xr_ref, cw_ref, cb_ref, wg_ref, bg_ref, lam_ref, rec_ref,
                pad_ref, af_ref, hf_ref, ab_ref, hb_ref):
    w = LRU_WIDTH
    pad_ref[...] = jnp.zeros_like(pad_ref)
    pad_ref[LRU_PAD_FRONT:LRU_PAD_FRONT + CTX_LEN, :] = xr_ref[SEQ:S_ALL, :]
    pad_ref[LRU_PAD_FRONT + LRU_LAT0:LRU_PAD_FRONT + LRU_LAT0 + SEQ, :] = xr_ref[0:SEQ, :]
    lam = lam_ref[...]
    neg_c_softplus = -LRU_C * (jnp.maximum(-lam, 0.0) + jnp.log(1.0 + jnp.exp(-jnp.abs(lam))))
    left = LRU_CONV // 2
    for ch in range(LRU_ROWS // LRU_CHUNK):
        r0 = ch * LRU_CHUNK
        u = cb_ref[...]
        for j in range(LRU_CONV):
            start = LRU_PAD_FRONT + r0 - left + j
            u = u + cw_ref[j:j + 1, :] * pad_ref[start:start + LRU_CHUNK, :]
        gz = jnp.dot(u.astype(BF16), wg_ref[...], preferred_element_type=F32) + bg_ref[...]
        for d, (a_ref, h_ref) in enumerate(((af_ref, hf_ref), (ab_ref, hb_ref))):
            r = jax.nn.sigmoid(gz[:, 2 * d * w:(2 * d + 1) * w])
            i = jax.nn.sigmoid(gz[:, (2 * d + 1) * w:(2 * d + 2) * w])
            a = jnp.exp(neg_c_softplus[d:d + 1, :] * r)
            a_ref[r0:r0 + LRU_CHUNK, :] = a
            h_ref[r0:r0 + LRU_CHUNK, :] = jnp.sqrt(1.0 - a * a) * (i * u)

    n_ctx_groups = CTX_LEN // SUBLANES
    sub = lax.broadcasted_iota(jnp.int32, (SUBLANES, w), 0)

    def tile_scan(a, b, descending):
        for s in (1, 2, 4):
            if descending:
                ok, shift = sub < SUBLANES - s, SUBLANES - s
            else:
                ok, shift = sub >= s, s
            a_s = jnp.where(ok, pltpu.roll(a, shift, 0), 1.0)
            b_s = jnp.where(ok, pltpu.roll(b, shift, 0), 0.0)
            a, b = a * a_s, a * b_s + b
        return a, b

    def group(tt, carry):
        hf, hb = carry
        is_lat = tt >= n_ctx_groups
        rf = pl.ds(pl.multiple_of(tt * SUBLANES + jnp.where(is_lat, LRU_GAP, 0), SUBLANES), SUBLANES)
        rb = pl.ds(pl.multiple_of(
            jnp.where(is_lat, LRU_LAT0 + SEQ + CTX_LEN, CTX_LEN) - (tt + 1) * SUBLANES, SUBLANES), SUBLANES)
        a, b = tile_scan(af_ref[rf, :], hf_ref[rf, :], False)
        h = a * hf + b
        hf_ref[rf, :] = h
        hf = jnp.broadcast_to(h[SUBLANES - 1:SUBLANES, :], (SUBLANES, w))
        a, b = tile_scan(ab_ref[rb, :], hb_ref[rb, :], True)
        h = a * hb + b
        hb_ref[rb, :] = h
        hb = jnp.broadcast_to(h[0:1, :], (SUBLANES, w))
        return hf, hb

    zero = jnp.zeros((SUBLANES, w), F32)
    lax.fori_loop(0, S_ALL // SUBLANES, group, (zero, zero), unroll=2)
    dt = rec_ref.dtype
    rec_ref[0:SEQ, :] = (hf_ref[LRU_LAT0:LRU_LAT0 + SEQ, :] + hb_ref[LRU_LAT0:LRU_LAT0 + SEQ, :]).astype(dt)
    rec_ref[SEQ:S_ALL, :] = (hf_ref[0:CTX_LEN, :] + hb_ref[0:CTX_LEN, :]).astype(dt)


def rglru(xr, conv_w, conv_b, w_gates, b_gates, lam):
    b = xr.shape[0]
    w = LRU_WIDTH
    full = lambda r, c_: pl.BlockSpec((r, c_), lambda i: (0, 0))
    rows = pltpu.VMEM((LRU_ROWS, w), F32)
    return pl.pallas_call(
        _lru_kernel,
        grid=(b,),
        in_specs=[pl.BlockSpec((None, S_ALL, w), lambda i: (i, 0, 0)),
                  full(LRU_CONV, w), full(1, w), full(w, 4 * w), full(1, 4 * w), full(2, w)],
        out_specs=pl.BlockSpec((None, S_ALL, w), lambda i: (i, 0, 0)),
        out_shape=jax.ShapeDtypeStruct((b, S_ALL, w), BF16),
        scratch_shapes=[pltpu.VMEM((LRU_ROWS + 2 * LRU_PAD_FRONT, w), F32), rows, rows, rows, rows],
        compiler_params=_params(("parallel",)),
        name="rglru",
    )(xr, conv_w, conv_b, w_gates, b_gates, lam)


def _attn_kernel(q_ref, k_ref, v_ref, *rest, heads):
    n_cast = (len(rest) - 1) // 2
    o_ref = rest[n_cast]
    cast_pairs = list(zip(rest[:n_cast], rest[n_cast + 1:]))
    tq, sk = q_ref.shape[0], k_ref.shape[0]
    q_lo = lax.broadcasted_iota(jnp.int32, (tq, LANES), 1) < HEAD_DIM
    v_lo = lax.broadcasted_iota(jnp.int32, (sk, LANES), 1) < HEAD_DIM

    def keep(x, is_lo, half):
        if half is None:
            return x
        return jnp.where(is_lo if half == 0 else jnp.logical_not(is_lo), x, jnp.zeros_like(x))

    for og, members in enumerate(heads):
        acc = None
        for qg, qh, kg, vg, vh in members:
            q = keep(q_ref[:, qg * LANES:(qg + 1) * LANES], q_lo, qh)
            s = lax.dot_general(q, k_ref[:, kg * LANES:(kg + 1) * LANES],
                                (((1,), (1,)), ((), ())), preferred_element_type=F32)
            p = jnp.exp2(s - jnp.max(s, axis=-1, keepdims=True))
            l = jnp.sum(p, axis=-1, keepdims=True)
            v = keep(v_ref[:, vg * LANES:(vg + 1) * LANES], v_lo, vh)
            o = jnp.dot(p.astype(BF16), v, preferred_element_type=F32) / l
            acc = o if acc is None else acc + o
        o_ref[:, og * LANES:(og + 1) * LANES] = acc.astype(o_ref.dtype)
        for src, dst in cast_pairs:
            r0, r1 = (og * src.shape[0] // len(heads), (og + 1) * src.shape[0] // len(heads))
            dst[r0:r1, :] = src[r0:r1, :].astype(BF16)


def attention(q, k, v, heads, q_block, n_q_blocks, tq, k_rows, cast=()):
    b, s, qw = q.shape
    kw, vw = k.shape[-1], v.shape[-1]
    ow = len(heads) * LANES
    n_steps = b * n_q_blocks
    kv_spec = lambda width: pl.BlockSpec((None, k_rows, width), lambda i, t: (i, s // k_rows - 1, 0))
    slab = lambda w: pl.BlockSpec((w.shape[0] // n_steps, w.shape[1]), lambda i, t: (i * n_q_blocks + t, 0))
    assert all(w.shape[0] % (16 * len(heads) * n_steps) == 0 for w in cast)
    out = pl.pallas_call(
        functools.partial(_attn_kernel, heads=heads),
        grid=(b, n_q_blocks),
        in_specs=[pl.BlockSpec((None, tq, qw), lambda i, t: (i, q_block + t, 0)), kv_spec(kw), kv_spec(vw)]
        + [slab(w) for w in cast],
        out_specs=[pl.BlockSpec((None, tq, ow), lambda i, t: (i, t, 0))] + [slab(w) for w in cast],
        out_shape=[jax.ShapeDtypeStruct((b, n_q_blocks * tq, ow), BF16)]
        + [jax.ShapeDtypeStruct(w.shape, BF16) for w in cast],
        compiler_params=_params(("parallel", "parallel")),
        name="attention",
    )(q, k, v, *cast)
    return out if cast else out[0]


GQA_HEADS = tuple(((p, 0, 0, 0, 0), (p, 1, 0, 0, 1)) for p in range(GQA_Q_HEADS // 2))
MLA_HEAD_GROUPS = tuple(((2 * p, None, 2 * p, p, 0), (2 * p + 1, None, 2 * p + 1, p, 1))
                        for p in range(MLA_HEADS // 2))


def attention_latent(q, k, v, heads, tq, cast=()):
    return attention(q, k, v, heads, 0, SEQ // tq, tq, S_ALL, cast)


def attention_context(q, k, v, heads):
    return attention(q, k, v, heads, SEQ // CTX_LEN, 1, CTX_LEN, CTX_LEN)


def _merge0_kernel(rec_ref, gate_ref, att_l_ref, att_c_ref, x_ref, ctx_ref, mod_ref,
                   w_rec_ref, w_att_ref, x1_ref):
    tm = x_ref.shape[0]
    row0 = pl.program_id(1) * tm
    is_ctx = row0 >= SEQ
    rg = (rec_ref[...].astype(F32) * jax.nn.gelu(gate_ref[...].astype(F32))).astype(BF16)
    att = jnp.where(is_ctx, att_c_ref[...], att_l_ref[...])
    y = (jnp.dot(rg, w_rec_ref[...], preferred_element_type=F32)
         + jnp.dot(att, w_att_ref[...], preferred_element_type=F32))
    x1_ref[...] = jnp.where(is_ctx, ctx_ref[...], x_ref[...]) + _mod_rows(mod_ref, 2, row0, tm) * y


def merge0(rec, gate, att_l, att_c, x, ctx, modp, w_rec, w_att):
    b = x.shape[0]
    tm = CTX_LEN
    n_lat = SEQ // tm
    tok = lambda width: pl.BlockSpec((None, tm, width), lambda i, t: (i, t, 0))
    lat = lambda width: pl.BlockSpec((None, tm, width), lambda i, t: (i, jnp.minimum(t, n_lat - 1), 0))
    cx = lambda width: pl.BlockSpec((None, tm, width), lambda i, t: (i, 0, 0))
    full = lambda r, c_: pl.BlockSpec((r, c_), lambda i, t: (0, 0))
    return pl.pallas_call(
        _merge0_kernel,
        grid=(b, S_ALL // tm),
        in_specs=[tok(LRU_WIDTH), tok(LRU_WIDTH), lat(HALF_MIX), cx(HALF_MIX), lat(D_MODEL), cx(D_MODEL),
                  pl.BlockSpec((None, N_MOD, 2, 1, D_MODEL), lambda i, t: (i, 0, 0, 0, 0)),
                  full(LRU_WIDTH, D_MODEL), full(HALF_MIX, D_MODEL)],
        out_specs=tok(D_MODEL),
        out_shape=jax.ShapeDtypeStruct((b, S_ALL, D_MODEL), F32),
        compiler_params=_params(("parallel", "parallel")),
        name="merge0",
    )(rec, gate, att_l, att_c, x, ctx, modp, w_rec, w_att)


def _swiglu(x, w1_ref, w3_ref, w2_ref, act_ref):
    for c in range(w1_ref.shape[1] // FFN_TILE_F):
        sl = slice(c * FFN_TILE_F, (c + 1) * FFN_TILE_F)
        h1 = jnp.dot(x, w1_ref[:, sl], preferred_element_type=F32)
        h3 = jnp.dot(x, w3_ref[:, sl], preferred_element_type=F32)
        act_ref[:, sl] = (h1 * jax.nn.sigmoid(h1) * h3).astype(BF16)
    return jnp.dot(act_ref[...], w2_ref[...], preferred_element_type=F32)


def _accumulate(acc_ref, part, j):
    @pl.when(j == 0)
    def _():
        acc_ref[...] = part

    @pl.when(j > 0)
    def _():
        acc_ref[...] += part


def _ffn_kernel(x1_ref, mod_ref, g2_ref, w1_ref, w3_ref, w2_ref, o_ref, act_ref):
    tm = x1_ref.shape[0]
    row0 = pl.program_id(1) * tm
    x = _norm_mod(x1_ref[...], g2_ref[...], mod_ref, row0, 3, 4).astype(BF16)
    y = _swiglu(x, w1_ref, w3_ref, w2_ref, act_ref)
    o_ref[...] = x1_ref[...] + _mod_rows(mod_ref, 5, row0, tm) * y


def ffn_residual(x1, modp, g2, w1, w3, w2):
    b = x1.shape[0]
    tm = FFN_TILE_M
    f = w1.shape[1]
    assert f % FFN_TILE_F == 0
    tok = lambda: pl.BlockSpec((None, tm, D_MODEL), lambda i, t: (i, t, 0))
    resident = lambda a: pl.BlockSpec(a.shape, lambda i, t: (0, 0), pipeline_mode=pl.Buffered(1))
    return pl.pallas_call(
        _ffn_kernel,
        grid=(b, S_ALL // tm),
        in_specs=[tok(),
                  pl.BlockSpec((None, N_MOD, 2, 1, D_MODEL), lambda i, t: (i, 0, 0, 0, 0)),
                  pl.BlockSpec((1, D_MODEL), lambda i, t: (0, 0)),
                  resident(w1), resident(w3), resident(w2)],
        out_specs=tok(),
        out_shape=jax.ShapeDtypeStruct((b, S_ALL, D_MODEL), F32),
        scratch_shapes=[pltpu.VMEM((tm, f), BF16)],
        compiler_params=_params(("parallel", "parallel")),
        name="ffn",
    )(x1, modp, g2, w1, w3, w2)


def _moe_kernel(te_ref, nv_ref, x_ref, w1_ref, w3_ref, w2_ref, o_ref, xb_ref, acc_ref, act_ref):
    i, j = pl.program_id(0), pl.program_id(1)
    last = pl.num_programs(1) - 1
    valid = i < nv_ref[0]

    @pl.when(valid)
    def _():
        @pl.when(j == 0)
        def _():
            xb_ref[...] = x_ref[...].astype(BF16)

        _accumulate(acc_ref, _swiglu(xb_ref[...], w1_ref, w3_ref, w2_ref, act_ref), j)

        @pl.when(j == last)
        def _():
            o_ref[...] = acc_ref[...]

    @pl.when(jnp.logical_and(jnp.logical_not(valid), j == last))
    def _():
        o_ref[...] = jnp.zeros_like(o_ref)


def grouped_swiglu(xs, tile_expert, n_valid, w1, w3, w2, tm, tf):
    n_rows, d = xs.shape
    f = w1.shape[-1]
    assert n_rows % tm == 0 and f % tf == 0
    n_f = f // tf

    def col(i, j, nv):
        return jnp.where(i < nv[0], j, n_f - 1)

    grid_spec = pltpu.PrefetchScalarGridSpec(
        num_scalar_prefetch=2,
        grid=(n_rows // tm, n_f),
        in_specs=[pl.BlockSpec((tm, d), lambda i, j, te, nv: (jnp.minimum(i, nv[0] - 1), 0)),
                  pl.BlockSpec((None, d, tf), lambda i, j, te, nv: (te[i], 0, col(i, j, nv))),
                  pl.BlockSpec((None, d, tf), lambda i, j, te, nv: (te[i], 0, col(i, j, nv))),
                  pl.BlockSpec((None, tf, d), lambda i, j, te, nv: (te[i], col(i, j, nv), 0))],
        out_specs=pl.BlockSpec((tm, d), lambda i, j, te, nv: (i, 0)),
        scratch_shapes=[pltpu.VMEM((tm, d), BF16), pltpu.VMEM((tm, d), F32), pltpu.VMEM((tm, tf), BF16)],
    )
    return pl.pallas_call(
        _moe_kernel,
        grid_spec=grid_spec,
        out_shape=jax.ShapeDtypeStruct((n_rows, d), F32),
        compiler_params=_params(("parallel", "arbitrary")),
        name="moe",
    )(tile_expert, n_valid, xs, w1, w3, w2)


def route(logits, tm):
    n_tok = logits.shape[0]
    n_assign = n_tok * TOP_K
    idx = jnp.arange(N_EXPERTS, dtype=jnp.int32)[None, :]
    m1 = jnp.max(logits, axis=-1, keepdims=True)
    e1 = jnp.min(jnp.where(logits == m1, idx, N_EXPERTS), axis=-1, keepdims=True)
    rest = jnp.where(idx == e1, jnp.finfo(F32).min, logits)
    m2 = jnp.max(rest, axis=-1, keepdims=True)
    e2 = jnp.min(jnp.where(rest == m2, idx, N_EXPERTS), axis=-1, keepdims=True)
    t = jnp.exp(m2 - m1)
    gates = jnp.concatenate([1.0 / (1.0 + t), t / (1.0 + t)], axis=-1)
    flat_e = jnp.concatenate([e1, e2], axis=-1).reshape(n_assign, 1)
    onehot = (flat_e == idx).astype(F32)
    blk = MOE_ROUTE_BLOCK
    nb = n_assign // blk
    oh = onehot.reshape(nb, blk, N_EXPERTS)
    lower = lambda n: (jnp.arange(n)[:, None] > jnp.arange(n)[None, :]).astype(F32)
    within = jnp.einsum('ij,bjk->bik', lower(blk), oh)
    blk_cnt = jnp.sum(oh, axis=1)
    blk_off = jnp.dot(lower(nb), blk_cnt, precision=lax.Precision.HIGHEST)
    rank = jnp.sum((within + blk_off[:, None, :]) * oh, axis=-1).reshape(n_assign)
    counts = jnp.sum(blk_cnt, axis=0).astype(jnp.int32)
    padded = (counts + tm - 1) // tm * tm
    ends = []
    for e in range(N_EXPERTS):
        ends.append(padded[e] + (ends[-1] if ends else 0))
    pend = jnp.stack(ends)
    pstart = pend - padded
    dest = (jnp.sum(onehot * pstart.astype(F32)[None, :], axis=-1) + rank).astype(jnp.int32)
    n_tiles = n_assign // tm + N_EXPERTS
    starts = jnp.arange(n_tiles, dtype=jnp.int32) * tm
    tile_expert = jnp.minimum(jnp.sum((pend[None, :] <= starts[:, None]).astype(jnp.int32), axis=-1),
                              N_EXPERTS - 1)
    n_valid = (pend[-1] // tm).reshape(1)
    meta = jnp.concatenate([pend, padded, n_valid])
    return gates, dest, tile_expert, n_valid, meta


def _row_copies(dest_ref, src, dst, sem, rows, gather, side_work=()):
    n_groups = rows // SUBLANES
    for g in range(n_groups):
        for r in range(g * SUBLANES, (g + 1) * SUBLANES):
            for kk in range(TOP_K):
                d = dest_ref[0, TOP_K * r + kk]
                if gather:
                    copy = pltpu.make_async_copy(src.at[pl.ds(d, 1), :], dst.at[kk, pl.ds(r, 1), :], sem)
                else:
                    copy = pltpu.make_async_copy(src.at[pl.ds(r, 1), :], dst.at[pl.ds(d, 1), :], sem)
                copy.start(priority=kk % 2)
        for thunk in side_work[g * len(side_work) // n_groups:(g + 1) * len(side_work) // n_groups]:
            thunk()
    if gather:
        pltpu.make_async_copy(dst, dst, sem).wait()
    else:
        for _ in range(TOP_K):
            pltpu.make_async_copy(src, src, sem).wait()


def _dispatch_kernel(meta_ref, dest_ref, h_ref, w_ref, xs_ref, wb_ref, zero_ref, sem, zsem):
    tm_moe = zero_ref.shape[0]

    @pl.when(pl.program_id(0) == 0)
    def _():
        zero_ref[...] = jnp.zeros_like(zero_ref)
        n_tiles = xs_ref.shape[0] // tm_moe
        fills = []
        for e in range(N_EXPERTS):
            start = pl.multiple_of(jnp.maximum(meta_ref[e] - tm_moe, 0), tm_moe)
            fills.append((meta_ref[N_EXPERTS + e] > 0, start))
        for i in range(n_tiles - N_EXPERTS, n_tiles):
            fills.append((i >= meta_ref[2 * N_EXPERTS], i * tm_moe))
        for cond, start in fills:
            @pl.when(cond)
            def _():
                pltpu.make_async_copy(zero_ref, xs_ref.at[pl.ds(start, tm_moe), :], zsem).start()
        for cond, start in fills:
            @pl.when(cond)
            def _():
                pltpu.make_async_copy(zero_ref, xs_ref.at[pl.ds(start, tm_moe), :], zsem).wait()

    def cast_rows(c):
        def thunk():
            wb_ref[c * 16:(c + 1) * 16, :] = w_ref[c * 16:(c + 1) * 16, :].astype(BF16)
        return thunk

    casts = [cast_rows(c) for c in range(w_ref.shape[0] // 16)]
    _row_copies(dest_ref, h_ref, xs_ref, sem, h_ref.shape[0], gather=False, side_work=casts)


def moe_dispatch(h2, dest, meta, n_slots, w):
    n_tok, d = h2.shape
    tm = MOE_DISPATCH_TILE
    n_steps = n_tok // tm
    wf = w.reshape(-1, w.shape[-1])
    assert wf.shape[0] % (16 * n_steps) == 0
    slab = pl.BlockSpec((wf.shape[0] // n_steps, wf.shape[1]), lambda t, m: (t, 0))
    grid_spec = pltpu.PrefetchScalarGridSpec(
        num_scalar_prefetch=1,
        grid=(n_steps,),
        in_specs=[pl.BlockSpec((None, 1, TOP_K * tm), lambda t, m: (t, 0, 0), memory_space=pltpu.SMEM),
                  pl.BlockSpec((tm, d), lambda t, m: (t, 0)), slab],
        out_specs=[pl.BlockSpec(memory_space=pl.ANY), slab],
        scratch_shapes=[pltpu.VMEM((MOE_TILE_M, d), h2.dtype), pltpu.SemaphoreType.DMA,
                        pltpu.SemaphoreType.DMA],
    )
    xs, wb = pl.pallas_call(
        _dispatch_kernel,
        grid_spec=grid_spec,
        out_shape=[jax.ShapeDtypeStruct((n_slots, d), h2.dtype), jax.ShapeDtypeStruct(wf.shape, BF16)],
        compiler_params=_params(("arbitrary",)),
        name="moe_dispatch",
    )(meta, dest.reshape(n_steps, 1, TOP_K * tm), h2, wf)
    return xs, wb.reshape(w.shape)


def _combine_kernel(dest_ref, x1_ref, g_ref, mod_ref, ys_ref, o_ref, buf_ref, sem):
    tm = x1_ref.shape[0]
    _row_copies(dest_ref, ys_ref, buf_ref, sem, tm, gather=True)
    g = g_ref[...]
    moe = g[:, 0:1] * buf_ref[0] + g[:, 1:2] * buf_ref[1]
    o_ref[...] = x1_ref[...] + mod_ref[5, 0] * moe


def moe_combine(x1, gates, dest, ys, modp):
    b, s, d = x1.shape
    tm = MOE_COMBINE_TILE
    nt = s // tm
    return pl.pallas_call(
        _combine_kernel,
        grid=(b, nt),
        in_specs=[pl.BlockSpec((None, 1, TOP_K * tm), lambda i, t: (i * nt + t, 0, 0), memory_space=pltpu.SMEM),
                  pl.BlockSpec((None, tm, d), lambda i, t: (i, t, 0)),
                  pl.BlockSpec((None, tm, TOP_K), lambda i, t: (i, t, 0)),
                  pl.BlockSpec((None, N_MOD, 2, 1, d), lambda i, t: (i, 0, 0, 0, 0)),
                  pl.BlockSpec(memory_space=pl.ANY)],
        out_specs=pl.BlockSpec((None, tm, d), lambda i, t: (i, t, 0)),
        out_shape=jax.ShapeDtypeStruct((b, s, d), F32),
        scratch_shapes=[pltpu.VMEM((TOP_K, tm, d), F32), pltpu.SemaphoreType.DMA],
        compiler_params=_params(("arbitrary", "arbitrary")),
        name="moe_combine",
    )(dest.reshape(b * nt, 1, TOP_K * tm), x1, gates.reshape(b, s, TOP_K), modp, ys)


def moe_residual(x1, h2, logits, modp, w1, w3, w2):
    b, s, d = x1.shape
    n_tok = b * s
    tm = MOE_TILE_M
    gates, dest, tile_expert, n_valid, meta = route(logits.reshape(n_tok, N_EXPERTS), tm)
    n_slots = (n_tok * TOP_K // tm + N_EXPERTS) * tm
    xs, w2 = moe_dispatch(h2.reshape(n_tok, d), dest, meta, n_slots, w2)
    ys = grouped_swiglu(xs, tile_expert, n_valid, w1, w3, w2, tm, MOE_TILE_F)
    return moe_combine(x1, gates, dest, ys, modp)


def _proj1_kernel(x_ref, g_ref, mod_ref, w_ref, qan_ref, qb_ref, qbs_ref, kvan_ref, kbk_ref, kbv_ref,
                  cos_ref, sin_ref, gains_ref, ones_ref, hy_ref, q_ref, k_ref, v_ref):
    tm = x_ref.shape[0]
    rs = tm // PROJ1_ROW_SPLIT
    c_q, c_kv = HY_IN, HY_IN + MLA_Q_RANK
    c_r = c_kv + MLA_KV_RANK
    first_half = lax.broadcasted_iota(jnp.int32, (rs, LANES), 1) < MLA_NOPE + MLA_ROPE // 2
    sh = MLA_ROPE // 2
    qg, qg_sw, kg, kg_sw = (gains_ref[i:i + 1, :] for i in range(4))

    def rms(a, g):
        return a * lax.rsqrt(jnp.mean(a * a, axis=-1, keepdims=True) + EPS) * g

    for r0 in range(0, tm, rs):
        rows = slice(r0, r0 + rs)
        h = _norm_mod(x_ref[rows, :], g_ref[...], mod_ref, pl.program_id(1) * tm + r0, 0, 1)
        z = jnp.dot(h.astype(BF16), w_ref[...], preferred_element_type=F32)
        hy_ref[rows, :] = z[:, :HY_IN]
        qa = rms(z[:, c_q:c_kv], qan_ref[...]).astype(BF16)
        kva = rms(z[:, c_kv:c_r], kvan_ref[...]).astype(BF16)
        q = jnp.dot(qa, qb_ref[...], preferred_element_type=F32)
        q_sw = jnp.dot(qa, qbs_ref[...], preferred_element_type=F32)
        kk = jnp.dot(kva, kbk_ref[...], preferred_element_type=F32)
        v_ref[rows, :] = jnp.dot(kva, kbv_ref[...], preferred_element_type=F32).astype(BF16)
        k_rope = pltpu.roll(z[:, c_r:c_r + LANES], MLA_NOPE, 1)
        k_rope_sw = jnp.where(first_half, pltpu.roll(k_rope, LANES - sh, 1), pltpu.roll(k_rope, sh, 1))
        cos, sin = cos_ref[rows, :], sin_ref[rows, :]

        def head(a, a_sw, gain, gain_sw):
            inv = lax.rsqrt(_group_sums(a, ones_ref) / MLA_QK + EPS)
            return (a * (inv * gain) * cos + a_sw * (inv * gain_sw) * sin).astype(BF16)

        for hh in range(MLA_HEADS):
            sl = slice(hh * LANES, (hh + 1) * LANES)
            q_ref[rows, sl] = head(q[:, sl], q_sw[:, sl], qg, qg_sw)
            k_ref[rows, sl] = head(kk[:, sl] + k_rope, k_rope_sw, kg, kg_sw)


def _swap_rope_halves(a):
    lead = a.shape[:-1]
    g = a.reshape(*lead, -1, LANES)
    lo, mid = MLA_NOPE, MLA_NOPE + MLA_ROPE // 2
    g = jnp.concatenate([g[..., :lo], g[..., mid:MLA_QK], g[..., lo:mid], g[..., MLA_QK:]], axis=-1)
    return g.reshape(a.shape)


def proj1(xa, g, modp, w_in, qan, qb, kvan, kbk, kbv, cos, sin, q_gain, k_gain):
    b = xa.shape[0]
    tm = TOKEN_TILE
    tok = lambda width: pl.BlockSpec((None, tm, width), lambda i, t: (i, t, 0))
    full = lambda a: pl.BlockSpec(a.shape, lambda i, t: (0, 0))
    hw = MLA_HEADS * LANES
    qbs = _swap_rope_halves(qb)
    gains = jnp.concatenate([q_gain, _swap_rope_halves(q_gain), k_gain, _swap_rope_halves(k_gain)], axis=0)
    ones = jnp.ones((LANES, LANES), BF16)
    return pl.pallas_call(
        _proj1_kernel,
        grid=(b, S_ALL // tm),
        in_specs=[tok(D_MODEL), full(g),
                  pl.BlockSpec((None, N_MOD, 2, 1, D_MODEL), lambda i, t: (i, 0, 0, 0, 0)),
                  full(w_in), full(qan), full(qb), full(qbs), full(kvan), full(kbk), full(kbv),
                  pl.BlockSpec((tm, LANES), lambda i, t: (t, 0)),
                  pl.BlockSpec((tm, LANES), lambda i, t: (t, 0)),
                  full(gains), full(ones)],
        out_specs=[tok(HY_IN), tok(hw), tok(hw), tok(MLA_HEADS * MLA_V)],
        out_shape=[jax.ShapeDtypeStruct((b, S_ALL, HY_IN), F32),
                   jax.ShapeDtypeStruct((b, S_ALL, hw), BF16),
                   jax.ShapeDtypeStruct((b, S_ALL, hw), BF16),
                   jax.ShapeDtypeStruct((b, S_ALL, MLA_HEADS * MLA_V), BF16)],
        compiler_params=_params(("parallel", "parallel")),
        name="proj1",
    )(xa, g, modp, w_in, qan, qb, qbs, kvan, kbk, kbv, cos, sin, gains, ones)


def _merge1_kernel(hy_ref, att_ref, x_ref, mod_ref, g2_ref, w_hy_ref, w_att_ref, r_ref,
                   x1_ref, h2_ref, lg_ref):
    y = (jnp.dot(hy_ref[...].astype(BF16), w_hy_ref[...], preferred_element_type=F32)
         + jnp.dot(att_ref[...], w_att_ref[...], preferred_element_type=F32))
    x1 = x_ref[...] + mod_ref[2, 0] * y
    x1_ref[...] = x1
    h2 = _norm_mod(x1, g2_ref[...], mod_ref, 0, 3, 4)
    h2_ref[...] = h2
    lane = lax.broadcasted_iota(jnp.int32, lg_ref.shape, 1)
    logits = jnp.zeros(lg_ref.shape, F32)
    for e in range(N_EXPERTS):
        col = jnp.sum(h2 * r_ref[e:e + 1, :], axis=-1, keepdims=True)
        logits = jnp.where(lane == e, col, logits)
    lg_ref[...] = logits


def merge1(hy, att, xa, modp, g2, w_hy, w_att, router):
    b = xa.shape[0]
    tm = MERGE1_TILE
    tok = lambda width: pl.BlockSpec((None, tm, width), lambda i, t: (i, t, 0))
    full = lambda a: pl.BlockSpec(a.shape, lambda i, t: (0, 0))
    return pl.pallas_call(
        _merge1_kernel,
        grid=(b, SEQ // tm),
        in_specs=[tok(HY_WIDTH), tok(HALF_MIX), tok(D_MODEL),
                  pl.BlockSpec((None, N_MOD, 2, 1, D_MODEL), lambda i, t: (i, 0, 0, 0, 0)),
                  full(g2), full(w_hy), full(w_att), full(router)],
        out_specs=[tok(D_MODEL), tok(D_MODEL), tok(N_EXPERTS)],
        out_shape=[jax.ShapeDtypeStruct((b, SEQ, D_MODEL), F32),
                   jax.ShapeDtypeStruct((b, SEQ, D_MODEL), F32),
                   jax.ShapeDtypeStruct((b, SEQ, N_EXPERTS), F32)],
        compiler_params=_params(("parallel", "parallel")),
        name="merge1",
    )(hy, att, xa, modp, g2, w_hy, w_att, router)


HY_N = 2 * SEQ
HY_TW_ROWS = 256
HY_SPEC_FBLK = 512
HY_CONV_FBLK = 512
HY_LANE_TILES = SEQ // LANES


def _twiddle_kernel(ca_ref, sa_ref, cb_ref, sb_ref, fre_ref, fim_ref, ic_ref, is_ref):
    rows = fre_ref.shape[0]
    r = pl.program_id(0) * rows + lax.broadcasted_iota(jnp.int32, (rows, LANES), 0)
    alt_r = (1 - 2 * (r & 1)).astype(F32)
    cb, sb = cb_ref[...], sb_ref[...]
    for a in range(HY_LANE_TILES):
        ca, sa = ca_ref[:, a:a + 1], sa_ref[:, a:a + 1]
        c = ca * cb - sa * sb
        s = sa * cb + ca * sb
        col = a * LANES + lax.broadcasted_iota(jnp.int32, (rows, LANES), 1)
        alt_c = (1 - 2 * (col & 1)).astype(F32)
        w = jnp.where(col == 0, 1.0 / HY_N, 2.0 / HY_N)
        sl = slice(a * LANES, (a + 1) * LANES)
        fre_ref[:, sl] = c.astype(BF16)
        fim_ref[:, sl] = jnp.where(r == 0, alt_c, -s).astype(BF16)
        blk, off = divmod(a * LANES, HY_CONV_FBLK)
        ic_ref[blk, :, off:off + LANES] = (c * w).astype(BF16)
        is_ref[blk, :, off:off + LANES] = jnp.where(col == 0, alt_r / HY_N, -s * w).astype(BF16)


def dft_matrices():
    idx = jnp.arange(SEQ, dtype=jnp.int32)[:, None]
    step = 2.0 * math.pi / HY_N
    ph_a = ((idx * (LANES * jnp.arange(HY_LANE_TILES, dtype=jnp.int32))[None, :]) % HY_N).astype(F32) * step
    ph_b = ((idx * jnp.arange(LANES, dtype=jnp.int32)[None, :]) % HY_N).astype(F32) * step
    rows = HY_TW_ROWS
    tab = lambda width: pl.BlockSpec((rows, width), lambda i: (i, 0))
    fwd = jax.ShapeDtypeStruct((SEQ, SEQ), BF16)
    n_blk = SEQ // HY_CONV_FBLK
    inv = jax.ShapeDtypeStruct((n_blk, SEQ, HY_CONV_FBLK), BF16)
    inv_spec = pl.BlockSpec((n_blk, rows, HY_CONV_FBLK), lambda i: (0, i, 0))
    return pl.pallas_call(
        _twiddle_kernel,
        grid=(SEQ // rows,),
        in_specs=[tab(HY_LANE_TILES), tab(HY_LANE_TILES), tab(LANES), tab(LANES)],
        out_specs=[tab(SEQ), tab(SEQ), inv_spec, inv_spec],
        out_shape=[fwd, fwd, inv, inv],
        compiler_params=_params(("parallel",)),
        name="twiddle",
    )(jnp.cos(ph_a), jnp.sin(ph_a), jnp.cos(ph_b), jnp.sin(ph_b))


def _spec_kernel(fre_ref, fim_ref, h_ref, k_ref):
    w = HY_WIDTH
    h = h_ref[...].astype(BF16)
    re = jnp.dot(fre_ref[...], h, preferred_element_type=F32)
    im = jnp.dot(fim_ref[...], h, preferred_element_type=F32)
    hb0 = h[0:1, w:].astype(F32)
    first = (pl.program_id(1) * re.shape[0] + lax.broadcasted_iota(jnp.int32, (re.shape[0], 1), 0)) == 0
    k_ref[0] = re[:, :w] + re[:, w:] - hb0
    k_ref[1] = jnp.where(first, im[:, :w] + im[:, w:] - hb0, im[:, :w] - im[:, w:])


def filter_spectra(filt, fre, fim):
    fb = HY_SPEC_FBLK
    return pl.pallas_call(
        _spec_kernel,
        grid=(HY_ORDER, SEQ // fb),
        in_specs=[pl.BlockSpec((fb, SEQ), lambda o, f: (f, 0)),
                  pl.BlockSpec((fb, SEQ), lambda o, f: (f, 0)),
                  pl.BlockSpec((SEQ, 2 * HY_WIDTH), lambda o, f: (0, o))],
        out_specs=pl.BlockSpec((None, 2, fb, HY_WIDTH), lambda o, f: (o, 0, f, 0)),
        out_shape=jax.ShapeDtypeStruct((HY_ORDER, 2, SEQ, HY_WIDTH), F32),
        compiler_params=_params(("parallel", "parallel")),
        name="filter_spectra",
    )(fre, fim, filt)


def _short_conv(z_ref, w_ref, b_ref, part):
    z = z_ref[...]
    l = z.shape[0]
    t = lax.broadcasted_iota(jnp.int32, (l, 1), 0)
    prev = jnp.where(t == 0, 0.0, pltpu.roll(z, 1, 0))
    nxt = jnp.where(t == l - 1, 0.0, pltpu.roll(z, l - 1, 0))
    cs = slice(part * HY_WIDTH, (part + 1) * HY_WIDTH)
    return (w_ref[0:1, cs] * prev + w_ref[1:2, cs] * z + w_ref[2:3, cs] * nxt) + b_ref[:, cs]


def _hyconv_kernel(u_ref, xg_ref, cw_ref, cb_ref, fre_ref, fim_ref, ic_ref, is_ref, k_ref, skip_ref,
                   y_ref, ub_ref, acc_ref, *, order):
    f = pl.program_id(1)

    def u_f32():
        return _short_conv(u_ref, cw_ref, cb_ref, 0) if order == 0 else u_ref[...]

    @pl.when(f == 0)
    def _():
        ub_ref[...] = u_f32().astype(BF16)

    x_re = jnp.dot(fre_ref[...], ub_ref[...], preferred_element_type=F32)
    x_im = jnp.dot(fim_ref[...], ub_ref[...], preferred_element_type=F32)
    k_re, k_im = k_ref[0], k_ref[1]
    first = (f * x_re.shape[0] + lax.broadcasted_iota(jnp.int32, (x_re.shape[0], 1), 0)) == 0
    y_re = x_re * k_re - jnp.where(first, 0.0, x_im * k_im)
    y_im = x_im * jnp.where(first, k_im, k_re) + jnp.where(first, 0.0, x_re * k_im)
    part = (jnp.dot(ic_ref[...], y_re.astype(BF16), preferred_element_type=F32)
            + jnp.dot(is_ref[...], y_im.astype(BF16), preferred_element_type=F32))
    _accumulate(acc_ref, part, f)

    @pl.when(f == pl.num_programs(1) - 1)
    def _():
        xg = _short_conv(xg_ref, cw_ref, cb_ref, order + 1)
        y_ref[...] = xg * (acc_ref[...] + skip_ref[order:order + 1, :] * u_f32())


def hyena_conv(order, u, hyz, conv_w, conv_b, fre, fim, ic, is_, spectra, skip):
    b = hyz.shape[0]
    fb = HY_CONV_FBLK
    w = HY_WIDTH
    lat = lambda part: pl.BlockSpec((None, SEQ, w), lambda i, f: (i, 0, part))
    full = lambda a: pl.BlockSpec(a.shape, lambda i, f: (0, 0))
    return pl.pallas_call(
        functools.partial(_hyconv_kernel, order=order),
        grid=(b, SEQ // fb),
        in_specs=[lat(0), lat(order + 1), full(conv_w), full(conv_b),
                  pl.BlockSpec((fb, SEQ), lambda i, f: (f, 0)),
                  pl.BlockSpec((fb, SEQ), lambda i, f: (f, 0)),
                  pl.BlockSpec((None, SEQ, fb), lambda i, f: (f, 0, 0)),
                  pl.BlockSpec((None, SEQ, fb), lambda i, f: (f, 0, 0)),
                  pl.BlockSpec((None, 2, fb, w), lambda i, f: (order, 0, f, 0)),
                  full(skip)],
        out_specs=pl.BlockSpec((None, SEQ, w), lambda i, f: (i, 0, 0)),
        out_shape=jax.ShapeDtypeStruct((b, SEQ, w), F32),
        scratch_shapes=[pltpu.VMEM((SEQ, w), BF16), pltpu.VMEM((SEQ, w), F32)],
        compiler_params=_params(("parallel", "arbitrary")),
        name="hyena_conv",
    )(u, hyz, conv_w, conv_b, fre, fim, ic, is_, spectra, skip)


def hyena_filters(length, w1, b1, w2, b2, w3, freq):
    hp = lax.Precision.HIGHEST
    t = jnp.arange(length, dtype=F32)[:, None]
    t_norm = t / max(length - 1, 1)
    bands = jnp.linspace(1e-4, HY_BANDS - 1, HY_BANDS, dtype=F32)
    ang = 2.0 * math.pi * t * bands / length
    z = jnp.concatenate([t_norm, jnp.cos(ang), -jnp.sin(ang)], axis=-1)
    h = jnp.sin(freq * (jnp.dot(z, w1, precision=hp) + b1))
    h = jnp.sin(freq * (jnp.dot(h, w2, precision=hp) + b2))
    h = jnp.dot(h, w3, precision=hp)
    deltas = jnp.abs(jnp.linspace(HY_MIN_DECAY, HY_MAX_DECAY, HY_WIDTH, dtype=F32))
    window = jnp.exp(-t_norm * deltas) + HY_SHIFT
    return h * jnp.tile(window, (1, HY_ORDER * 2))


def hyena(hyz, conv_w, conv_b, fw1, fb1, fw2, fb2, fw3, freq, skip):
    fre, fim, ic, is_ = dft_matrices()
    filt = hyena_filters(SEQ, fw1, fb1, fw2, fb2, fw3, freq)
    spectra = filter_spectra(filt, fre, fim)
    y = hyz
    for o in range(HY_ORDER):
        y = hyena_conv(o, y, hyz, conv_w, conv_b.reshape(1, -1), fre, fim, ic, is_, spectra, skip)
    return y


def _grid_angles(rot_dim):
    n_freq = rot_dim // 4
    inv_freq = ROPE_THETA ** (-jnp.arange(n_freq, dtype=F32) / n_freq)
    t = jnp.arange(SEQ)
    r = (t // GRID_W).astype(F32)
    c_ = (t % GRID_W).astype(F32)
    return jnp.concatenate([r[:, None] * inv_freq, c_[:, None] * inv_freq], axis=-1)


def _rope_tables(rot_dim, lane_offsets):
    ang = _grid_angles(rot_dim)
    c, s = jnp.cos(ang), jnp.sin(ang)
    cos_parts, sin_parts, lane = [], [], 0
    for off in lane_offsets:
        cos_parts += [jnp.ones((SEQ, off - lane), F32), c, c]
        sin_parts += [jnp.zeros((SEQ, off - lane), F32), -s, s]
        lane = off + rot_dim
    cos_parts.append(jnp.ones((SEQ, LANES - lane), F32))
    sin_parts.append(jnp.zeros((SEQ, LANES - lane), F32))
    cos = jnp.concatenate([jnp.concatenate(cos_parts, axis=1), jnp.ones((CTX_LEN, LANES), F32)], axis=0)
    sin = jnp.concatenate([jnp.concatenate(sin_parts, axis=1), jnp.zeros((CTX_LEN, LANES), F32)], axis=0)
    return cos, sin


def _block_diag(w):
    nb, bs, _ = w.shape
    eye = jnp.eye(nb, dtype=w.dtype)
    return (eye[:, None, :, None] * w[:, :, None, :]).reshape(nb * bs, nb * bs)


def _gqa_pair_order():
    g = GQA_Q_HEADS // GQA_KV_HEADS
    heads = []
    for p in range(g):
        heads += [p, g + p]
    return np.concatenate([np.arange(h * HEAD_DIM, (h + 1) * HEAD_DIM) for h in heads])


def _pad_heads(w, n_heads, width):
    k = w.shape[0]
    w = w.reshape(k, n_heads, width)
    return jnp.pad(w, ((0, 0), (0, 0), (0, LANES - width))).reshape(k, n_heads * LANES)


def kernel(x, c, ctx, c_ctx, mod_w, mod_b, norm1_g, norm2_g, ab_w_in, ab_w_out, lru_conv_w, lru_conv_b, lru_w_a, lru_b_a, lru_w_x, lru_b_x, lru_lambda, gqa_q_norm, gqa_k_norm, ffn_w1, ffn_w3, ffn_w2, cd_w_in, cd_w_out, hy_conv_w, hy_conv_b, hy_filt_w1, hy_filt_b1, hy_filt_w2, hy_filt_b2, hy_filt_w3, hy_sin_freq, hy_skip, mla_q_a_norm, mla_q_b, mla_kv_a_norm, mla_kv_b, mla_q_norm, mla_k_norm, moe_router, moe_w1, moe_w3, moe_w2):
    batch = x.shape[0]
    bf = lambda w: w.astype(BF16)
    row = lambda v: v.reshape(1, -1)

    silu_all = jnp.concatenate([jax.nn.silu(c), jax.nn.silu(c_ctx)[None, :],
                                jnp.zeros((16 - batch - 1, D_MODEL), F32)], axis=0)
    mods = []
    mod_all = stacked_matmul(silu_all, mod_w, 1536) + mod_b[:, None, :]
    for layer in range(DEPTH):
        m = mod_all[layer]
        lat = m[:batch].reshape(batch, N_MOD, 1, D_MODEL)
        cx = jnp.broadcast_to(m[batch].reshape(1, N_MOD, 1, D_MODEL), lat.shape)
        mods.append(jnp.stack([lat, cx], axis=2))

    perm = _gqa_pair_order()
    q0 = 2 * LRU_WIDTH
    w_in0 = ab_w_in[0]
    w_in0 = jnp.concatenate([w_in0[:, :q0], w_in0[:, q0 + perm], w_in0[:, q0 + GQA_Q_HEADS * HEAD_DIM:]], axis=1)
    cos_g, sin_g = _rope_tables(HEAD_DIM, (0, HEAD_DIM))
    q_gain = row(jnp.tile(gqa_q_norm[0], 2) * (HEAD_DIM ** -0.5 * LOG2_E))
    k_gain = row(jnp.tile(gqa_k_norm[0], 2))
    xr, gate, q, k, v = proj0(x, ctx, row(norm1_g[0]), mods[0], bf(w_in0), cos_g, sin_g, q_gain, k_gain)

    w_gates = jnp.concatenate([_block_diag(lru_w_a[0, 0]), _block_diag(lru_w_x[0, 0]),
                               _block_diag(lru_w_a[0, 1]), _block_diag(lru_w_x[0, 1])], axis=1)
    b_gates = jnp.concatenate([lru_b_a[0, 0].reshape(-1), lru_b_x[0, 0].reshape(-1),
                               lru_b_a[0, 1].reshape(-1), lru_b_x[0, 1].reshape(-1)])
    rec = rglru(xr, lru_conv_w[0], row(lru_conv_b[0]), bf(w_gates), row(b_gates), lru_lambda[0])

    att_l, moe_w1b, moe_w3b = attention_latent(
        q, k, v, GQA_HEADS, GQA_TILE_Q,
        cast=(moe_w1[0].reshape(-1, moe_w1.shape[-1]), moe_w3[0].reshape(-1, moe_w3.shape[-1])))
    att_c = attention_context(q, k, v, GQA_HEADS)
    w_out0 = ab_w_out[0]
    x1 = merge0(rec, gate, att_l, att_c, x, ctx, mods[0], bf(w_out0[:LRU_WIDTH]), bf(w_out0[LRU_WIDTH:][perm]))
    xa = ffn_residual(x1, mods[0], row(norm2_g[0]), bf(ffn_w1[0]), bf(ffn_w3[0]), bf(ffn_w2[0]))

    w_in1 = jnp.pad(cd_w_in[0], ((0, 0), (0, LANES - MLA_ROPE)))
    cos_m, sin_m = _rope_tables(MLA_ROPE, (MLA_NOPE,))
    qb = _pad_heads(mla_q_b[0], MLA_HEADS, MLA_QK)
    kvb = mla_kv_b[0].reshape(MLA_KV_RANK, MLA_HEADS, MLA_NOPE + MLA_V)
    kbk = _pad_heads(kvb[:, :, :MLA_NOPE].reshape(MLA_KV_RANK, -1), MLA_HEADS, MLA_NOPE)
    kbv = kvb[:, :, MLA_NOPE:].reshape(MLA_KV_RANK, -1)
    pad_gain = lambda g_: row(jnp.pad(g_, (0, LANES - MLA_QK)))
    hyz, q, k, v = proj1(xa, row(norm1_g[1]), mods[1], bf(w_in1), row(mla_q_a_norm[0]), bf(qb),
                         row(mla_kv_a_norm[0]), bf(kbk), bf(kbv), cos_m, sin_m,
                         pad_gain(mla_q_norm[0] * (MLA_QK ** -0.5 * LOG2_E)), pad_gain(mla_k_norm[0]))
    att = attention_latent(q, k, v, MLA_HEAD_GROUPS, MLA_TILE_Q)
    hy = hyena(hyz, hy_conv_w[0], hy_conv_b[0], hy_filt_w1[0], hy_filt_b1[0], hy_filt_w2[0],
               hy_filt_b2[0], hy_filt_w3[0], hy_sin_freq[0], hy_skip[0])
    w_out1 = cd_w_out[0]
    x1, h2, logits = merge1(hy, att, xa, mods[1], row(norm2_g[1]),
                            bf(w_out1[:HY_WIDTH]), bf(w_out1[HY_WIDTH:]), moe_router[0].T)
    return moe_residual(x1, h2, logits, mods[1], moe_w1b.reshape(moe_w1.shape[1:]),
                        moe_w3b.reshape(moe_w3.shape[1:]), moe_w2[0])
```
